```python
import math
import jax, jax.numpy as jnp
from jax import lax
import numpy as np

D_MODEL = 2048
BATCH = 4
SEQ = 2048
DEPTH = 2
DEC_BATCH = 16
DEC_SEQ = 2048
PAST_LEN = 128

RET_HEADS = 8
RET_DV = D_MODEL // RET_HEADS
RET_DK = RET_DV // 2
RET_DECAY_FWD = 5.0
RET_DECAY_BWD = 5.5
ROPE_BASE = 10000.0
MLSTM_HEADS = 4
MLSTM_DV = D_MODEL // MLSTM_HEADS
MLSTM_DK = MLSTM_DV // 2
SSD_INNER = 2 * D_MODEL
SSD_HEADDIM = 64
SSD_HEADS = SSD_INNER // SSD_HEADDIM
SSD_GROUPS = 8
SSD_HEADS_PER_GROUP = SSD_HEADS // SSD_GROUPS
SSD_STATE = 128
CHUNK = 128
CONV_W = 5
MOE_GROUPS = 4
MOE_EXPERTS_PER_GROUP = 8
MOE_EXPERTS = MOE_GROUPS * MOE_EXPERTS_PER_GROUP
MOE_TOP_K = 2
EXPERT_FF = D_MODEL // 4
MOE_BLOCK = 128
N_MOD = 6
EPS = 1e-6

RET_QK = RET_HEADS * RET_DK
RET_V = RET_HEADS * RET_DV
MLSTM_QK = MLSTM_HEADS * MLSTM_DK
MLSTM_V = MLSTM_HEADS * MLSTM_DV
MLSTM_NGATE = 4 * MLSTM_HEADS
EVEN_SPLITS = (RET_QK, RET_QK, RET_V, RET_V, MLSTM_QK, MLSTM_QK, MLSTM_V, MLSTM_V, MLSTM_NGATE)
EVEN_PROJ = 2 * RET_QK + 2 * RET_V + 2 * MLSTM_QK + 2 * MLSTM_V + MLSTM_NGATE
EVEN_MIX = RET_V + MLSTM_V
SSD_BC = SSD_GROUPS * SSD_STATE
SSD_CONV_CH = SSD_INNER + 2 * SSD_BC
ODD_SPLITS = (SSD_INNER, SSD_CONV_CH, 2 * SSD_HEADS)
ODD_PROJ = SSD_INNER + SSD_CONV_CH + 2 * SSD_HEADS

kernel_name = 'hybrid_retention_mlstm_ssd_hmoe_encoder'

F32 = jnp.float32


def split_cols(a, sizes):
    idx = [int(v) for v in np.cumsum(sizes)[:-1]]
    return jnp.split(a, idx, axis=-1)


def rms_norm(x, g):
    xf = x.astype(F32)
    y = xf * lax.rsqrt(jnp.mean(xf * xf, axis=-1, keepdims=True) + EPS) * g.astype(F32)
    return y.astype(x.dtype)


def head_norm(y, g):
    yc = y - jnp.mean(y, axis=-1, keepdims=True)
    var = jnp.mean(yc * yc, axis=-1, keepdims=True)
    return yc * lax.rsqrt(var + EPS) * g.astype(F32).reshape(y.shape[-2], y.shape[-1])


def dwconv_centred(x, w, b):
    ch = x.shape[-1]
    pad = (CONV_W - 1) // 2
    y = lax.conv_general_dilated(x, w[:, None, :].astype(x.dtype), (1,), [(pad, pad)],
                                 dimension_numbers=('NWC', 'WIO', 'NWC'), feature_group_count=ch)
    return y + b.astype(x.dtype)


def rope(x):
    s, d = x.shape[1], x.shape[-1]
    half = d // 2
    inv = ROPE_BASE ** (-jnp.arange(half, dtype=F32) / half)
    ang = jnp.arange(s, dtype=F32)[:, None] * inv[None, :]
    cos = jnp.cos(ang)[None, :, None, :]
    sin = jnp.sin(ang)[None, :, None, :]
    xf = x.astype(F32)
    x1, x2 = xf[..., :half], xf[..., half:]
    return jnp.concatenate([x1 * cos - x2 * sin, x1 * sin + x2 * cos], axis=-1)


def flip_seq(a):
    return jnp.flip(a, axis=1)


def to_chunks(a):
    b, s = a.shape[:2]
    a = a.reshape(b, s // CHUNK, CHUNK, *a.shape[2:])
    return jnp.moveaxis(a, (1, 3), (0, 2))


def from_chunks(a):
    a = jnp.moveaxis(a, (0, 2), (1, 3))
    return a.reshape(a.shape[0], a.shape[1] * a.shape[2], *a.shape[3:])


def retention_dir(q, k, v, log_gamma, strict):
    bsz, _, nh, dk = q.shape
    dv = v.shape[-1]
    idx = jnp.arange(CHUNK, dtype=F32)
    diff = idx[:, None] - idx[None, :]
    mask = (diff > 0) if strict else (diff >= 0)
    dmat = jnp.where(mask, jnp.exp(log_gamma[:, None, None] * jnp.where(mask, diff, 0.0)), 0.0)
    q_dec = jnp.exp(log_gamma[:, None] * (idx + 1.0))
    k_dec = jnp.exp(log_gamma[:, None] * (CHUNK - 1.0 - idx))
    c_dec = jnp.exp(log_gamma * CHUNK)

    def step(state, inp):
        qc, kc, vc = inp
        sc = jnp.einsum('bhld,bhsd->bhls', qc, kc) * dmat
        out = jnp.einsum('bhls,bhsv->bhlv', sc, vc) + q_dec[..., None] * jnp.einsum('bhld,bhdv->bhlv', qc, state)
        state = c_dec[:, None, None] * state + jnp.einsum('bhld,bhlv->bhdv', kc * k_dec[..., None], vc)
        return state, out

    s0 = jnp.zeros((bsz, nh, dk, dv), F32)
    _, out = lax.scan(step, s0, (to_chunks(q), to_chunks(k), to_chunks(v)))
    return from_chunks(out)


def mlstm_dir(q, k, v, ig, lf):
    bsz, _, nh, dk = q.shape
    dv = v.shape[-1]
    tril = jnp.tril(jnp.ones((CHUNK, CHUNK), bool))

    def step(carry, inp):
        c_st, n_st, m_st = carry
        qc, kc, vc, ic, fc = inp
        bcum = jnp.cumsum(fc, axis=-1)
        logd = jnp.where(tril, bcum[..., :, None] - bcum[..., None, :] + ic[..., None, :], -jnp.inf)
        m_inter = bcum + m_st[..., None]
        m_row = jnp.maximum(m_inter, jnp.max(logd, axis=-1))
        sc = jnp.einsum('bhld,bhsd->bhls', qc, kc) * jnp.exp(logd - m_row[..., None])
        inter = jnp.exp(m_inter - m_row)
        num = jnp.einsum('bhls,bhsv->bhlv', sc, vc) + inter[..., None] * jnp.einsum('bhld,bhdv->bhlv', qc, c_st)
        den = jnp.sum(sc, axis=-1) + inter * jnp.einsum('bhld,bhd->bhl', qc, n_st)
        h = num / jnp.maximum(jnp.abs(den), jnp.exp(-m_row))[..., None]
        b_end = bcum[..., -1]
        logw = b_end[..., None] - bcum + ic
        m_new = jnp.maximum(b_end + m_st, jnp.max(logw, axis=-1))
        kw = kc * jnp.exp(logw - m_new[..., None])[..., None]
        dec = jnp.exp(b_end + m_st - m_new)
        c_st = dec[..., None, None] * c_st + jnp.einsum('bhld,bhlv->bhdv', kw, vc)
        n_st = dec[..., None] * n_st + jnp.sum(kw, axis=2)
        return (c_st, n_st, m_new), h

    init = (jnp.zeros((bsz, nh, dk, dv), F32), jnp.zeros((bsz, nh, dk), F32), jnp.zeros((bsz, nh), F32))
    _, out = lax.scan(step, init, (to_chunks(q), to_chunks(k), to_chunks(v), to_chunks(ig), to_chunks(lf)))
    return from_chunks(out)


def ssd_dir(x, bm, cm, dt, a):
    bsz, _, nh, hp = x.shape
    tril = jnp.tril(jnp.ones((CHUNK, CHUNK), bool))

    def step(state, inp):
        xc, bc, cc, dtc = inp
        acum = jnp.cumsum(dtc * a[:, None], axis=-1)
        decay = jnp.exp(jnp.where(tril, acum[..., :, None] - acum[..., None, :], -jnp.inf))
        decay = decay.reshape(bsz, SSD_GROUPS, SSD_HEADS_PER_GROUP, CHUNK, CHUNK)
        xdt = (xc * dtc[..., None]).reshape(bsz, SSD_GROUPS, SSD_HEADS_PER_GROUP, CHUNK, hp)
        cb = jnp.einsum('bgln,bgsn->bgls', cc, bc)
        y = jnp.einsum('bgkls,bgksp->bgklp', cb[:, :, None] * decay, xdt)
        acum_g = acum.reshape(bsz, SSD_GROUPS, SSD_HEADS_PER_GROUP, CHUNK)
        y = y + jnp.exp(acum_g)[..., None] * jnp.einsum('bgln,bgkpn->bgklp', cc, state)
        a_end = acum_g[..., -1]
        w = jnp.exp(a_end[..., None] - acum_g)
        state = jnp.exp(a_end)[..., None, None] * state + jnp.einsum('bgklp,bgln->bgkpn', xdt * w[..., None], bc)
        return state, y.reshape(bsz, nh, CHUNK, hp)

    s0 = jnp.zeros((bsz, SSD_GROUPS, SSD_HEADS_PER_GROUP, hp, SSD_STATE), F32)
    _, y = lax.scan(step, s0, (to_chunks(x), to_chunks(bm), to_chunks(cm), to_chunks(dt)))
    return from_chunks(y)


def retention_mlstm_mixer(h, w_in, gate_b, conv_w, conv_b, ret_gn, mlstm_gn, w_out):
    bsz, s, _ = h.shape
    proj = h @ w_in
    rq, rk, rv, rg, mq, mk, mv, mo, mg = split_cols(proj, EVEN_SPLITS)
    q = rope(rq.reshape(bsz, s, RET_HEADS, RET_DK))
    k = rope(rk.reshape(bsz, s, RET_HEADS, RET_DK)) * (RET_DK ** -0.5)
    v = rv.reshape(bsz, s, RET_HEADS, RET_DV).astype(F32)
    heads = jnp.arange(RET_HEADS, dtype=F32)
    lg_f = jnp.log1p(-jnp.exp2(-RET_DECAY_FWD - heads))
    lg_b = jnp.log1p(-jnp.exp2(-RET_DECAY_BWD - heads))
    ret = retention_dir(q, k, v, lg_f, False) + flip_seq(
        retention_dir(flip_seq(q), flip_seq(k), flip_seq(v), lg_b, True))
    ret = head_norm(ret, ret_gn).reshape(bsz, s, RET_V) * jax.nn.silu(rg.astype(F32))
    qk = jax.nn.silu(dwconv_centred(jnp.concatenate([mq, mk], axis=-1), conv_w, conv_b))
    mq, mk = jnp.split(qk, 2, axis=-1)
    q = mq.reshape(bsz, s, MLSTM_HEADS, MLSTM_DK).astype(F32) * (MLSTM_DK ** -0.5)
    k = mk.reshape(bsz, s, MLSTM_HEADS, MLSTM_DK).astype(F32)
    v = mv.reshape(bsz, s, MLSTM_HEADS, MLSTM_DV).astype(F32)
    gates = (mg.astype(F32) + gate_b.astype(F32)).reshape(bsz, s, 4, MLSTM_HEADS)
    i_f, f_f, i_b, f_b = gates[:, :, 0], gates[:, :, 1], gates[:, :, 2], gates[:, :, 3]
    ml = mlstm_dir(q, k, v, i_f, jax.nn.log_sigmoid(f_f)) + flip_seq(
        mlstm_dir(flip_seq(q), flip_seq(k), flip_seq(v), flip_seq(i_b), flip_seq(jax.nn.log_sigmoid(f_b))))
    ml = head_norm(ml, mlstm_gn).reshape(bsz, s, MLSTM_V) * jax.nn.sigmoid(mo.astype(F32))
    mix = jnp.concatenate([ret, ml], axis=-1).astype(h.dtype)
    return mix @ w_out


def ssd_mixer(h, w_in, conv_w, conv_b, dt_bias, a_log, d_skip, norm_w, w_out):
    bsz, s, _ = h.shape
    proj = h @ w_in
    z, xbc, dt_raw = split_cols(proj, ODD_SPLITS)
    xbc = jax.nn.silu(dwconv_centred(xbc, conv_w, conv_b))
    xs, bm, cm = split_cols(xbc, (SSD_INNER, SSD_BC, SSD_BC))
    xs = xs.reshape(bsz, s, SSD_HEADS, SSD_HEADDIM).astype(F32)
    bm = bm.reshape(bsz, s, SSD_GROUPS, SSD_STATE).astype(F32)
    cm = cm.reshape(bsz, s, SSD_GROUPS, SSD_STATE).astype(F32)
    dt = jax.nn.softplus(dt_raw.astype(F32).reshape(bsz, s, 2, SSD_HEADS) + dt_bias.astype(F32))
    a = -jnp.exp(a_log.astype(F32))
    y = ssd_dir(xs, bm, cm, dt[:, :, 0], a[0]) + flip_seq(
        ssd_dir(flip_seq(xs), flip_seq(bm), flip_seq(cm), flip_seq(dt[:, :, 1]), a[1]))
    y = y + d_skip.astype(F32)[:, None] * xs
    y = y.reshape(bsz, s, SSD_INNER) * jax.nn.silu(z.astype(F32))
    y = rms_norm(y, norm_w).astype(h.dtype)
    return y @ w_out


def hier_moe(h, grp_w, grp_b, exp_w, exp_b, w_gate, w_up, w_down):
    bsz, s, d = h.shape
    n_tok = bsz * s
    t = h.reshape(n_tok, d)
    g_logits = (t @ grp_w).astype(F32) + grp_b.astype(F32)
    g_idx = jnp.argmax(g_logits, axis=-1)
    g_prob = jnp.take_along_axis(jax.nn.softmax(g_logits, axis=-1), g_idx[:, None], axis=1)
    e_logits = ((t @ exp_w).astype(F32) + exp_b.astype(F32)).reshape(n_tok, MOE_GROUPS, MOE_EXPERTS_PER_GROUP)
    e_logits = jnp.take_along_axis(e_logits, g_idx[:, None, None], axis=1)[:, 0]
    top_v, top_i = lax.top_k(e_logits, MOE_TOP_K)
    gate = (jax.nn.softmax(top_v, axis=-1) * g_prob).reshape(-1)
    expert_id = (g_idx[:, None] * MOE_EXPERTS_PER_GROUP + top_i).reshape(-1).astype(jnp.int32)
    token_id = jnp.repeat(jnp.arange(n_tok, dtype=jnp.int32), MOE_TOP_K)
    n_assign = n_tok * MOE_TOP_K
    order = jnp.argsort(expert_id)
    e_sorted = expert_id[order]
    counts = jnp.bincount(expert_id, length=MOE_EXPERTS)
    starts = jnp.cumsum(counts) - counts
    padded = (counts + MOE_BLOCK - 1) // MOE_BLOCK * MOE_BLOCK
    p_ends = jnp.cumsum(padded)
    p_starts = p_ends - padded
    dest = p_starts[e_sorted] + jnp.arange(n_assign, dtype=jnp.int32) - starts[e_sorted]
    n_rows = n_assign + MOE_EXPERTS * MOE_BLOCK
    n_blocks = n_rows // MOE_BLOCK
    row_tok = jnp.full((n_rows,), n_tok, jnp.int32).at[dest].set(token_id[order])
    row_gate = jnp.zeros((n_rows,), F32).at[dest].set(gate[order])
    blk_exp = jnp.minimum(jnp.searchsorted(p_ends, jnp.arange(n_blocks, dtype=jnp.int32) * MOE_BLOCK, side='right'),
                          MOE_EXPERTS - 1)
    t_pad = jnp.concatenate([t, jnp.zeros((1, d), t.dtype)], axis=0)

    def expert_block(inp):
        tok, gw, e = inp
        xb = t_pad[tok]
        hid = jax.nn.silu(xb @ w_gate[e]) * (xb @ w_up[e])
        return ((hid @ w_down[e]) * gw[:, None]).astype(t.dtype)

    y = lax.map(expert_block, (row_tok.reshape(n_blocks, MOE_BLOCK), row_gate.reshape(n_blocks, MOE_BLOCK), blk_exp))
    out = jax.ops.segment_sum(y.reshape(n_rows, d), row_tok, num_segments=n_tok + 1)[:n_tok]
    return out.reshape(bsz, s, d)


def trunk(x, c, p):
    for i in range(DEPTH):
        mod = (jax.nn.silu(c) @ p['ada_w'][i] + p['ada_b'][i])[:, None, :]
        sh1, sc1, g1, sh2, sc2, g2 = jnp.split(mod, N_MOD, axis=-1)
        h = rms_norm(x, p['norm1'][i]) * (1.0 + sc1) + sh1
        j = i // 2
        if i % 2 == 0:
            mix = retention_mlstm_mixer(h, p['ev_w_in'][j], p['ev_gate_b'][j], p['ev_conv_w'][j], p['ev_conv_b'][j],
                                        p['ev_ret_gn'][j], p['ev_mlstm_gn'][j], p['ev_w_out'][j])
        else:
            mix = ssd_mixer(h, p['od_w_in'][j], p['od_conv_w'][j], p['od_conv_b'][j], p['od_dt_bias'][j],
                            p['od_a_log'][j], p['od_d_skip'][j], p['od_norm'][j], p['od_w_out'][j])
        x = x + g1 * mix
        h = rms_norm(x, p['norm2'][i]) * (1.0 + sc2) + sh2
        x = x + g2 * hier_moe(h, p['moe_grp_w'][i], p['moe_grp_b'][i], p['moe_exp_w'][i], p['moe_exp_b'][i],
                              p['moe_w_gate'][i], p['moe_w_up'][i], p['moe_w_down'][i])
    return rms_norm(x, p['final_norm'])


def setup_inputs(seed: int = 0) -> dict:
    key = jax.random.key(seed)
    ks = list(jax.random.split(key, 48))
    nk = ks.pop
    n_even = (DEPTH + 1) // 2
    n_odd = DEPTH // 2

    def nrm(k, shape, scale):
        return scale * jax.random.normal(k, shape, F32)

    def gain(k, shape):
        return 1.0 + 0.1 * jax.random.normal(k, shape, F32)

    gi = nrm(nk(), (n_even, 2, 1, MLSTM_HEADS), 0.1)
    gf = 3.0 + 3.0 * jax.random.uniform(nk(), (n_even, 2, 1, MLSTM_HEADS), F32)
    ev_gate_b = jnp.concatenate([gi, gf], axis=2).reshape(n_even, MLSTM_NGATE)
    u = jax.random.uniform(nk(), (n_odd, 2, SSD_HEADS), F32)
    dt0 = jnp.exp(u * (math.log(0.1) - math.log(1e-3)) + math.log(1e-3))
    od_dt_bias = dt0 + jnp.log(-jnp.expm1(-dt0))
    od_a_log = jnp.log(jax.random.uniform(nk(), (n_odd, 2, SSD_HEADS), F32, 1.0, 16.0))

    return {
        'x_prompt': nrm(nk(), (BATCH, SEQ, D_MODEL), 1.0),
        'x_sample': nrm(nk(), (DEC_BATCH, DEC_SEQ, D_MODEL), 1.0),
        'c_prompt': nrm(nk(), (BATCH, D_MODEL), 1.0),
        'c_sample': nrm(nk(), (DEC_BATCH, D_MODEL), 1.0),
        'ada_w': nrm(nk(), (DEPTH, D_MODEL, N_MOD * D_MODEL), 0.5 * D_MODEL ** -0.5),
        'ada_b': nrm(nk(), (DEPTH, N_MOD * D_MODEL), 0.02),
        'norm1': gain(nk(), (DEPTH, D_MODEL)),
        'norm2': gain(nk(), (DEPTH, D_MODEL)),
        'ev_w_in': nrm(nk(), (n_even, D_MODEL, EVEN_PROJ), D_MODEL ** -0.5),
        'ev_gate_b': ev_gate_b,
        'ev_conv_w': nrm(nk(), (n_even, CONV_W, 2 * MLSTM_QK), CONV_W ** -0.5),
        'ev_conv_b': nrm(nk(), (n_even, 2 * MLSTM_QK), 0.02),
        'ev_ret_gn': gain(nk(), (n_even, RET_V)),
        'ev_mlstm_gn': gain(nk(), (n_even, MLSTM_V)),
        'ev_w_out': nrm(nk(), (n_even, EVEN_MIX, D_MODEL), EVEN_MIX ** -0.5),
        'od_w_in': nrm(nk(), (n_odd, D_MODEL, ODD_PROJ), D_MODEL ** -0.5),
        'od_conv_w': nrm(nk(), (n_odd, CONV_W, SSD_CONV_CH), CONV_W ** -0.5),
        'od_conv_b': nrm(nk(), (n_odd, SSD_CONV_CH), 0.02),
        'od_dt_bias': od_dt_bias,
        'od_a_log': od_a_log,
        'od_d_skip': gain(nk(), (n_odd, SSD_HEADS)),
        'od_norm': gain(nk(), (n_odd, SSD_INNER)),
        'od_w_out': nrm(nk(), (n_odd, SSD_INNER, D_MODEL), SSD_INNER ** -0.5),
        'moe_grp_w': nrm(nk(), (DEPTH, D_MODEL, MOE_GROUPS), D_MODEL ** -0.5),
        'moe_grp_b': nrm(nk(), (DEPTH, MOE_GROUPS), 0.01),
        'moe_exp_w': nrm(nk(), (DEPTH, D_MODEL, MOE_EXPERTS), D_MODEL ** -0.5),
        'moe_exp_b': nrm(nk(), (DEPTH, MOE_EXPERTS), 0.01),
        'moe_w_gate': nrm(nk(), (DEPTH, MOE_EXPERTS, D_MODEL, EXPERT_FF), D_MODEL ** -0.5),
        'moe_w_up': nrm(nk(), (DEPTH, MOE_EXPERTS, D_MODEL, EXPERT_FF), D_MODEL ** -0.5),
        'moe_w_down': nrm(nk(), (DEPTH, MOE_EXPERTS, EXPERT_FF, D_MODEL), EXPERT_FF ** -0.5),
        'final_norm': gain(nk(), (D_MODEL,)),
    }


def reference(x_prompt, x_sample, c_prompt, c_sample, ada_w, ada_b, norm1, norm2,
              ev_w_in, ev_gate_b, ev_conv_w, ev_conv_b, ev_ret_gn, ev_mlstm_gn, ev_w_out,
              od_w_in, od_conv_w, od_conv_b, od_dt_bias, od_a_log, od_d_skip, od_norm, od_w_out,
              moe_grp_w, moe_grp_b, moe_exp_w, moe_exp_b, moe_w_gate, moe_w_up, moe_w_down, final_norm):
    params = dict(ada_w=ada_w, ada_b=ada_b, norm1=norm1, norm2=norm2,
                  ev_w_in=ev_w_in, ev_gate_b=ev_gate_b, ev_conv_w=ev_conv_w, ev_conv_b=ev_conv_b,
                  ev_ret_gn=ev_ret_gn, ev_mlstm_gn=ev_mlstm_gn, ev_w_out=ev_w_out,
                  od_w_in=od_w_in, od_conv_w=od_conv_w, od_conv_b=od_conv_b, od_dt_bias=od_dt_bias,
                  od_a_log=od_a_log, od_d_skip=od_d_skip, od_norm=od_norm, od_w_out=od_w_out,
                  moe_grp_w=moe_grp_w, moe_grp_b=moe_grp_b, moe_exp_w=moe_exp_w, moe_exp_b=moe_exp_b,
                  moe_w_gate=moe_w_gate, moe_w_up=moe_w_up, moe_w_down=moe_w_down, final_norm=final_norm)
    y_prompt = trunk(x_prompt, c_prompt, params)
    y_sample = trunk(x_sample, c_sample, params)
    return (y_prompt, y_sample)
```

```python
import functools
import math

import jax
import jax.numpy as jnp
import numpy as np
from jax import lax
from jax.experimental import pallas as pl
from jax.experimental.pallas import tpu as pltpu

F32 = jnp.float32
BF16 = jnp.bfloat16
I32 = jnp.int32

D_MODEL = 2048
N_MOD = 6
EPS = 1e-6
CONV_W = 5
CONV_HALO = 8

RET_HEADS = 8
RET_DV = D_MODEL // RET_HEADS
RET_DK = RET_DV // 2
RET_DECAY_FWD = 5.0
RET_DECAY_BWD = 5.5
ROPE_BASE = 10000.0
RET_CHUNK = 256
MLSTM_HEADS = 4
MLSTM_DV = D_MODEL // MLSTM_HEADS
MLSTM_DK = MLSTM_DV // 2
MLSTM_CHUNK = 128
SSD_INNER = 2 * D_MODEL
SSD_HEADDIM = 64
SSD_HEADS = SSD_INNER // SSD_HEADDIM
SSD_GROUPS = 8
SSD_HPG = SSD_HEADS // SSD_GROUPS
SSD_STATE = 128
SSD_CHUNK = 128
MOE_GROUPS = 4
MOE_EPG = 8
MOE_EXPERTS = MOE_GROUPS * MOE_EPG
EXPERT_FF = D_MODEL // 4
MOE_BLOCK = 128

RET_QK = RET_HEADS * RET_DK
RET_V = RET_HEADS * RET_DV
MLSTM_QK = MLSTM_HEADS * MLSTM_DK
MLSTM_V = MLSTM_HEADS * MLSTM_DV
MLSTM_NGATE = 4 * MLSTM_HEADS
EVEN_MAIN = 2 * RET_QK + 2 * RET_V + 2 * MLSTM_QK + 2 * MLSTM_V
EVEN_MIX = RET_V + MLSTM_V
SSD_BC = SSD_GROUPS * SSD_STATE
SSD_CONV_CH = SSD_INNER + 2 * SSD_BC
ODD_MAIN = SSD_INNER + SSD_CONV_CH

PROLOGUE_ROWS = 256
LANES = 128
VMEM_LIMIT = 56 * 1024 * 1024

NEG_INF = float("-inf")


def _cparams(sem, vmem=VMEM_LIMIT):
    return pltpu.CompilerParams(dimension_semantics=sem, vmem_limit_bytes=vmem)


def _split3(x):
    hi = x.astype(BF16)
    r = x - hi.astype(F32)
    mid = r.astype(BF16)
    lo = (r - mid.astype(F32)).astype(BF16)
    return hi, mid, lo


def _dot(a, b):
    return jnp.dot(a, b, preferred_element_type=F32)


def _dot_nt(a, b):
    return lax.dot_general(a, b, (((1,), (1,)), ((), ())), preferred_element_type=F32)


def _dot_tn(a, b):
    return lax.dot_general(a, b, (((0,), (0,)), ((), ())), preferred_element_type=F32)


def _dot01_left(m01, x):
    hi, mid, lo = _split3(x)
    return _dot(m01, hi) + _dot(m01, mid) + _dot(m01, lo)


def _dot01_right(x, m01):
    hi, mid, lo = _split3(x)
    return _dot(hi, m01) + _dot(mid, m01) + _dot(lo, m01)


def _tri(n, kind):
    r = lax.broadcasted_iota(I32, (n, n), 0)
    c = lax.broadcasted_iota(I32, (n, n), 1)
    m = {"le": r <= c, "ge": r >= c, "gt": r > c}[kind]
    return jnp.where(m, 1.0, 0.0).astype(BF16)


def _shr(x, pow2):
    return lax.shift_right_arithmetic(x, jnp.int32(int(math.log2(pow2))))


def _sigmoid(x):
    return 1.0 / (1.0 + jnp.exp(-x))


def _silu(x):
    return x * _sigmoid(x)


def _softplus(x):
    return jnp.maximum(x, 0.0) + jnp.log1p(jnp.exp(-jnp.abs(x)))


def _log_sigmoid(x):
    return jnp.minimum(x, 0.0) - jnp.log1p(jnp.exp(-jnp.abs(x)))


def _rms(x, g):
    ms = jnp.mean(x * x, axis=-1, keepdims=True)
    return x * lax.rsqrt(ms + EPS) * g


def _head_norm(y, g):
    mu = jnp.mean(y, axis=-1, keepdims=True)
    yc = y - mu
    var = jnp.mean(yc * yc, axis=-1, keepdims=True)
    return yc * lax.rsqrt(var + EPS) * g


def _mod_body(c_ref, w_ref, b_ref, o_ref):
    c = c_ref[...]
    o_ref[0] = _dot(_silu(c).astype(BF16), w_ref[0].astype(BF16)) + b_ref[0]


def _modulation(c_pad, ada_w, ada_b):
    depth, d, n = ada_w.shape
    m = c_pad.shape[0]
    tn = 1024
    return pl.pallas_call(
        _mod_body,
        out_shape=jax.ShapeDtypeStruct((depth, m, n), F32),
        grid=(depth, n // tn),
        in_specs=[
            pl.BlockSpec((m, d), lambda l, j: (0, 0)),
            pl.BlockSpec((1, d, tn), lambda l, j: (l, 0, j)),
            pl.BlockSpec((1, 1, tn), lambda l, j: (l, 0, j)),
        ],
        out_specs=pl.BlockSpec((1, m, tn), lambda l, j: (l, 0, j)),
        compiler_params=_cparams(("parallel", "parallel")),
        name="modulation",
    )(c_pad, ada_w, ada_b.reshape(depth, 1, n))


def _mm_body(*refs, prologue, epilogue, side, two_lhs):
    it = iter(refs)
    x_ref = next(it)
    x2_ref = next(it) if two_lhs else None
    g_ref = next(it) if prologue != "none" else None
    sc_ref = next(it) if prologue == "normmod" else None
    sh_ref = next(it) if prologue == "normmod" else None
    w_ref = next(it)
    ws_ref = next(it) if side else None
    res_ref = next(it) if epilogue == "residual" else None
    gate_ref = next(it) if epilogue == "residual" else None
    o_ref = next(it)
    os_ref = next(it) if side else None
    h_scr = next(it) if prologue != "none" else None

    if prologue != "none":
        @pl.when(pl.program_id(1) == 0)
        def _():
            rows_per = PROLOGUE_ROWS

            def chunk(i, carry):
                rows = pl.ds(pl.multiple_of(i * rows_per, rows_per), rows_per)
                y = _rms(x_ref[rows, :].astype(F32), g_ref[...])
                if prologue == "normmod":
                    y = y * (1.0 + sc_ref[0]) + sh_ref[0]
                hb = y.astype(BF16)
                h_scr[rows, :] = hb
                if side:
                    h_lo = (y - hb.astype(F32)).astype(BF16)
                    ws = ws_ref[...]
                    w_hi = ws.astype(BF16)
                    w_lo = (ws - w_hi.astype(F32)).astype(BF16)
                    os_ref[rows, :] = _dot(hb, w_hi) + _dot(h_lo, w_hi) + _dot(hb, w_lo)
                return carry

            lax.fori_loop(0, x_ref.shape[0] // rows_per, chunk, 0)
        lhs = h_scr[...]
    else:
        lhs = x_ref[...]
    if two_lhs:
        k1 = x_ref.shape[1]
        acc = _dot(lhs, w_ref[:k1, :]) + _dot(x2_ref[...], w_ref[k1:, :])
    else:
        acc = _dot(lhs, w_ref[...])
    if epilogue == "residual":
        o_ref[...] = res_ref[...] + gate_ref[0] * acc
    else:
        o_ref[...] = acc.astype(o_ref.dtype)


def _fused_matmul(x, w, *, seq, x2=None, prologue="none", g=None, sc=None, sh=None, w_side=None,
                  res=None, gate=None, out_dtype=BF16, tm=1024, tn=1024, name="proj"):
    t, k = x.shape
    n = w.shape[1]
    tm = min(tm, seq)
    tn = min(tn, n)
    assert t % tm == 0 and seq % tm == 0 and n % tn == 0
    tps = seq // tm
    epilogue = "residual" if res is not None else "plain"
    side = w_side is not None
    two_lhs = x2 is not None
    assert not (two_lhs and prologue != "none")
    in_specs = [pl.BlockSpec((tm, k), lambda i, j: (i, 0))]
    args = [x]
    if two_lhs:
        in_specs.append(pl.BlockSpec((tm, x2.shape[1]), lambda i, j: (i, 0)))
        args.append(x2)
        k = k + x2.shape[1]
    if prologue != "none":
        in_specs.append(pl.BlockSpec((1, k), lambda i, j: (0, 0)))
        args.append(g.reshape(1, k))
    if prologue == "normmod":
        in_specs += [pl.BlockSpec((1, 1, k), lambda i, j: (i // tps, 0, 0))] * 2
        args += [sc, sh]
    in_specs.append(pl.BlockSpec((k, tn), lambda i, j: (0, j)))
    args.append(w)
    if side:
        in_specs.append(pl.BlockSpec((k, LANES), lambda i, j: (0, 0)))
        args.append(w_side)
    if epilogue == "residual":
        in_specs += [pl.BlockSpec((tm, tn), lambda i, j: (i, j)),
                     pl.BlockSpec((1, 1, tn), lambda i, j: (i // tps, 0, j))]
        args += [res, gate]
        out_dtype = F32
    out_shape = [jax.ShapeDtypeStruct((t, n), out_dtype)]
    out_specs = [pl.BlockSpec((tm, tn), lambda i, j: (i, j))]
    if side:
        out_shape.append(jax.ShapeDtypeStruct((t, LANES), F32))
        out_specs.append(pl.BlockSpec((tm, LANES), lambda i, j: (i, 0)))
    scratch = [pltpu.VMEM((tm, k), BF16)] if prologue != "none" else []
    outs = pl.pallas_call(
        functools.partial(_mm_body, prologue=prologue, epilogue=epilogue, side=side, two_lhs=two_lhs),
        out_shape=out_shape,
        grid=(t // tm, n // tn),
        in_specs=in_specs,
        out_specs=out_specs,
        scratch_shapes=scratch,
        compiler_params=_cparams(("parallel", "arbitrary")),
        name=name,
    )(*args)
    return outs if side else outs[0]


def _conv_silu_chunks(src_ref, w_ref, b_ref, pad_scr, emit, seq, rows=128):
    ch = pad_scr.shape[1]
    halo = CONV_HALO
    zeros = jnp.zeros((halo, ch), F32)
    pad_scr[pl.ds(0, halo), :] = zeros
    pad_scr[pl.ds(seq + halo, halo), :] = zeros

    def fill(i, carry):
        r0 = pl.multiple_of(i * rows, rows)
        pad_scr[pl.ds(pl.multiple_of(r0 + halo, halo), rows), :] = src_ref[0, pl.ds(r0, rows), :].astype(F32)
        return carry

    lax.fori_loop(0, seq // rows, fill, 0)
    win = rows + 2 * halo
    half = (CONV_W - 1) // 2

    def body(i, carry):
        r0 = pl.multiple_of(i * rows, rows)
        for lane0 in range(0, ch, LANES):
            cols = slice(lane0, lane0 + LANES)
            window = pad_scr[pl.ds(r0, win), cols]
            acc = jnp.zeros((rows, LANES), F32) + b_ref[:, cols]
            for j in range(CONV_W):
                d = j - half
                shifted = window if d == 0 else pltpu.roll(window, (-d) % win, axis=0)
                acc = acc + w_ref[j:j + 1, cols] * shifted[halo:halo + rows, :]
            emit(r0, lane0, _silu(acc))
        return carry

    lax.fori_loop(0, seq // rows, body, 0)


def _ret_body(lg_ref, q_ref, k_ref, v_ref, g_ref, cos_ref, sin_ref, gn_ref, o_ref,
              qs_scr, ks_scr, acc_scr, st_scr, *, seq):
    c_len = RET_CHUNK
    n_chunks = seq // c_len
    h = pl.program_id(1)
    lgf = lg_ref[0, h]
    lgb = lg_ref[1, h]
    ri = lax.broadcasted_iota(I32, (c_len, c_len), 0)
    ci = lax.broadcasted_iota(I32, (c_len, c_len), 1)
    diff = (ri - ci).astype(F32)
    dmat = jnp.exp(jnp.where(diff >= 0, lgf * diff, -lgb * diff))
    pos = lax.broadcasted_iota(I32, (c_len, 1), 0).astype(F32)
    qdec_f = jnp.exp(lgf * (pos + 1.0))
    kdec_f = jnp.exp(lgf * (c_len - 1.0 - pos))
    cdec_f = jnp.exp(jnp.full((1, 1), c_len, F32) * lgf)
    qdec_b = jnp.exp(lgb * (c_len - pos))
    kdec_b = jnp.exp(lgb * pos)
    cdec_b = jnp.exp(jnp.full((1, 1), c_len, F32) * lgb)
    half = RET_DK // 2

    def rope(x, rows):
        return x * cos_ref[rows, :] + pltpu.roll(x, half, axis=1) * sin_ref[rows, :]

    st_scr[...] = jnp.zeros_like(st_scr)

    def fwd(c, carry):
        r0 = pl.multiple_of(c * c_len, c_len)
        rows = pl.ds(r0, c_len)
        q = rope(q_ref[0, rows, :].astype(F32), rows)
        k = rope(k_ref[0, rows, :].astype(F32), rows) * (RET_DK ** -0.5)
        qb = q.astype(BF16)
        qs_scr[rows, :] = qb
        ks_scr[rows, :] = k
        v = v_ref[0, rows, :]
        p = (_dot_nt(qb, k.astype(BF16)) * dmat).astype(BF16)
        acc_scr[rows, :] = _dot(p, v) + qdec_f * _dot(qb, st_scr[...].astype(BF16))
        st_scr[...] = cdec_f * st_scr[...] + _dot_tn((k * kdec_f).astype(BF16), v)
        return carry

    lax.fori_loop(0, n_chunks, fwd, 0)
    st_scr[...] = jnp.zeros_like(st_scr)

    def bwd(i, carry):
        c = n_chunks - 1 - i
        r0 = pl.multiple_of(c * c_len, c_len)
        rows = pl.ds(r0, c_len)
        qb = qs_scr[rows, :]
        k = ks_scr[rows, :]
        v = v_ref[0, rows, :]
        o = acc_scr[rows, :] + qdec_b * _dot(qb, st_scr[...].astype(BF16))
        gate = g_ref[0, rows, :].astype(F32)
        o_ref[0, rows, :] = (_head_norm(o, gn_ref[0]) * _silu(gate)).astype(o_ref.dtype)
        st_scr[...] = cdec_b * st_scr[...] + _dot_tn((k * kdec_b).astype(BF16), v)
        return carry

    lax.fori_loop(0, n_chunks, bwd, 0)


def _retention(proj, lg, cos_t, sin_t, ret_gn, nb, seq):
    kq, kv = RET_DK, RET_DV
    grid_spec = dict(
        grid=(nb, RET_HEADS),
        in_specs=[
            pl.BlockSpec(memory_space=pltpu.SMEM),
            pl.BlockSpec((1, seq, kq), lambda b, h: (b, 0, h)),
            pl.BlockSpec((1, seq, kq), lambda b, h: (b, 0, RET_QK // kq + h)),
            pl.BlockSpec((1, seq, kv), lambda b, h: (b, 0, 2 * RET_QK // kv + h)),
            pl.BlockSpec((1, seq, kv), lambda b, h: (b, 0, (2 * RET_QK + RET_V) // kv + h)),
            pl.BlockSpec((seq, kq), lambda b, h: (0, 0)),
            pl.BlockSpec((seq, kq), lambda b, h: (0, 0)),
            pl.BlockSpec((1, 1, kv), lambda b, h: (h, 0, 0)),
        ],
        out_specs=pl.BlockSpec((1, seq, kv), lambda b, h: (b, 0, h)),
        scratch_shapes=[
            pltpu.VMEM((seq, kq), BF16),
            pltpu.VMEM((seq, kq), F32),
            pltpu.VMEM((seq, kv), F32),
            pltpu.VMEM((kq, kv), F32),
        ],
    )
    return pl.pallas_call(
        functools.partial(_ret_body, seq=seq),
        out_shape=jax.ShapeDtypeStruct((nb, seq, RET_V), BF16),
        **grid_spec,
        compiler_params=_cparams(("parallel", "parallel")),
        name="retention",
    )(lg, proj, proj, proj, proj, cos_t, sin_t, ret_gn.reshape(RET_HEADS, 1, kv))


def _col_of(mat, c):
    lane = lax.broadcasted_iota(I32, mat.shape, 1)
    return jnp.sum(jnp.where(lane == c, mat, 0.0), axis=1, keepdims=True)


def _mlstm_body(gb_ref, q_ref, k_ref, v_ref, o_gate_ref, wq_ref, wk_ref, bq_ref, bk_ref,
                gr_ref, gn_ref, o_ref,
                pad_scr, qs_scr, ks_scr, acc_scr, c_scr, n_scr, m_scr, row_scr, col_scr, *, seq):
    ln = MLSTM_CHUNK
    n_chunks = seq // ln
    h = pl.program_id(1)

    def emit_q(r0, lane0, y):
        qs_scr[pl.ds(r0, y.shape[0]), lane0:lane0 + LANES] = (y * (MLSTM_DK ** -0.5)).astype(BF16)

    def emit_k(r0, lane0, y):
        ks_scr[pl.ds(r0, y.shape[0]), lane0:lane0 + LANES] = y

    _conv_silu_chunks(q_ref, wq_ref, bq_ref, pad_scr, emit_q, seq)
    _conv_silu_chunks(k_ref, wk_ref, bk_ref, pad_scr, emit_k, seq)

    tri_le = _tri(ln, "le")
    tri_ge = _tri(ln, "ge")
    for d in range(2):
        ig = gr_ref[0, 0, 2 * d] + gb_ref[(2 * d) * MLSTM_HEADS + h]
        lf = _log_sigmoid(gr_ref[0, 0, 2 * d + 1] + gb_ref[(2 * d + 1) * MLSTM_HEADS + h])
        bc = _dot01_right(lf, tri_le if d == 0 else tri_ge)
        row_scr[2 * d] = ig
        row_scr[2 * d + 1] = bc
        col_scr[2 * d] = ig.T
        col_scr[2 * d + 1] = bc.T

    ri = lax.broadcasted_iota(I32, (ln, ln), 0)
    ci = lax.broadcasted_iota(I32, (ln, ln), 1)

    def chunk_step(c, d):
        r0 = pl.multiple_of(c * ln, ln)
        rows = pl.ds(r0, ln)
        qb = qs_scr[rows, :]
        kf = ks_scr[rows, :]
        v = v_ref[0, rows, :]
        i_row = row_scr[2 * d, pl.ds(c, 1), :]
        b_row = row_scr[2 * d + 1, pl.ds(c, 1), :]
        i_col = _col_of(col_scr[2 * d], c)
        b_col = _col_of(col_scr[2 * d + 1], c)
        m_st = m_scr[...]
        mask = (ri >= ci) if d == 0 else (ri <= ci)
        logd = jnp.where(mask, b_col - b_row + i_row, NEG_INF)
        m_inter = b_col + m_st
        m_row = jnp.maximum(m_inter, jnp.max(logd, axis=1, keepdims=True))
        sc = _dot_nt(qb, kf.astype(BF16)) * jnp.exp(logd - m_row)
        inter = jnp.exp(m_inter - m_row)
        num = _dot(sc.astype(BF16), v) + inter * _dot(qb, c_scr[...].astype(BF16))
        den = jnp.sum(sc, axis=1, keepdims=True) + inter * jnp.sum(
            qb.astype(F32) * n_scr[...], axis=1, keepdims=True)
        hh = num / jnp.maximum(jnp.abs(den), jnp.exp(-m_row))
        b_end = b_row[:, ln - 1:ln] if d == 0 else b_row[:, 0:1]
        logw = b_end - b_col + i_col
        m_new = jnp.maximum(b_end + m_st, jnp.max(logw, axis=0, keepdims=True))
        kw = kf * jnp.exp(logw - m_new)
        dec = jnp.exp(b_end + m_st - m_new)
        c_scr[...] = dec * c_scr[...] + _dot_tn(kw.astype(BF16), v)
        n_scr[...] = dec * n_scr[...] + jnp.sum(kw, axis=0, keepdims=True)
        m_scr[...] = m_new
        return rows, hh

    def reset():
        c_scr[...] = jnp.zeros_like(c_scr)
        n_scr[...] = jnp.zeros_like(n_scr)
        m_scr[...] = jnp.zeros_like(m_scr)

    reset()

    def fwd(c, carry):
        rows, hh = chunk_step(c, 0)
        acc_scr[rows, :] = hh
        return carry

    lax.fori_loop(0, n_chunks, fwd, 0)
    reset()

    def bwd(i, carry):
        rows, hh = chunk_step(n_chunks - 1 - i, 1)
        y = _head_norm(acc_scr[rows, :] + hh, gn_ref[0])
        o_ref[0, rows, :] = (y * _sigmoid(o_gate_ref[0, rows, :].astype(F32))).astype(o_ref.dtype)
        return carry

    lax.fori_loop(0, n_chunks, bwd, 0)


def _mlstm(proj, gate_rows, gate_b, conv_w, conv_b, mlstm_gn, nb, seq):
    kq, kv = MLSTM_DK, MLSTM_DV
    q0 = 2 * RET_QK + 2 * RET_V
    k0 = q0 + MLSTM_QK
    v0 = k0 + MLSTM_QK
    o0 = v0 + MLSTM_V
    grid_spec = dict(
        grid=(nb, MLSTM_HEADS),
        in_specs=[
            pl.BlockSpec(memory_space=pltpu.SMEM),
            pl.BlockSpec((1, seq, kq), lambda b, h: (b, 0, q0 // kq + h)),
            pl.BlockSpec((1, seq, kq), lambda b, h: (b, 0, k0 // kq + h)),
            pl.BlockSpec((1, seq, kv), lambda b, h: (b, 0, v0 // kv + h)),
            pl.BlockSpec((1, seq, kv), lambda b, h: (b, 0, o0 // kv + h)),
            pl.BlockSpec((CONV_W, kq), lambda b, h: (0, h)),
            pl.BlockSpec((CONV_W, kq), lambda b, h: (0, MLSTM_QK // kq + h)),
            pl.BlockSpec((1, kq), lambda b, h: (0, h)),
            pl.BlockSpec((1, kq), lambda b, h: (0, MLSTM_QK // kq + h)),
            pl.BlockSpec((1, 1, 4, LANES, MLSTM_CHUNK), lambda b, h: (b, h, 0, 0, 0)),
            pl.BlockSpec((1, 1, kv), lambda b, h: (h, 0, 0)),
        ],
        out_specs=pl.BlockSpec((1, seq, kv), lambda b, h: (b, 0, h)),
        scratch_shapes=[
            pltpu.VMEM((seq + 2 * CONV_HALO, kq), F32),
            pltpu.VMEM((seq, kq), BF16),
            pltpu.VMEM((seq, kq), F32),
            pltpu.VMEM((seq, kv), F32),
            pltpu.VMEM((kq, kv), F32),
            pltpu.VMEM((1, kq), F32),
            pltpu.VMEM((1, 1), F32),
            pltpu.VMEM((4, LANES, MLSTM_CHUNK), F32),
            pltpu.VMEM((4, MLSTM_CHUNK, LANES), F32),
        ],
    )
    return pl.pallas_call(
        functools.partial(_mlstm_body, seq=seq),
        out_shape=jax.ShapeDtypeStruct((nb, seq, MLSTM_V), BF16),
        **grid_spec,
        compiler_params=_cparams(("parallel", "parallel")),
        name="mlstm",
    )(gate_b, proj, proj, proj, proj, conv_w, conv_w, conv_b.reshape(1, -1), conv_b.reshape(1, -1),
      gate_rows, mlstm_gn.reshape(MLSTM_HEADS, 1, kv))


def _ssd_body(z_ref, x_ref, b_ref, c_ref, wx_ref, wb_ref, wc_ref, bx_ref, bb_ref, bc_ref,
              dt_ref, bias_ref, alog_ref, dskip_ref, o_ref,
              padx_scr, padn_scr, xs_scr, bs_scr, cs_scr, y_scr, st_scr, dtc_scr, acr_scr, acc_scr,
              *, seq):
    ln = SSD_CHUNK
    n_chunks = seq // ln
    hp = SSD_HEADDIM
    width = SSD_HPG * hp

    def emit_x(r0, lane0, y):
        xs_scr[pl.ds(r0, y.shape[0]), lane0:lane0 + LANES] = y

    def emit_b(r0, lane0, y):
        bs_scr[pl.ds(r0, y.shape[0]), lane0:lane0 + LANES] = y.astype(BF16)

    def emit_c(r0, lane0, y):
        cs_scr[pl.ds(r0, y.shape[0]), lane0:lane0 + LANES] = y.astype(BF16)

    _conv_silu_chunks(x_ref, wx_ref, bx_ref, padx_scr, emit_x, seq)
    _conv_silu_chunks(b_ref, wb_ref, bb_ref, padn_scr, emit_b, seq)
    _conv_silu_chunks(c_ref, wc_ref, bc_ref, padn_scr, emit_c, seq)

    tri_le = _tri(ln, "le")
    tri_ge = _tri(ln, "ge")
    for d in range(2):
        dt = _softplus(dt_ref[0, 0, d] + bias_ref[0, d])
        adt = dt * (-jnp.exp(alog_ref[0, d]))
        acum = _dot01_right(adt, tri_le if d == 0 else tri_ge)
        acr_scr[d] = acum
        dtc_scr[d] = dt.T
        acc_scr[d] = acum.T

    er = lax.broadcasted_iota(I32, (LANES, width), 0)
    ec = lax.broadcasted_iota(I32, (LANES, width), 1)
    head_match = (er & (SSD_HPG - 1)) == _shr(ec, hp)
    ri = lax.broadcasted_iota(I32, (ln, ln), 0)
    ci = lax.broadcasted_iota(I32, (ln, ln), 1)
    lane_in_pair = lax.broadcasted_iota(I32, (ln, LANES), 1)

    def expand(mat_t, c):
        sel = jnp.where(head_match & (_shr(er, SSD_HPG) == c), 1.0, 0.0).astype(BF16)
        return _dot01_right(mat_t, sel)

    def state_update(d, c, x, bcm, dtx, acx):
        a_end = acx[ln - 1:ln, :] if d == 0 else acx[0:1, :]
        xw = (x * dtx * jnp.exp(a_end - acx)).astype(BF16)
        st_scr[d] = jnp.exp(a_end) * st_scr[d] + _dot_tn(bcm, xw)

    st_scr[...] = jnp.zeros_like(st_scr)

    def fwd(c, carry):
        r0 = pl.multiple_of(c * ln, ln)
        rows = pl.ds(r0, ln)
        x = xs_scr[rows, :]
        bcm = bs_scr[rows, :]
        ccm = cs_scr[rows, :]
        dtx = [expand(dtc_scr[d], c) for d in range(2)]
        acx = [expand(acc_scr[d], c) for d in range(2)]
        xdt = [x * dtx[d] for d in range(2)]
        cb = _dot_nt(ccm, bcm)
        arow = [acr_scr[d, pl.ds(pl.multiple_of(c * SSD_HPG, SSD_HPG), SSD_HPG), :] for d in range(2)]
        pieces = []
        for pair in range(SSD_HPG // 2):
            lo = pair * LANES
            acc = jnp.zeros((ln, LANES), F32)
            for sub in range(2):
                k = 2 * pair + sub
                in_head = _shr(lane_in_pair, hp) == sub
                for d in range(2):
                    mask = (ri >= ci) if d == 0 else (ri <= ci)
                    a_col = acx[d][:, k * hp:k * hp + 1]
                    dec = jnp.exp(jnp.where(mask, a_col - arow[d][k:k + 1, :], NEG_INF))
                    rhs = jnp.where(in_head, xdt[d][:, lo:lo + LANES], 0.0).astype(BF16)
                    acc = acc + _dot((cb * dec).astype(BF16), rhs)
            pieces.append(acc)
        y = jnp.concatenate(pieces, axis=1)
        y = y + jnp.exp(acx[0]) * _dot(ccm, st_scr[0].astype(BF16))
        y_scr[rows, :] = y
        state_update(0, c, x, bcm, dtx[0], acx[0])
        return carry

    lax.fori_loop(0, n_chunks, fwd, 0)

    def bwd(i, carry):
        c = n_chunks - 1 - i
        r0 = pl.multiple_of(c * ln, ln)
        rows = pl.ds(r0, ln)
        x = xs_scr[rows, :]
        bcm = bs_scr[rows, :]
        ccm = cs_scr[rows, :]
        dtx = expand(dtc_scr[1], c)
        acx = expand(acc_scr[1], c)
        y = y_scr[rows, :] + jnp.exp(acx) * _dot(ccm, st_scr[1].astype(BF16))
        y = y + dskip_ref[0] * x
        o_ref[0, rows, :] = (y * _silu(z_ref[0, rows, :].astype(F32))).astype(o_ref.dtype)
        state_update(1, c, x, bcm, dtx, acx)
        return carry

    lax.fori_loop(0, n_chunks, bwd, 0)


def _ssd(proj, dt_rows, bias_col, alog_col, dskip_x, conv_w, conv_b, nb, seq):
    width = SSD_HPG * SSD_HEADDIM
    ns = SSD_STATE
    x0 = SSD_INNER
    b0 = 2 * SSD_INNER
    c0 = b0 + SSD_BC
    cb = conv_b.reshape(1, -1)
    return pl.pallas_call(
        functools.partial(_ssd_body, seq=seq),
        out_shape=jax.ShapeDtypeStruct((nb, seq, SSD_INNER), BF16),
        grid=(nb, SSD_GROUPS),
        in_specs=[
            pl.BlockSpec((1, seq, width), lambda b, g: (b, 0, g)),
            pl.BlockSpec((1, seq, width), lambda b, g: (b, 0, x0 // width + g)),
            pl.BlockSpec((1, seq, ns), lambda b, g: (b, 0, b0 // ns + g)),
            pl.BlockSpec((1, seq, ns), lambda b, g: (b, 0, c0 // ns + g)),
            pl.BlockSpec((CONV_W, width), lambda b, g: (0, g)),
            pl.BlockSpec((CONV_W, ns), lambda b, g: (0, SSD_INNER // ns + g)),
            pl.BlockSpec((CONV_W, ns), lambda b, g: (0, (SSD_INNER + SSD_BC) // ns + g)),
            pl.BlockSpec((1, width), lambda b, g: (0, g)),
            pl.BlockSpec((1, ns), lambda b, g: (0, SSD_INNER // ns + g)),
            pl.BlockSpec((1, ns), lambda b, g: (0, (SSD_INNER + SSD_BC) // ns + g)),
            pl.BlockSpec((1, 1, 2, LANES, SSD_CHUNK), lambda b, g: (b, g, 0, 0, 0)),
            pl.BlockSpec((1, 2, LANES, 1), lambda b, g: (g, 0, 0, 0)),
            pl.BlockSpec((1, 2, LANES, 1), lambda b, g: (g, 0, 0, 0)),
            pl.BlockSpec((1, 1, width), lambda b, g: (g, 0, 0)),
        ],
        out_specs=pl.BlockSpec((1, seq, width), lambda b, g: (b, 0, g)),
        scratch_shapes=[
            pltpu.VMEM((seq + 2 * CONV_HALO, width), F32),
            pltpu.VMEM((seq + 2 * CONV_HALO, ns), F32),
            pltpu.VMEM((seq, width), F32),
            pltpu.VMEM((seq, ns), BF16),
            pltpu.VMEM((seq, ns), BF16),
            pltpu.VMEM((seq, width), F32),
            pltpu.VMEM((2, ns, width), F32),
            pltpu.VMEM((2, SSD_CHUNK, LANES), F32),
            pltpu.VMEM((2, LANES, SSD_CHUNK), F32),
            pltpu.VMEM((2, SSD_CHUNK, LANES), F32),
        ],
        compiler_params=_cparams(("parallel", "parallel")),
        name="ssd",
    )(proj, proj, proj, proj, conv_w, conv_w, conv_w, cb, cb, cb,
      dt_rows, bias_col, alog_col, dskip_x)


META_E = 0
META_G = 2
META_R = 4
ROUTE_E0 = MOE_GROUPS


def _router_body(x_ref, g_ref, sc_ref, sh_ref, w_ref, b_ref, h_ref, meta_ref, cnt_ref, carry_scr):
    tm = x_ref.shape[0]

    @pl.when(pl.program_id(0) == 0)
    def _():
        carry_scr[...] = jnp.zeros_like(carry_scr)

    y = _rms(x_ref[...], g_ref[...]) * (1.0 + sc_ref[0]) + sh_ref[0]
    h_ref[...] = y
    h_hi = y.astype(BF16)
    h_lo = (y - h_hi.astype(F32)).astype(BF16)
    w = w_ref[...]
    w_hi = w.astype(BF16)
    w_lo = (w - w_hi.astype(F32)).astype(BF16)
    logits = _dot(h_hi, w_hi) + _dot(h_lo, w_hi) + _dot(h_hi, w_lo) + b_ref[...]

    lane = lax.broadcasted_iota(I32, (tm, LANES), 1)
    lane_f = lane.astype(F32)
    big = float(LANES)
    is_grp = lane < MOE_GROUPS
    gl = jnp.where(is_grp, logits, NEG_INF)
    gmax = jnp.max(gl, axis=1, keepdims=True)
    gidx = jnp.min(jnp.where(gl == gmax, lane_f, big), axis=1, keepdims=True)
    gprob = 1.0 / jnp.sum(jnp.where(is_grp, jnp.exp(gl - gmax), 0.0), axis=1, keepdims=True)

    el = lane - ROUTE_E0
    el_f = el.astype(F32)
    valid = (el >= 0) & (el < MOE_EXPERTS)
    in_grp = valid & (_shr(el, MOE_EPG).astype(F32) == gidx)
    ev = jnp.where(in_grp, logits, NEG_INF)
    v1 = jnp.max(ev, axis=1, keepdims=True)
    i1 = jnp.min(jnp.where(ev == v1, el_f, big), axis=1, keepdims=True)
    ev2 = jnp.where(el_f == i1, NEG_INF, ev)
    v2 = jnp.max(ev2, axis=1, keepdims=True)
    i2 = jnp.min(jnp.where(ev2 == v2, el_f, big), axis=1, keepdims=True)
    p2 = jnp.exp(v2 - v1)
    s1 = 1.0 / (1.0 + p2)
    gate1 = s1 * gprob
    gate2 = p2 * s1 * gprob

    oh1 = jnp.where(el_f == i1, 1.0, 0.0)
    oh2 = jnp.where(el_f == i2, 1.0, 0.0)
    oh = oh1 + oh2
    before = _dot(_tri(tm, "gt"), oh.astype(BF16)) + carry_scr[...]
    rank1 = jnp.sum(oh1 * before, axis=1, keepdims=True)
    rank2 = jnp.sum(oh2 * before, axis=1, keepdims=True)
    carry_scr[...] = carry_scr[...] + jnp.sum(oh, axis=0, keepdims=True)
    cnt_ref[...] = jnp.broadcast_to(carry_scr[...], cnt_ref.shape)

    meta = jnp.zeros((tm, LANES), F32)
    for col, val in ((META_E, i1), (META_E + 1, i2), (META_G, gate1), (META_G + 1, gate2),
                     (META_R, rank1), (META_R + 1, rank2)):
        meta = jnp.where(lane == col, val, meta)
    meta_ref[...] = meta


def _router(x, g, sc, sh, w_route, b_route, seq, tm=512):
    t, k = x.shape
    tm = min(tm, seq)
    tps = seq // tm
    return pl.pallas_call(
        _router_body,
        out_shape=[jax.ShapeDtypeStruct((t, k), F32),
                   jax.ShapeDtypeStruct((t, LANES), F32),
                   jax.ShapeDtypeStruct((8, LANES), F32)],
        grid=(t // tm,),
        in_specs=[
            pl.BlockSpec((tm, k), lambda i: (i, 0)),
            pl.BlockSpec((1, k), lambda i: (0, 0)),
            pl.BlockSpec((1, 1, k), lambda i: (i // tps, 0, 0)),
            pl.BlockSpec((1, 1, k), lambda i: (i // tps, 0, 0)),
            pl.BlockSpec((k, LANES), lambda i: (0, 0)),
            pl.BlockSpec((1, LANES), lambda i: (0, 0)),
        ],
        out_specs=[pl.BlockSpec((tm, k), lambda i: (i, 0)),
                   pl.BlockSpec((tm, LANES), lambda i: (i, 0)),
                   pl.BlockSpec((8, LANES), lambda i: (0, 0))],
        scratch_shapes=[pltpu.VMEM((1, LANES), F32)],
        compiler_params=_cparams(("arbitrary",)),
        name="router",
    )(x, g.reshape(1, k), sc, sh, w_route, b_route)


def _row_copy(src_hbm, dst, sem, src_row, dst_row):
    return pltpu.make_async_copy(src_hbm.at[pl.ds(src_row, 1)], dst.at[pl.ds(dst_row, 1)], sem)


def _start_row_gather(idx_ref, base, n_rows, stride, src_hbm, dst, sem):
    def body(r, carry):
        _row_copy(src_hbm, dst, sem, idx_ref[base + r * stride], r).start()
        return carry
    lax.fori_loop(0, n_rows, body, 0, unroll=8)


def _wait_row_gather(src_hbm, dst, sem, n_rows):
    def body(r, carry):
        _row_copy(src_hbm, dst, sem, 0, r).wait()
        return carry
    lax.fori_loop(0, n_rows, body, 0, unroll=8)


def _expert_body(blk_exp_ref, row_tok_ref, n_used_ref, h_hbm, wgu_ref, wd_ref, y_ref, xb_scr, sem):
    del blk_exp_ref
    i = pl.program_id(0)
    n_steps = pl.num_programs(0)
    slot = i % 2

    def start(step, s):
        _start_row_gather(row_tok_ref, step * MOE_BLOCK, MOE_BLOCK, 1, h_hbm, xb_scr.at[s], sem.at[s])

    @pl.when(i == 0)
    def _():
        start(0, 0)

    @pl.when(i + 1 < n_steps)
    def _():
        start(i + 1, 1 - slot)

    _wait_row_gather(h_hbm, xb_scr.at[slot], sem.at[slot], MOE_BLOCK)

    @pl.when(i < n_used_ref[0])
    def _():
        xb = xb_scr[slot].astype(BF16)
        a = _dot(xb, wgu_ref[0])
        hid = (_silu(a[:, :EXPERT_FF]) * a[:, EXPERT_FF:]).astype(BF16)
        y_ref[...] = _dot(hid, wd_ref[0])

    @pl.when(i >= n_used_ref[0])
    def _():
        y_ref[...] = jnp.zeros_like(y_ref)


def _experts(h, w_gu, w_down, blk_exp, row_tok, n_used):
    t, d = h.shape
    n_rows = row_tok.shape[0]
    n_blocks = n_rows // MOE_BLOCK
    grid_spec = pltpu.PrefetchScalarGridSpec(
        num_scalar_prefetch=3,
        grid=(n_blocks,),
        in_specs=[
            pl.BlockSpec(memory_space=pl.ANY),
            pl.BlockSpec((1, d, 2 * EXPERT_FF), lambda i, be, rt, nu: (be[i], 0, 0)),
            pl.BlockSpec((1, EXPERT_FF, d), lambda i, be, rt, nu: (be[i], 0, 0)),
        ],
        out_specs=pl.BlockSpec((MOE_BLOCK, d), lambda i, be, rt, nu: (i, 0)),
        scratch_shapes=[pltpu.VMEM((2, MOE_BLOCK, d), F32), pltpu.SemaphoreType.DMA((2,))],
    )
    return pl.pallas_call(
        _expert_body,
        out_shape=jax.ShapeDtypeStruct((n_rows, d), F32),
        grid_spec=grid_spec,
        compiler_params=_cparams(("arbitrary",)),
        name="experts",
    )(blk_exp, row_tok, n_used, h, w_gu, w_down)


def _combine_body(dest_ref, y_hbm, x_ref, gate_ref, meta_ref, fn_ref, o_ref, ya_scr, sem, *, final):
    i = pl.program_id(0)
    n_steps = pl.num_programs(0)
    tm = x_ref.shape[0]
    slot = i % 2

    def start(step, s):
        for j in range(2):
            _start_row_gather(dest_ref, step * tm * 2 + j, tm, 2, y_hbm, ya_scr.at[s, j], sem.at[s, j])

    @pl.when(i == 0)
    def _():
        start(0, 0)

    @pl.when(i + 1 < n_steps)
    def _():
        start(i + 1, 1 - slot)

    for j in range(2):
        _wait_row_gather(y_hbm, ya_scr.at[slot, j], sem.at[slot, j], tm)
    meta = meta_ref[...]
    moe = (ya_scr[slot, 0] * meta[:, META_G:META_G + 1]
           + ya_scr[slot, 1] * meta[:, META_G + 1:META_G + 2])
    out = x_ref[...] + gate_ref[0] * moe
    if final:
        out = _rms(out, fn_ref[...])
    o_ref[...] = out


def _combine(y, x, gate, meta, dest, final_norm, seq, final, tm=256):
    t, d = x.shape
    tm = min(tm, seq)
    tps = seq // tm
    grid_spec = pltpu.PrefetchScalarGridSpec(
        num_scalar_prefetch=1,
        grid=(t // tm,),
        in_specs=[
            pl.BlockSpec(memory_space=pl.ANY),
            pl.BlockSpec((tm, d), lambda i, ds: (i, 0)),
            pl.BlockSpec((1, 1, d), lambda i, ds: (i // tps, 0, 0)),
            pl.BlockSpec((tm, LANES), lambda i, ds: (i, 0)),
            pl.BlockSpec((1, d), lambda i, ds: (0, 0)),
        ],
        out_specs=pl.BlockSpec((tm, d), lambda i, ds: (i, 0)),
        scratch_shapes=[pltpu.VMEM((2, 2, tm, d), F32), pltpu.SemaphoreType.DMA((2, 2))],
    )
    return pl.pallas_call(
        functools.partial(_combine_body, final=final),
        out_shape=jax.ShapeDtypeStruct((t, d), F32),
        grid_spec=grid_spec,
        compiler_params=_cparams(("arbitrary",)),
        name="moe_combine",
    )(dest, y, x, gate, meta, final_norm.reshape(1, d))


def _moe_layer(x, g, sc, sh, gate, grp_w, grp_b, exp_w, exp_b, w_gate, w_up, w_down,
               final_norm, seq, final):
    t, d = x.shape
    pad = LANES - MOE_GROUPS - MOE_EXPERTS
    w_route = jnp.concatenate([grp_w, exp_w, jnp.zeros((d, pad), F32)], axis=1)
    b_route = jnp.concatenate([grp_b, exp_b, jnp.zeros((pad,), F32)]).reshape(1, LANES)
    h, meta, cnt = _router(x, g, sc, sh, w_route, b_route, seq)

    expert = meta[:, META_E:META_E + 2].astype(I32)
    rank = meta[:, META_R:META_R + 2].astype(I32)
    counts = cnt[0, ROUTE_E0:ROUTE_E0 + MOE_EXPERTS].astype(I32)
    padded = (counts + MOE_BLOCK - 1) // MOE_BLOCK * MOE_BLOCK
    p_ends = jnp.cumsum(padded)
    p_starts = p_ends - padded
    dest = (p_starts[expert] + rank).reshape(-1)
    n_rows = t * 2 + MOE_EXPERTS * MOE_BLOCK
    n_blocks = n_rows // MOE_BLOCK
    token_id = jnp.repeat(jnp.arange(t, dtype=I32), 2)
    row_tok = jnp.zeros((n_rows,), I32).at[dest].set(token_id)
    blk_exp = jnp.minimum(
        jnp.searchsorted(p_ends, jnp.arange(n_blocks, dtype=I32) * MOE_BLOCK, side="right"),
        MOE_EXPERTS - 1).astype(I32)
    n_used = (p_ends[-1:] // MOE_BLOCK).astype(I32)

    w_gu = jnp.concatenate([w_gate, w_up], axis=-1).astype(BF16)
    y = _experts(h, w_gu, w_down.astype(BF16), blk_exp, row_tok, n_used)
    return _combine(y, x, gate, meta, dest, final_norm, seq, final)


def _rope_tables(seq):
    half = RET_DK // 2
    inv = ROPE_BASE ** (-jnp.arange(half, dtype=F32) / half)
    ang = jnp.arange(seq, dtype=F32)[:, None] * inv[None, :]
    cos, sin = jnp.cos(ang), jnp.sin(ang)
    return jnp.concatenate([cos, cos], axis=1), jnp.concatenate([-sin, sin], axis=1)


def _pad_rows(a, axis, n):
    pad = [(0, 0)] * a.ndim
    pad[axis] = (0, n - a.shape[axis])
    return jnp.pad(a, pad)


def kernel(x_prompt, x_sample, c_prompt, c_sample, ada_w, ada_b, norm1, norm2, ev_w_in, ev_gate_b, ev_conv_w, ev_conv_b, ev_ret_gn, ev_mlstm_gn, ev_w_out, od_w_in, od_conv_w, od_conv_b, od_dt_bias, od_a_log, od_d_skip, od_norm, od_w_out, moe_grp_w, moe_grp_b, moe_exp_w, moe_exp_b, moe_w_gate, moe_w_up, moe_w_down, final_norm):
    n_prompt = x_prompt.shape[0]
    seq, d = x_prompt.shape[1], x_prompt.shape[2]
    assert x_sample.shape[1] == seq and d == D_MODEL
    assert seq % RET_CHUNK == 0 and seq // MLSTM_CHUNK <= LANES // SSD_HPG
    x = jnp.concatenate([x_prompt, x_sample], axis=0)
    nb = x.shape[0]
    t = nb * seq
    x = x.reshape(t, d)
    depth = ada_w.shape[0]

    c_all = jnp.concatenate([c_prompt, c_sample], axis=0)
    c_pad = _pad_rows(c_all, 0, -(-nb // 8) * 8)
    mod = _modulation(c_pad, ada_w, ada_b)[:, :nb].reshape(depth, nb, N_MOD, 1, d)

    heads = jnp.arange(RET_HEADS, dtype=F32)
    lg = jnp.stack([jnp.log1p(-jnp.exp2(-RET_DECAY_FWD - heads)),
                    jnp.log1p(-jnp.exp2(-RET_DECAY_BWD - heads))])
    cos_t, sin_t = _rope_tables(seq)

    for i in range(depth):
        sh1, sc1, g1, sh2, sc2, g2 = (mod[i, :, m] for m in range(N_MOD))
        j = i // 2
        if i % 2 == 0:
            w_in = ev_w_in[j]
            w_side = _pad_rows(w_in[:, EVEN_MAIN:], 1, LANES)
            proj, gates = _fused_matmul(x, w_in[:, :EVEN_MAIN].astype(BF16), seq=seq, prologue="normmod",
                                        g=norm1[i], sc=sc1, sh=sh1, w_side=w_side, name="even_in_proj")
            n_chunks = seq // MLSTM_CHUNK
            gr = gates[:, :MLSTM_NGATE].reshape(nb, n_chunks, MLSTM_CHUNK, 4, MLSTM_HEADS)
            gr = _pad_rows(gr.transpose(0, 4, 3, 1, 2), 3, LANES)
            proj = proj.reshape(nb, seq, EVEN_MAIN)
            ret = _retention(proj, lg, cos_t, sin_t, ev_ret_gn[j], nb, seq)
            ml = _mlstm(proj, gr, ev_gate_b[j], ev_conv_w[j], ev_conv_b[j], ev_mlstm_gn[j], nb, seq)
            x = _fused_matmul(ret.reshape(t, RET_V), ev_w_out[j].astype(BF16), seq=seq,
                              x2=ml.reshape(t, MLSTM_V),
                              res=x, gate=g1, tn=512, name="even_out_proj")
        else:
            w_in = od_w_in[j]
            proj, dt_raw = _fused_matmul(x, w_in[:, :ODD_MAIN].astype(BF16), seq=seq, prologue="normmod",
                                         g=norm1[i], sc=sc1, sh=sh1, w_side=w_in[:, ODD_MAIN:],
                                         name="odd_in_proj")
            n_chunks = seq // SSD_CHUNK
            dtr = dt_raw.reshape(nb, n_chunks, SSD_CHUNK, 2, SSD_GROUPS, SSD_HPG)
            dtr = dtr.transpose(0, 4, 3, 1, 5, 2).reshape(nb, SSD_GROUPS, 2, n_chunks * SSD_HPG, SSD_CHUNK)
            dtr = _pad_rows(dtr, 3, LANES)

            def per_row(p):
                p = p.reshape(2, SSD_GROUPS, SSD_HPG).transpose(1, 0, 2)
                return jnp.tile(p, (1, 1, LANES // SSD_HPG))[..., None]

            def per_lane(p):
                p = p.reshape(*p.shape[:-1], SSD_GROUPS, SSD_HPG)
                p = jnp.moveaxis(p, -2, 0)
                return jnp.repeat(p, SSD_HEADDIM, axis=-1)[..., None, :]

            y = _ssd(proj.reshape(nb, seq, ODD_MAIN), dtr, per_row(od_dt_bias[j]), per_row(od_a_log[j]),
                     per_lane(od_d_skip[j]), od_conv_w[j], od_conv_b[j], nb, seq)
            x = _fused_matmul(y.reshape(t, SSD_INNER), od_w_out[j].astype(BF16), seq=seq, prologue="norm",
                              g=od_norm[j], res=x, gate=g1, tn=512, name="odd_out_proj")
        x = _moe_layer(x, norm2[i], sc2, sh2, g2, moe_grp_w[i], moe_grp_b[i], moe_exp_w[i], moe_exp_b[i],
                       moe_w_gate[i], moe_w_up[i], moe_w_down[i], final_norm, seq, final=(i == depth - 1))
    y = x.reshape(nb, seq, d)
    return (y[:n_prompt], y[n_prompt:])
```

```python
import functools
import math

import jax
import jax.numpy as jnp
import numpy as np
from jax import lax
from jax.experimental import pallas as pl
from jax.experimental.pallas import tpu as pltpu

F32 = jnp.float32
BF16 = jnp.bfloat16
I32 = jnp.int32

D_MODEL = 2048
N_MOD = 6
EPS = 1e-6
CONV_W = 5
CONV_HALO = 8

RET_HEADS = 8
RET_DV = D_MODEL // RET_HEADS
RET_DK = RET_DV // 2
RET_DECAY_FWD = 5.0
RET_DECAY_BWD = 5.5
ROPE_BASE = 10000.0
RET_CHUNK = 256
MLSTM_HEADS = 4
MLSTM_DV = D_MODEL // MLSTM_HEADS
MLSTM_DK = MLSTM_DV // 2
MLSTM_CHUNK = 128
SSD_INNER = 2 * D_MODEL
SSD_HEADDIM = 64
SSD_HEADS = SSD_INNER // SSD_HEADDIM
SSD_GROUPS = 8
SSD_HPG = SSD_HEADS // SSD_GROUPS
SSD_STATE = 128
SSD_CHUNK = 128
MOE_GROUPS = 4
MOE_EPG = 8
MOE_EXPERTS = MOE_GROUPS * MOE_EPG
EXPERT_FF = D_MODEL // 4
MOE_BLOCK = 128

RET_QK = RET_HEADS * RET_DK
RET_V = RET_HEADS * RET_DV
MLSTM_QK = MLSTM_HEADS * MLSTM_DK
MLSTM_V = MLSTM_HEADS * MLSTM_DV
MLSTM_NGATE = 4 * MLSTM_HEADS
EVEN_MAIN = 2 * RET_QK + 2 * RET_V + 2 * MLSTM_QK + 2 * MLSTM_V
EVEN_MIX = RET_V + MLSTM_V
SSD_BC = SSD_GROUPS * SSD_STATE
SSD_CONV_CH = SSD_INNER + 2 * SSD_BC
ODD_MAIN = SSD_INNER + SSD_CONV_CH

PROLOGUE_ROWS = 256
LANES = 128
VMEM_LIMIT = 56 * 1024 * 1024

NEG_INF = float("-inf")


def _cparams(sem, vmem=VMEM_LIMIT):
    return pltpu.CompilerParams(dimension_semantics=sem, vmem_limit_bytes=vmem)


def _split3(x):
    hi = x.astype(BF16)
    r = x - hi.astype(F32)
    mid = r.astype(BF16)
    lo = (r - mid.astype(F32)).astype(BF16)
    return hi, mid, lo


def _dot(a, b):
    return jnp.dot(a, b, preferred_element_type=F32)


def _dot_nt(a, b):
    return lax.dot_general(a, b, (((1,), (1,)), ((), ())), preferred_element_type=F32)


def _dot_tn(a, b):
    return lax.dot_general(a, b, (((0,), (0,)), ((), ())), preferred_element_type=F32)


def _dot01_left(m01, x):
    hi, mid, lo = _split3(x)
    return _dot(m01, hi) + _dot(m01, mid) + _dot(m01, lo)


def _dot01_right(x, m01):
    hi, mid, lo = _split3(x)
    return _dot(hi, m01) + _dot(mid, m01) + _dot(lo, m01)


def _tri(n, kind):
    r = lax.broadcasted_iota(I32, (n, n), 0)
    c = lax.broadcasted_iota(I32, (n, n), 1)
    m = {"le": r <= c, "ge": r >= c, "gt": r > c}[kind]
    return jnp.where(m, 1.0, 0.0).astype(BF16)


def _shr(x, pow2):
    return lax.shift_right_arithmetic(x, jnp.int32(int(math.log2(pow2))))


def _sigmoid(x):
    return 1.0 / (1.0 + jnp.exp(-x))


def _silu(x):
    return x * _sigmoid(x)


def _softplus(x):
    return jnp.maximum(x, 0.0) + jnp.log1p(jnp.exp(-jnp.abs(x)))


def _log_sigmoid(x):
    return jnp.minimum(x, 0.0) - jnp.log1p(jnp.exp(-jnp.abs(x)))


def _rms(x, g):
    ms = jnp.mean(x * x, axis=-1, keepdims=True)
    return x * lax.rsqrt(ms + EPS) * g


def _head_norm(y, g):
    mu = jnp.mean(y, axis=-1, keepdims=True)
    yc = y - mu
    var = jnp.mean(yc * yc, axis=-1, keepdims=True)
    return yc * lax.rsqrt(var + EPS) * g


def _mod_body(c_ref, w_ref, b_ref, o_ref):
    c = c_ref[...]
    o_ref[0] = _dot(_silu(c).astype(BF16), w_ref[0].astype(BF16)) + b_ref[0]


def _modulation(c_pad, ada_w, ada_b):
    depth, d, n = ada_w.shape
    m = c_pad.shape[0]
    tn = 1024
    return pl.pallas_call(
        _mod_body,
        out_shape=jax.ShapeDtypeStruct((depth, m, n), F32),
        grid=(depth, n // tn),
        in_specs=[
            pl.BlockSpec((m, d), lambda l, j: (0, 0)),
            pl.BlockSpec((1, d, tn), lambda l, j: (l, 0, j)),
            pl.BlockSpec((1, 1, tn), lambda l, j: (l, 0, j)),
        ],
        out_specs=pl.BlockSpec((1, m, tn), lambda l, j: (l, 0, j)),
        compiler_params=_cparams(("parallel", "parallel")),
        name="modulation",
    )(c_pad, ada_w, ada_b.reshape(depth, 1, n))


def _mm_body(*refs, prologue, epilogue, side, two_lhs):
    it = iter(refs)
    x_ref = next(it)
    x2_ref = next(it) if two_lhs else None
    g_ref = next(it) if prologue != "none" else None
    sc_ref = next(it) if prologue == "normmod" else None
    sh_ref = next(it) if prologue == "normmod" else None
    w_ref = next(it)
    ws_ref = next(it) if side else None
    res_ref = next(it) if epilogue == "residual" else None
    gate_ref = next(it) if epilogue == "residual" else None
    o_ref = next(it)
    os_ref = next(it) if side else None
    h_scr = next(it) if prologue != "none" else None

    if prologue != "none":
        @pl.when(pl.program_id(1) == 0)
        def _():
            rows_per = PROLOGUE_ROWS

            def chunk(i, carry):
                rows = pl.ds(pl.multiple_of(i * rows_per, rows_per), rows_per)
                y = _rms(x_ref[rows, :].astype(F32), g_ref[...])
                if prologue == "normmod":
                    y = y * (1.0 + sc_ref[0]) + sh_ref[0]
                hb = y.astype(BF16)
                h_scr[rows, :] = hb
                if side:
                    h_lo = (y - hb.astype(F32)).astype(BF16)
                    ws = ws_ref[...]
                    w_hi = ws.astype(BF16)
                    w_lo = (ws - w_hi.astype(F32)).astype(BF16)
                    os_ref[rows, :] = _dot(hb, w_hi) + _dot(h_lo, w_hi) + _dot(hb, w_lo)
                return carry

            lax.fori_loop(0, x_ref.shape[0] // rows_per, chunk, 0)
        lhs = h_scr[...]
    else:
        lhs = x_ref[...]
    if two_lhs:
        k1 = x_ref.shape[1]
        acc = _dot(lhs, w_ref[:k1, :]) + _dot(x2_ref[...], w_ref[k1:, :])
    else:
        acc = _dot(lhs, w_ref[...])
    if epilogue == "residual":
        o_ref[...] = res_ref[...] + gate_ref[0] * acc
    else:
        o_ref[...] = acc.astype(o_ref.dtype)


def _fused_matmul(x, w, *, seq, x2=None, prologue="none", g=None, sc=None, sh=None, w_side=None,
                  res=None, gate=None, out_dtype=BF16, tm=1024, tn=1024, name="proj"):
    t, k = x.shape
    n = w.shape[1]
    tm = min(tm, seq)
    tn = min(tn, n)
    assert t % tm == 0 and seq % tm == 0 and n % tn == 0
    tps = seq // tm
    epilogue = "residual" if res is not None else "plain"
    side = w_side is not None
    two_lhs = x2 is not None
    assert not (two_lhs and prologue != "none")
    in_specs = [pl.BlockSpec((tm, k), lambda i, j: (i, 0))]
    args = [x]
    if two_lhs:
        in_specs.append(pl.BlockSpec((tm, x2.shape[1]), lambda i, j: (i, 0)))
        args.append(x2)
        k = k + x2.shape[1]
    if prologue != "none":
        in_specs.append(pl.BlockSpec((1, k), lambda i, j: (0, 0)))
        args.append(g.reshape(1, k))
    if prologue == "normmod":
        in_specs += [pl.BlockSpec((1, 1, k), lambda i, j: (i // tps, 0, 0))] * 2
        args += [sc, sh]
    in_specs.append(pl.BlockSpec((k, tn), lambda i, j: (0, j)))
    args.append(w)
    if side:
        in_specs.append(pl.BlockSpec((k, LANES), lambda i, j: (0, 0)))
        args.append(w_side)
    if epilogue == "residual":
        in_specs += [pl.BlockSpec((tm, tn), lambda i, j: (i, j)),
                     pl.BlockSpec((1, 1, tn), lambda i, j: (i // tps, 0, j))]
        args += [res, gate]
        out_dtype = F32
    out_shape = [jax.ShapeDtypeStruct((t, n), out_dtype)]
    out_specs = [pl.BlockSpec((tm, tn), lambda i, j: (i, j))]
    if side:
        out_shape.append(jax.ShapeDtypeStruct((t, LANES), F32))
        out_specs.append(pl.BlockSpec((tm, LANES), lambda i, j: (i, 0)))
    scratch = [pltpu.VMEM((tm, k), BF16)] if prologue != "none" else []
    outs = pl.pallas_call(
        functools.partial(_mm_body, prologue=prologue, epilogue=epilogue, side=side, two_lhs=two_lhs),
        out_shape=out_shape,
        grid=(t // tm, n // tn),
        in_specs=in_specs,
        out_specs=out_specs,
        scratch_shapes=scratch,
        compiler_params=_cparams(("parallel", "arbitrary")),
        name=name,
    )(*args)
    return outs if side else outs[0]


def _conv_silu_chunks(src_ref, w_ref, b_ref, pad_scr, emit, seq, rows=128):
    ch = pad_scr.shape[1]
    halo = CONV_HALO
    zeros = jnp.zeros((halo, ch), F32)
    pad_scr[pl.ds(0, halo), :] = zeros
    pad_scr[pl.ds(seq + halo, halo), :] = zeros

    def fill(i, carry):
        r0 = pl.multiple_of(i * rows, rows)
        pad_scr[pl.ds(pl.multiple_of(r0 + halo, halo), rows), :] = src_ref[0, pl.ds(r0, rows), :].astype(F32)
        return carry

    lax.fori_loop(0, seq // rows, fill, 0)
    win = rows + 2 * halo
    half = (CONV_W - 1) // 2

    def body(i, carry):
        r0 = pl.multiple_of(i * rows, rows)
        for lane0 in range(0, ch, LANES):
            cols = slice(lane0, lane0 + LANES)
            window = pad_scr[pl.ds(r0, win), cols]
            acc = jnp.zeros((rows, LANES), F32) + b_ref[:, cols]
            for j in range(CONV_W):
                d = j - half
                shifted = window if d == 0 else pltpu.roll(window, (-d) % win, axis=0)
                acc = acc + w_ref[j:j + 1, cols] * shifted[halo:halo + rows, :]
            emit(r0, lane0, _silu(acc))
        return carry

    lax.fori_loop(0, seq // rows, body, 0)


def _ret_body(lg_ref, q_ref, k_ref, v_ref, g_ref, cos_ref, sin_ref, gn_ref, o_ref,
              qs_scr, ks_scr, acc_scr, st_scr, *, seq):
    c_len = RET_CHUNK
    n_chunks = seq // c_len
    h = pl.program_id(1)
    lgf = lg_ref[0, h]
    lgb = lg_ref[1, h]
    ri = lax.broadcasted_iota(I32, (c_len, c_len), 0)
    ci = lax.broadcasted_iota(I32, (c_len, c_len), 1)
    diff = (ri - ci).astype(F32)
    dmat = jnp.exp(jnp.where(diff >= 0, lgf * diff, -lgb * diff))
    pos = lax.broadcasted_iota(I32, (c_len, 1), 0).astype(F32)
    qdec_f = jnp.exp(lgf * (pos + 1.0))
    kdec_f = jnp.exp(lgf * (c_len - 1.0 - pos))
    cdec_f = jnp.exp(jnp.full((1, 1), c_len, F32) * lgf)
    qdec_b = jnp.exp(lgb * (c_len - pos))
    kdec_b = jnp.exp(lgb * pos)
    cdec_b = jnp.exp(jnp.full((1, 1), c_len, F32) * lgb)
    half = RET_DK // 2

    def rope(x, rows):
        return x * cos_ref[rows, :] + pltpu.roll(x, half, axis=1) * sin_ref[rows, :]

    def chunk_rows(c):
        return pl.ds(pl.multiple_of(c * c_len, c_len), c_len)

    def rope_pass(c, carry):
        rows = chunk_rows(c)
        qs_scr[rows, :] = rope(q_ref[0, rows, :].astype(F32), rows).astype(BF16)
        ks_scr[rows, :] = rope(k_ref[0, rows, :].astype(F32), rows) * (RET_DK ** -0.5)
        return carry

    lax.fori_loop(0, n_chunks, rope_pass, 0)
    st_scr[...] = jnp.zeros_like(st_scr)

    def step(i, carry):
        rows = chunk_rows(i)
        qb = qs_scr[rows, :]
        k = ks_scr[rows, :]
        v = v_ref[0, rows, :]
        p = (_dot_nt(qb, k.astype(BF16)) * dmat).astype(BF16)
        acc_scr[0, rows, :] = _dot(p, v) + qdec_f * _dot(qb, st_scr[0].astype(BF16))
        st_scr[0] = cdec_f * st_scr[0] + _dot_tn((k * kdec_f).astype(BF16), v)

        rows = chunk_rows(n_chunks - 1 - i)
        k = ks_scr[rows, :]
        acc_scr[1, rows, :] = qdec_b * _dot(qs_scr[rows, :], st_scr[1].astype(BF16))
        st_scr[1] = cdec_b * st_scr[1] + _dot_tn((k * kdec_b).astype(BF16), v_ref[0, rows, :])
        return carry

    lax.fori_loop(0, n_chunks, step, 0)

    def finish(c, carry):
        rows = chunk_rows(c)
        o = acc_scr[0, rows, :] + acc_scr[1, rows, :]
        gate = g_ref[0, rows, :].astype(F32)
        o_ref[0, rows, :] = (_head_norm(o, gn_ref[0]) * _silu(gate)).astype(o_ref.dtype)
        return carry

    lax.fori_loop(0, n_chunks, finish, 0)


def _retention(proj, lg, cos_t, sin_t, ret_gn, nb, seq):
    kq, kv = RET_DK, RET_DV
    grid_spec = dict(
        grid=(nb, RET_HEADS),
        in_specs=[
            pl.BlockSpec(memory_space=pltpu.SMEM),
            pl.BlockSpec((1, seq, kq), lambda b, h: (b, 0, h)),
            pl.BlockSpec((1, seq, kq), lambda b, h: (b, 0, RET_QK // kq + h)),
            pl.BlockSpec((1, seq, kv), lambda b, h: (b, 0, 2 * RET_QK // kv + h)),
            pl.BlockSpec((1, seq, kv), lambda b, h: (b, 0, (2 * RET_QK + RET_V) // kv + h)),
            pl.BlockSpec((seq, kq), lambda b, h: (0, 0)),
            pl.BlockSpec((seq, kq), lambda b, h: (0, 0)),
            pl.BlockSpec((1, 1, kv), lambda b, h: (h, 0, 0)),
        ],
        out_specs=pl.BlockSpec((1, seq, kv), lambda b, h: (b, 0, h)),
        scratch_shapes=[
            pltpu.VMEM((seq, kq), BF16),
            pltpu.VMEM((seq, kq), F32),
            pltpu.VMEM((2, seq, kv), F32),
            pltpu.VMEM((2, kq, kv), F32),
        ],
    )
    return pl.pallas_call(
        functools.partial(_ret_body, seq=seq),
        out_shape=jax.ShapeDtypeStruct((nb, seq, RET_V), BF16),
        **grid_spec,
        compiler_params=_cparams(("parallel", "parallel")),
        name="retention",
    )(lg, proj, proj, proj, proj, cos_t, sin_t, ret_gn.reshape(RET_HEADS, 1, kv))


def _col_of(mat, c):
    lane = lax.broadcasted_iota(I32, mat.shape, 1)
    return jnp.sum(jnp.where(lane == c, mat, 0.0), axis=1, keepdims=True)


def _mlstm_body(gb_ref, q_ref, k_ref, v_ref, o_gate_ref, wq_ref, wk_ref, bq_ref, bk_ref,
                gr_ref, gn_ref, o_ref,
                pad_scr, qs_scr, ks_scr, acc_scr, c_scr, n_scr, m_scr, row_scr, col_scr, *, seq):
    ln = MLSTM_CHUNK
    n_chunks = seq // ln
    h = pl.program_id(1)

    def emit_q(r0, lane0, y):
        qs_scr[pl.ds(r0, y.shape[0]), lane0:lane0 + LANES] = (y * (MLSTM_DK ** -0.5)).astype(BF16)

    def emit_k(r0, lane0, y):
        ks_scr[pl.ds(r0, y.shape[0]), lane0:lane0 + LANES] = y

    _conv_silu_chunks(q_ref, wq_ref, bq_ref, pad_scr, emit_q, seq)
    _conv_silu_chunks(k_ref, wk_ref, bk_ref, pad_scr, emit_k, seq)

    tri_le = _tri(ln, "le")
    tri_ge = _tri(ln, "ge")
    for d in range(2):
        ig = gr_ref[0, 0, 2 * d] + gb_ref[(2 * d) * MLSTM_HEADS + h]
        lf = _log_sigmoid(gr_ref[0, 0, 2 * d + 1] + gb_ref[(2 * d + 1) * MLSTM_HEADS + h])
        bc = _dot01_right(lf, tri_le if d == 0 else tri_ge)
        row_scr[2 * d] = ig
        row_scr[2 * d + 1] = bc
        col_scr[2 * d] = ig.T
        col_scr[2 * d + 1] = bc.T

    ri = lax.broadcasted_iota(I32, (ln, ln), 0)
    ci = lax.broadcasted_iota(I32, (ln, ln), 1)

    def chunk_step(c, d):
        r0 = pl.multiple_of(c * ln, ln)
        rows = pl.ds(r0, ln)
        qb = qs_scr[rows, :]
        kf = ks_scr[rows, :]
        v = v_ref[0, rows, :]
        i_row = row_scr[2 * d, pl.ds(c, 1), :]
        b_row = row_scr[2 * d + 1, pl.ds(c, 1), :]
        i_col = _col_of(col_scr[2 * d], c)
        b_col = _col_of(col_scr[2 * d + 1], c)
        m_st = m_scr[d]
        mask = (ri >= ci) if d == 0 else (ri <= ci)
        logd = jnp.where(mask, b_col - b_row + i_row, NEG_INF)
        m_inter = b_col + m_st
        m_row = jnp.maximum(m_inter, jnp.max(logd, axis=1, keepdims=True))
        sc = _dot_nt(qb, kf.astype(BF16)) * jnp.exp(logd - m_row)
        inter = jnp.exp(m_inter - m_row)
        num = _dot(sc.astype(BF16), v) + inter * _dot(qb, c_scr[d].astype(BF16))
        den = jnp.sum(sc, axis=1, keepdims=True) + inter * jnp.sum(
            qb.astype(F32) * n_scr[d], axis=1, keepdims=True)
        hh = num / jnp.maximum(jnp.abs(den), jnp.exp(-m_row))
        b_end = b_row[:, ln - 1:ln] if d == 0 else b_row[:, 0:1]
        logw = b_end - b_col + i_col
        m_new = jnp.maximum(b_end + m_st, jnp.max(logw, axis=0, keepdims=True))
        kw = kf * jnp.exp(logw - m_new)
        dec = jnp.exp(b_end + m_st - m_new)
        c_scr[d] = dec * c_scr[d] + _dot_tn(kw.astype(BF16), v)
        n_scr[d] = dec * n_scr[d] + jnp.sum(kw, axis=0, keepdims=True)
        m_scr[d] = m_new
        acc_scr[d, rows, :] = hh

    c_scr[...] = jnp.zeros_like(c_scr)
    n_scr[...] = jnp.zeros_like(n_scr)
    m_scr[...] = jnp.zeros_like(m_scr)

    def step(i, carry):
        chunk_step(i, 0)
        chunk_step(n_chunks - 1 - i, 1)
        return carry

    lax.fori_loop(0, n_chunks, step, 0)

    def finish(c, carry):
        rows = pl.ds(pl.multiple_of(c * ln, ln), ln)
        y = _head_norm(acc_scr[0, rows, :] + acc_scr[1, rows, :], gn_ref[0])
        o_ref[0, rows, :] = (y * _sigmoid(o_gate_ref[0, rows, :].astype(F32))).astype(o_ref.dtype)
        return carry

    lax.fori_loop(0, n_chunks, finish, 0)


def _mlstm(proj, gate_rows, gate_b, conv_w, conv_b, mlstm_gn, nb, seq):
    kq, kv = MLSTM_DK, MLSTM_DV
    q0 = 2 * RET_QK + 2 * RET_V
    k0 = q0 + MLSTM_QK
    v0 = k0 + MLSTM_QK
    o0 = v0 + MLSTM_V
    grid_spec = dict(
        grid=(nb, MLSTM_HEADS),
        in_specs=[
            pl.BlockSpec(memory_space=pltpu.SMEM),
            pl.BlockSpec((1, seq, kq), lambda b, h: (b, 0, q0 // kq + h)),
            pl.BlockSpec((1, seq, kq), lambda b, h: (b, 0, k0 // kq + h)),
            pl.BlockSpec((1, seq, kv), lambda b, h: (b, 0, v0 // kv + h)),
            pl.BlockSpec((1, seq, kv), lambda b, h: (b, 0, o0 // kv + h)),
            pl.BlockSpec((CONV_W, kq), lambda b, h: (0, h)),
            pl.BlockSpec((CONV_W, kq), lambda b, h: (0, MLSTM_QK // kq + h)),
            pl.BlockSpec((1, kq), lambda b, h: (0, h)),
            pl.BlockSpec((1, kq), lambda b, h: (0, MLSTM_QK // kq + h)),
            pl.BlockSpec((1, 1, 4, LANES, MLSTM_CHUNK), lambda b, h: (b, h, 0, 0, 0)),
            pl.BlockSpec((1, 1, kv), lambda b, h: (h, 0, 0)),
        ],
        out_specs=pl.BlockSpec((1, seq, kv), lambda b, h: (b, 0, h)),
        scratch_shapes=[
            pltpu.VMEM((seq + 2 * CONV_HALO, kq), F32),
            pltpu.VMEM((seq, kq), BF16),
            pltpu.VMEM((seq, kq), F32),
            pltpu.VMEM((2, seq, kv), F32),
            pltpu.VMEM((2, kq, kv), F32),
            pltpu.VMEM((2, 1, kq), F32),
            pltpu.VMEM((2, 1, 1), F32),
            pltpu.VMEM((4, LANES, MLSTM_CHUNK), F32),
            pltpu.VMEM((4, MLSTM_CHUNK, LANES), F32),
        ],
    )
    return pl.pallas_call(
        functools.partial(_mlstm_body, seq=seq),
        out_shape=jax.ShapeDtypeStruct((nb, seq, MLSTM_V), BF16),
        **grid_spec,
        compiler_params=_cparams(("parallel", "parallel")),
        name="mlstm",
    )(gate_b, proj, proj, proj, proj, conv_w, conv_w, conv_b.reshape(1, -1), conv_b.reshape(1, -1),
      gate_rows, mlstm_gn.reshape(MLSTM_HEADS, 1, kv))


def _ssd_body(z_ref, x_ref, b_ref, c_ref, wx_ref, wb_ref, wc_ref, bx_ref, bb_ref, bc_ref,
              dt_ref, bias_ref, alog_ref, dskip_ref, o_ref,
              padx_scr, padn_scr, xs_scr, bs_scr, bst_scr, cs_scr, y_scr, st_scr,
              acr_scr, dtr_scr, er_scr, ur_scr, act_scr, *, seq):
    ln = SSD_CHUNK
    n_chunks = seq // ln
    hp = SSD_HEADDIM
    n_pairs = SSD_HPG // 2

    def emit_x(r0, lane0, y):
        xs_scr[pl.ds(r0, ln), lane0:lane0 + LANES] = y

    def emit_b(r0, lane0, y):
        bs_scr[pl.ds(r0, ln), :] = y.astype(BF16)
        bst_scr[:, pl.ds(r0, ln)] = y.T.astype(BF16)

    def emit_c(r0, lane0, y):
        cs_scr[pl.ds(r0, ln), :] = y.astype(BF16)

    _conv_silu_chunks(x_ref, wx_ref, bx_ref, padx_scr, emit_x, seq, rows=ln)
    _conv_silu_chunks(b_ref, wb_ref, bb_ref, padn_scr, emit_b, seq, rows=ln)
    _conv_silu_chunks(c_ref, wc_ref, bc_ref, padn_scr, emit_c, seq, rows=ln)

    for d in range(2):
        dt = _softplus(dt_ref[0, 0, d] + bias_ref[0, d])
        adt = dt * (-jnp.exp(alog_ref[0, d]))
        acum = _dot01_right(adt, _tri(ln, "le" if d == 0 else "ge"))
        a_end = acum[:, ln - 1:ln] if d == 0 else acum[:, 0:1]
        acr_scr[d] = acum
        dtr_scr[d] = dt
        er_scr[d] = jnp.exp(acum)
        ur_scr[d] = dt * jnp.exp(a_end - acum)
        act_scr[d] = acum.T

    ri = lax.broadcasted_iota(I32, (ln, ln), 0)
    ci = lax.broadcasted_iota(I32, (ln, ln), 1)
    in_first = lax.broadcasted_iota(I32, (1, LANES), 1) < hp
    on_diag = ri == ci

    def chunk_rows(c):
        return pl.ds(pl.multiple_of(c * ln, ln), ln)

    def dir_step(d, c):
        rows = chunk_rows(c)
        xb = xs_scr[rows, :].astype(BF16)
        bcm = bs_scr[rows, :]
        bct = bst_scr[:, rows].astype(F32)
        ccm = cs_scr[rows, :]
        cb = _dot_nt(ccm, bcm)
        carried = _dot(ccm, st_scr[d].astype(BF16)).astype(BF16)
        head_rows = pl.ds(pl.multiple_of(c * SSD_HPG, SSD_HPG), SSD_HPG)
        arow = acr_scr[d, head_rows, :]
        dtrow = dtr_scr[d, head_rows, :]
        erow = er_scr[d, head_rows, :]
        urow = ur_scr[d, head_rows, :]
        acols = pltpu.roll(act_scr[d], (LANES - c * SSD_HPG) & (LANES - 1), axis=1)
        mask = (ri >= ci) if d == 0 else (ri <= ci)
        end = ln - 1 if d == 0 else 0
        pieces = []
        for pair in range(n_pairs):
            lanes = slice(pair * LANES, (pair + 1) * LANES)
            lhs, rhs, lhs_state, keep = [], [], [], []
            for sub in range(2):
                k = 2 * pair + sub
                sel = in_first if sub == 0 else ~in_first
                dec = jnp.exp(jnp.where(mask, acols[:, k:k + 1] - arow[k:k + 1, :], NEG_INF))
                lhs.append((cb * dec * dtrow[k:k + 1, :]).astype(BF16))
                rhs.append(jnp.where(sel, xb[:, lanes], jnp.zeros((ln, LANES), BF16)))
                lhs_state.append((bct * urow[k:k + 1, :]).astype(BF16))
                keep.append(jnp.exp(arow[k:k + 1, end:end + 1]))
            for sub in range(2):
                k = 2 * pair + sub
                sel = in_first if sub == 0 else ~in_first
                lhs.append(jnp.where(on_diag, erow[k:k + 1, :], 0.0).astype(BF16))
                rhs.append(jnp.where(sel, carried[:, lanes], jnp.zeros((ln, LANES), BF16)))
            pieces.append(_dot(jnp.concatenate(lhs, axis=1), jnp.concatenate(rhs, axis=0)))
            st_scr[d, :, lanes] = (jnp.where(in_first, keep[0], keep[1]) * st_scr[d, :, lanes]
                                   + _dot(jnp.concatenate(lhs_state, axis=1),
                                          jnp.concatenate(rhs[:2], axis=0)))
        y_scr[d, rows, :] = jnp.concatenate(pieces, axis=1)

    st_scr[...] = jnp.zeros_like(st_scr)

    def step(i, carry):
        dir_step(0, i)
        dir_step(1, n_chunks - 1 - i)
        return carry

    lax.fori_loop(0, n_chunks, step, 0)

    def finish(c, carry):
        rows = chunk_rows(c)
        y = y_scr[0, rows, :] + y_scr[1, rows, :] + dskip_ref[0] * xs_scr[rows, :]
        o_ref[0, rows, :] = (y * _silu(z_ref[0, rows, :].astype(F32))).astype(o_ref.dtype)
        return carry

    lax.fori_loop(0, n_chunks, finish, 0)


def _ssd(proj, dt_rows, bias_col, alog_col, dskip_x, conv_w, conv_b, nb, seq):
    width = SSD_HPG * SSD_HEADDIM
    ns = SSD_STATE
    x0 = SSD_INNER
    b0 = 2 * SSD_INNER
    c0 = b0 + SSD_BC
    cb = conv_b.reshape(1, -1)
    return pl.pallas_call(
        functools.partial(_ssd_body, seq=seq),
        out_shape=jax.ShapeDtypeStruct((nb, seq, SSD_INNER), BF16),
        grid=(nb, SSD_GROUPS),
        in_specs=[
            pl.BlockSpec((1, seq, width), lambda b, g: (b, 0, g)),
            pl.BlockSpec((1, seq, width), lambda b, g: (b, 0, x0 // width + g)),
            pl.BlockSpec((1, seq, ns), lambda b, g: (b, 0, b0 // ns + g)),
            pl.BlockSpec((1, seq, ns), lambda b, g: (b, 0, c0 // ns + g)),
            pl.BlockSpec((CONV_W, width), lambda b, g: (0, g)),
            pl.BlockSpec((CONV_W, ns), lambda b, g: (0, SSD_INNER // ns + g)),
            pl.BlockSpec((CONV_W, ns), lambda b, g: (0, (SSD_INNER + SSD_BC) // ns + g)),
            pl.BlockSpec((1, width), lambda b, g: (0, g)),
            pl.BlockSpec((1, ns), lambda b, g: (0, SSD_INNER // ns + g)),
            pl.BlockSpec((1, ns), lambda b, g: (0, (SSD_INNER + SSD_BC) // ns + g)),
            pl.BlockSpec((1, 1, 2, LANES, SSD_CHUNK), lambda b, g: (b, g, 0, 0, 0)),
            pl.BlockSpec((1, 2, LANES, 1), lambda b, g: (g, 0, 0, 0)),
            pl.BlockSpec((1, 2, LANES, 1), lambda b, g: (g, 0, 0, 0)),
            pl.BlockSpec((1, 1, width), lambda b, g: (g, 0, 0)),
        ],
        out_specs=pl.BlockSpec((1, seq, width), lambda b, g: (b, 0, g)),
        scratch_shapes=[
            pltpu.VMEM((seq + 2 * CONV_HALO, width), F32),
            pltpu.VMEM((seq + 2 * CONV_HALO, ns), F32),
            pltpu.VMEM((seq, width), F32),
            pltpu.VMEM((seq, ns), BF16),
            pltpu.VMEM((ns, seq), BF16),
            pltpu.VMEM((seq, ns), BF16),
            pltpu.VMEM((2, seq, width), F32),
            pltpu.VMEM((2, ns, width), F32),
            pltpu.VMEM((2, LANES, SSD_CHUNK), F32),
            pltpu.VMEM((2, LANES, SSD_CHUNK), F32),
            pltpu.VMEM((2, LANES, SSD_CHUNK), F32),
            pltpu.VMEM((2, LANES, SSD_CHUNK), F32),
            pltpu.VMEM((2, SSD_CHUNK, LANES), F32),
        ],
        compiler_params=_cparams(("parallel", "parallel")),
        name="ssd",
    )(proj, proj, proj, proj, conv_w, conv_w, conv_w, cb, cb, cb,
      dt_rows, bias_col, alog_col, dskip_x)


META_E = 0
META_G = 2
META_R = 4
ROUTE_E0 = MOE_GROUPS


def _router_body(x_ref, g_ref, sc_ref, sh_ref, w_ref, b_ref, h_ref, meta_ref, cnt_ref, carry_scr):
    tm = x_ref.shape[0]

    @pl.when(pl.program_id(0) == 0)
    def _():
        carry_scr[...] = jnp.zeros_like(carry_scr)

    y = _rms(x_ref[...], g_ref[...]) * (1.0 + sc_ref[0]) + sh_ref[0]
    h_ref[...] = y
    h_hi = y.astype(BF16)
    h_lo = (y - h_hi.astype(F32)).astype(BF16)
    w = w_ref[...]
    w_hi = w.astype(BF16)
    w_lo = (w - w_hi.astype(F32)).astype(BF16)
    logits = _dot(h_hi, w_hi) + _dot(h_lo, w_hi) + _dot(h_hi, w_lo) + b_ref[...]

    lane = lax.broadcasted_iota(I32, (tm, LANES), 1)
    lane_f = lane.astype(F32)
    big = float(LANES)
    is_grp = lane < MOE_GROUPS
    gl = jnp.where(is_grp, logits, NEG_INF)
    gmax = jnp.max(gl, axis=1, keepdims=True)
    gidx = jnp.min(jnp.where(gl == gmax, lane_f, big), axis=1, keepdims=True)
    gprob = 1.0 / jnp.sum(jnp.where(is_grp, jnp.exp(gl - gmax), 0.0), axis=1, keepdims=True)

    el = lane - ROUTE_E0
    el_f = el.astype(F32)
    valid = (el >= 0) & (el < MOE_EXPERTS)
    in_grp = valid & (_shr(el, MOE_EPG).astype(F32) == gidx)
    ev = jnp.where(in_grp, logits, NEG_INF)
    v1 = jnp.max(ev, axis=1, keepdims=True)
    i1 = jnp.min(jnp.where(ev == v1, el_f, big), axis=1, keepdims=True)
    ev2 = jnp.where(el_f == i1, NEG_INF, ev)
    v2 = jnp.max(ev2, axis=1, keepdims=True)
    i2 = jnp.min(jnp.where(ev2 == v2, el_f, big), axis=1, keepdims=True)
    p2 = jnp.exp(v2 - v1)
    s1 = 1.0 / (1.0 + p2)
    gate1 = s1 * gprob
    gate2 = p2 * s1 * gprob

    oh1 = jnp.where(el_f == i1, 1.0, 0.0)
    oh2 = jnp.where(el_f == i2, 1.0, 0.0)
    oh = oh1 + oh2
    before = _dot(_tri(tm, "gt"), oh.astype(BF16)) + carry_scr[...]
    rank1 = jnp.sum(oh1 * before, axis=1, keepdims=True)
    rank2 = jnp.sum(oh2 * before, axis=1, keepdims=True)
    carry_scr[...] = carry_scr[...] + jnp.sum(oh, axis=0, keepdims=True)
    cnt_ref[...] = jnp.broadcast_to(carry_scr[...], cnt_ref.shape)

    meta = jnp.zeros((tm, LANES), F32)
    for col, val in ((META_E, i1), (META_E + 1, i2), (META_G, gate1), (META_G + 1, gate2),
                     (META_R, rank1), (META_R + 1, rank2)):
        meta = jnp.where(lane == col, val, meta)
    meta_ref[...] = meta


def _router(x, g, sc, sh, w_route, b_route, seq, tm=512):
    t, k = x.shape
    tm = min(tm, seq)
    tps = seq // tm
    return pl.pallas_call(
        _router_body,
        out_shape=[jax.ShapeDtypeStruct((t, k), F32),
                   jax.ShapeDtypeStruct((t, LANES), F32),
                   jax.ShapeDtypeStruct((8, LANES), F32)],
        grid=(t // tm,),
        in_specs=[
            pl.BlockSpec((tm, k), lambda i: (i, 0)),
            pl.BlockSpec((1, k), lambda i: (0, 0)),
            pl.BlockSpec((1, 1, k), lambda i: (i // tps, 0, 0)),
            pl.BlockSpec((1, 1, k), lambda i: (i // tps, 0, 0)),
            pl.BlockSpec((k, LANES), lambda i: (0, 0)),
            pl.BlockSpec((1, LANES), lambda i: (0, 0)),
        ],
        out_specs=[pl.BlockSpec((tm, k), lambda i: (i, 0)),
                   pl.BlockSpec((tm, LANES), lambda i: (i, 0)),
                   pl.BlockSpec((8, LANES), lambda i: (0, 0))],
        scratch_shapes=[pltpu.VMEM((1, LANES), F32)],
        compiler_params=_cparams(("arbitrary",)),
        name="router",
    )(x, g.reshape(1, k), sc, sh, w_route, b_route)


def _row_copy(src_hbm, dst, sem, src_row, dst_row):
    return pltpu.make_async_copy(src_hbm.at[pl.ds(src_row, 1)], dst.at[pl.ds(dst_row, 1)], sem)


def _start_row_gather(idx_ref, base, n_rows, stride, src_hbm, dst, sem):
    def body(r, carry):
        _row_copy(src_hbm, dst, sem, idx_ref[base + r * stride], r).start()
        return carry
    lax.fori_loop(0, n_rows, body, 0, unroll=8)


def _wait_row_gather(src_hbm, dst, sem, n_rows):
    pltpu.make_async_copy(src_hbm.at[pl.ds(0, n_rows)], dst, sem).wait()


def _expert_body(blk_exp_ref, row_tok_ref, h_hbm, wgu0_ref, wd0_ref, wgu1_ref, wd1_ref, y_ref,
                 xb0_scr, xb1_scr, sem):
    del blk_exp_ref
    i = pl.program_id(0)
    n_steps = pl.num_programs(0)
    xb = (xb0_scr, xb1_scr)

    def start_loop(block, s):
        _start_row_gather(row_tok_ref, block * MOE_BLOCK, MOE_BLOCK, 1, h_hbm, xb[s], sem.at[s])

    def start_inline(block, s):
        base = block * MOE_BLOCK
        for r in range(MOE_BLOCK):
            _row_copy(h_hbm, xb[s], sem.at[s], row_tok_ref[base + r], r).start()

    def ffn(s, wgu_ref, wd_ref):
        a = _dot(xb[s][...].astype(BF16), wgu_ref[0])
        hid = (_silu(a[:, :EXPERT_FF]) * a[:, EXPERT_FF:]).astype(BF16)
        y_ref[pl.ds(s * MOE_BLOCK, MOE_BLOCK), :] = _dot(hid, wd_ref[0])

    @pl.when(i == 0)
    def _():
        start_loop(0, 0)

    _wait_row_gather(h_hbm, xb[0], sem.at[0], MOE_BLOCK)
    start_inline(2 * i + 1, 1)
    ffn(0, wgu0_ref, wd0_ref)
    _wait_row_gather(h_hbm, xb[1], sem.at[1], MOE_BLOCK)

    @pl.when(i + 1 < n_steps)
    def _():
        start_inline(2 * i + 2, 0)
        ffn(1, wgu1_ref, wd1_ref)

    @pl.when(i + 1 >= n_steps)
    def _():
        ffn(1, wgu1_ref, wd1_ref)


def _experts(h, w_gu, w_down, blk_exp, row_tok):
    t, d = h.shape
    n_rows = row_tok.shape[0]
    n_blocks = n_rows // MOE_BLOCK
    assert n_blocks % 2 == 0
    grid_spec = pltpu.PrefetchScalarGridSpec(
        num_scalar_prefetch=2,
        grid=(n_blocks // 2,),
        in_specs=[
            pl.BlockSpec(memory_space=pl.ANY),
            pl.BlockSpec((1, d, 2 * EXPERT_FF), lambda i, be, rt: (be[2 * i], 0, 0)),
            pl.BlockSpec((1, EXPERT_FF, d), lambda i, be, rt: (be[2 * i], 0, 0)),
            pl.BlockSpec((1, d, 2 * EXPERT_FF), lambda i, be, rt: (be[2 * i + 1], 0, 0)),
            pl.BlockSpec((1, EXPERT_FF, d), lambda i, be, rt: (be[2 * i + 1], 0, 0)),
        ],
        out_specs=pl.BlockSpec((2 * MOE_BLOCK, d), lambda i, be, rt: (i, 0)),
        scratch_shapes=[pltpu.VMEM((MOE_BLOCK, d), F32), pltpu.VMEM((MOE_BLOCK, d), F32),
                        pltpu.SemaphoreType.DMA((2,))],
    )
    return pl.pallas_call(
        _expert_body,
        out_shape=jax.ShapeDtypeStruct((n_rows, d), F32),
        grid_spec=grid_spec,
        compiler_params=_cparams(("arbitrary",)),
        name="experts",
    )(blk_exp, row_tok, h, w_gu, w_down, w_gu, w_down)


def _combine_body(dest_ref, y_hbm, x_ref, gate_ref, meta_ref, fn_ref, o_ref,
                  ya00_scr, ya01_scr, ya10_scr, ya11_scr, sem, *, final):
    i = pl.program_id(0)
    n_steps = pl.num_programs(0)
    half = x_ref.shape[0] // 2
    ya = ((ya00_scr, ya01_scr), (ya10_scr, ya11_scr))

    def start_loop(tok0, h):
        for j in range(2):
            _start_row_gather(dest_ref, tok0 * 2 + j, half, 2, y_hbm, ya[h][j], sem.at[h, j])

    def start_inline(tok0, h):
        for r in range(half):
            for j in range(2):
                _row_copy(y_hbm, ya[h][j], sem.at[h, j], dest_ref[(tok0 + r) * 2 + j], r).start()

    def wait(h):
        for j in range(2):
            _wait_row_gather(y_hbm, ya[h][j], sem.at[h, j], half)

    def combine(h):
        rows = pl.ds(h * half, half)
        meta = meta_ref[rows, :]
        moe = (ya[h][0][...] * meta[:, META_G:META_G + 1]
               + ya[h][1][...] * meta[:, META_G + 1:META_G + 2])
        out = x_ref[rows, :] + gate_ref[0] * moe
        if final:
            out = _rms(out, fn_ref[...])
        o_ref[rows, :] = out

    tok0 = i * (2 * half)

    @pl.when(i == 0)
    def _():
        start_loop(0, 0)

    wait(0)
    start_inline(tok0 + half, 1)
    combine(0)
    wait(1)

    @pl.when(i + 1 < n_steps)
    def _():
        start_inline(tok0 + 2 * half, 0)
        combine(1)

    @pl.when(i + 1 >= n_steps)
    def _():
        combine(1)


def _combine(y, x, gate, meta, dest, final_norm, seq, final, tm=256):
    t, d = x.shape
    tm = min(tm, seq)
    tps = seq // tm
    half = tm // 2
    grid_spec = pltpu.PrefetchScalarGridSpec(
        num_scalar_prefetch=1,
        grid=(t // tm,),
        in_specs=[
            pl.BlockSpec(memory_space=pl.ANY),
            pl.BlockSpec((tm, d), lambda i, ds: (i, 0)),
            pl.BlockSpec((1, 1, d), lambda i, ds: (i // tps, 0, 0)),
            pl.BlockSpec((tm, LANES), lambda i, ds: (i, 0)),
            pl.BlockSpec((1, d), lambda i, ds: (0, 0)),
        ],
        out_specs=pl.BlockSpec((tm, d), lambda i, ds: (i, 0)),
        scratch_shapes=[pltpu.VMEM((half, d), F32)] * 4 + [pltpu.SemaphoreType.DMA((2, 2))],
    )
    return pl.pallas_call(
        functools.partial(_combine_body, final=final),
        out_shape=jax.ShapeDtypeStruct((t, d), F32),
        grid_spec=grid_spec,
        compiler_params=_cparams(("arbitrary",)),
        name="moe_combine",
    )(dest, y, x, gate, meta, final_norm.reshape(1, d))


def _moe_layer(x, g, sc, sh, gate, grp_w, grp_b, exp_w, exp_b, w_gate, w_up, w_down,
               final_norm, seq, final):
    t, d = x.shape
    pad = LANES - MOE_GROUPS - MOE_EXPERTS
    w_route = jnp.concatenate([grp_w, exp_w, jnp.zeros((d, pad), F32)], axis=1)
    b_route = jnp.concatenate([grp_b, exp_b, jnp.zeros((pad,), F32)]).reshape(1, LANES)
    h, meta, cnt = _router(x, g, sc, sh, w_route, b_route, seq)

    expert = meta[:, META_E:META_E + 2].astype(I32)
    rank = meta[:, META_R:META_R + 2].astype(I32)
    counts = cnt[0, ROUTE_E0:ROUTE_E0 + MOE_EXPERTS].astype(I32)
    padded = (counts + MOE_BLOCK - 1) // MOE_BLOCK * MOE_BLOCK
    p_ends = jnp.cumsum(padded)
    p_starts = p_ends - padded
    dest = (p_starts[expert] + rank).reshape(-1)
    n_rows = t * 2 + MOE_EXPERTS * MOE_BLOCK
    n_blocks = n_rows // MOE_BLOCK
    token_id = jnp.repeat(jnp.arange(t, dtype=I32), 2)
    row_tok = jnp.zeros((n_rows,), I32).at[dest].set(token_id)
    blk_start = jnp.arange(n_blocks, dtype=I32) * MOE_BLOCK
    blk_exp = jnp.minimum(jnp.sum((p_ends[None, :] <= blk_start[:, None]).astype(I32), axis=1),
                          MOE_EXPERTS - 1)

    w_gu = jnp.concatenate([w_gate, w_up], axis=-1).astype(BF16)
    y = _experts(h, w_gu, w_down.astype(BF16), blk_exp, row_tok)
    return _combine(y, x, gate, meta, dest, final_norm, seq, final)


def _rope_tables(seq):
    half = RET_DK // 2
    inv = ROPE_BASE ** (-jnp.arange(half, dtype=F32) / half)
    ang = jnp.arange(seq, dtype=F32)[:, None] * inv[None, :]
    cos, sin = jnp.cos(ang), jnp.sin(ang)
    return jnp.concatenate([cos, cos], axis=1), jnp.concatenate([-sin, sin], axis=1)


def _pad_rows(a, axis, n):
    pad = [(0, 0)] * a.ndim
    pad[axis] = (0, n - a.shape[axis])
    return jnp.pad(a, pad)


def kernel(x_prompt, x_sample, c_prompt, c_sample, ada_w, ada_b, norm1, norm2, ev_w_in, ev_gate_b, ev_conv_w, ev_conv_b, ev_ret_gn, ev_mlstm_gn, ev_w_out, od_w_in, od_conv_w, od_conv_b, od_dt_bias, od_a_log, od_d_skip, od_norm, od_w_out, moe_grp_w, moe_grp_b, moe_exp_w, moe_exp_b, moe_w_gate, moe_w_up, moe_w_down, final_norm):
    n_prompt = x_prompt.shape[0]
    seq, d = x_prompt.shape[1], x_prompt.shape[2]
    assert x_sample.shape[1] == seq and d == D_MODEL
    assert seq % RET_CHUNK == 0 and seq // MLSTM_CHUNK <= LANES // SSD_HPG
    x = jnp.concatenate([x_prompt, x_sample], axis=0)
    nb = x.shape[0]
    t = nb * seq
    x = x.reshape(t, d)
    depth = ada_w.shape[0]

    c_all = jnp.concatenate([c_prompt, c_sample], axis=0)
    c_pad = _pad_rows(c_all, 0, -(-nb // 8) * 8)
    mod = _modulation(c_pad, ada_w, ada_b)[:, :nb].reshape(depth, nb, N_MOD, 1, d)

    heads = jnp.arange(RET_HEADS, dtype=F32)
    lg = jnp.stack([jnp.log1p(-jnp.exp2(-RET_DECAY_FWD - heads)),
                    jnp.log1p(-jnp.exp2(-RET_DECAY_BWD - heads))])
    cos_t, sin_t = _rope_tables(seq)

    for i in range(depth):
        sh1, sc1, g1, sh2, sc2, g2 = (mod[i, :, m] for m in range(N_MOD))
        j = i // 2
        if i % 2 == 0:
            w_in = ev_w_in[j]
            w_side = _pad_rows(w_in[:, EVEN_MAIN:], 1, LANES)
            proj, gates = _fused_matmul(x, w_in[:, :EVEN_MAIN].astype(BF16), seq=seq, prologue="normmod",
                                        g=norm1[i], sc=sc1, sh=sh1, w_side=w_side, name="even_in_proj")
            n_chunks = seq // MLSTM_CHUNK
            gr = gates[:, :MLSTM_NGATE].reshape(nb, n_chunks, MLSTM_CHUNK, 4, MLSTM_HEADS)
            gr = _pad_rows(gr.transpose(0, 4, 3, 1, 2), 3, LANES)
            proj = proj.reshape(nb, seq, EVEN_MAIN)
            ret = _retention(proj, lg, cos_t, sin_t, ev_ret_gn[j], nb, seq)
            ml = _mlstm(proj, gr, ev_gate_b[j], ev_conv_w[j], ev_conv_b[j], ev_mlstm_gn[j], nb, seq)
            x = _fused_matmul(ret.reshape(t, RET_V), ev_w_out[j].astype(BF16), seq=seq,
                              x2=ml.reshape(t, MLSTM_V),
                              res=x, gate=g1, tn=512, name="even_out_proj")
        else:
            w_in = od_w_in[j]
            proj, dt_raw = _fused_matmul(x, w_in[:, :ODD_MAIN].astype(BF16), seq=seq, prologue="normmod",
                                         g=norm1[i], sc=sc1, sh=sh1, w_side=w_in[:, ODD_MAIN:],
                                         name="odd_in_proj")
            n_chunks = seq // SSD_CHUNK
            dtr = dt_raw.reshape(nb, n_chunks, SSD_CHUNK, 2, SSD_GROUPS, SSD_HPG)
            dtr = dtr.transpose(0, 4, 3, 1, 5, 2).reshape(nb, SSD_GROUPS, 2, n_chunks * SSD_HPG, SSD_CHUNK)
            dtr = _pad_rows(dtr, 3, LANES)

            def per_row(p):
                p = p.reshape(2, SSD_GROUPS, SSD_HPG).transpose(1, 0, 2)
                return jnp.tile(p, (1, 1, LANES // SSD_HPG))[..., None]

            def per_lane(p):
                p = p.reshape(*p.shape[:-1], SSD_GROUPS, SSD_HPG)
                p = jnp.moveaxis(p, -2, 0)
                return jnp.repeat(p, SSD_HEADDIM, axis=-1)[..., None, :]

            y = _ssd(proj.reshape(nb, seq, ODD_MAIN), dtr, per_row(od_dt_bias[j]), per_row(od_a_log[j]),
                     per_lane(od_d_skip[j]), od_conv_w[j], od_conv_b[j], nb, seq)
            x = _fused_matmul(y.reshape(t, SSD_INNER), od_w_out[j].astype(BF16), seq=seq, prologue="norm",
                              g=od_norm[j], res=x, gate=g1, tn=512, name="odd_out_proj")
        x = _moe_layer(x, norm2[i], sc2, sh2, g2, moe_grp_w[i], moe_grp_b[i], moe_exp_w[i], moe_exp_b[i],
                       moe_w_gate[i], moe_w_up[i], moe_w_down[i], final_norm, seq, final=(i == depth - 1))
    y = x.reshape(nb, seq, d)
    return (y[:n_prompt], y[n_prompt:])
```

```python
import functools
import math

import jax
import jax.numpy as jnp
import numpy as np
from jax import lax
from jax.experimental import pallas as pl
from jax.experimental.pallas import tpu as pltpu

F32 = jnp.float32
BF16 = jnp.bfloat16
I32 = jnp.int32

D_MODEL = 2048
N_MOD = 6
EPS = 1e-6
CONV_W = 5
CONV_HALO = 8

RET_HEADS = 8
RET_DV = D_MODEL // RET_HEADS
RET_DK = RET_DV // 2
RET_DECAY_FWD = 5.0
RET_DECAY_BWD = 5.5
ROPE_BASE = 10000.0
RET_CHUNK = 256
MLSTM_HEADS = 4
MLSTM_DV = D_MODEL // MLSTM_HEADS
MLSTM_DK = MLSTM_DV // 2
MLSTM_CHUNK = 128
GATE_ROWS = 8
SSD_INNER = 2 * D_MODEL
SSD_HEADDIM = 64
SSD_HEADS = SSD_INNER // SSD_HEADDIM
SSD_GROUPS = 8
SSD_HPG = SSD_HEADS // SSD_GROUPS
SSD_STATE = 128
SSD_CHUNK = 128
MOE_GROUPS = 4
MOE_EPG = 8
MOE_EXPERTS = MOE_GROUPS * MOE_EPG
EXPERT_FF = D_MODEL // 4
MOE_BLOCK = 128

RET_QK = RET_HEADS * RET_DK
RET_V = RET_HEADS * RET_DV
MLSTM_QK = MLSTM_HEADS * MLSTM_DK
MLSTM_V = MLSTM_HEADS * MLSTM_DV
MLSTM_NGATE = 4 * MLSTM_HEADS
EVEN_MAIN = 2 * RET_QK + 2 * RET_V + 2 * MLSTM_QK + 2 * MLSTM_V
EVEN_MIX = RET_V + MLSTM_V
SSD_BC = SSD_GROUPS * SSD_STATE
SSD_CONV_CH = SSD_INNER + 2 * SSD_BC
ODD_MAIN = SSD_INNER + SSD_CONV_CH

PROLOGUE_ROWS = 256
CAST_ROWS = 256
GATHER_UNROLL = 32
LANES = 128
VMEM_LIMIT = 56 * 1024 * 1024

NEG_INF = float("-inf")


def _cparams(sem, vmem=VMEM_LIMIT):
    return pltpu.CompilerParams(dimension_semantics=sem, vmem_limit_bytes=vmem)


def _split3(x):
    hi = x.astype(BF16)
    r = x - hi.astype(F32)
    mid = r.astype(BF16)
    lo = (r - mid.astype(F32)).astype(BF16)
    return hi, mid, lo


def _dot(a, b):
    return jnp.dot(a, b, preferred_element_type=F32)


def _dot_nt(a, b):
    return lax.dot_general(a, b, (((1,), (1,)), ((), ())), preferred_element_type=F32)


def _dot_tn(a, b):
    return lax.dot_general(a, b, (((0,), (0,)), ((), ())), preferred_element_type=F32)


def _dot01_left(m01, x):
    hi, mid, lo = _split3(x)
    return _dot(m01, hi) + _dot(m01, mid) + _dot(m01, lo)


def _dot01_right(x, m01):
    hi, mid, lo = _split3(x)
    return _dot(hi, m01) + _dot(mid, m01) + _dot(lo, m01)


def _tri(n, kind):
    r = lax.broadcasted_iota(I32, (n, n), 0)
    c = lax.broadcasted_iota(I32, (n, n), 1)
    m = {"le": r <= c, "ge": r >= c, "gt": r > c}[kind]
    return jnp.where(m, 1.0, 0.0).astype(BF16)


def _shr(x, pow2):
    return lax.shift_right_arithmetic(x, jnp.int32(int(math.log2(pow2))))


def _sigmoid(x):
    return 1.0 / (1.0 + jnp.exp(-x))


def _silu(x):
    return x * _sigmoid(x)


def _softplus(x):
    return jnp.maximum(x, 0.0) + jnp.log1p(jnp.exp(-jnp.abs(x)))


def _log_sigmoid(x):
    return jnp.minimum(x, 0.0) - jnp.log1p(jnp.exp(-jnp.abs(x)))


def _rms(x, g):
    ms = jnp.mean(x * x, axis=-1, keepdims=True)
    return x * lax.rsqrt(ms + EPS) * g


def _head_norm(y, g):
    mu = jnp.mean(y, axis=-1, keepdims=True)
    yc = y - mu
    var = jnp.mean(yc * yc, axis=-1, keepdims=True)
    return yc * lax.rsqrt(var + EPS) * g


def _mod_body(c_ref, w_ref, b_ref, o_ref):
    c = c_ref[...]
    o_ref[0] = _dot(_silu(c).astype(BF16), w_ref[0].astype(BF16)) + b_ref[0]


def _modulation(c_pad, ada_w, ada_b):
    depth, d, n = ada_w.shape
    m = c_pad.shape[0]
    tn = 1024
    return pl.pallas_call(
        _mod_body,
        out_shape=jax.ShapeDtypeStruct((depth, m, n), F32),
        grid=(depth, n // tn),
        in_specs=[
            pl.BlockSpec((m, d), lambda l, j: (0, 0)),
            pl.BlockSpec((1, d, tn), lambda l, j: (l, 0, j)),
            pl.BlockSpec((1, 1, tn), lambda l, j: (l, 0, j)),
        ],
        out_specs=pl.BlockSpec((1, m, tn), lambda l, j: (l, 0, j)),
        compiler_params=_cparams(("parallel", "parallel")),
        name="modulation",
    )(c_pad, ada_w, ada_b.reshape(depth, 1, n))


def _mm_body(*refs, prologue, epilogue, side, two_lhs):
    it = iter(refs)
    x_ref = next(it)
    x2_ref = next(it) if two_lhs else None
    g_ref = next(it) if prologue != "none" else None
    sc_ref = next(it) if prologue == "normmod" else None
    sh_ref = next(it) if prologue == "normmod" else None
    w_ref = next(it)
    ws_ref = next(it) if side else None
    res_ref = next(it) if epilogue == "residual" else None
    gate_ref = next(it) if epilogue == "residual" else None
    o_ref = next(it)
    os_ref = next(it) if side else None
    h_scr = next(it) if prologue != "none" else None

    if prologue != "none":
        @pl.when(pl.program_id(1) == 0)
        def _():
            rows_per = PROLOGUE_ROWS

            def chunk(i, carry):
                rows = pl.ds(pl.multiple_of(i * rows_per, rows_per), rows_per)
                y = _rms(x_ref[rows, :].astype(F32), g_ref[...])
                if prologue == "normmod":
                    y = y * (1.0 + sc_ref[0]) + sh_ref[0]
                hb = y.astype(BF16)
                h_scr[rows, :] = hb
                if side:
                    h_lo = (y - hb.astype(F32)).astype(BF16)
                    ws = ws_ref[...]
                    w_hi = ws.astype(BF16)
                    w_lo = (ws - w_hi.astype(F32)).astype(BF16)
                    os_ref[0, :, rows] = (_dot(hb, w_hi) + _dot(h_lo, w_hi) + _dot(hb, w_lo)).T
                return carry

            lax.fori_loop(0, x_ref.shape[0] // rows_per, chunk, 0)
        lhs = h_scr[...]
    else:
        lhs = x_ref[...]
    if two_lhs:
        k1 = x_ref.shape[1]
        acc = _dot(lhs, w_ref[:k1, :]) + _dot(x2_ref[...], w_ref[k1:, :])
    else:
        acc = _dot(lhs, w_ref[...])
    if epilogue == "residual":
        o_ref[...] = res_ref[...] + gate_ref[0] * acc
    else:
        o_ref[...] = acc.astype(o_ref.dtype)


def _fused_matmul(x, w, *, seq, x2=None, prologue="none", g=None, sc=None, sh=None, w_side=None,
                  res=None, gate=None, out_dtype=BF16, tm=1024, tn=1024, name="proj"):
    t, k = x.shape
    n = w.shape[1]
    tm = min(tm, seq)
    tn = min(tn, n)
    assert t % tm == 0 and seq % tm == 0 and n % tn == 0
    tps = seq // tm
    epilogue = "residual" if res is not None else "plain"
    side = w_side is not None
    two_lhs = x2 is not None
    assert not (two_lhs and prologue != "none")
    in_specs = [pl.BlockSpec((tm, k), lambda i, j: (i, 0))]
    args = [x]
    if two_lhs:
        in_specs.append(pl.BlockSpec((tm, x2.shape[1]), lambda i, j: (i, 0)))
        args.append(x2)
        k = k + x2.shape[1]
    if prologue != "none":
        in_specs.append(pl.BlockSpec((1, k), lambda i, j: (0, 0)))
        args.append(g.reshape(1, k))
    if prologue == "normmod":
        in_specs += [pl.BlockSpec((1, 1, k), lambda i, j: (i // tps, 0, 0))] * 2
        args += [sc, sh]
    in_specs.append(pl.BlockSpec((k, tn), lambda i, j: (0, j)))
    args.append(w)
    if side:
        in_specs.append(pl.BlockSpec((k, LANES), lambda i, j: (0, 0)))
        args.append(w_side)
    if epilogue == "residual":
        in_specs += [pl.BlockSpec((tm, tn), lambda i, j: (i, j)),
                     pl.BlockSpec((1, 1, tn), lambda i, j: (i // tps, 0, j))]
        args += [res, gate]
        out_dtype = F32
    out_shape = [jax.ShapeDtypeStruct((t, n), out_dtype)]
    out_specs = [pl.BlockSpec((tm, tn), lambda i, j: (i, j))]
    if side:
        out_shape.append(jax.ShapeDtypeStruct((t // seq, LANES, seq), F32))
        out_specs.append(pl.BlockSpec((1, LANES, tm), lambda i, j: (i // tps, 0, i % tps)))
    scratch = [pltpu.VMEM((tm, k), BF16)] if prologue != "none" else []
    outs = pl.pallas_call(
        functools.partial(_mm_body, prologue=prologue, epilogue=epilogue, side=side, two_lhs=two_lhs),
        out_shape=out_shape,
        grid=(t // tm, n // tn),
        in_specs=in_specs,
        out_specs=out_specs,
        scratch_shapes=scratch,
        compiler_params=_cparams(("parallel", "arbitrary")),
        name=name,
    )(*args)
    return outs if side else outs[0]


def _conv_silu_chunks(src_ref, w_ref, b_ref, pad_scr, emit, seq, rows=128):
    ch = pad_scr.shape[1]
    halo = CONV_HALO
    zeros = jnp.zeros((halo, ch), F32)
    pad_scr[pl.ds(0, halo), :] = zeros
    pad_scr[pl.ds(seq + halo, halo), :] = zeros

    def fill(i, carry):
        r0 = pl.multiple_of(i * rows, rows)
        pad_scr[pl.ds(pl.multiple_of(r0 + halo, halo), rows), :] = src_ref[0, pl.ds(r0, rows), :].astype(F32)
        return carry

    lax.fori_loop(0, seq // rows, fill, 0)
    win = rows + 2 * halo
    half = (CONV_W - 1) // 2

    def body(i, carry):
        r0 = pl.multiple_of(i * rows, rows)
        for lane0 in range(0, ch, LANES):
            cols = slice(lane0, lane0 + LANES)
            window = pad_scr[pl.ds(r0, win), cols]
            acc = jnp.zeros((rows, LANES), F32) + b_ref[:, cols]
            for j in range(CONV_W):
                d = j - half
                shifted = window if d == 0 else pltpu.roll(window, (-d) % win, axis=0)
                acc = acc + w_ref[j:j + 1, cols] * shifted[halo:halo + rows, :]
            emit(r0, lane0, _silu(acc))
        return carry

    lax.fori_loop(0, seq // rows, body, 0)


def _ret_body(lg_ref, q_ref, k_ref, v_ref, g_ref, cos_ref, sin_ref, gn_ref, o_ref,
              qs_scr, ks_scr, acc_scr, st_scr, *, seq):
    c_len = RET_CHUNK
    n_chunks = seq // c_len
    h = pl.program_id(1)
    lgf = lg_ref[0, h]
    lgb = lg_ref[1, h]
    ri = lax.broadcasted_iota(I32, (c_len, c_len), 0)
    ci = lax.broadcasted_iota(I32, (c_len, c_len), 1)
    diff = (ri - ci).astype(F32)
    dmat = jnp.exp(jnp.where(diff >= 0, lgf * diff, -lgb * diff))
    pos = lax.broadcasted_iota(I32, (c_len, 1), 0).astype(F32)
    qdec_f = jnp.exp(lgf * (pos + 1.0))
    kdec_f = jnp.exp(lgf * (c_len - 1.0 - pos))
    cdec_f = jnp.exp(jnp.full((1, 1), c_len, F32) * lgf)
    qdec_b = jnp.exp(lgb * (c_len - pos))
    kdec_b = jnp.exp(lgb * pos)
    cdec_b = jnp.exp(jnp.full((1, 1), c_len, F32) * lgb)
    half = RET_DK // 2

    def rope(x, rows):
        return x * cos_ref[rows, :] + pltpu.roll(x, half, axis=1) * sin_ref[rows, :]

    def chunk_rows(c):
        return pl.ds(pl.multiple_of(c * c_len, c_len), c_len)

    def rope_pass(c, carry):
        rows = chunk_rows(c)
        qs_scr[rows, :] = rope(q_ref[0, rows, :].astype(F32), rows).astype(BF16)
        ks_scr[rows, :] = rope(k_ref[0, rows, :].astype(F32), rows) * (RET_DK ** -0.5)
        return carry

    lax.fori_loop(0, n_chunks, rope_pass, 0)
    st_scr[...] = jnp.zeros_like(st_scr)

    def step(i, carry):
        rows = chunk_rows(i)
        qb = qs_scr[rows, :]
        k = ks_scr[rows, :]
        v = v_ref[0, rows, :]
        p = (_dot_nt(qb, k.astype(BF16)) * dmat).astype(BF16)
        acc_scr[0, rows, :] = _dot(p, v) + qdec_f * _dot(qb, st_scr[0].astype(BF16))
        st_scr[0] = cdec_f * st_scr[0] + _dot_tn((k * kdec_f).astype(BF16), v)

        rows = chunk_rows(n_chunks - 1 - i)
        k = ks_scr[rows, :]
        acc_scr[1, rows, :] = qdec_b * _dot(qs_scr[rows, :], st_scr[1].astype(BF16))
        st_scr[1] = cdec_b * st_scr[1] + _dot_tn((k * kdec_b).astype(BF16), v_ref[0, rows, :])
        return carry

    lax.fori_loop(0, n_chunks, step, 0)

    def finish(c, carry):
        rows = chunk_rows(c)
        o = acc_scr[0, rows, :] + acc_scr[1, rows, :]
        gate = g_ref[0, rows, :].astype(F32)
        o_ref[0, rows, :] = (_head_norm(o, gn_ref[0]) * _silu(gate)).astype(o_ref.dtype)
        return carry

    lax.fori_loop(0, n_chunks, finish, 0)


def _retention(proj, lg, cos_t, sin_t, ret_gn, nb, seq):
    kq, kv = RET_DK, RET_DV
    grid_spec = dict(
        grid=(nb, RET_HEADS),
        in_specs=[
            pl.BlockSpec(memory_space=pltpu.SMEM),
            pl.BlockSpec((1, seq, kq), lambda b, h: (b, 0, h)),
            pl.BlockSpec((1, seq, kq), lambda b, h: (b, 0, RET_QK // kq + h)),
            pl.BlockSpec((1, seq, kv), lambda b, h: (b, 0, 2 * RET_QK // kv + h)),
            pl.BlockSpec((1, seq, kv), lambda b, h: (b, 0, (2 * RET_QK + RET_V) // kv + h)),
            pl.BlockSpec((seq, kq), lambda b, h: (0, 0)),
            pl.BlockSpec((seq, kq), lambda b, h: (0, 0)),
            pl.BlockSpec((1, 1, kv), lambda b, h: (h, 0, 0)),
        ],
        out_specs=pl.BlockSpec((1, seq, kv), lambda b, h: (b, 0, h)),
        scratch_shapes=[
            pltpu.VMEM((seq, kq), BF16),
            pltpu.VMEM((seq, kq), F32),
            pltpu.VMEM((2, seq, kv), F32),
            pltpu.VMEM((2, kq, kv), F32),
        ],
    )
    return pl.pallas_call(
        functools.partial(_ret_body, seq=seq),
        out_shape=jax.ShapeDtypeStruct((nb, seq, RET_V), BF16),
        **grid_spec,
        compiler_params=_cparams(("parallel", "parallel")),
        name="retention",
    )(lg, proj, proj, proj, proj, cos_t, sin_t, ret_gn.reshape(RET_HEADS, 1, kv))


def _col_of(mat, c):
    lane = lax.broadcasted_iota(I32, mat.shape, 1)
    return jnp.sum(jnp.where(lane == c, mat, 0.0), axis=1, keepdims=True)


def _mlstm_body(gb_ref, q_ref, k_ref, v_ref, o_gate_ref, wq_ref, wk_ref, bq_ref, bk_ref,
                gr_ref, gn_ref, o_ref,
                pad_scr, qs_scr, ks_scr, acc_scr, c_scr, n_scr, m_scr, row_scr, col_scr, *, seq):
    ln = MLSTM_CHUNK
    n_chunks = seq // ln
    h = pl.program_id(1)

    def emit_q(r0, lane0, y):
        qs_scr[pl.ds(r0, y.shape[0]), lane0:lane0 + LANES] = (y * (MLSTM_DK ** -0.5)).astype(BF16)

    def emit_k(r0, lane0, y):
        ks_scr[pl.ds(r0, y.shape[0]), lane0:lane0 + LANES] = y

    _conv_silu_chunks(q_ref, wq_ref, bq_ref, pad_scr, emit_q, seq)
    _conv_silu_chunks(k_ref, wk_ref, bk_ref, pad_scr, emit_k, seq)

    row_scr[...] = jnp.zeros_like(row_scr)
    for t in range(4):
        for c in range(n_chunks):
            row_scr[t, c:c + 1, :] = gr_ref[0, t:t + 1, c * ln:(c + 1) * ln]
    tri_le = _tri(ln, "le")
    tri_ge = _tri(ln, "ge")
    for d in range(2):
        ig = row_scr[2 * d] + gb_ref[(2 * d) * MLSTM_HEADS + h]
        lf = _log_sigmoid(row_scr[2 * d + 1] + gb_ref[(2 * d + 1) * MLSTM_HEADS + h])
        bc = _dot01_right(lf, tri_le if d == 0 else tri_ge)
        row_scr[2 * d] = ig
        row_scr[2 * d + 1] = bc
        col_scr[2 * d] = ig.T
        col_scr[2 * d + 1] = bc.T

    ri = lax.broadcasted_iota(I32, (ln, ln), 0)
    ci = lax.broadcasted_iota(I32, (ln, ln), 1)

    def chunk_step(c, d):
        r0 = pl.multiple_of(c * ln, ln)
        rows = pl.ds(r0, ln)
        qb = qs_scr[rows, :]
        kf = ks_scr[rows, :]
        v = v_ref[0, rows, :]
        i_row = row_scr[2 * d, pl.ds(c, 1), :]
        b_row = row_scr[2 * d + 1, pl.ds(c, 1), :]
        i_col = _col_of(col_scr[2 * d], c)
        b_col = _col_of(col_scr[2 * d + 1], c)
        m_st = m_scr[d]
        mask = (ri >= ci) if d == 0 else (ri <= ci)
        logd = jnp.where(mask, b_col - b_row + i_row, NEG_INF)
        m_inter = b_col + m_st
        m_row = jnp.maximum(m_inter, jnp.max(logd, axis=1, keepdims=True))
        sc = _dot_nt(qb, kf.astype(BF16)) * jnp.exp(logd - m_row)
        inter = jnp.exp(m_inter - m_row)
        num = _dot(sc.astype(BF16), v) + inter * _dot(qb, c_scr[d].astype(BF16))
        den = jnp.sum(sc, axis=1, keepdims=True) + inter * jnp.sum(
            qb.astype(F32) * n_scr[d], axis=1, keepdims=True)
        hh = num / jnp.maximum(jnp.abs(den), jnp.exp(-m_row))
        b_end = b_row[:, ln - 1:ln] if d == 0 else b_row[:, 0:1]
        logw = b_end - b_col + i_col
        m_new = jnp.maximum(b_end + m_st, jnp.max(logw, axis=0, keepdims=True))
        kw = kf * jnp.exp(logw - m_new)
        dec = jnp.exp(b_end + m_st - m_new)
        c_scr[d] = dec * c_scr[d] + _dot_tn(kw.astype(BF16), v)
        n_scr[d] = dec * n_scr[d] + jnp.sum(kw, axis=0, keepdims=True)
        m_scr[d] = m_new
        acc_scr[d, rows, :] = hh

    c_scr[...] = jnp.zeros_like(c_scr)
    n_scr[...] = jnp.zeros_like(n_scr)
    m_scr[...] = jnp.zeros_like(m_scr)

    def step(i, carry):
        chunk_step(i, 0)
        chunk_step(n_chunks - 1 - i, 1)
        return carry

    lax.fori_loop(0, n_chunks, step, 0)

    def finish(c, carry):
        rows = pl.ds(pl.multiple_of(c * ln, ln), ln)
        y = _head_norm(acc_scr[0, rows, :] + acc_scr[1, rows, :], gn_ref[0])
        o_ref[0, rows, :] = (y * _sigmoid(o_gate_ref[0, rows, :].astype(F32))).astype(o_ref.dtype)
        return carry

    lax.fori_loop(0, n_chunks, finish, 0)


def _mlstm(proj, gate_rows, gate_b, conv_w, conv_b, mlstm_gn, nb, seq):
    kq, kv = MLSTM_DK, MLSTM_DV
    q0 = 2 * RET_QK + 2 * RET_V
    k0 = q0 + MLSTM_QK
    v0 = k0 + MLSTM_QK
    o0 = v0 + MLSTM_V
    grid_spec = dict(
        grid=(nb, MLSTM_HEADS),
        in_specs=[
            pl.BlockSpec(memory_space=pltpu.SMEM),
            pl.BlockSpec((1, seq, kq), lambda b, h: (b, 0, q0 // kq + h)),
            pl.BlockSpec((1, seq, kq), lambda b, h: (b, 0, k0 // kq + h)),
            pl.BlockSpec((1, seq, kv), lambda b, h: (b, 0, v0 // kv + h)),
            pl.BlockSpec((1, seq, kv), lambda b, h: (b, 0, o0 // kv + h)),
            pl.BlockSpec((CONV_W, kq), lambda b, h: (0, h)),
            pl.BlockSpec((CONV_W, kq), lambda b, h: (0, MLSTM_QK // kq + h)),
            pl.BlockSpec((1, kq), lambda b, h: (0, h)),
            pl.BlockSpec((1, kq), lambda b, h: (0, MLSTM_QK // kq + h)),
            pl.BlockSpec((1, GATE_ROWS, seq), lambda b, h: (b, h, 0)),
            pl.BlockSpec((1, 1, kv), lambda b, h: (h, 0, 0)),
        ],
        out_specs=pl.BlockSpec((1, seq, kv), lambda b, h: (b, 0, h)),
        scratch_shapes=[
            pltpu.VMEM((seq + 2 * CONV_HALO, kq), F32),
            pltpu.VMEM((seq, kq), BF16),
            pltpu.VMEM((seq, kq), F32),
            pltpu.VMEM((2, seq, kv), F32),
            pltpu.VMEM((2, kq, kv), F32),
            pltpu.VMEM((2, 1, kq), F32),
            pltpu.VMEM((2, 1, 1), F32),
            pltpu.VMEM((4, LANES, MLSTM_CHUNK), F32),
            pltpu.VMEM((4, MLSTM_CHUNK, LANES), F32),
        ],
    )
    return pl.pallas_call(
        functools.partial(_mlstm_body, seq=seq),
        out_shape=jax.ShapeDtypeStruct((nb, seq, MLSTM_V), BF16),
        **grid_spec,
        compiler_params=_cparams(("parallel", "parallel")),
        name="mlstm",
    )(gate_b, proj, proj, proj, proj, conv_w, conv_w, conv_b.reshape(1, -1), conv_b.reshape(1, -1),
      gate_rows, mlstm_gn.reshape(MLSTM_HEADS, 1, kv))


def _ssd_body(z_ref, x_ref, b_ref, c_ref, wx_ref, wb_ref, wc_ref, bx_ref, bb_ref, bc_ref,
              dtf_ref, dtb_ref, bias_ref, alog_ref, dskip_ref, o_ref,
              padx_scr, padn_scr, xs_scr, bs_scr, bst_scr, cs_scr, y_scr, st_scr,
              acr_scr, dtr_scr, er_scr, ur_scr, act_scr, *, seq):
    ln = SSD_CHUNK
    n_chunks = seq // ln
    hp = SSD_HEADDIM
    n_pairs = SSD_HPG // 2

    def emit_x(r0, lane0, y):
        xs_scr[pl.ds(r0, ln), lane0:lane0 + LANES] = y

    def emit_b(r0, lane0, y):
        bs_scr[pl.ds(r0, ln), :] = y.astype(BF16)
        bst_scr[:, pl.ds(r0, ln)] = y.T.astype(BF16)

    def emit_c(r0, lane0, y):
        cs_scr[pl.ds(r0, ln), :] = y.astype(BF16)

    _conv_silu_chunks(x_ref, wx_ref, bx_ref, padx_scr, emit_x, seq, rows=ln)
    _conv_silu_chunks(b_ref, wb_ref, bb_ref, padn_scr, emit_b, seq, rows=ln)
    _conv_silu_chunks(c_ref, wc_ref, bc_ref, padn_scr, emit_c, seq, rows=ln)

    for d, dt_ref in enumerate((dtf_ref, dtb_ref)):
        acr_scr[d] = jnp.zeros((LANES, ln), F32)
        for c in range(n_chunks):
            acr_scr[d, c * SSD_HPG:(c + 1) * SSD_HPG, :] = dt_ref[0, :, c * ln:(c + 1) * ln]
        dt = _softplus(acr_scr[d] + bias_ref[0, d])
        adt = dt * (-jnp.exp(alog_ref[0, d]))
        acum = _dot01_right(adt, _tri(ln, "le" if d == 0 else "ge"))
        a_end = acum[:, ln - 1:ln] if d == 0 else acum[:, 0:1]
        acr_scr[d] = acum
        dtr_scr[d] = dt
        er_scr[d] = jnp.exp(acum)
        ur_scr[d] = dt * jnp.exp(a_end - acum)
        act_scr[d] = acum.T

    ri = lax.broadcasted_iota(I32, (ln, ln), 0)
    ci = lax.broadcasted_iota(I32, (ln, ln), 1)
    in_first = lax.broadcasted_iota(I32, (1, LANES), 1) < hp
    on_diag = ri == ci

    def chunk_rows(c):
        return pl.ds(pl.multiple_of(c * ln, ln), ln)

    def dir_step(d, c):
        rows = chunk_rows(c)
        xb = xs_scr[rows, :].astype(BF16)
        bcm = bs_scr[rows, :]
        bct = bst_scr[:, rows].astype(F32)
        ccm = cs_scr[rows, :]
        cb = _dot_nt(ccm, bcm)
        carried = _dot(ccm, st_scr[d].astype(BF16)).astype(BF16)
        head_rows = pl.ds(pl.multiple_of(c * SSD_HPG, SSD_HPG), SSD_HPG)
        arow = acr_scr[d, head_rows, :]
        dtrow = dtr_scr[d, head_rows, :]
        erow = er_scr[d, head_rows, :]
        urow = ur_scr[d, head_rows, :]
        acols = pltpu.roll(act_scr[d], (LANES - c * SSD_HPG) & (LANES - 1), axis=1)
        mask = (ri >= ci) if d == 0 else (ri <= ci)
        end = ln - 1 if d == 0 else 0
        pieces = []
        for pair in range(n_pairs):
            lanes = slice(pair * LANES, (pair + 1) * LANES)
            lhs, rhs, lhs_state, keep = [], [], [], []
            for sub in range(2):
                k = 2 * pair + sub
                sel = in_first if sub == 0 else ~in_first
                dec = jnp.exp(jnp.where(mask, acols[:, k:k + 1] - arow[k:k + 1, :], NEG_INF))
                lhs.append((cb * dec * dtrow[k:k + 1, :]).astype(BF16))
                rhs.append(jnp.where(sel, xb[:, lanes], jnp.zeros((ln, LANES), BF16)))
                lhs_state.append((bct * urow[k:k + 1, :]).astype(BF16))
                keep.append(jnp.exp(arow[k:k + 1, end:end + 1]))
            for sub in range(2):
                k = 2 * pair + sub
                sel = in_first if sub == 0 else ~in_first
                lhs.append(jnp.where(on_diag, erow[k:k + 1, :], 0.0).astype(BF16))
                rhs.append(jnp.where(sel, carried[:, lanes], jnp.zeros((ln, LANES), BF16)))
            pieces.append(_dot(jnp.concatenate(lhs, axis=1), jnp.concatenate(rhs, axis=0)))
            st_scr[d, :, lanes] = (jnp.where(in_first, keep[0], keep[1]) * st_scr[d, :, lanes]
                                   + _dot(jnp.concatenate(lhs_state, axis=1),
                                          jnp.concatenate(rhs[:2], axis=0)))
        y_scr[d, rows, :] = jnp.concatenate(pieces, axis=1)

    st_scr[...] = jnp.zeros_like(st_scr)

    def step(i, carry):
        dir_step(0, i)
        dir_step(1, n_chunks - 1 - i)
        return carry

    lax.fori_loop(0, n_chunks, step, 0)

    def finish(c, carry):
        rows = chunk_rows(c)
        y = y_scr[0, rows, :] + y_scr[1, rows, :] + dskip_ref[0] * xs_scr[rows, :]
        o_ref[0, rows, :] = (y * _silu(z_ref[0, rows, :].astype(F32))).astype(o_ref.dtype)
        return carry

    lax.fori_loop(0, n_chunks, finish, 0)


def _ssd(proj, dt_t, bias_col, alog_col, dskip_x, conv_w, conv_b, nb, seq):
    width = SSD_HPG * SSD_HEADDIM
    ns = SSD_STATE
    x0 = SSD_INNER
    b0 = 2 * SSD_INNER
    c0 = b0 + SSD_BC
    cb = conv_b.reshape(1, -1)
    return pl.pallas_call(
        functools.partial(_ssd_body, seq=seq),
        out_shape=jax.ShapeDtypeStruct((nb, seq, SSD_INNER), BF16),
        grid=(nb, SSD_GROUPS),
        in_specs=[
            pl.BlockSpec((1, seq, width), lambda b, g: (b, 0, g)),
            pl.BlockSpec((1, seq, width), lambda b, g: (b, 0, x0 // width + g)),
            pl.BlockSpec((1, seq, ns), lambda b, g: (b, 0, b0 // ns + g)),
            pl.BlockSpec((1, seq, ns), lambda b, g: (b, 0, c0 // ns + g)),
            pl.BlockSpec((CONV_W, width), lambda b, g: (0, g)),
            pl.BlockSpec((CONV_W, ns), lambda b, g: (0, SSD_INNER // ns + g)),
            pl.BlockSpec((CONV_W, ns), lambda b, g: (0, (SSD_INNER + SSD_BC) // ns + g)),
            pl.BlockSpec((1, width), lambda b, g: (0, g)),
            pl.BlockSpec((1, ns), lambda b, g: (0, SSD_INNER // ns + g)),
            pl.BlockSpec((1, ns), lambda b, g: (0, (SSD_INNER + SSD_BC) // ns + g)),
            pl.BlockSpec((1, SSD_HPG, seq), lambda b, g: (b, g, 0)),
            pl.BlockSpec((1, SSD_HPG, seq), lambda b, g: (b, SSD_GROUPS + g, 0)),
            pl.BlockSpec((1, 2, LANES, 1), lambda b, g: (g, 0, 0, 0)),
            pl.BlockSpec((1, 2, LANES, 1), lambda b, g: (g, 0, 0, 0)),
            pl.BlockSpec((1, 1, width), lambda b, g: (g, 0, 0)),
        ],
        out_specs=pl.BlockSpec((1, seq, width), lambda b, g: (b, 0, g)),
        scratch_shapes=[
            pltpu.VMEM((seq + 2 * CONV_HALO, width), F32),
            pltpu.VMEM((seq + 2 * CONV_HALO, ns), F32),
            pltpu.VMEM((seq, width), F32),
            pltpu.VMEM((seq, ns), BF16),
            pltpu.VMEM((ns, seq), BF16),
            pltpu.VMEM((seq, ns), BF16),
            pltpu.VMEM((2, seq, width), F32),
            pltpu.VMEM((2, ns, width), F32),
            pltpu.VMEM((2, LANES, SSD_CHUNK), F32),
            pltpu.VMEM((2, LANES, SSD_CHUNK), F32),
            pltpu.VMEM((2, LANES, SSD_CHUNK), F32),
            pltpu.VMEM((2, LANES, SSD_CHUNK), F32),
            pltpu.VMEM((2, SSD_CHUNK, LANES), F32),
        ],
        compiler_params=_cparams(("parallel", "parallel")),
        name="ssd",
    )(proj, proj, proj, proj, conv_w, conv_w, conv_w, cb, cb, cb,
      dt_t, dt_t, bias_col, alog_col, dskip_x)


META_E = 0
META_G = 2
META_R = 4
ROUTE_E0 = MOE_GROUPS


def _router_body(x_ref, g_ref, sc_ref, sh_ref, w_ref, b_ref, h_ref, meta_ref, cnt_ref, carry_scr):
    tm = x_ref.shape[0]

    @pl.when(pl.program_id(0) == 0)
    def _():
        carry_scr[...] = jnp.zeros_like(carry_scr)

    y = _rms(x_ref[...], g_ref[...]) * (1.0 + sc_ref[0]) + sh_ref[0]
    h_ref[...] = y
    h_hi = y.astype(BF16)
    h_lo = (y - h_hi.astype(F32)).astype(BF16)
    w = w_ref[...]
    w_hi = w.astype(BF16)
    w_lo = (w - w_hi.astype(F32)).astype(BF16)
    logits = _dot(h_hi, w_hi) + _dot(h_lo, w_hi) + _dot(h_hi, w_lo) + b_ref[...]

    lane = lax.broadcasted_iota(I32, (tm, LANES), 1)
    lane_f = lane.astype(F32)
    big = float(LANES)
    is_grp = lane < MOE_GROUPS
    gl = jnp.where(is_grp, logits, NEG_INF)
    gmax = jnp.max(gl, axis=1, keepdims=True)
    gidx = jnp.min(jnp.where(gl == gmax, lane_f, big), axis=1, keepdims=True)
    gprob = 1.0 / jnp.sum(jnp.where(is_grp, jnp.exp(gl - gmax), 0.0), axis=1, keepdims=True)

    el = lane - ROUTE_E0
    el_f = el.astype(F32)
    valid = (el >= 0) & (el < MOE_EXPERTS)
    in_grp = valid & (_shr(el, MOE_EPG).astype(F32) == gidx)
    ev = jnp.where(in_grp, logits, NEG_INF)
    v1 = jnp.max(ev, axis=1, keepdims=True)
    i1 = jnp.min(jnp.where(ev == v1, el_f, big), axis=1, keepdims=True)
    ev2 = jnp.where(el_f == i1, NEG_INF, ev)
    v2 = jnp.max(ev2, axis=1, keepdims=True)
    i2 = jnp.min(jnp.where(ev2 == v2, el_f, big), axis=1, keepdims=True)
    p2 = jnp.exp(v2 - v1)
    s1 = 1.0 / (1.0 + p2)
    gate1 = s1 * gprob
    gate2 = p2 * s1 * gprob

    oh1 = jnp.where(el_f == i1, 1.0, 0.0)
    oh2 = jnp.where(el_f == i2, 1.0, 0.0)
    oh = oh1 + oh2
    before = _dot(_tri(tm, "gt"), oh.astype(BF16)) + carry_scr[...]
    rank1 = jnp.sum(oh1 * before, axis=1, keepdims=True)
    rank2 = jnp.sum(oh2 * before, axis=1, keepdims=True)
    carry_scr[...] = carry_scr[...] + jnp.sum(oh, axis=0, keepdims=True)
    cnt_ref[...] = jnp.broadcast_to(carry_scr[...], cnt_ref.shape)

    meta = jnp.zeros((tm, LANES), F32)
    for col, val in ((META_E, i1), (META_E + 1, i2), (META_G, gate1), (META_G + 1, gate2),
                     (META_R, rank1), (META_R + 1, rank2)):
        meta = jnp.where(lane == col, val, meta)
    meta_ref[...] = meta


def _router(x, g, sc, sh, w_route, b_route, seq, tm=512):
    t, k = x.shape
    tm = min(tm, seq)
    tps = seq // tm
    return pl.pallas_call(
        _router_body,
        out_shape=[jax.ShapeDtypeStruct((t, k), F32),
                   jax.ShapeDtypeStruct((t, LANES), F32),
                   jax.ShapeDtypeStruct((8, LANES), F32)],
        grid=(t // tm,),
        in_specs=[
            pl.BlockSpec((tm, k), lambda i: (i, 0)),
            pl.BlockSpec((1, k), lambda i: (0, 0)),
            pl.BlockSpec((1, 1, k), lambda i: (i // tps, 0, 0)),
            pl.BlockSpec((1, 1, k), lambda i: (i // tps, 0, 0)),
            pl.BlockSpec((k, LANES), lambda i: (0, 0)),
            pl.BlockSpec((1, LANES), lambda i: (0, 0)),
        ],
        out_specs=[pl.BlockSpec((tm, k), lambda i: (i, 0)),
                   pl.BlockSpec((tm, LANES), lambda i: (i, 0)),
                   pl.BlockSpec((8, LANES), lambda i: (0, 0))],
        scratch_shapes=[pltpu.VMEM((1, LANES), F32)],
        compiler_params=_cparams(("arbitrary",)),
        name="router",
    )(x, g.reshape(1, k), sc, sh, w_route, b_route)


def _row_copy(src_hbm, dst, sem, src_row, dst_row):
    return pltpu.make_async_copy(src_hbm.at[pl.ds(src_row, 1)], dst.at[pl.ds(dst_row, 1)], sem)


def _start_row_gather(idx_ref, base, n_rows, stride, src_hbm, dst, sem):
    def body(r, carry):
        _row_copy(src_hbm, dst, sem, idx_ref[base + r * stride], r).start()
        return carry
    lax.fori_loop(0, n_rows, body, 0, unroll=GATHER_UNROLL)


def _wait_row_gather(src_hbm, dst, sem, n_rows):
    pltpu.make_async_copy(src_hbm.at[pl.ds(0, n_rows)], dst, sem).wait()


def _expert_body(blk_exp_ref, row_tok_ref, h_hbm, wg_ref, wu_ref, wd_ref, y_ref,
                 xb_scr, wgu_scr, wdn_scr, sem):
    i = pl.program_id(0)
    n_steps = pl.num_programs(0)
    slot = i % 2
    d = xb_scr.shape[2]

    def start(block, s):
        _start_row_gather(row_tok_ref, block * MOE_BLOCK, MOE_BLOCK, 1, h_hbm, xb_scr.at[s], sem.at[s])

    @pl.when(i == 0)
    def _():
        start(0, 0)

    _wait_row_gather(h_hbm, xb_scr.at[slot], sem.at[slot], MOE_BLOCK)

    @pl.when(i + 1 < n_steps)
    def _():
        start(i + 1, 1 - slot)

    @pl.when((i == 0) | (blk_exp_ref[i] != blk_exp_ref[jnp.maximum(i - 1, 0)]))
    def _():
        def cast_up(c, carry):
            rows = pl.ds(pl.multiple_of(c * CAST_ROWS, CAST_ROWS), CAST_ROWS)
            wgu_scr[rows, :EXPERT_FF] = wg_ref[0, rows, :].astype(BF16)
            wgu_scr[rows, EXPERT_FF:] = wu_ref[0, rows, :].astype(BF16)
            return carry

        def cast_down(c, carry):
            rows = pl.ds(pl.multiple_of(c * CAST_ROWS, CAST_ROWS), CAST_ROWS)
            wdn_scr[rows, :] = wd_ref[0, rows, :].astype(BF16)
            return carry

        lax.fori_loop(0, d // CAST_ROWS, cast_up, 0)
        lax.fori_loop(0, EXPERT_FF // CAST_ROWS, cast_down, 0)

    a = _dot(xb_scr[slot].astype(BF16), wgu_scr[...])
    hid = (_silu(a[:, :EXPERT_FF]) * a[:, EXPERT_FF:]).astype(BF16)
    y_ref[...] = _dot(hid, wdn_scr[...])


def _experts(h, w_gate, w_up, w_down, blk_exp, row_tok):
    t, d = h.shape
    n_rows = row_tok.shape[0]
    n_blocks = n_rows // MOE_BLOCK
    grid_spec = pltpu.PrefetchScalarGridSpec(
        num_scalar_prefetch=2,
        grid=(n_blocks,),
        in_specs=[
            pl.BlockSpec(memory_space=pl.ANY),
            pl.BlockSpec((1, d, EXPERT_FF), lambda i, be, rt: (be[i], 0, 0)),
            pl.BlockSpec((1, d, EXPERT_FF), lambda i, be, rt: (be[i], 0, 0)),
            pl.BlockSpec((1, EXPERT_FF, d), lambda i, be, rt: (be[i], 0, 0)),
        ],
        out_specs=pl.BlockSpec((MOE_BLOCK, d), lambda i, be, rt: (i, 0)),
        scratch_shapes=[pltpu.VMEM((2, MOE_BLOCK, d), F32),
                        pltpu.VMEM((d, 2 * EXPERT_FF), BF16),
                        pltpu.VMEM((EXPERT_FF, d), BF16),
                        pltpu.SemaphoreType.DMA((2,))],
    )
    return pl.pallas_call(
        _expert_body,
        out_shape=jax.ShapeDtypeStruct((n_rows, d), F32),
        grid_spec=grid_spec,
        compiler_params=_cparams(("arbitrary",)),
        name="experts",
    )(blk_exp, row_tok, h, w_gate, w_up, w_down)


def _combine_body(dest_ref, y_hbm, x_ref, gate_ref, meta_ref, fn_ref, o_ref, ya_scr, sem,
                  *, final, tile0):
    i = pl.program_id(0)
    n_steps = pl.num_programs(0)
    tm = x_ref.shape[0]
    slot = i % 2

    def start(step, s):
        for j in range(2):
            _start_row_gather(dest_ref, (tile0 + step) * tm * 2 + j, tm, 2, y_hbm,
                              ya_scr.at[s, j], sem.at[s, j])

    @pl.when(i == 0)
    def _():
        start(0, 0)

    for j in range(2):
        _wait_row_gather(y_hbm, ya_scr.at[slot, j], sem.at[slot, j], tm)

    @pl.when(i + 1 < n_steps)
    def _():
        start(i + 1, 1 - slot)

    meta = meta_ref[...]
    moe = (ya_scr[slot, 0] * meta[:, META_G:META_G + 1]
           + ya_scr[slot, 1] * meta[:, META_G + 1:META_G + 2])
    out = x_ref[...] + gate_ref[0] * moe
    if final:
        out = _rms(out, fn_ref[...])
    o_ref[...] = out


def _combine(y, x, gate, meta, dest, final_norm, seq, final, tok0=0, n_tok=None, tm=256):
    t, d = x.shape
    n_tok = t if n_tok is None else n_tok
    tm = min(tm, seq)
    tps = seq // tm
    assert tok0 % seq == 0 and n_tok % seq == 0
    tile0 = tok0 // tm
    grid_spec = pltpu.PrefetchScalarGridSpec(
        num_scalar_prefetch=1,
        grid=(n_tok // tm,),
        in_specs=[
            pl.BlockSpec(memory_space=pl.ANY),
            pl.BlockSpec((tm, d), lambda i, ds: (tile0 + i, 0)),
            pl.BlockSpec((1, 1, d), lambda i, ds: ((tile0 + i) // tps, 0, 0)),
            pl.BlockSpec((tm, LANES), lambda i, ds: (tile0 + i, 0)),
            pl.BlockSpec((1, d), lambda i, ds: (0, 0)),
        ],
        out_specs=pl.BlockSpec((tm, d), lambda i, ds: (i, 0)),
        scratch_shapes=[pltpu.VMEM((2, 2, tm, d), F32), pltpu.SemaphoreType.DMA((2, 2))],
    )
    return pl.pallas_call(
        functools.partial(_combine_body, final=final, tile0=tile0),
        out_shape=jax.ShapeDtypeStruct((n_tok, d), F32),
        grid_spec=grid_spec,
        compiler_params=_cparams(("arbitrary",)),
        name="moe_combine",
    )(dest, y, x, gate, meta, final_norm.reshape(1, d))


def _moe_layer(x, g, sc, sh, gate, grp_w, grp_b, exp_w, exp_b, w_gate, w_up, w_down,
               final_norm, seq, final, split=None):
    t, d = x.shape
    pad = LANES - MOE_GROUPS - MOE_EXPERTS
    w_route = jnp.concatenate([grp_w, exp_w, jnp.zeros((d, pad), F32)], axis=1)
    b_route = jnp.concatenate([grp_b, exp_b, jnp.zeros((pad,), F32)]).reshape(1, LANES)
    h, meta, cnt = _router(x, g, sc, sh, w_route, b_route, seq)

    expert = meta[:, META_E:META_E + 2].astype(I32)
    rank = meta[:, META_R:META_R + 2].astype(I32)
    counts = cnt[0, ROUTE_E0:ROUTE_E0 + MOE_EXPERTS].astype(I32)
    padded = (counts + MOE_BLOCK - 1) // MOE_BLOCK * MOE_BLOCK
    p_ends = jnp.cumsum(padded)
    p_starts = p_ends - padded
    dest = (p_starts[expert] + rank).reshape(-1)
    n_rows = t * 2 + MOE_EXPERTS * MOE_BLOCK
    n_blocks = n_rows // MOE_BLOCK
    token_id = jnp.repeat(jnp.arange(t, dtype=I32), 2)
    row_tok = jnp.zeros((n_rows,), I32).at[dest].set(token_id)
    blk_start = jnp.arange(n_blocks, dtype=I32) * MOE_BLOCK
    blk_exp = jnp.minimum(jnp.sum((p_ends[None, :] <= blk_start[:, None]).astype(I32), axis=1),
                          MOE_EXPERTS - 1)

    y = _experts(h, w_gate, w_up, w_down, blk_exp, row_tok)
    if split is None:
        return _combine(y, x, gate, meta, dest, final_norm, seq, final)
    return tuple(_combine(y, x, gate, meta, dest, final_norm, seq, final, tok0=a, n_tok=b - a)
                 for a, b in ((0, split), (split, t)))


def _rope_tables(seq):
    half = RET_DK // 2
    inv = ROPE_BASE ** (-jnp.arange(half, dtype=F32) / half)
    ang = jnp.arange(seq, dtype=F32)[:, None] * inv[None, :]
    cos, sin = jnp.cos(ang), jnp.sin(ang)
    return jnp.concatenate([cos, cos], axis=1), jnp.concatenate([-sin, sin], axis=1)


def _pad_rows(a, axis, n):
    pad = [(0, 0)] * a.ndim
    pad[axis] = (0, n - a.shape[axis])
    return jnp.pad(a, pad)


def kernel(x_prompt, x_sample, c_prompt, c_sample, ada_w, ada_b, norm1, norm2, ev_w_in, ev_gate_b, ev_conv_w, ev_conv_b, ev_ret_gn, ev_mlstm_gn, ev_w_out, od_w_in, od_conv_w, od_conv_b, od_dt_bias, od_a_log, od_d_skip, od_norm, od_w_out, moe_grp_w, moe_grp_b, moe_exp_w, moe_exp_b, moe_w_gate, moe_w_up, moe_w_down, final_norm):
    n_prompt = x_prompt.shape[0]
    seq, d = x_prompt.shape[1], x_prompt.shape[2]
    assert x_sample.shape[1] == seq and d == D_MODEL
    assert seq % RET_CHUNK == 0 and seq // MLSTM_CHUNK <= LANES // SSD_HPG
    x = jnp.concatenate([x_prompt, x_sample], axis=0)
    nb = x.shape[0]
    t = nb * seq
    x = x.reshape(t, d)
    depth = ada_w.shape[0]

    c_all = jnp.concatenate([c_prompt, c_sample], axis=0)
    c_pad = _pad_rows(c_all, 0, -(-nb // 8) * 8)
    mod = _modulation(c_pad, ada_w, ada_b)[:, :nb].reshape(depth, nb, N_MOD, 1, d)

    heads = jnp.arange(RET_HEADS, dtype=F32)
    lg = jnp.stack([jnp.log1p(-jnp.exp2(-RET_DECAY_FWD - heads)),
                    jnp.log1p(-jnp.exp2(-RET_DECAY_BWD - heads))])
    cos_t, sin_t = _rope_tables(seq)

    for i in range(depth):
        sh1, sc1, g1, sh2, sc2, g2 = (mod[i, :, m] for m in range(N_MOD))
        j = i // 2
        if i % 2 == 0:
            w_in = ev_w_in[j]
            w_side = w_in[:, EVEN_MAIN:].reshape(d, 4, MLSTM_HEADS).transpose(0, 2, 1)
            w_side = _pad_rows(_pad_rows(w_side, 2, GATE_ROWS).reshape(d, -1), 1, LANES)
            proj, gates = _fused_matmul(x, w_in[:, :EVEN_MAIN].astype(BF16), seq=seq, prologue="normmod",
                                        g=norm1[i], sc=sc1, sh=sh1, w_side=w_side, name="even_in_proj")
            proj = proj.reshape(nb, seq, EVEN_MAIN)
            ret = _retention(proj, lg, cos_t, sin_t, ev_ret_gn[j], nb, seq)
            ml = _mlstm(proj, gates, ev_gate_b[j], ev_conv_w[j], ev_conv_b[j], ev_mlstm_gn[j], nb, seq)
            x = _fused_matmul(ret.reshape(t, RET_V), ev_w_out[j].astype(BF16), seq=seq,
                              x2=ml.reshape(t, MLSTM_V),
                              res=x, gate=g1, tn=512, name="even_out_proj")
        else:
            w_in = od_w_in[j]
            proj, dt_raw = _fused_matmul(x, w_in[:, :ODD_MAIN].astype(BF16), seq=seq, prologue="normmod",
                                         g=norm1[i], sc=sc1, sh=sh1, w_side=w_in[:, ODD_MAIN:],
                                         name="odd_in_proj")
            def per_row(p):
                p = p.reshape(2, SSD_GROUPS, SSD_HPG).transpose(1, 0, 2)
                return jnp.tile(p, (1, 1, LANES // SSD_HPG))[..., None]

            def per_lane(p):
                p = p.reshape(*p.shape[:-1], SSD_GROUPS, SSD_HPG)
                p = jnp.moveaxis(p, -2, 0)
                return jnp.repeat(p, SSD_HEADDIM, axis=-1)[..., None, :]

            y = _ssd(proj.reshape(nb, seq, ODD_MAIN), dt_raw, per_row(od_dt_bias[j]), per_row(od_a_log[j]),
                     per_lane(od_d_skip[j]), od_conv_w[j], od_conv_b[j], nb, seq)
            x = _fused_matmul(y.reshape(t, SSD_INNER), od_w_out[j].astype(BF16), seq=seq, prologue="norm",
                              g=od_norm[j], res=x, gate=g1, tn=512, name="odd_out_proj")
        last = i == depth - 1
        x = _moe_layer(x, norm2[i], sc2, sh2, g2, moe_grp_w[i], moe_grp_b[i], moe_exp_w[i], moe_exp_b[i],
                       moe_w_gate[i], moe_w_up[i], moe_w_down[i], final_norm, seq, final=last,
                       split=n_prompt * seq if last else None)
    y_prompt, y_sample = x
    return (y_prompt.reshape(n_prompt, seq, d), y_sample.reshape(nb - n_prompt, seq, d))
```

```python
import functools
import math

import jax
import jax.numpy as jnp
import numpy as np
from jax import lax
from jax.experimental import pallas as pl
from jax.experimental.pallas import tpu as pltpu

F32 = jnp.float32
BF16 = jnp.bfloat16
I32 = jnp.int32

D_MODEL = 2048
N_MOD = 6
EPS = 1e-6
CONV_W = 5
CONV_HALO = 8

RET_HEADS = 8
RET_DV = D_MODEL // RET_HEADS
RET_DK = RET_DV // 2
RET_DECAY_FWD = 5.0
RET_DECAY_BWD = 5.5
ROPE_BASE = 10000.0
RET_CHUNK = 256
MLSTM_HEADS = 4
MLSTM_DV = D_MODEL // MLSTM_HEADS
MLSTM_DK = MLSTM_DV // 2
MLSTM_CHUNK = 128
GATE_ROWS = 8
SSD_INNER = 2 * D_MODEL
SSD_HEADDIM = 64
SSD_HEADS = SSD_INNER // SSD_HEADDIM
SSD_GROUPS = 8
SSD_HPG = SSD_HEADS // SSD_GROUPS
SSD_STATE = 128
SSD_CHUNK = 128
MOE_GROUPS = 4
MOE_EPG = 8
MOE_EXPERTS = MOE_GROUPS * MOE_EPG
EXPERT_FF = D_MODEL // 4
MOE_BLOCK = 128

RET_QK = RET_HEADS * RET_DK
RET_V = RET_HEADS * RET_DV
MLSTM_QK = MLSTM_HEADS * MLSTM_DK
MLSTM_V = MLSTM_HEADS * MLSTM_DV
MLSTM_NGATE = 4 * MLSTM_HEADS
EVEN_MAIN = 2 * RET_QK + 2 * RET_V + 2 * MLSTM_QK + 2 * MLSTM_V
EVEN_MIX = RET_V + MLSTM_V
SSD_BC = SSD_GROUPS * SSD_STATE
SSD_CONV_CH = SSD_INNER + 2 * SSD_BC
ODD_MAIN = SSD_INNER + SSD_CONV_CH

PROLOGUE_ROWS = 256
CAST_ROWS = 256
GATHER_UNROLL = 32
SCAN_UNROLL = 2
LANES = 128
VMEM_LIMIT = 56 * 1024 * 1024

NEG_INF = float("-inf")


def _cparams(sem, vmem=VMEM_LIMIT):
    return pltpu.CompilerParams(dimension_semantics=sem, vmem_limit_bytes=vmem)


def _split3(x):
    hi = x.astype(BF16)
    r = x - hi.astype(F32)
    mid = r.astype(BF16)
    lo = (r - mid.astype(F32)).astype(BF16)
    return hi, mid, lo


def _dot(a, b):
    return jnp.dot(a, b, preferred_element_type=F32)


def _dot_nt(a, b):
    return lax.dot_general(a, b, (((1,), (1,)), ((), ())), preferred_element_type=F32)


def _dot_tn(a, b):
    return lax.dot_general(a, b, (((0,), (0,)), ((), ())), preferred_element_type=F32)


def _dot01_left(m01, x):
    hi, mid, lo = _split3(x)
    return _dot(m01, hi) + _dot(m01, mid) + _dot(m01, lo)


def _dot01_right(x, m01):
    hi, mid, lo = _split3(x)
    return _dot(hi, m01) + _dot(mid, m01) + _dot(lo, m01)


def _tri(n, kind):
    r = lax.broadcasted_iota(I32, (n, n), 0)
    c = lax.broadcasted_iota(I32, (n, n), 1)
    m = {"le": r <= c, "ge": r >= c, "gt": r > c}[kind]
    return jnp.where(m, 1.0, 0.0).astype(BF16)


def _shr(x, pow2):
    return lax.shift_right_arithmetic(x, jnp.int32(int(math.log2(pow2))))


def _sigmoid(x):
    return 1.0 / (1.0 + jnp.exp(-x))


def _silu(x):
    return x * _sigmoid(x)


def _softplus(x):
    return jnp.maximum(x, 0.0) + jnp.log1p(jnp.exp(-jnp.abs(x)))


def _log_sigmoid(x):
    return jnp.minimum(x, 0.0) - jnp.log1p(jnp.exp(-jnp.abs(x)))


def _rms(x, g):
    ms = jnp.mean(x * x, axis=-1, keepdims=True)
    return x * lax.rsqrt(ms + EPS) * g


def _head_norm(y, g):
    mu = jnp.mean(y, axis=-1, keepdims=True)
    yc = y - mu
    var = jnp.mean(yc * yc, axis=-1, keepdims=True)
    return yc * lax.rsqrt(var + EPS) * g


def _mod_body(c_ref, w_ref, b_ref, o_ref):
    c = c_ref[...]
    o_ref[0] = _dot(_silu(c).astype(BF16), w_ref[0].astype(BF16)) + b_ref[0]


def _modulation(c_pad, ada_w, ada_b):
    depth, d, n = ada_w.shape
    m = c_pad.shape[0]
    tn = 1024
    return pl.pallas_call(
        _mod_body,
        out_shape=jax.ShapeDtypeStruct((depth, m, n), F32),
        grid=(depth, n // tn),
        in_specs=[
            pl.BlockSpec((m, d), lambda l, j: (0, 0)),
            pl.BlockSpec((1, d, tn), lambda l, j: (l, 0, j)),
            pl.BlockSpec((1, 1, tn), lambda l, j: (l, 0, j)),
        ],
        out_specs=pl.BlockSpec((1, m, tn), lambda l, j: (l, 0, j)),
        compiler_params=_cparams(("parallel", "parallel")),
        name="modulation",
    )(c_pad, ada_w, ada_b.reshape(depth, 1, n))


def _mm_body(*refs, prologue, epilogue, side, two_lhs):
    it = iter(refs)
    x_ref = next(it)
    x2_ref = next(it) if two_lhs else None
    g_ref = next(it) if prologue != "none" else None
    sc_ref = next(it) if prologue == "normmod" else None
    sh_ref = next(it) if prologue == "normmod" else None
    w_ref = next(it)
    ws_ref = next(it) if side else None
    res_ref = next(it) if epilogue == "residual" else None
    gate_ref = next(it) if epilogue == "residual" else None
    o_ref = next(it)
    os_ref = next(it) if side else None
    h_scr = next(it) if prologue != "none" else None

    if prologue != "none":
        @pl.when(pl.program_id(1) == 0)
        def _():
            rows_per = PROLOGUE_ROWS

            def chunk(i, carry):
                rows = pl.ds(pl.multiple_of(i * rows_per, rows_per), rows_per)
                y = _rms(x_ref[rows, :].astype(F32), g_ref[...])
                if prologue == "normmod":
                    y = y * (1.0 + sc_ref[0]) + sh_ref[0]
                hb = y.astype(BF16)
                h_scr[rows, :] = hb
                if side:
                    h_lo = (y - hb.astype(F32)).astype(BF16)
                    ws = ws_ref[...]
                    w_hi = ws.astype(BF16)
                    w_lo = (ws - w_hi.astype(F32)).astype(BF16)
                    os_ref[0, :, rows] = (_dot(hb, w_hi) + _dot(h_lo, w_hi) + _dot(hb, w_lo)).T
                return carry

            lax.fori_loop(0, x_ref.shape[0] // rows_per, chunk, 0)
        lhs = h_scr[...]
    else:
        lhs = x_ref[...]
    if two_lhs:
        k1 = x_ref.shape[1]
        acc = _dot(lhs, w_ref[:k1, :]) + _dot(x2_ref[...], w_ref[k1:, :])
    else:
        acc = _dot(lhs, w_ref[...])
    if epilogue == "residual":
        o_ref[...] = res_ref[...] + gate_ref[0] * acc
    else:
        o_ref[...] = acc.astype(o_ref.dtype)


def _fused_matmul(x, w, *, seq, x2=None, prologue="none", g=None, sc=None, sh=None, w_side=None,
                  res=None, gate=None, out_dtype=BF16, tm=1024, tn=1024, name="proj"):
    t, k = x.shape
    n = w.shape[1]
    tm = min(tm, seq)
    tn = min(tn, n)
    assert t % tm == 0 and seq % tm == 0 and n % tn == 0
    tps = seq // tm
    epilogue = "residual" if res is not None else "plain"
    side = w_side is not None
    two_lhs = x2 is not None
    assert not (two_lhs and prologue != "none")
    in_specs = [pl.BlockSpec((tm, k), lambda i, j: (i, 0))]
    args = [x]
    if two_lhs:
        in_specs.append(pl.BlockSpec((tm, x2.shape[1]), lambda i, j: (i, 0)))
        args.append(x2)
        k = k + x2.shape[1]
    if prologue != "none":
        in_specs.append(pl.BlockSpec((1, k), lambda i, j: (0, 0)))
        args.append(g.reshape(1, k))
    if prologue == "normmod":
        in_specs += [pl.BlockSpec((1, 1, k), lambda i, j: (i // tps, 0, 0))] * 2
        args += [sc, sh]
    in_specs.append(pl.BlockSpec((k, tn), lambda i, j: (0, j)))
    args.append(w)
    if side:
        in_specs.append(pl.BlockSpec((k, LANES), lambda i, j: (0, 0)))
        args.append(w_side)
    if epilogue == "residual":
        in_specs += [pl.BlockSpec((tm, tn), lambda i, j: (i, j)),
                     pl.BlockSpec((1, 1, tn), lambda i, j: (i // tps, 0, j))]
        args += [res, gate]
        out_dtype = F32
    out_shape = [jax.ShapeDtypeStruct((t, n), out_dtype)]
    out_specs = [pl.BlockSpec((tm, tn), lambda i, j: (i, j))]
    if side:
        out_shape.append(jax.ShapeDtypeStruct((t // seq, LANES, seq), F32))
        out_specs.append(pl.BlockSpec((1, LANES, tm), lambda i, j: (i // tps, 0, i % tps)))
    scratch = [pltpu.VMEM((tm, k), BF16)] if prologue != "none" else []
    outs = pl.pallas_call(
        functools.partial(_mm_body, prologue=prologue, epilogue=epilogue, side=side, two_lhs=two_lhs),
        out_shape=out_shape,
        grid=(t // tm, n // tn),
        in_specs=in_specs,
        out_specs=out_specs,
        scratch_shapes=scratch,
        compiler_params=_cparams(("parallel", "arbitrary")),
        name=name,
    )(*args)
    return outs if side else outs[0]


def _conv_silu_chunks(src_ref, w_ref, b_ref, pad_scr, emit, seq, rows=128):
    ch = pad_scr.shape[1]
    halo = CONV_HALO
    zeros = jnp.zeros((halo, ch), F32)
    pad_scr[pl.ds(0, halo), :] = zeros
    pad_scr[pl.ds(seq + halo, halo), :] = zeros

    def fill(i, carry):
        r0 = pl.multiple_of(i * rows, rows)
        pad_scr[pl.ds(pl.multiple_of(r0 + halo, halo), rows), :] = src_ref[0, pl.ds(r0, rows), :].astype(F32)
        return carry

    lax.fori_loop(0, seq // rows, fill, 0)
    win = rows + 2 * halo
    half = (CONV_W - 1) // 2

    def body(i, carry):
        r0 = pl.multiple_of(i * rows, rows)
        for lane0 in range(0, ch, LANES):
            cols = slice(lane0, lane0 + LANES)
            window = pad_scr[pl.ds(r0, win), cols]
            acc = jnp.zeros((rows, LANES), F32) + b_ref[:, cols]
            for j in range(CONV_W):
                d = j - half
                shifted = window if d == 0 else pltpu.roll(window, (-d) % win, axis=0)
                acc = acc + w_ref[j:j + 1, cols] * shifted[halo:halo + rows, :]
            emit(r0, lane0, _silu(acc))
        return carry

    lax.fori_loop(0, seq // rows, body, 0)


def _ret_body(lg_ref, q_ref, k_ref, v_ref, g_ref, cos_ref, sin_ref, gn_ref, o_ref,
              qs_scr, ks_scr, acc_scr, st_scr, *, seq):
    c_len = RET_CHUNK
    n_chunks = seq // c_len
    h = pl.program_id(1)
    lgf = lg_ref[0, h]
    lgb = lg_ref[1, h]
    ri = lax.broadcasted_iota(I32, (c_len, c_len), 0)
    ci = lax.broadcasted_iota(I32, (c_len, c_len), 1)
    diff = (ri - ci).astype(F32)
    dmat = jnp.exp(jnp.where(diff >= 0, lgf * diff, -lgb * diff))
    pos = lax.broadcasted_iota(I32, (c_len, 1), 0).astype(F32)
    qdec_f = jnp.exp(lgf * (pos + 1.0))
    kdec_f = jnp.exp(lgf * (c_len - 1.0 - pos))
    cdec_f = jnp.exp(jnp.full((1, 1), c_len, F32) * lgf)
    qdec_b = jnp.exp(lgb * (c_len - pos))
    kdec_b = jnp.exp(lgb * pos)
    cdec_b = jnp.exp(jnp.full((1, 1), c_len, F32) * lgb)
    half = RET_DK // 2

    def rope(x, rows):
        return x * cos_ref[rows, :] + pltpu.roll(x, half, axis=1) * sin_ref[rows, :]

    def chunk_rows(c):
        return pl.ds(pl.multiple_of(c * c_len, c_len), c_len)

    def rope_pass(c, carry):
        rows = chunk_rows(c)
        qs_scr[rows, :] = rope(q_ref[0, rows, :].astype(F32), rows).astype(BF16)
        ks_scr[rows, :] = rope(k_ref[0, rows, :].astype(F32), rows) * (RET_DK ** -0.5)
        return carry

    lax.fori_loop(0, n_chunks, rope_pass, 0)
    st_scr[...] = jnp.zeros_like(st_scr)

    def step(i, carry):
        rows = chunk_rows(i)
        qb = qs_scr[rows, :]
        k = ks_scr[rows, :]
        v = v_ref[0, rows, :]
        p = (_dot_nt(qb, k.astype(BF16)) * dmat).astype(BF16)
        acc_scr[0, rows, :] = _dot(p, v) + qdec_f * _dot(qb, st_scr[0].astype(BF16))
        st_scr[0] = cdec_f * st_scr[0] + _dot_tn((k * kdec_f).astype(BF16), v)

        rows = chunk_rows(n_chunks - 1 - i)
        k = ks_scr[rows, :]
        acc_scr[1, rows, :] = qdec_b * _dot(qs_scr[rows, :], st_scr[1].astype(BF16))
        st_scr[1] = cdec_b * st_scr[1] + _dot_tn((k * kdec_b).astype(BF16), v_ref[0, rows, :])
        return carry

    lax.fori_loop(0, n_chunks, step, 0, unroll=SCAN_UNROLL)

    def finish(c, carry):
        rows = chunk_rows(c)
        o = acc_scr[0, rows, :] + acc_scr[1, rows, :]
        gate = g_ref[0, rows, :].astype(F32)
        o_ref[0, rows, :] = (_head_norm(o, gn_ref[0]) * _silu(gate)).astype(o_ref.dtype)
        return carry

    lax.fori_loop(0, n_chunks, finish, 0)


def _retention(proj, lg, cos_t, sin_t, ret_gn, nb, seq):
    kq, kv = RET_DK, RET_DV
    grid_spec = dict(
        grid=(nb, RET_HEADS),
        in_specs=[
            pl.BlockSpec(memory_space=pltpu.SMEM),
            pl.BlockSpec((1, seq, kq), lambda b, h: (b, 0, h)),
            pl.BlockSpec((1, seq, kq), lambda b, h: (b, 0, RET_QK // kq + h)),
            pl.BlockSpec((1, seq, kv), lambda b, h: (b, 0, 2 * RET_QK // kv + h)),
            pl.BlockSpec((1, seq, kv), lambda b, h: (b, 0, (2 * RET_QK + RET_V) // kv + h)),
            pl.BlockSpec((seq, kq), lambda b, h: (0, 0)),
            pl.BlockSpec((seq, kq), lambda b, h: (0, 0)),
            pl.BlockSpec((1, 1, kv), lambda b, h: (h, 0, 0)),
        ],
        out_specs=pl.BlockSpec((1, seq, kv), lambda b, h: (b, 0, h)),
        scratch_shapes=[
            pltpu.VMEM((seq, kq), BF16),
            pltpu.VMEM((seq, kq), F32),
            pltpu.VMEM((2, seq, kv), F32),
            pltpu.VMEM((2, kq, kv), F32),
        ],
    )
    return pl.pallas_call(
        functools.partial(_ret_body, seq=seq),
        out_shape=jax.ShapeDtypeStruct((nb, seq, RET_V), BF16),
        **grid_spec,
        compiler_params=_cparams(("parallel", "parallel")),
        name="retention",
    )(lg, proj, proj, proj, proj, cos_t, sin_t, ret_gn.reshape(RET_HEADS, 1, kv))


def _col_of(mat, c):
    lane = lax.broadcasted_iota(I32, mat.shape, 1)
    return jnp.sum(jnp.where(lane == c, mat, 0.0), axis=1, keepdims=True)


def _mlstm_body(gb_ref, q_ref, k_ref, v_ref, o_gate_ref, wq_ref, wk_ref, bq_ref, bk_ref,
                gr_ref, gn_ref, o_ref,
                pad_scr, qs_scr, ks_scr, acc_scr, c_scr, n_scr, m_scr, row_scr, col_scr, *, seq):
    ln = MLSTM_CHUNK
    n_chunks = seq // ln
    h = pl.program_id(1)

    def emit_q(r0, lane0, y):
        qs_scr[pl.ds(r0, y.shape[0]), lane0:lane0 + LANES] = (y * (MLSTM_DK ** -0.5)).astype(BF16)

    def emit_k(r0, lane0, y):
        ks_scr[pl.ds(r0, y.shape[0]), lane0:lane0 + LANES] = y

    _conv_silu_chunks(q_ref, wq_ref, bq_ref, pad_scr, emit_q, seq)
    _conv_silu_chunks(k_ref, wk_ref, bk_ref, pad_scr, emit_k, seq)

    row_scr[...] = jnp.zeros_like(row_scr)
    for t in range(4):
        for c in range(n_chunks):
            row_scr[t, c:c + 1, :] = gr_ref[0, t:t + 1, c * ln:(c + 1) * ln]
    tri_le = _tri(ln, "le")
    tri_ge = _tri(ln, "ge")
    for d in range(2):
        ig = row_scr[2 * d] + gb_ref[(2 * d) * MLSTM_HEADS + h]
        lf = _log_sigmoid(row_scr[2 * d + 1] + gb_ref[(2 * d + 1) * MLSTM_HEADS + h])
        bc = _dot01_right(lf, tri_le if d == 0 else tri_ge)
        row_scr[2 * d] = ig
        row_scr[2 * d + 1] = bc
        col_scr[2 * d] = ig.T
        col_scr[2 * d + 1] = bc.T

    ri = lax.broadcasted_iota(I32, (ln, ln), 0)
    ci = lax.broadcasted_iota(I32, (ln, ln), 1)

    def chunk_step(c, d):
        r0 = pl.multiple_of(c * ln, ln)
        rows = pl.ds(r0, ln)
        qb = qs_scr[rows, :]
        kf = ks_scr[rows, :]
        v = v_ref[0, rows, :]
        i_row = row_scr[2 * d, pl.ds(c, 1), :]
        b_row = row_scr[2 * d + 1, pl.ds(c, 1), :]
        i_col = _col_of(col_scr[2 * d], c)
        b_col = _col_of(col_scr[2 * d + 1], c)
        m_st = m_scr[d]
        mask = (ri >= ci) if d == 0 else (ri <= ci)
        logd = jnp.where(mask, b_col - b_row + i_row, NEG_INF)
        m_inter = b_col + m_st
        m_row = jnp.maximum(m_inter, jnp.max(logd, axis=1, keepdims=True))
        sc = _dot_nt(qb, kf.astype(BF16)) * jnp.exp(logd - m_row)
        inter = jnp.exp(m_inter - m_row)
        num = _dot(sc.astype(BF16), v) + inter * _dot(qb, c_scr[d].astype(BF16))
        den = jnp.sum(sc, axis=1, keepdims=True) + inter * jnp.sum(
            qb.astype(F32) * n_scr[d], axis=1, keepdims=True)
        hh = num / jnp.maximum(jnp.abs(den), jnp.exp(-m_row))
        b_end = b_row[:, ln - 1:ln] if d == 0 else b_row[:, 0:1]
        logw = b_end - b_col + i_col
        m_new = jnp.maximum(b_end + m_st, jnp.max(logw, axis=0, keepdims=True))
        kw = kf * jnp.exp(logw - m_new)
        dec = jnp.exp(b_end + m_st - m_new)
        c_scr[d] = dec * c_scr[d] + _dot_tn(kw.astype(BF16), v)
        n_scr[d] = dec * n_scr[d] + jnp.sum(kw, axis=0, keepdims=True)
        m_scr[d] = m_new
        acc_scr[d, rows, :] = hh

    c_scr[...] = jnp.zeros_like(c_scr)
    n_scr[...] = jnp.zeros_like(n_scr)
    m_scr[...] = jnp.zeros_like(m_scr)

    def step(i, carry):
        chunk_step(i, 0)
        chunk_step(n_chunks - 1 - i, 1)
        return carry

    lax.fori_loop(0, n_chunks, step, 0, unroll=SCAN_UNROLL)

    def finish(c, carry):
        rows = pl.ds(pl.multiple_of(c * ln, ln), ln)
        y = _head_norm(acc_scr[0, rows, :] + acc_scr[1, rows, :], gn_ref[0])
        o_ref[0, rows, :] = (y * _sigmoid(o_gate_ref[0, rows, :].astype(F32))).astype(o_ref.dtype)
        return carry

    lax.fori_loop(0, n_chunks, finish, 0)


def _mlstm(proj, gate_rows, gate_b, conv_w, conv_b, mlstm_gn, nb, seq):
    kq, kv = MLSTM_DK, MLSTM_DV
    q0 = 2 * RET_QK + 2 * RET_V
    k0 = q0 + MLSTM_QK
    v0 = k0 + MLSTM_QK
    o0 = v0 + MLSTM_V
    grid_spec = dict(
        grid=(nb, MLSTM_HEADS),
        in_specs=[
            pl.BlockSpec(memory_space=pltpu.SMEM),
            pl.BlockSpec((1, seq, kq), lambda b, h: (b, 0, q0 // kq + h)),
            pl.BlockSpec((1, seq, kq), lambda b, h: (b, 0, k0 // kq + h)),
            pl.BlockSpec((1, seq, kv), lambda b, h: (b, 0, v0 // kv + h)),
            pl.BlockSpec((1, seq, kv), lambda b, h: (b, 0, o0 // kv + h)),
            pl.BlockSpec((CONV_W, kq), lambda b, h: (0, h)),
            pl.BlockSpec((CONV_W, kq), lambda b, h: (0, MLSTM_QK // kq + h)),
            pl.BlockSpec((1, kq), lambda b, h: (0, h)),
            pl.BlockSpec((1, kq), lambda b, h: (0, MLSTM_QK // kq + h)),
            pl.BlockSpec((1, GATE_ROWS, seq), lambda b, h: (b, h, 0)),
            pl.BlockSpec((1, 1, kv), lambda b, h: (h, 0, 0)),
        ],
        out_specs=pl.BlockSpec((1, seq, kv), lambda b, h: (b, 0, h)),
        scratch_shapes=[
            pltpu.VMEM((seq + 2 * CONV_HALO, kq), F32),
            pltpu.VMEM((seq, kq), BF16),
            pltpu.VMEM((seq, kq), F32),
            pltpu.VMEM((2, seq, kv), F32),
            pltpu.VMEM((2, kq, kv), F32),
            pltpu.VMEM((2, 1, kq), F32),
            pltpu.VMEM((2, 1, 1), F32),
            pltpu.VMEM((4, LANES, MLSTM_CHUNK), F32),
            pltpu.VMEM((4, MLSTM_CHUNK, LANES), F32),
        ],
    )
    return pl.pallas_call(
        functools.partial(_mlstm_body, seq=seq),
        out_shape=jax.ShapeDtypeStruct((nb, seq, MLSTM_V), BF16),
        **grid_spec,
        compiler_params=_cparams(("parallel", "parallel")),
        name="mlstm",
    )(gate_b, proj, proj, proj, proj, conv_w, conv_w, conv_b.reshape(1, -1), conv_b.reshape(1, -1),
      gate_rows, mlstm_gn.reshape(MLSTM_HEADS, 1, kv))


def _ssd_body(z_ref, x_ref, b_ref, c_ref, wx_ref, wb_ref, wc_ref, bx_ref, bb_ref, bc_ref,
              dtf_ref, dtb_ref, bias_ref, alog_ref, dskip_ref, o_ref,
              padx_scr, padn_scr, xs_scr, bs_scr, bst_scr, cs_scr, y_scr, st_scr,
              acr_scr, dtr_scr, er_scr, ur_scr, act_scr, *, seq):
    ln = SSD_CHUNK
    n_chunks = seq // ln
    hp = SSD_HEADDIM
    n_pairs = SSD_HPG // 2

    def emit_x(r0, lane0, y):
        xs_scr[pl.ds(r0, ln), lane0:lane0 + LANES] = y

    def emit_b(r0, lane0, y):
        bs_scr[pl.ds(r0, ln), :] = y.astype(BF16)
        bst_scr[:, pl.ds(r0, ln)] = y.T.astype(BF16)

    def emit_c(r0, lane0, y):
        cs_scr[pl.ds(r0, ln), :] = y.astype(BF16)

    _conv_silu_chunks(x_ref, wx_ref, bx_ref, padx_scr, emit_x, seq, rows=ln)
    _conv_silu_chunks(b_ref, wb_ref, bb_ref, padn_scr, emit_b, seq, rows=ln)
    _conv_silu_chunks(c_ref, wc_ref, bc_ref, padn_scr, emit_c, seq, rows=ln)

    for d, dt_ref in enumerate((dtf_ref, dtb_ref)):
        acr_scr[d] = jnp.zeros((LANES, ln), F32)
        for c in range(n_chunks):
            acr_scr[d, c * SSD_HPG:(c + 1) * SSD_HPG, :] = dt_ref[0, :, c * ln:(c + 1) * ln]
        dt = _softplus(acr_scr[d] + bias_ref[0, d])
        adt = dt * (-jnp.exp(alog_ref[0, d]))
        acum = _dot01_right(adt, _tri(ln, "le" if d == 0 else "ge"))
        a_end = acum[:, ln - 1:ln] if d == 0 else acum[:, 0:1]
        acr_scr[d] = acum
        dtr_scr[d] = dt
        er_scr[d] = jnp.exp(acum)
        ur_scr[d] = dt * jnp.exp(a_end - acum)
        act_scr[d] = acum.T

    ri = lax.broadcasted_iota(I32, (ln, ln), 0)
    ci = lax.broadcasted_iota(I32, (ln, ln), 1)
    in_first = lax.broadcasted_iota(I32, (1, LANES), 1) < hp
    on_diag = ri == ci

    def chunk_rows(c):
        return pl.ds(pl.multiple_of(c * ln, ln), ln)

    def dir_step(d, c):
        rows = chunk_rows(c)
        xb = xs_scr[rows, :].astype(BF16)
        bcm = bs_scr[rows, :]
        bct = bst_scr[:, rows].astype(F32)
        ccm = cs_scr[rows, :]
        cb = _dot_nt(ccm, bcm)
        carried = _dot(ccm, st_scr[d].astype(BF16)).astype(BF16)
        head_rows = pl.ds(pl.multiple_of(c * SSD_HPG, SSD_HPG), SSD_HPG)
        arow = acr_scr[d, head_rows, :]
        dtrow = dtr_scr[d, head_rows, :]
        erow = er_scr[d, head_rows, :]
        urow = ur_scr[d, head_rows, :]
        acols = pltpu.roll(act_scr[d], (LANES - c * SSD_HPG) & (LANES - 1), axis=1)
        mask = (ri >= ci) if d == 0 else (ri <= ci)
        end = ln - 1 if d == 0 else 0
        pieces = []
        for pair in range(n_pairs):
            lanes = slice(pair * LANES, (pair + 1) * LANES)
            lhs, rhs, lhs_state, keep = [], [], [], []
            for sub in range(2):
                k = 2 * pair + sub
                sel = in_first if sub == 0 else ~in_first
                dec = jnp.exp(jnp.where(mask, acols[:, k:k + 1] - arow[k:k + 1, :], NEG_INF))
                lhs.append((cb * dec * dtrow[k:k + 1, :]).astype(BF16))
                rhs.append(jnp.where(sel, xb[:, lanes], jnp.zeros((ln, LANES), BF16)))
                lhs_state.append((bct * urow[k:k + 1, :]).astype(BF16))
                keep.append(jnp.exp(arow[k:k + 1, end:end + 1]))
            for sub in range(2):
                k = 2 * pair + sub
                sel = in_first if sub == 0 else ~in_first
                lhs.append(jnp.where(on_diag, erow[k:k + 1, :], 0.0).astype(BF16))
                rhs.append(jnp.where(sel, carried[:, lanes], jnp.zeros((ln, LANES), BF16)))
            pieces.append(_dot(jnp.concatenate(lhs, axis=1), jnp.concatenate(rhs, axis=0)))
            st_scr[d, :, lanes] = (jnp.where(in_first, keep[0], keep[1]) * st_scr[d, :, lanes]
                                   + _dot(jnp.concatenate(lhs_state, axis=1),
                                          jnp.concatenate(rhs[:2], axis=0)))
        y_scr[d, rows, :] = jnp.concatenate(pieces, axis=1)

    st_scr[...] = jnp.zeros_like(st_scr)

    def step(i, carry):
        dir_step(0, i)
        dir_step(1, n_chunks - 1 - i)
        return carry

    lax.fori_loop(0, n_chunks, step, 0, unroll=SCAN_UNROLL)

    def finish(c, carry):
        rows = chunk_rows(c)
        y = y_scr[0, rows, :] + y_scr[1, rows, :] + dskip_ref[0] * xs_scr[rows, :]
        o_ref[0, rows, :] = (y * _silu(z_ref[0, rows, :].astype(F32))).astype(o_ref.dtype)
        return carry

    lax.fori_loop(0, n_chunks, finish, 0)


def _ssd(proj, dt_t, bias_col, alog_col, dskip_x, conv_w, conv_b, nb, seq):
    width = SSD_HPG * SSD_HEADDIM
    ns = SSD_STATE
    x0 = SSD_INNER
    b0 = 2 * SSD_INNER
    c0 = b0 + SSD_BC
    cb = conv_b.reshape(1, -1)
    return pl.pallas_call(
        functools.partial(_ssd_body, seq=seq),
        out_shape=jax.ShapeDtypeStruct((nb, seq, SSD_INNER), BF16),
        grid=(nb, SSD_GROUPS),
        in_specs=[
            pl.BlockSpec((1, seq, width), lambda b, g: (b, 0, g)),
            pl.BlockSpec((1, seq, width), lambda b, g: (b, 0, x0 // width + g)),
            pl.BlockSpec((1, seq, ns), lambda b, g: (b, 0, b0 // ns + g)),
            pl.BlockSpec((1, seq, ns), lambda b, g: (b, 0, c0 // ns + g)),
            pl.BlockSpec((CONV_W, width), lambda b, g: (0, g)),
            pl.BlockSpec((CONV_W, ns), lambda b, g: (0, SSD_INNER // ns + g)),
            pl.BlockSpec((CONV_W, ns), lambda b, g: (0, (SSD_INNER + SSD_BC) // ns + g)),
            pl.BlockSpec((1, width), lambda b, g: (0, g)),
            pl.BlockSpec((1, ns), lambda b, g: (0, SSD_INNER // ns + g)),
            pl.BlockSpec((1, ns), lambda b, g: (0, (SSD_INNER + SSD_BC) // ns + g)),
            pl.BlockSpec((1, SSD_HPG, seq), lambda b, g: (b, g, 0)),
            pl.BlockSpec((1, SSD_HPG, seq), lambda b, g: (b, SSD_GROUPS + g, 0)),
            pl.BlockSpec((1, 2, LANES, 1), lambda b, g: (g, 0, 0, 0)),
            pl.BlockSpec((1, 2, LANES, 1), lambda b, g: (g, 0, 0, 0)),
            pl.BlockSpec((1, 1, width), lambda b, g: (g, 0, 0)),
        ],
        out_specs=pl.BlockSpec((1, seq, width), lambda b, g: (b, 0, g)),
        scratch_shapes=[
            pltpu.VMEM((seq + 2 * CONV_HALO, width), F32),
            pltpu.VMEM((seq + 2 * CONV_HALO, ns), F32),
            pltpu.VMEM((seq, width), F32),
            pltpu.VMEM((seq, ns), BF16),
            pltpu.VMEM((ns, seq), BF16),
            pltpu.VMEM((seq, ns), BF16),
            pltpu.VMEM((2, seq, width), F32),
            pltpu.VMEM((2, ns, width), F32),
            pltpu.VMEM((2, LANES, SSD_CHUNK), F32),
            pltpu.VMEM((2, LANES, SSD_CHUNK), F32),
            pltpu.VMEM((2, LANES, SSD_CHUNK), F32),
            pltpu.VMEM((2, LANES, SSD_CHUNK), F32),
            pltpu.VMEM((2, SSD_CHUNK, LANES), F32),
        ],
        compiler_params=_cparams(("parallel", "parallel")),
        name="ssd",
    )(proj, proj, proj, proj, conv_w, conv_w, conv_w, cb, cb, cb,
      dt_t, dt_t, bias_col, alog_col, dskip_x)


META_E = 0
META_G = 2
META_R = 4
ROUTE_E0 = MOE_GROUPS


def _router_body(x_ref, g_ref, sc_ref, sh_ref, w_ref, b_ref, h_ref, meta_ref, cnt_ref, carry_scr):
    tm = x_ref.shape[0]

    @pl.when(pl.program_id(0) == 0)
    def _():
        carry_scr[...] = jnp.zeros_like(carry_scr)

    y = _rms(x_ref[...], g_ref[...]) * (1.0 + sc_ref[0]) + sh_ref[0]
    h_ref[...] = y
    h_hi = y.astype(BF16)
    h_lo = (y - h_hi.astype(F32)).astype(BF16)
    w = w_ref[...]
    w_hi = w.astype(BF16)
    w_lo = (w - w_hi.astype(F32)).astype(BF16)
    logits = _dot(h_hi, w_hi) + _dot(h_lo, w_hi) + _dot(h_hi, w_lo) + b_ref[...]

    lane = lax.broadcasted_iota(I32, (tm, LANES), 1)
    lane_f = lane.astype(F32)
    big = float(LANES)
    is_grp = lane < MOE_GROUPS
    gl = jnp.where(is_grp, logits, NEG_INF)
    gmax = jnp.max(gl, axis=1, keepdims=True)
    gidx = jnp.min(jnp.where(gl == gmax, lane_f, big), axis=1, keepdims=True)
    gprob = 1.0 / jnp.sum(jnp.where(is_grp, jnp.exp(gl - gmax), 0.0), axis=1, keepdims=True)

    el = lane - ROUTE_E0
    el_f = el.astype(F32)
    valid = (el >= 0) & (el < MOE_EXPERTS)
    in_grp = valid & (_shr(el, MOE_EPG).astype(F32) == gidx)
    ev = jnp.where(in_grp, logits, NEG_INF)
    v1 = jnp.max(ev, axis=1, keepdims=True)
    i1 = jnp.min(jnp.where(ev == v1, el_f, big), axis=1, keepdims=True)
    ev2 = jnp.where(el_f == i1, NEG_INF, ev)
    v2 = jnp.max(ev2, axis=1, keepdims=True)
    i2 = jnp.min(jnp.where(ev2 == v2, el_f, big), axis=1, keepdims=True)
    p2 = jnp.exp(v2 - v1)
    s1 = 1.0 / (1.0 + p2)
    gate1 = s1 * gprob
    gate2 = p2 * s1 * gprob

    oh1 = jnp.where(el_f == i1, 1.0, 0.0)
    oh2 = jnp.where(el_f == i2, 1.0, 0.0)
    oh = oh1 + oh2
    before = _dot(_tri(tm, "gt"), oh.astype(BF16)) + carry_scr[...]
    rank1 = jnp.sum(oh1 * before, axis=1, keepdims=True)
    rank2 = jnp.sum(oh2 * before, axis=1, keepdims=True)
    carry_scr[...] = carry_scr[...] + jnp.sum(oh, axis=0, keepdims=True)
    cnt_ref[...] = jnp.broadcast_to(carry_scr[...], cnt_ref.shape)

    meta = jnp.zeros((tm, LANES), F32)
    for col, val in ((META_E, i1), (META_E + 1, i2), (META_G, gate1), (META_G + 1, gate2),
                     (META_R, rank1), (META_R + 1, rank2)):
        meta = jnp.where(lane == col, val, meta)
    meta_ref[...] = meta


def _router(x, g, sc, sh, w_route, b_route, seq, tm=512):
    t, k = x.shape
    tm = min(tm, seq)
    tps = seq // tm
    return pl.pallas_call(
        _router_body,
        out_shape=[jax.ShapeDtypeStruct((t, k), F32),
                   jax.ShapeDtypeStruct((t, LANES), F32),
                   jax.ShapeDtypeStruct((8, LANES), F32)],
        grid=(t // tm,),
        in_specs=[
            pl.BlockSpec((tm, k), lambda i: (i, 0)),
            pl.BlockSpec((1, k), lambda i: (0, 0)),
            pl.BlockSpec((1, 1, k), lambda i: (i // tps, 0, 0)),
            pl.BlockSpec((1, 1, k), lambda i: (i // tps, 0, 0)),
            pl.BlockSpec((k, LANES), lambda i: (0, 0)),
            pl.BlockSpec((1, LANES), lambda i: (0, 0)),
        ],
        out_specs=[pl.BlockSpec((tm, k), lambda i: (i, 0)),
                   pl.BlockSpec((tm, LANES), lambda i: (i, 0)),
                   pl.BlockSpec((8, LANES), lambda i: (0, 0))],
        scratch_shapes=[pltpu.VMEM((1, LANES), F32)],
        compiler_params=_cparams(("arbitrary",)),
        name="router",
    )(x, g.reshape(1, k), sc, sh, w_route, b_route)


def _row_copy(src_hbm, dst, sem, src_row, dst_row):
    return pltpu.make_async_copy(src_hbm.at[pl.ds(src_row, 1)], dst.at[pl.ds(dst_row, 1)], sem)


def _start_row_gather(idx_ref, base, n_rows, stride, src_hbm, dst, sem):
    def body(r, carry):
        _row_copy(src_hbm, dst, sem, idx_ref[base + r * stride], r).start()
        return carry
    lax.fori_loop(0, n_rows, body, 0, unroll=GATHER_UNROLL)


def _wait_row_gather(src_hbm, dst, sem, n_rows):
    pltpu.make_async_copy(src_hbm.at[pl.ds(0, n_rows)], dst, sem).wait()


def _expert_body(blk_exp_ref, row_tok_ref, h_hbm, wg_ref, wu_ref, wd_ref, y_ref,
                 xb_scr, wgu_scr, wdn_scr, sem):
    i = pl.program_id(0)
    n_steps = pl.num_programs(0)
    slot = i % 2
    d = xb_scr.shape[2]

    def start(block, s):
        _start_row_gather(row_tok_ref, block * MOE_BLOCK, MOE_BLOCK, 1, h_hbm, xb_scr.at[s], sem.at[s])

    @pl.when(i == 0)
    def _():
        start(0, 0)

    _wait_row_gather(h_hbm, xb_scr.at[slot], sem.at[slot], MOE_BLOCK)

    @pl.when(i + 1 < n_steps)
    def _():
        start(i + 1, 1 - slot)

    @pl.when((i == 0) | (blk_exp_ref[i] != blk_exp_ref[jnp.maximum(i - 1, 0)]))
    def _():
        def cast_up(c, carry):
            rows = pl.ds(pl.multiple_of(c * CAST_ROWS, CAST_ROWS), CAST_ROWS)
            wgu_scr[rows, :EXPERT_FF] = wg_ref[0, 0, rows, :].astype(BF16)
            wgu_scr[rows, EXPERT_FF:] = wu_ref[0, 0, rows, :].astype(BF16)
            return carry

        def cast_down(c, carry):
            rows = pl.ds(pl.multiple_of(c * CAST_ROWS, CAST_ROWS), CAST_ROWS)
            wdn_scr[rows, :] = wd_ref[0, 0, rows, :].astype(BF16)
            return carry

        lax.fori_loop(0, d // CAST_ROWS, cast_up, 0)
        lax.fori_loop(0, EXPERT_FF // CAST_ROWS, cast_down, 0)

    a = _dot(xb_scr[slot].astype(BF16), wgu_scr[...])
    hid = (_silu(a[:, :EXPERT_FF]) * a[:, EXPERT_FF:]).astype(BF16)
    y_ref[...] = _dot(hid, wdn_scr[...])


def _experts(h, w_gate, w_up, w_down, layer, blk_exp, row_tok):
    t, d = h.shape
    n_rows = row_tok.shape[0]
    n_blocks = n_rows // MOE_BLOCK
    grid_spec = pltpu.PrefetchScalarGridSpec(
        num_scalar_prefetch=2,
        grid=(n_blocks,),
        in_specs=[
            pl.BlockSpec(memory_space=pl.ANY),
            pl.BlockSpec((1, 1, d, EXPERT_FF), lambda i, be, rt: (layer, be[i], 0, 0)),
            pl.BlockSpec((1, 1, d, EXPERT_FF), lambda i, be, rt: (layer, be[i], 0, 0)),
            pl.BlockSpec((1, 1, EXPERT_FF, d), lambda i, be, rt: (layer, be[i], 0, 0)),
        ],
        out_specs=pl.BlockSpec((MOE_BLOCK, d), lambda i, be, rt: (i, 0)),
        scratch_shapes=[pltpu.VMEM((2, MOE_BLOCK, d), F32),
                        pltpu.VMEM((d, 2 * EXPERT_FF), BF16),
                        pltpu.VMEM((EXPERT_FF, d), BF16),
                        pltpu.SemaphoreType.DMA((2,))],
    )
    return pl.pallas_call(
        _expert_body,
        out_shape=jax.ShapeDtypeStruct((n_rows, d), F32),
        grid_spec=grid_spec,
        compiler_params=_cparams(("arbitrary",)),
        name="experts",
    )(blk_exp, row_tok, h, w_gate, w_up, w_down)


def _combine_body(dest_ref, y_hbm, x_ref, gate_ref, meta_ref, fn_ref, o_ref, ya_scr, sem,
                  *, final, tile0):
    i = pl.program_id(0)
    n_steps = pl.num_programs(0)
    tm = x_ref.shape[0]
    slot = i % 2

    def start(step, s):
        for j in range(2):
            _start_row_gather(dest_ref, (tile0 + step) * tm * 2 + j, tm, 2, y_hbm,
                              ya_scr.at[s, j], sem.at[s, j])

    @pl.when(i == 0)
    def _():
        start(0, 0)

    for j in range(2):
        _wait_row_gather(y_hbm, ya_scr.at[slot, j], sem.at[slot, j], tm)

    @pl.when(i + 1 < n_steps)
    def _():
        start(i + 1, 1 - slot)

    meta = meta_ref[...]
    moe = (ya_scr[slot, 0] * meta[:, META_G:META_G + 1]
           + ya_scr[slot, 1] * meta[:, META_G + 1:META_G + 2])
    out = x_ref[...] + gate_ref[0] * moe
    if final:
        out = _rms(out, fn_ref[...])
    o_ref[...] = out


def _combine(y, x, gate, meta, dest, final_norm, seq, final, tok0=0, n_tok=None, tm=256):
    t, d = x.shape
    n_tok = t if n_tok is None else n_tok
    tm = min(tm, seq)
    tps = seq // tm
    assert tok0 % seq == 0 and n_tok % seq == 0
    tile0 = tok0 // tm
    grid_spec = pltpu.PrefetchScalarGridSpec(
        num_scalar_prefetch=1,
        grid=(n_tok // tm,),
        in_specs=[
            pl.BlockSpec(memory_space=pl.ANY),
            pl.BlockSpec((tm, d), lambda i, ds: (tile0 + i, 0)),
            pl.BlockSpec((1, 1, d), lambda i, ds: ((tile0 + i) // tps, 0, 0)),
            pl.BlockSpec((tm, LANES), lambda i, ds: (tile0 + i, 0)),
            pl.BlockSpec((1, d), lambda i, ds: (0, 0)),
        ],
        out_specs=pl.BlockSpec((tm, d), lambda i, ds: (i, 0)),
        scratch_shapes=[pltpu.VMEM((2, 2, tm, d), F32), pltpu.SemaphoreType.DMA((2, 2))],
    )
    return pl.pallas_call(
        functools.partial(_combine_body, final=final, tile0=tile0),
        out_shape=jax.ShapeDtypeStruct((n_tok, d), F32),
        grid_spec=grid_spec,
        compiler_params=_cparams(("arbitrary",)),
        name="moe_combine",
    )(dest, y, x, gate, meta, final_norm.reshape(1, d))


def _moe_layer(x, g, sc, sh, gate, grp_w, grp_b, exp_w, exp_b, w_gate, w_up, w_down, layer,
               final_norm, seq, final, split=None):
    t, d = x.shape
    pad = LANES - MOE_GROUPS - MOE_EXPERTS
    w_route = jnp.concatenate([grp_w, exp_w, jnp.zeros((d, pad), F32)], axis=1)
    b_route = jnp.concatenate([grp_b, exp_b, jnp.zeros((pad,), F32)]).reshape(1, LANES)
    h, meta, cnt = _router(x, g, sc, sh, w_route, b_route, seq)

    expert = meta[:, META_E:META_E + 2].astype(I32)
    rank = meta[:, META_R:META_R + 2].astype(I32)
    counts = cnt[0, ROUTE_E0:ROUTE_E0 + MOE_EXPERTS].astype(I32)
    padded = (counts + MOE_BLOCK - 1) // MOE_BLOCK * MOE_BLOCK
    p_ends = jnp.cumsum(padded)
    p_starts = p_ends - padded
    dest = (p_starts[expert] + rank).reshape(-1)
    n_rows = t * 2 + MOE_EXPERTS * MOE_BLOCK
    n_blocks = n_rows // MOE_BLOCK
    token_id = jnp.repeat(jnp.arange(t, dtype=I32), 2)
    row_tok = jnp.zeros((n_rows,), I32).at[dest].set(token_id)
    blk_start = jnp.arange(n_blocks, dtype=I32) * MOE_BLOCK
    blk_exp = jnp.minimum(jnp.sum((p_ends[None, :] <= blk_start[:, None]).astype(I32), axis=1),
                          MOE_EXPERTS - 1)

    y = _experts(h, w_gate, w_up, w_down, layer, blk_exp, row_tok)
    if split is None:
        return _combine(y, x, gate, meta, dest, final_norm, seq, final)
    return tuple(_combine(y, x, gate, meta, dest, final_norm, seq, final, tok0=a, n_tok=b - a)
                 for a, b in ((0, split), (split, t)))


def _rope_tables(seq):
    half = RET_DK // 2
    inv = ROPE_BASE ** (-jnp.arange(half, dtype=F32) / half)
    ang = jnp.arange(seq, dtype=F32)[:, None] * inv[None, :]
    cos, sin = jnp.cos(ang), jnp.sin(ang)
    return jnp.concatenate([cos, cos], axis=1), jnp.concatenate([-sin, sin], axis=1)


def _pad_rows(a, axis, n):
    pad = [(0, 0)] * a.ndim
    pad[axis] = (0, n - a.shape[axis])
    return jnp.pad(a, pad)


def kernel(x_prompt, x_sample, c_prompt, c_sample, ada_w, ada_b, norm1, norm2, ev_w_in, ev_gate_b, ev_conv_w, ev_conv_b, ev_ret_gn, ev_mlstm_gn, ev_w_out, od_w_in, od_conv_w, od_conv_b, od_dt_bias, od_a_log, od_d_skip, od_norm, od_w_out, moe_grp_w, moe_grp_b, moe_exp_w, moe_exp_b, moe_w_gate, moe_w_up, moe_w_down, final_norm):
    n_prompt = x_prompt.shape[0]
    seq, d = x_prompt.shape[1], x_prompt.shape[2]
    assert x_sample.shape[1] == seq and d == D_MODEL
    assert seq % RET_CHUNK == 0 and seq // MLSTM_CHUNK <= LANES // SSD_HPG
    x = jnp.concatenate([x_prompt, x_sample], axis=0)
    nb = x.shape[0]
    t = nb * seq
    x = x.reshape(t, d)
    depth = ada_w.shape[0]

    c_all = jnp.concatenate([c_prompt, c_sample], axis=0)
    c_pad = _pad_rows(c_all, 0, -(-nb // 8) * 8)
    mod = _modulation(c_pad, ada_w, ada_b)[:, :nb].reshape(depth, nb, N_MOD, 1, d)

    heads = jnp.arange(RET_HEADS, dtype=F32)
    lg = jnp.stack([jnp.log1p(-jnp.exp2(-RET_DECAY_FWD - heads)),
                    jnp.log1p(-jnp.exp2(-RET_DECAY_BWD - heads))])
    cos_t, sin_t = _rope_tables(seq)

    for i in range(depth):
        sh1, sc1, g1, sh2, sc2, g2 = (mod[i, :, m] for m in range(N_MOD))
        j = i // 2
        if i % 2 == 0:
            w_in = ev_w_in[j]
            w_side = w_in[:, EVEN_MAIN:].reshape(d, 4, MLSTM_HEADS).transpose(0, 2, 1)
            w_side = _pad_rows(_pad_rows(w_side, 2, GATE_ROWS).reshape(d, -1), 1, LANES)
            proj, gates = _fused_matmul(x, w_in[:, :EVEN_MAIN].astype(BF16), seq=seq, prologue="normmod",
                                        g=norm1[i], sc=sc1, sh=sh1, w_side=w_side, name="even_in_proj")
            proj = proj.reshape(nb, seq, EVEN_MAIN)
            ret = _retention(proj, lg, cos_t, sin_t, ev_ret_gn[j], nb, seq)
            ml = _mlstm(proj, gates, ev_gate_b[j], ev_conv_w[j], ev_conv_b[j], ev_mlstm_gn[j], nb, seq)
            x = _fused_matmul(ret.reshape(t, RET_V), ev_w_out[j].astype(BF16), seq=seq,
                              x2=ml.reshape(t, MLSTM_V),
                              res=x, gate=g1, tn=512, name="even_out_proj")
        else:
            w_in = od_w_in[j]
            proj, dt_raw = _fused_matmul(x, w_in[:, :ODD_MAIN].astype(BF16), seq=seq, prologue="normmod",
                                         g=norm1[i], sc=sc1, sh=sh1, w_side=w_in[:, ODD_MAIN:],
                                         name="odd_in_proj")
            def per_row(p):
                p = p.reshape(2, SSD_GROUPS, SSD_HPG).transpose(1, 0, 2)
                return jnp.tile(p, (1, 1, LANES // SSD_HPG))[..., None]

            def per_lane(p):
                p = p.reshape(*p.shape[:-1], SSD_GROUPS, SSD_HPG)
                p = jnp.moveaxis(p, -2, 0)
                return jnp.repeat(p, SSD_HEADDIM, axis=-1)[..., None, :]

            y = _ssd(proj.reshape(nb, seq, ODD_MAIN), dt_raw, per_row(od_dt_bias[j]), per_row(od_a_log[j]),
                     per_lane(od_d_skip[j]), od_conv_w[j], od_conv_b[j], nb, seq)
            x = _fused_matmul(y.reshape(t, SSD_INNER), od_w_out[j].astype(BF16), seq=seq, prologue="norm",
                              g=od_norm[j], res=x, gate=g1, tn=512, name="odd_out_proj")
        last = i == depth - 1
        x = _moe_layer(x, norm2[i], sc2, sh2, g2, moe_grp_w[i], moe_grp_b[i], moe_exp_w[i], moe_exp_b[i],
                       moe_w_gate, moe_w_up, moe_w_down, i, final_norm, seq, final=last,
                       split=n_prompt * seq if last else None)
    y_prompt, y_sample = x
    return (y_prompt.reshape(n_prompt, seq, d), y_sample.reshape(nb - n_prompt, seq, d))
```

```python
import functools
import math

import jax
import jax.numpy as jnp
import numpy as np
from jax import lax
from jax.experimental import pallas as pl
from jax.experimental.pallas import tpu as pltpu

F32 = jnp.float32
BF16 = jnp.bfloat16
I32 = jnp.int32

D_MODEL = 2048
N_MOD = 6
EPS = 1e-6
CONV_W = 5
CONV_HALO = 8

RET_HEADS = 8
RET_DV = D_MODEL // RET_HEADS
RET_DK = RET_DV // 2
RET_DECAY_FWD = 5.0
RET_DECAY_BWD = 5.5
ROPE_BASE = 10000.0
RET_CHUNK = 256
MLSTM_HEADS = 4
MLSTM_DV = D_MODEL // MLSTM_HEADS
MLSTM_DK = MLSTM_DV // 2
MLSTM_CHUNK = 128
GATE_ROWS = 8
SSD_INNER = 2 * D_MODEL
SSD_HEADDIM = 64
SSD_HEADS = SSD_INNER // SSD_HEADDIM
SSD_GROUPS = 8
SSD_HPG = SSD_HEADS // SSD_GROUPS
SSD_STATE = 128
SSD_CHUNK = 128
MOE_GROUPS = 4
MOE_EPG = 8
MOE_EXPERTS = MOE_GROUPS * MOE_EPG
EXPERT_FF = D_MODEL // 4
MOE_BLOCK = 128

RET_QK = RET_HEADS * RET_DK
RET_V = RET_HEADS * RET_DV
MLSTM_QK = MLSTM_HEADS * MLSTM_DK
MLSTM_V = MLSTM_HEADS * MLSTM_DV
MLSTM_NGATE = 4 * MLSTM_HEADS
EVEN_MAIN = 2 * RET_QK + 2 * RET_V + 2 * MLSTM_QK + 2 * MLSTM_V
EVEN_MIX = RET_V + MLSTM_V
SSD_BC = SSD_GROUPS * SSD_STATE
SSD_CONV_CH = SSD_INNER + 2 * SSD_BC
ODD_MAIN = SSD_INNER + SSD_CONV_CH

PROLOGUE_ROWS = 256
CAST_ROWS = 256
GATHER_UNROLL = 32
SCAN_UNROLL = 2
LANES = 128
VMEM_LIMIT = 56 * 1024 * 1024

NEG_INF = float("-inf")


def _cparams(sem, vmem=VMEM_LIMIT):
    return pltpu.CompilerParams(dimension_semantics=sem, vmem_limit_bytes=vmem)


def _split3(x):
    hi = x.astype(BF16)
    r = x - hi.astype(F32)
    mid = r.astype(BF16)
    lo = (r - mid.astype(F32)).astype(BF16)
    return hi, mid, lo


def _dot(a, b):
    return jnp.dot(a, b, preferred_element_type=F32)


def _dot_nt(a, b):
    return lax.dot_general(a, b, (((1,), (1,)), ((), ())), preferred_element_type=F32)


def _dot_tn(a, b):
    return lax.dot_general(a, b, (((0,), (0,)), ((), ())), preferred_element_type=F32)


def _dot01_left(m01, x):
    hi, mid, lo = _split3(x)
    return _dot(m01, hi) + _dot(m01, mid) + _dot(m01, lo)


def _dot01_right(x, m01):
    hi, mid, lo = _split3(x)
    return _dot(hi, m01) + _dot(mid, m01) + _dot(lo, m01)


def _tri(n, kind):
    r = lax.broadcasted_iota(I32, (n, n), 0)
    c = lax.broadcasted_iota(I32, (n, n), 1)
    m = {"le": r <= c, "ge": r >= c, "gt": r > c}[kind]
    return jnp.where(m, 1.0, 0.0).astype(BF16)


def _shr(x, pow2):
    return lax.shift_right_arithmetic(x, jnp.int32(int(math.log2(pow2))))


def _sigmoid(x):
    return 1.0 / (1.0 + jnp.exp(-x))


def _silu(x):
    return x * _sigmoid(x)


def _softplus(x):
    return jnp.maximum(x, 0.0) + jnp.log1p(jnp.exp(-jnp.abs(x)))


def _log_sigmoid(x):
    return jnp.minimum(x, 0.0) - jnp.log1p(jnp.exp(-jnp.abs(x)))


def _rms(x, g):
    ms = jnp.mean(x * x, axis=-1, keepdims=True)
    return x * lax.rsqrt(ms + EPS) * g


def _head_norm(y, g):
    mu = jnp.mean(y, axis=-1, keepdims=True)
    yc = y - mu
    var = jnp.mean(yc * yc, axis=-1, keepdims=True)
    return yc * lax.rsqrt(var + EPS) * g


def _mod_body(c_ref, w_ref, b_ref, o_ref):
    c = c_ref[...]
    o_ref[0] = _dot(_silu(c).astype(BF16), w_ref[0].astype(BF16)) + b_ref[0]


def _modulation(c_pad, ada_w, ada_b):
    depth, d, n = ada_w.shape
    m = c_pad.shape[0]
    tn = 1024
    return pl.pallas_call(
        _mod_body,
        out_shape=jax.ShapeDtypeStruct((depth, m, n), F32),
        grid=(depth, n // tn),
        in_specs=[
            pl.BlockSpec((m, d), lambda l, j: (0, 0)),
            pl.BlockSpec((1, d, tn), lambda l, j: (l, 0, j)),
            pl.BlockSpec((1, 1, tn), lambda l, j: (l, 0, j)),
        ],
        out_specs=pl.BlockSpec((1, m, tn), lambda l, j: (l, 0, j)),
        compiler_params=_cparams(("parallel", "parallel")),
        name="modulation",
    )(c_pad, ada_w, ada_b.reshape(depth, 1, n))


def _mm_body(*refs, prologue, epilogue, side, two_lhs):
    it = iter(refs)
    x_ref = next(it)
    x2_ref = next(it) if two_lhs else None
    g_ref = next(it) if prologue != "none" else None
    sc_ref = next(it) if prologue == "normmod" else None
    sh_ref = next(it) if prologue == "normmod" else None
    w_ref = next(it)
    ws_ref = next(it) if side else None
    res_ref = next(it) if epilogue == "residual" else None
    gate_ref = next(it) if epilogue == "residual" else None
    o_ref = next(it)
    os_ref = next(it) if side else None
    h_scr = next(it) if prologue != "none" else None

    if prologue != "none":
        @pl.when(pl.program_id(1) == 0)
        def _():
            rows_per = PROLOGUE_ROWS

            def chunk(i, carry):
                rows = pl.ds(pl.multiple_of(i * rows_per, rows_per), rows_per)
                y = _rms(x_ref[rows, :].astype(F32), g_ref[...])
                if prologue == "normmod":
                    y = y * (1.0 + sc_ref[0]) + sh_ref[0]
                hb = y.astype(BF16)
                h_scr[rows, :] = hb
                if side:
                    h_lo = (y - hb.astype(F32)).astype(BF16)
                    ws = ws_ref[...]
                    w_hi = ws.astype(BF16)
                    w_lo = (ws - w_hi.astype(F32)).astype(BF16)
                    os_ref[0, :, rows] = (_dot(hb, w_hi) + _dot(h_lo, w_hi) + _dot(hb, w_lo)).T
                return carry

            lax.fori_loop(0, x_ref.shape[0] // rows_per, chunk, 0)
        lhs = h_scr[...]
    else:
        lhs = x_ref[...]
    if two_lhs:
        k1 = x_ref.shape[1]
        acc = _dot(lhs, w_ref[:k1, :]) + _dot(x2_ref[...], w_ref[k1:, :])
    else:
        acc = _dot(lhs, w_ref[...])
    if epilogue == "residual":
        o_ref[...] = res_ref[...] + gate_ref[0] * acc
    else:
        o_ref[...] = acc.astype(o_ref.dtype)


def _fused_matmul(x, w, *, seq, x2=None, prologue="none", g=None, sc=None, sh=None, w_side=None,
                  res=None, gate=None, out_dtype=BF16, tm=1024, tn=1024, name="proj"):
    t, k = x.shape
    n = w.shape[1]
    tm = min(tm, seq)
    tn = min(tn, n)
    assert t % tm == 0 and seq % tm == 0 and n % tn == 0
    tps = seq // tm
    epilogue = "residual" if res is not None else "plain"
    side = w_side is not None
    two_lhs = x2 is not None
    assert not (two_lhs and prologue != "none")
    in_specs = [pl.BlockSpec((tm, k), lambda i, j: (i, 0))]
    args = [x]
    if two_lhs:
        in_specs.append(pl.BlockSpec((tm, x2.shape[1]), lambda i, j: (i, 0)))
        args.append(x2)
        k = k + x2.shape[1]
    if prologue != "none":
        in_specs.append(pl.BlockSpec((1, k), lambda i, j: (0, 0)))
        args.append(g.reshape(1, k))
    if prologue == "normmod":
        in_specs += [pl.BlockSpec((1, 1, k), lambda i, j: (i // tps, 0, 0))] * 2
        args += [sc, sh]
    in_specs.append(pl.BlockSpec((k, tn), lambda i, j: (0, j)))
    args.append(w)
    if side:
        in_specs.append(pl.BlockSpec((k, LANES), lambda i, j: (0, 0)))
        args.append(w_side)
    if epilogue == "residual":
        in_specs += [pl.BlockSpec((tm, tn), lambda i, j: (i, j)),
                     pl.BlockSpec((1, 1, tn), lambda i, j: (i // tps, 0, j))]
        args += [res, gate]
        out_dtype = F32
    out_shape = [jax.ShapeDtypeStruct((t, n), out_dtype)]
    out_specs = [pl.BlockSpec((tm, tn), lambda i, j: (i, j))]
    if side:
        out_shape.append(jax.ShapeDtypeStruct((t // seq, LANES, seq), F32))
        out_specs.append(pl.BlockSpec((1, LANES, tm), lambda i, j: (i // tps, 0, i % tps)))
    scratch = [pltpu.VMEM((tm, k), BF16)] if prologue != "none" else []
    outs = pl.pallas_call(
        functools.partial(_mm_body, prologue=prologue, epilogue=epilogue, side=side, two_lhs=two_lhs),
        out_shape=out_shape,
        grid=(t // tm, n // tn),
        in_specs=in_specs,
        out_specs=out_specs,
        scratch_shapes=scratch,
        compiler_params=_cparams(("parallel", "arbitrary")),
        name=name,
    )(*args)
    return outs if side else outs[0]


CONV_ROWS = 128


def _conv_fill(src_ref, pad_scr, seq):
    ch = pad_scr.shape[1]
    halo = CONV_HALO
    rows = CONV_ROWS
    zeros = jnp.zeros((halo, ch), F32)
    pad_scr[pl.ds(0, halo), :] = zeros
    pad_scr[pl.ds(seq + halo, halo), :] = zeros

    def fill(i, carry):
        r0 = pl.multiple_of(i * rows, rows)
        pad_scr[pl.ds(pl.multiple_of(r0 + halo, halo), rows), :] = src_ref[0, pl.ds(r0, rows), :].astype(F32)
        return carry

    lax.fori_loop(0, seq // rows, fill, 0)


def _conv_silu_rows(pad_scr, w_ref, b_ref, r0, emit):
    ch = pad_scr.shape[1]
    halo = CONV_HALO
    rows = CONV_ROWS
    win = rows + 2 * halo
    half = (CONV_W - 1) // 2
    for lane0 in range(0, ch, LANES):
        cols = slice(lane0, lane0 + LANES)
        window = pad_scr[pl.ds(r0, win), cols]
        acc = jnp.zeros((rows, LANES), F32) + b_ref[:, cols]
        for j in range(CONV_W):
            d = j - half
            shifted = window if d == 0 else pltpu.roll(window, (-d) % win, axis=0)
            acc = acc + w_ref[j:j + 1, cols] * shifted[halo:halo + rows, :]
        emit(lane0, _silu(acc))


def _two_ended_scan(n_chunks, prepare, step, finish):
    assert n_chunks % 2 == 0
    half = n_chunks // 2

    def first(i, carry):
        prepare(i)
        prepare(n_chunks - 1 - i)
        step(i)
        return carry

    def second(i, carry):
        step(i)
        finish(i)
        finish(n_chunks - 1 - i)
        return carry

    unroll = SCAN_UNROLL if half % SCAN_UNROLL == 0 else 1
    lax.fori_loop(0, half, first, 0, unroll=unroll)
    lax.fori_loop(half, n_chunks, second, 0, unroll=unroll)


def _ret_body(lg_ref, q_ref, k_ref, v_ref, g_ref, cos_ref, sin_ref, gn_ref, o_ref,
              qs_scr, ks_scr, acc_scr, st_scr, *, seq):
    c_len = RET_CHUNK
    n_chunks = seq // c_len
    h = pl.program_id(1)
    lgf = lg_ref[0, h]
    lgb = lg_ref[1, h]
    ri = lax.broadcasted_iota(I32, (c_len, c_len), 0)
    ci = lax.broadcasted_iota(I32, (c_len, c_len), 1)
    diff = (ri - ci).astype(F32)
    dmat = jnp.exp(jnp.where(diff >= 0, lgf * diff, -lgb * diff))
    pos = lax.broadcasted_iota(I32, (c_len, 1), 0).astype(F32)
    qdec_f = jnp.exp(lgf * (pos + 1.0))
    kdec_f = jnp.exp(lgf * (c_len - 1.0 - pos))
    cdec_f = jnp.exp(jnp.full((1, 1), c_len, F32) * lgf)
    qdec_b = jnp.exp(lgb * (c_len - pos))
    kdec_b = jnp.exp(lgb * pos)
    cdec_b = jnp.exp(jnp.full((1, 1), c_len, F32) * lgb)
    half = RET_DK // 2

    def rope(x, rows):
        return x * cos_ref[rows, :] + pltpu.roll(x, half, axis=1) * sin_ref[rows, :]

    def chunk_rows(c):
        return pl.ds(pl.multiple_of(c * c_len, c_len), c_len)

    def prepare(c):
        rows = chunk_rows(c)
        qs_scr[rows, :] = rope(q_ref[0, rows, :].astype(F32), rows).astype(BF16)
        ks_scr[rows, :] = rope(k_ref[0, rows, :].astype(F32), rows) * (RET_DK ** -0.5)

    st_scr[...] = jnp.zeros_like(st_scr)

    def step(i):
        rows = chunk_rows(i)
        qb = qs_scr[rows, :]
        k = ks_scr[rows, :]
        v = v_ref[0, rows, :]
        p = (_dot_nt(qb, k.astype(BF16)) * dmat).astype(BF16)
        acc_scr[0, rows, :] = _dot(p, v) + qdec_f * _dot(qb, st_scr[0].astype(BF16))
        st_scr[0] = cdec_f * st_scr[0] + _dot_tn((k * kdec_f).astype(BF16), v)

        rows = chunk_rows(n_chunks - 1 - i)
        k = ks_scr[rows, :]
        acc_scr[1, rows, :] = qdec_b * _dot(qs_scr[rows, :], st_scr[1].astype(BF16))
        st_scr[1] = cdec_b * st_scr[1] + _dot_tn((k * kdec_b).astype(BF16), v_ref[0, rows, :])

    def finish(c):
        rows = chunk_rows(c)
        o = acc_scr[0, rows, :] + acc_scr[1, rows, :]
        gate = g_ref[0, rows, :].astype(F32)
        o_ref[0, rows, :] = (_head_norm(o, gn_ref[0]) * _silu(gate)).astype(o_ref.dtype)

    _two_ended_scan(n_chunks, prepare, step, finish)


def _retention(proj, lg, cos_t, sin_t, ret_gn, nb, seq):
    kq, kv = RET_DK, RET_DV
    grid_spec = dict(
        grid=(nb, RET_HEADS),
        in_specs=[
            pl.BlockSpec(memory_space=pltpu.SMEM),
            pl.BlockSpec((1, seq, kq), lambda b, h: (b, 0, h)),
            pl.BlockSpec((1, seq, kq), lambda b, h: (b, 0, RET_QK // kq + h)),
            pl.BlockSpec((1, seq, kv), lambda b, h: (b, 0, 2 * RET_QK // kv + h)),
            pl.BlockSpec((1, seq, kv), lambda b, h: (b, 0, (2 * RET_QK + RET_V) // kv + h)),
            pl.BlockSpec((seq, kq), lambda b, h: (0, 0)),
            pl.BlockSpec((seq, kq), lambda b, h: (0, 0)),
            pl.BlockSpec((1, 1, kv), lambda b, h: (h, 0, 0)),
        ],
        out_specs=pl.BlockSpec((1, seq, kv), lambda b, h: (b, 0, h)),
        scratch_shapes=[
            pltpu.VMEM((seq, kq), BF16),
            pltpu.VMEM((seq, kq), F32),
            pltpu.VMEM((2, seq, kv), F32),
            pltpu.VMEM((2, kq, kv), F32),
        ],
    )
    return pl.pallas_call(
        functools.partial(_ret_body, seq=seq),
        out_shape=jax.ShapeDtypeStruct((nb, seq, RET_V), BF16),
        **grid_spec,
        compiler_params=_cparams(("parallel", "parallel")),
        name="retention",
    )(lg, proj, proj, proj, proj, cos_t, sin_t, ret_gn.reshape(RET_HEADS, 1, kv))


def _col_of(mat, c):
    lane = lax.broadcasted_iota(I32, mat.shape, 1)
    return jnp.sum(jnp.where(lane == c, mat, 0.0), axis=1, keepdims=True)


def _mlstm_body(gb_ref, q_ref, k_ref, v_ref, o_gate_ref, wq_ref, wk_ref, bq_ref, bk_ref,
                gr_ref, gn_ref, o_ref,
                padq_scr, padk_scr, qs_scr, ks_scr, acc_scr, c_scr, n_scr, m_scr, row_scr, col_scr,
                *, seq):
    ln = MLSTM_CHUNK
    assert ln == CONV_ROWS
    n_chunks = seq // ln
    h = pl.program_id(1)

    _conv_fill(q_ref, padq_scr, seq)
    _conv_fill(k_ref, padk_scr, seq)

    def prepare(c):
        r0 = pl.multiple_of(c * ln, ln)
        rows = pl.ds(r0, ln)

        def emit_q(lane0, y):
            qs_scr[rows, lane0:lane0 + LANES] = (y * (MLSTM_DK ** -0.5)).astype(BF16)

        def emit_k(lane0, y):
            ks_scr[rows, lane0:lane0 + LANES] = y

        _conv_silu_rows(padq_scr, wq_ref, bq_ref, r0, emit_q)
        _conv_silu_rows(padk_scr, wk_ref, bk_ref, r0, emit_k)

    row_scr[...] = jnp.zeros_like(row_scr)
    for t in range(4):
        for c in range(n_chunks):
            row_scr[t, c:c + 1, :] = gr_ref[0, t:t + 1, c * ln:(c + 1) * ln]
    tri_le = _tri(ln, "le")
    tri_ge = _tri(ln, "ge")
    for d in range(2):
        ig = row_scr[2 * d] + gb_ref[(2 * d) * MLSTM_HEADS + h]
        lf = _log_sigmoid(row_scr[2 * d + 1] + gb_ref[(2 * d + 1) * MLSTM_HEADS + h])
        bc = _dot01_right(lf, tri_le if d == 0 else tri_ge)
        row_scr[2 * d] = ig
        row_scr[2 * d + 1] = bc
        col_scr[2 * d] = ig.T
        col_scr[2 * d + 1] = bc.T

    ri = lax.broadcasted_iota(I32, (ln, ln), 0)
    ci = lax.broadcasted_iota(I32, (ln, ln), 1)

    def chunk_step(c, d):
        r0 = pl.multiple_of(c * ln, ln)
        rows = pl.ds(r0, ln)
        qb = qs_scr[rows, :]
        kf = ks_scr[rows, :]
        v = v_ref[0, rows, :]
        i_row = row_scr[2 * d, pl.ds(c, 1), :]
        b_row = row_scr[2 * d + 1, pl.ds(c, 1), :]
        i_col = _col_of(col_scr[2 * d], c)
        b_col = _col_of(col_scr[2 * d + 1], c)
        m_st = m_scr[d]
        mask = (ri >= ci) if d == 0 else (ri <= ci)
        logd = jnp.where(mask, b_col - b_row + i_row, NEG_INF)
        m_inter = b_col + m_st
        m_row = jnp.maximum(m_inter, jnp.max(logd, axis=1, keepdims=True))
        sc = _dot_nt(qb, kf.astype(BF16)) * jnp.exp(logd - m_row)
        inter = jnp.exp(m_inter - m_row)
        num = _dot(sc.astype(BF16), v) + inter * _dot(qb, c_scr[d].astype(BF16))
        den = jnp.sum(sc, axis=1, keepdims=True) + inter * jnp.sum(
            qb.astype(F32) * n_scr[d], axis=1, keepdims=True)
        hh = num / jnp.maximum(jnp.abs(den), jnp.exp(-m_row))
        b_end = b_row[:, ln - 1:ln] if d == 0 else b_row[:, 0:1]
        logw = b_end - b_col + i_col
        m_new = jnp.maximum(b_end + m_st, jnp.max(logw, axis=0, keepdims=True))
        kw = kf * jnp.exp(logw - m_new)
        dec = jnp.exp(b_end + m_st - m_new)
        c_scr[d] = dec * c_scr[d] + _dot_tn(kw.astype(BF16), v)
        n_scr[d] = dec * n_scr[d] + jnp.sum(kw, axis=0, keepdims=True)
        m_scr[d] = m_new
        acc_scr[d, rows, :] = hh

    c_scr[...] = jnp.zeros_like(c_scr)
    n_scr[...] = jnp.zeros_like(n_scr)
    m_scr[...] = jnp.zeros_like(m_scr)

    def step(i):
        chunk_step(i, 0)
        chunk_step(n_chunks - 1 - i, 1)

    def finish(c):
        rows = pl.ds(pl.multiple_of(c * ln, ln), ln)
        y = _head_norm(acc_scr[0, rows, :] + acc_scr[1, rows, :], gn_ref[0])
        o_ref[0, rows, :] = (y * _sigmoid(o_gate_ref[0, rows, :].astype(F32))).astype(o_ref.dtype)

    _two_ended_scan(n_chunks, prepare, step, finish)


def _mlstm(proj, gate_rows, gate_b, conv_w, conv_b, mlstm_gn, nb, seq):
    kq, kv = MLSTM_DK, MLSTM_DV
    q0 = 2 * RET_QK + 2 * RET_V
    k0 = q0 + MLSTM_QK
    v0 = k0 + MLSTM_QK
    o0 = v0 + MLSTM_V
    grid_spec = dict(
        grid=(nb, MLSTM_HEADS),
        in_specs=[
            pl.BlockSpec(memory_space=pltpu.SMEM),
            pl.BlockSpec((1, seq, kq), lambda b, h: (b, 0, q0 // kq + h)),
            pl.BlockSpec((1, seq, kq), lambda b, h: (b, 0, k0 // kq + h)),
            pl.BlockSpec((1, seq, kv), lambda b, h: (b, 0, v0 // kv + h)),
            pl.BlockSpec((1, seq, kv), lambda b, h: (b, 0, o0 // kv + h)),
            pl.BlockSpec((CONV_W, kq), lambda b, h: (0, h)),
            pl.BlockSpec((CONV_W, kq), lambda b, h: (0, MLSTM_QK // kq + h)),
            pl.BlockSpec((1, kq), lambda b, h: (0, h)),
            pl.BlockSpec((1, kq), lambda b, h: (0, MLSTM_QK // kq + h)),
            pl.BlockSpec((1, GATE_ROWS, seq), lambda b, h: (b, h, 0)),
            pl.BlockSpec((1, 1, kv), lambda b, h: (h, 0, 0)),
        ],
        out_specs=pl.BlockSpec((1, seq, kv), lambda b, h: (b, 0, h)),
        scratch_shapes=[
            pltpu.VMEM((seq + 2 * CONV_HALO, kq), F32),
            pltpu.VMEM((seq + 2 * CONV_HALO, kq), F32),
            pltpu.VMEM((seq, kq), BF16),
            pltpu.VMEM((seq, kq), F32),
            pltpu.VMEM((2, seq, kv), F32),
            pltpu.VMEM((2, kq, kv), F32),
            pltpu.VMEM((2, 1, kq), F32),
            pltpu.VMEM((2, 1, 1), F32),
            pltpu.VMEM((4, LANES, MLSTM_CHUNK), F32),
            pltpu.VMEM((4, MLSTM_CHUNK, LANES), F32),
        ],
    )
    return pl.pallas_call(
        functools.partial(_mlstm_body, seq=seq),
        out_shape=jax.ShapeDtypeStruct((nb, seq, MLSTM_V), BF16),
        **grid_spec,
        compiler_params=_cparams(("parallel", "parallel")),
        name="mlstm",
    )(gate_b, proj, proj, proj, proj, conv_w, conv_w, conv_b.reshape(1, -1), conv_b.reshape(1, -1),
      gate_rows, mlstm_gn.reshape(MLSTM_HEADS, 1, kv))


def _ssd_body(z_ref, x_ref, b_ref, c_ref, wx_ref, wb_ref, wc_ref, bx_ref, bb_ref, bc_ref,
              dtf_ref, dtb_ref, bias_ref, alog_ref, dskip_ref, o_ref,
              padx_scr, padb_scr, padc_scr, xs_scr, bs_scr, bst_scr, cs_scr, y_scr, st_scr,
              acr_scr, dtr_scr, er_scr, ur_scr, act_scr, *, seq):
    ln = SSD_CHUNK
    assert ln == CONV_ROWS
    n_chunks = seq // ln
    hp = SSD_HEADDIM
    n_pairs = SSD_HPG // 2

    _conv_fill(x_ref, padx_scr, seq)
    _conv_fill(b_ref, padb_scr, seq)
    _conv_fill(c_ref, padc_scr, seq)

    def prepare(c):
        r0 = pl.multiple_of(c * ln, ln)
        rows = pl.ds(r0, ln)

        def emit_x(lane0, y):
            xs_scr[rows, lane0:lane0 + LANES] = y

        def emit_b(lane0, y):
            bs_scr[rows, :] = y.astype(BF16)
            bst_scr[:, rows] = y.T.astype(BF16)

        def emit_c(lane0, y):
            cs_scr[rows, :] = y.astype(BF16)

        _conv_silu_rows(padx_scr, wx_ref, bx_ref, r0, emit_x)
        _conv_silu_rows(padb_scr, wb_ref, bb_ref, r0, emit_b)
        _conv_silu_rows(padc_scr, wc_ref, bc_ref, r0, emit_c)

    for d, dt_ref in enumerate((dtf_ref, dtb_ref)):
        acr_scr[d] = jnp.zeros((LANES, ln), F32)
        for c in range(n_chunks):
            acr_scr[d, c * SSD_HPG:(c + 1) * SSD_HPG, :] = dt_ref[0, :, c * ln:(c + 1) * ln]
        dt = _softplus(acr_scr[d] + bias_ref[0, d])
        adt = dt * (-jnp.exp(alog_ref[0, d]))
        acum = _dot01_right(adt, _tri(ln, "le" if d == 0 else "ge"))
        a_end = acum[:, ln - 1:ln] if d == 0 else acum[:, 0:1]
        acr_scr[d] = acum
        dtr_scr[d] = dt
        er_scr[d] = jnp.exp(acum)
        ur_scr[d] = dt * jnp.exp(a_end - acum)
        act_scr[d] = acum.T

    ri = lax.broadcasted_iota(I32, (ln, ln), 0)
    ci = lax.broadcasted_iota(I32, (ln, ln), 1)
    in_first = lax.broadcasted_iota(I32, (1, LANES), 1) < hp
    on_diag = ri == ci

    def chunk_rows(c):
        return pl.ds(pl.multiple_of(c * ln, ln), ln)

    def dir_step(d, c):
        rows = chunk_rows(c)
        xb = xs_scr[rows, :].astype(BF16)
        bcm = bs_scr[rows, :]
        bct = bst_scr[:, rows].astype(F32)
        ccm = cs_scr[rows, :]
        cb = _dot_nt(ccm, bcm)
        carried = _dot(ccm, st_scr[d].astype(BF16)).astype(BF16)
        head_rows = pl.ds(pl.multiple_of(c * SSD_HPG, SSD_HPG), SSD_HPG)
        arow = acr_scr[d, head_rows, :]
        dtrow = dtr_scr[d, head_rows, :]
        erow = er_scr[d, head_rows, :]
        urow = ur_scr[d, head_rows, :]
        acols = pltpu.roll(act_scr[d], (LANES - c * SSD_HPG) & (LANES - 1), axis=1)
        mask = (ri >= ci) if d == 0 else (ri <= ci)
        end = ln - 1 if d == 0 else 0
        pieces = []
        for pair in range(n_pairs):
            lanes = slice(pair * LANES, (pair + 1) * LANES)
            lhs, rhs, lhs_state, keep = [], [], [], []
            for sub in range(2):
                k = 2 * pair + sub
                sel = in_first if sub == 0 else ~in_first
                dec = jnp.exp(jnp.where(mask, acols[:, k:k + 1] - arow[k:k + 1, :], NEG_INF))
                lhs.append((cb * dec * dtrow[k:k + 1, :]).astype(BF16))
                rhs.append(jnp.where(sel, xb[:, lanes], jnp.zeros((ln, LANES), BF16)))
                lhs_state.append((bct * urow[k:k + 1, :]).astype(BF16))
                keep.append(jnp.exp(arow[k:k + 1, end:end + 1]))
            for sub in range(2):
                k = 2 * pair + sub
                sel = in_first if sub == 0 else ~in_first
                lhs.append(jnp.where(on_diag, erow[k:k + 1, :], 0.0).astype(BF16))
                rhs.append(jnp.where(sel, carried[:, lanes], jnp.zeros((ln, LANES), BF16)))
            pieces.append(_dot(jnp.concatenate(lhs, axis=1), jnp.concatenate(rhs, axis=0)))
            st_scr[d, :, lanes] = (jnp.where(in_first, keep[0], keep[1]) * st_scr[d, :, lanes]
                                   + _dot(jnp.concatenate(lhs_state, axis=1),
                                          jnp.concatenate(rhs[:2], axis=0)))
        y_scr[d, rows, :] = jnp.concatenate(pieces, axis=1)

    st_scr[...] = jnp.zeros_like(st_scr)

    def step(i):
        dir_step(0, i)
        dir_step(1, n_chunks - 1 - i)

    def finish(c):
        rows = chunk_rows(c)
        y = y_scr[0, rows, :] + y_scr[1, rows, :] + dskip_ref[0] * xs_scr[rows, :]
        o_ref[0, rows, :] = (y * _silu(z_ref[0, rows, :].astype(F32))).astype(o_ref.dtype)

    _two_ended_scan(n_chunks, prepare, step, finish)


def _ssd(proj, dt_t, bias_col, alog_col, dskip_x, conv_w, conv_b, nb, seq):
    width = SSD_HPG * SSD_HEADDIM
    ns = SSD_STATE
    x0 = SSD_INNER
    b0 = 2 * SSD_INNER
    c0 = b0 + SSD_BC
    cb = conv_b.reshape(1, -1)
    return pl.pallas_call(
        functools.partial(_ssd_body, seq=seq),
        out_shape=jax.ShapeDtypeStruct((nb, seq, SSD_INNER), BF16),
        grid=(nb, SSD_GROUPS),
        in_specs=[
            pl.BlockSpec((1, seq, width), lambda b, g: (b, 0, g)),
            pl.BlockSpec((1, seq, width), lambda b, g: (b, 0, x0 // width + g)),
            pl.BlockSpec((1, seq, ns), lambda b, g: (b, 0, b0 // ns + g)),
            pl.BlockSpec((1, seq, ns), lambda b, g: (b, 0, c0 // ns + g)),
            pl.BlockSpec((CONV_W, width), lambda b, g: (0, g)),
            pl.BlockSpec((CONV_W, ns), lambda b, g: (0, SSD_INNER // ns + g)),
            pl.BlockSpec((CONV_W, ns), lambda b, g: (0, (SSD_INNER + SSD_BC) // ns + g)),
            pl.BlockSpec((1, width), lambda b, g: (0, g)),
            pl.BlockSpec((1, ns), lambda b, g: (0, SSD_INNER // ns + g)),
            pl.BlockSpec((1, ns), lambda b, g: (0, (SSD_INNER + SSD_BC) // ns + g)),
            pl.BlockSpec((1, SSD_HPG, seq), lambda b, g: (b, g, 0)),
            pl.BlockSpec((1, SSD_HPG, seq), lambda b, g: (b, SSD_GROUPS + g, 0)),
            pl.BlockSpec((1, 2, LANES, 1), lambda b, g: (g, 0, 0, 0)),
            pl.BlockSpec((1, 2, LANES, 1), lambda b, g: (g, 0, 0, 0)),
            pl.BlockSpec((1, 1, width), lambda b, g: (g, 0, 0)),
        ],
        out_specs=pl.BlockSpec((1, seq, width), lambda b, g: (b, 0, g)),
        scratch_shapes=[
            pltpu.VMEM((seq + 2 * CONV_HALO, width), F32),
            pltpu.VMEM((seq + 2 * CONV_HALO, ns), F32),
            pltpu.VMEM((seq + 2 * CONV_HALO, ns), F32),
            pltpu.VMEM((seq, width), F32),
            pltpu.VMEM((seq, ns), BF16),
            pltpu.VMEM((ns, seq), BF16),
            pltpu.VMEM((seq, ns), BF16),
            pltpu.VMEM((2, seq, width), F32),
            pltpu.VMEM((2, ns, width), F32),
            pltpu.VMEM((2, LANES, SSD_CHUNK), F32),
            pltpu.VMEM((2, LANES, SSD_CHUNK), F32),
            pltpu.VMEM((2, LANES, SSD_CHUNK), F32),
            pltpu.VMEM((2, LANES, SSD_CHUNK), F32),
            pltpu.VMEM((2, SSD_CHUNK, LANES), F32),
        ],
        compiler_params=_cparams(("parallel", "parallel")),
        name="ssd",
    )(proj, proj, proj, proj, conv_w, conv_w, conv_w, cb, cb, cb,
      dt_t, dt_t, bias_col, alog_col, dskip_x)


META_E = 0
META_G = 2
META_R = 4
ROUTE_E0 = MOE_GROUPS


def _router_body(x_ref, g_ref, sc_ref, sh_ref, w_ref, b_ref, h_ref, meta_ref, cnt_ref, carry_scr):
    tm = x_ref.shape[0]

    @pl.when(pl.program_id(0) == 0)
    def _():
        carry_scr[...] = jnp.zeros_like(carry_scr)

    y = _rms(x_ref[...], g_ref[...]) * (1.0 + sc_ref[0]) + sh_ref[0]
    h_ref[...] = y
    h_hi = y.astype(BF16)
    h_lo = (y - h_hi.astype(F32)).astype(BF16)
    w = w_ref[...]
    w_hi = w.astype(BF16)
    w_lo = (w - w_hi.astype(F32)).astype(BF16)
    logits = _dot(h_hi, w_hi) + _dot(h_lo, w_hi) + _dot(h_hi, w_lo) + b_ref[...]

    lane = lax.broadcasted_iota(I32, (tm, LANES), 1)
    lane_f = lane.astype(F32)
    big = float(LANES)
    is_grp = lane < MOE_GROUPS
    gl = jnp.where(is_grp, logits, NEG_INF)
    gmax = jnp.max(gl, axis=1, keepdims=True)
    gidx = jnp.min(jnp.where(gl == gmax, lane_f, big), axis=1, keepdims=True)
    gprob = 1.0 / jnp.sum(jnp.where(is_grp, jnp.exp(gl - gmax), 0.0), axis=1, keepdims=True)

    el = lane - ROUTE_E0
    el_f = el.astype(F32)
    valid = (el >= 0) & (el < MOE_EXPERTS)
    in_grp = valid & (_shr(el, MOE_EPG).astype(F32) == gidx)
    ev = jnp.where(in_grp, logits, NEG_INF)
    v1 = jnp.max(ev, axis=1, keepdims=True)
    i1 = jnp.min(jnp.where(ev == v1, el_f, big), axis=1, keepdims=True)
    ev2 = jnp.where(el_f == i1, NEG_INF, ev)
    v2 = jnp.max(ev2, axis=1, keepdims=True)
    i2 = jnp.min(jnp.where(ev2 == v2, el_f, big), axis=1, keepdims=True)
    p2 = jnp.exp(v2 - v1)
    s1 = 1.0 / (1.0 + p2)
    gate1 = s1 * gprob
    gate2 = p2 * s1 * gprob

    oh1 = jnp.where(el_f == i1, 1.0, 0.0)
    oh2 = jnp.where(el_f == i2, 1.0, 0.0)
    oh = oh1 + oh2
    before = _dot(_tri(tm, "gt"), oh.astype(BF16)) + carry_scr[...]
    rank1 = jnp.sum(oh1 * before, axis=1, keepdims=True)
    rank2 = jnp.sum(oh2 * before, axis=1, keepdims=True)
    carry_scr[...] = carry_scr[...] + jnp.sum(oh, axis=0, keepdims=True)
    cnt_ref[...] = jnp.broadcast_to(carry_scr[...], cnt_ref.shape)

    meta = jnp.zeros((tm, LANES), F32)
    for col, val in ((META_E, i1), (META_E + 1, i2), (META_G, gate1), (META_G + 1, gate2),
                     (META_R, rank1), (META_R + 1, rank2)):
        meta = jnp.where(lane == col, val, meta)
    meta_ref[...] = meta


def _router(x, g, sc, sh, w_route, b_route, seq, tm=512):
    t, k = x.shape
    tm = min(tm, seq)
    tps = seq // tm
    return pl.pallas_call(
        _router_body,
        out_shape=[jax.ShapeDtypeStruct((t, k), F32),
                   jax.ShapeDtypeStruct((t, LANES), F32),
                   jax.ShapeDtypeStruct((8, LANES), F32)],
        grid=(t // tm,),
        in_specs=[
            pl.BlockSpec((tm, k), lambda i: (i, 0)),
            pl.BlockSpec((1, k), lambda i: (0, 0)),
            pl.BlockSpec((1, 1, k), lambda i: (i // tps, 0, 0)),
            pl.BlockSpec((1, 1, k), lambda i: (i // tps, 0, 0)),
            pl.BlockSpec((k, LANES), lambda i: (0, 0)),
            pl.BlockSpec((1, LANES), lambda i: (0, 0)),
        ],
        out_specs=[pl.BlockSpec((tm, k), lambda i: (i, 0)),
                   pl.BlockSpec((tm, LANES), lambda i: (i, 0)),
                   pl.BlockSpec((8, LANES), lambda i: (0, 0))],
        scratch_shapes=[pltpu.VMEM((1, LANES), F32)],
        compiler_params=_cparams(("arbitrary",)),
        name="router",
    )(x, g.reshape(1, k), sc, sh, w_route, b_route)


def _row_copy(src_hbm, dst, sem, src_row, dst_row):
    return pltpu.make_async_copy(src_hbm.at[pl.ds(src_row, 1)], dst.at[pl.ds(dst_row, 1)], sem)


def _start_row_gather(idx_ref, base, n_rows, stride, src_hbm, dst, sem):
    def body(r, carry):
        _row_copy(src_hbm, dst, sem, idx_ref[base + r * stride], r).start()
        return carry
    lax.fori_loop(0, n_rows, body, 0, unroll=GATHER_UNROLL)


def _wait_row_gather(src_hbm, dst, sem, n_rows):
    pltpu.make_async_copy(src_hbm.at[pl.ds(0, n_rows)], dst, sem).wait()


def _expert_body(blk_exp_ref, row_tok_ref, h_hbm, wg_ref, wu_ref, wd_ref, y_ref,
                 xb_scr, wgu_scr, wdn_scr, sem):
    i = pl.program_id(0)
    n_steps = pl.num_programs(0)
    slot = i % 2
    d = xb_scr.shape[2]

    def start(block, s):
        _start_row_gather(row_tok_ref, block * MOE_BLOCK, MOE_BLOCK, 1, h_hbm, xb_scr.at[s], sem.at[s])

    @pl.when(i == 0)
    def _():
        start(0, 0)

    _wait_row_gather(h_hbm, xb_scr.at[slot], sem.at[slot], MOE_BLOCK)

    @pl.when(i + 1 < n_steps)
    def _():
        start(i + 1, 1 - slot)

    @pl.when((i == 0) | (blk_exp_ref[i] != blk_exp_ref[jnp.maximum(i - 1, 0)]))
    def _():
        def cast_up(c, carry):
            rows = pl.ds(pl.multiple_of(c * CAST_ROWS, CAST_ROWS), CAST_ROWS)
            wgu_scr[rows, :EXPERT_FF] = wg_ref[0, 0, rows, :].astype(BF16)
            wgu_scr[rows, EXPERT_FF:] = wu_ref[0, 0, rows, :].astype(BF16)
            return carry

        def cast_down(c, carry):
            rows = pl.ds(pl.multiple_of(c * CAST_ROWS, CAST_ROWS), CAST_ROWS)
            wdn_scr[rows, :] = wd_ref[0, 0, rows, :].astype(BF16)
            return carry

        lax.fori_loop(0, d // CAST_ROWS, cast_up, 0)
        lax.fori_loop(0, EXPERT_FF // CAST_ROWS, cast_down, 0)

    a = _dot(xb_scr[slot].astype(BF16), wgu_scr[...])
    hid = (_silu(a[:, :EXPERT_FF]) * a[:, EXPERT_FF:]).astype(BF16)
    y_ref[...] = _dot(hid, wdn_scr[...])


def _experts(h, w_gate, w_up, w_down, layer, blk_exp, row_tok):
    t, d = h.shape
    n_rows = row_tok.shape[0]
    n_blocks = n_rows // MOE_BLOCK
    grid_spec = pltpu.PrefetchScalarGridSpec(
        num_scalar_prefetch=2,
        grid=(n_blocks,),
        in_specs=[
            pl.BlockSpec(memory_space=pl.ANY),
            pl.BlockSpec((1, 1, d, EXPERT_FF), lambda i, be, rt: (layer, be[i], 0, 0)),
            pl.BlockSpec((1, 1, d, EXPERT_FF), lambda i, be, rt: (layer, be[i], 0, 0)),
            pl.BlockSpec((1, 1, EXPERT_FF, d), lambda i, be, rt: (layer, be[i], 0, 0)),
        ],
        out_specs=pl.BlockSpec((MOE_BLOCK, d), lambda i, be, rt: (i, 0)),
        scratch_shapes=[pltpu.VMEM((2, MOE_BLOCK, d), F32),
                        pltpu.VMEM((d, 2 * EXPERT_FF), BF16),
                        pltpu.VMEM((EXPERT_FF, d), BF16),
                        pltpu.SemaphoreType.DMA((2,))],
    )
    return pl.pallas_call(
        _expert_body,
        out_shape=jax.ShapeDtypeStruct((n_rows, d), F32),
        grid_spec=grid_spec,
        compiler_params=_cparams(("arbitrary",)),
        name="experts",
    )(blk_exp, row_tok, h, w_gate, w_up, w_down)


def _combine_body(dest_ref, y_hbm, x_ref, gate_ref, meta_ref, fn_ref, o_ref, ya_scr, sem,
                  *, final, tile0):
    i = pl.program_id(0)
    n_steps = pl.num_programs(0)
    tm = x_ref.shape[0]
    slot = i % 2

    def start(step, s):
        for j in range(2):
            _start_row_gather(dest_ref, (tile0 + step) * tm * 2 + j, tm, 2, y_hbm,
                              ya_scr.at[s, j], sem.at[s, j])

    @pl.when(i == 0)
    def _():
        start(0, 0)

    for j in range(2):
        _wait_row_gather(y_hbm, ya_scr.at[slot, j], sem.at[slot, j], tm)

    @pl.when(i + 1 < n_steps)
    def _():
        start(i + 1, 1 - slot)

    meta = meta_ref[...]
    moe = (ya_scr[slot, 0] * meta[:, META_G:META_G + 1]
           + ya_scr[slot, 1] * meta[:, META_G + 1:META_G + 2])
    out = x_ref[...] + gate_ref[0] * moe
    if final:
        out = _rms(out, fn_ref[...])
    o_ref[...] = out


def _combine(y, x, gate, meta, dest, final_norm, seq, final, tok0=0, n_tok=None, tm=256):
    t, d = x.shape
    n_tok = t if n_tok is None else n_tok
    tm = min(tm, seq)
    tps = seq // tm
    assert tok0 % seq == 0 and n_tok % seq == 0
    tile0 = tok0 // tm
    grid_spec = pltpu.PrefetchScalarGridSpec(
        num_scalar_prefetch=1,
        grid=(n_tok // tm,),
        in_specs=[
            pl.BlockSpec(memory_space=pl.ANY),
            pl.BlockSpec((tm, d), lambda i, ds: (tile0 + i, 0)),
            pl.BlockSpec((1, 1, d), lambda i, ds: ((tile0 + i) // tps, 0, 0)),
            pl.BlockSpec((tm, LANES), lambda i, ds: (tile0 + i, 0)),
            pl.BlockSpec((1, d), lambda i, ds: (0, 0)),
        ],
        out_specs=pl.BlockSpec((tm, d), lambda i, ds: (i, 0)),
        scratch_shapes=[pltpu.VMEM((2, 2, tm, d), F32), pltpu.SemaphoreType.DMA((2, 2))],
    )
    return pl.pallas_call(
        functools.partial(_combine_body, final=final, tile0=tile0),
        out_shape=jax.ShapeDtypeStruct((n_tok, d), F32),
        grid_spec=grid_spec,
        compiler_params=_cparams(("arbitrary",)),
        name="moe_combine",
    )(dest, y, x, gate, meta, final_norm.reshape(1, d))


def _moe_layer(x, g, sc, sh, gate, grp_w, grp_b, exp_w, exp_b, w_gate, w_up, w_down, layer,
               final_norm, seq, final, split=None):
    t, d = x.shape
    pad = LANES - MOE_GROUPS - MOE_EXPERTS
    w_route = jnp.concatenate([grp_w, exp_w, jnp.zeros((d, pad), F32)], axis=1)
    b_route = jnp.concatenate([grp_b, exp_b, jnp.zeros((pad,), F32)]).reshape(1, LANES)
    h, meta, cnt = _router(x, g, sc, sh, w_route, b_route, seq)

    expert = meta[:, META_E:META_E + 2].astype(I32)
    rank = meta[:, META_R:META_R + 2].astype(I32)
    counts = cnt[0, ROUTE_E0:ROUTE_E0 + MOE_EXPERTS].astype(I32)
    padded = (counts + MOE_BLOCK - 1) // MOE_BLOCK * MOE_BLOCK
    p_ends = jnp.cumsum(padded)
    p_starts = p_ends - padded
    dest = (p_starts[expert] + rank).reshape(-1)
    n_rows = t * 2 + MOE_EXPERTS * MOE_BLOCK
    n_blocks = n_rows // MOE_BLOCK
    token_id = jnp.repeat(jnp.arange(t, dtype=I32), 2)
    row_tok = jnp.zeros((n_rows,), I32).at[dest].set(token_id)
    blk_start = jnp.arange(n_blocks, dtype=I32) * MOE_BLOCK
    blk_exp = jnp.minimum(jnp.sum((p_ends[None, :] <= blk_start[:, None]).astype(I32), axis=1),
                          MOE_EXPERTS - 1)

    y = _experts(h, w_gate, w_up, w_down, layer, blk_exp, row_tok)
    if split is None:
        return _combine(y, x, gate, meta, dest, final_norm, seq, final)
    return tuple(_combine(y, x, gate, meta, dest, final_norm, seq, final, tok0=a, n_tok=b - a)
                 for a, b in ((0, split), (split, t)))


def _rope_tables(seq):
    half = RET_DK // 2
    inv = ROPE_BASE ** (-jnp.arange(half, dtype=F32) / half)
    ang = jnp.arange(seq, dtype=F32)[:, None] * inv[None, :]
    cos, sin = jnp.cos(ang), jnp.sin(ang)
    return jnp.concatenate([cos, cos], axis=1), jnp.concatenate([-sin, sin], axis=1)


def _pad_rows(a, axis, n):
    pad = [(0, 0)] * a.ndim
    pad[axis] = (0, n - a.shape[axis])
    return jnp.pad(a, pad)


def kernel(x_prompt, x_sample, c_prompt, c_sample, ada_w, ada_b, norm1, norm2, ev_w_in, ev_gate_b, ev_conv_w, ev_conv_b, ev_ret_gn, ev_mlstm_gn, ev_w_out, od_w_in, od_conv_w, od_conv_b, od_dt_bias, od_a_log, od_d_skip, od_norm, od_w_out, moe_grp_w, moe_grp_b, moe_exp_w, moe_exp_b, moe_w_gate, moe_w_up, moe_w_down, final_norm):
    n_prompt = x_prompt.shape[0]
    seq, d = x_prompt.shape[1], x_prompt.shape[2]
    assert x_sample.shape[1] == seq and d == D_MODEL
    assert seq % RET_CHUNK == 0 and seq // MLSTM_CHUNK <= LANES // SSD_HPG
    x = jnp.concatenate([x_prompt, x_sample], axis=0)
    nb = x.shape[0]
    t = nb * seq
    x = x.reshape(t, d)
    depth = ada_w.shape[0]

    c_all = jnp.concatenate([c_prompt, c_sample], axis=0)
    c_pad = _pad_rows(c_all, 0, -(-nb // 8) * 8)
    mod = _modulation(c_pad, ada_w, ada_b)[:, :nb].reshape(depth, nb, N_MOD, 1, d)

    heads = jnp.arange(RET_HEADS, dtype=F32)
    lg = jnp.stack([jnp.log1p(-jnp.exp2(-RET_DECAY_FWD - heads)),
                    jnp.log1p(-jnp.exp2(-RET_DECAY_BWD - heads))])
    cos_t, sin_t = _rope_tables(seq)

    for i in range(depth):
        sh1, sc1, g1, sh2, sc2, g2 = (mod[i, :, m] for m in range(N_MOD))
        j = i // 2
        if i % 2 == 0:
            w_in = ev_w_in[j]
            w_side = w_in[:, EVEN_MAIN:].reshape(d, 4, MLSTM_HEADS).transpose(0, 2, 1)
            w_side = _pad_rows(_pad_rows(w_side, 2, GATE_ROWS).reshape(d, -1), 1, LANES)
            proj, gates = _fused_matmul(x, w_in[:, :EVEN_MAIN].astype(BF16), seq=seq, prologue="normmod",
                                        g=norm1[i], sc=sc1, sh=sh1, w_side=w_side, name="even_in_proj")
            proj = proj.reshape(nb, seq, EVEN_MAIN)
            ret = _retention(proj, lg, cos_t, sin_t, ev_ret_gn[j], nb, seq)
            ml = _mlstm(proj, gates, ev_gate_b[j], ev_conv_w[j], ev_conv_b[j], ev_mlstm_gn[j], nb, seq)
            x = _fused_matmul(ret.reshape(t, RET_V), ev_w_out[j].astype(BF16), seq=seq,
                              x2=ml.reshape(t, MLSTM_V),
                              res=x, gate=g1, tn=512, name="even_out_proj")
        else:
            w_in = od_w_in[j]
            proj, dt_raw = _fused_matmul(x, w_in[:, :ODD_MAIN].astype(BF16), seq=seq, prologue="normmod",
                                         g=norm1[i], sc=sc1, sh=sh1, w_side=w_in[:, ODD_MAIN:],
                                         name="odd_in_proj")
            def per_row(p):
                p = p.reshape(2, SSD_GROUPS, SSD_HPG).transpose(1, 0, 2)
                return jnp.tile(p, (1, 1, LANES // SSD_HPG))[..., None]

            def per_lane(p):
                p = p.reshape(*p.shape[:-1], SSD_GROUPS, SSD_HPG)
                p = jnp.moveaxis(p, -2, 0)
                return jnp.repeat(p, SSD_HEADDIM, axis=-1)[..., None, :]

            y = _ssd(proj.reshape(nb, seq, ODD_MAIN), dt_raw, per_row(od_dt_bias[j]), per_row(od_a_log[j]),
                     per_lane(od_d_skip[j]), od_conv_w[j], od_conv_b[j], nb, seq)
            x = _fused_matmul(y.reshape(t, SSD_INNER), od_w_out[j].astype(BF16), seq=seq, prologue="norm",
                              g=od_norm[j], res=x, gate=g1, tn=512, name="odd_out_proj")
        last = i == depth - 1
        x = _moe_layer(x, norm2[i], sc2, sh2, g2, moe_grp_w[i], moe_grp_b[i], moe_exp_w[i], moe_exp_b[i],
                       moe_w_gate, moe_w_up, moe_w_down, i, final_norm, seq, final=last,
                       split=n_prompt * seq if last else None)
    y_prompt, y_sample = x
    return (y_prompt.reshape(n_prompt, seq, d), y_sample.reshape(nb - n_prompt, seq, d))
```

```python
import functools
import math

import jax
import jax.numpy as jnp
import numpy as np
from jax import lax
from jax.experimental import pallas as pl
from jax.experimental.pallas import tpu as pltpu

F32 = jnp.float32
BF16 = jnp.bfloat16
I32 = jnp.int32

D_MODEL = 2048
N_MOD = 6
EPS = 1e-6
CONV_W = 5
CONV_HALO = 8

RET_HEADS = 8
RET_DV = D_MODEL // RET_HEADS
RET_DK = RET_DV // 2
RET_DECAY_FWD = 5.0
RET_DECAY_BWD = 5.5
ROPE_BASE = 10000.0
RET_CHUNK = 256
MLSTM_HEADS = 4
MLSTM_DV = D_MODEL // MLSTM_HEADS
MLSTM_DK = MLSTM_DV // 2
MLSTM_CHUNK = 128
GATE_ROWS = 8
SSD_INNER = 2 * D_MODEL
SSD_HEADDIM = 64
SSD_HEADS = SSD_INNER // SSD_HEADDIM
SSD_GROUPS = 8
SSD_HPG = SSD_HEADS // SSD_GROUPS
SSD_STATE = 128
SSD_CHUNK = 128
MOE_GROUPS = 4
MOE_EPG = 8
MOE_EXPERTS = MOE_GROUPS * MOE_EPG
EXPERT_FF = D_MODEL // 4
MOE_BLOCK = 128

RET_QK = RET_HEADS * RET_DK
RET_V = RET_HEADS * RET_DV
MLSTM_QK = MLSTM_HEADS * MLSTM_DK
MLSTM_V = MLSTM_HEADS * MLSTM_DV
MLSTM_NGATE = 4 * MLSTM_HEADS
EVEN_MAIN = 2 * RET_QK + 2 * RET_V + 2 * MLSTM_QK + 2 * MLSTM_V
EVEN_MIX = RET_V + MLSTM_V
SSD_BC = SSD_GROUPS * SSD_STATE
SSD_CONV_CH = SSD_INNER + 2 * SSD_BC
ODD_MAIN = SSD_INNER + SSD_CONV_CH

PROLOGUE_ROWS = 256
CAST_ROWS = 256
GATHER_UNROLL = 32
SCAN_UNROLL = 2
LANES = 128
VMEM_LIMIT = 56 * 1024 * 1024

NEG_INF = float("-inf")


def _cparams(sem, vmem=VMEM_LIMIT):
    return pltpu.CompilerParams(dimension_semantics=sem, vmem_limit_bytes=vmem)


def _split3(x):
    hi = x.astype(BF16)
    r = x - hi.astype(F32)
    mid = r.astype(BF16)
    lo = (r - mid.astype(F32)).astype(BF16)
    return hi, mid, lo


def _dot(a, b):
    return jnp.dot(a, b, preferred_element_type=F32)


def _dot_nt(a, b):
    return lax.dot_general(a, b, (((1,), (1,)), ((), ())), preferred_element_type=F32)


def _dot_tn(a, b):
    return lax.dot_general(a, b, (((0,), (0,)), ((), ())), preferred_element_type=F32)


def _dot01_left(m01, x):
    hi, mid, lo = _split3(x)
    return _dot(m01, hi) + _dot(m01, mid) + _dot(m01, lo)


def _dot01_right(x, m01):
    hi, mid, lo = _split3(x)
    return _dot(hi, m01) + _dot(mid, m01) + _dot(lo, m01)


def _tri(n, kind):
    r = lax.broadcasted_iota(I32, (n, n), 0)
    c = lax.broadcasted_iota(I32, (n, n), 1)
    m = {"le": r <= c, "ge": r >= c, "gt": r > c}[kind]
    return jnp.where(m, 1.0, 0.0).astype(BF16)


def _shr(x, pow2):
    return lax.shift_right_arithmetic(x, jnp.int32(int(math.log2(pow2))))


def _sigmoid(x):
    return 1.0 / (1.0 + jnp.exp(-x))


def _silu(x):
    return x * _sigmoid(x)


def _softplus(x):
    return jnp.maximum(x, 0.0) + jnp.log1p(jnp.exp(-jnp.abs(x)))


def _log_sigmoid(x):
    return jnp.minimum(x, 0.0) - jnp.log1p(jnp.exp(-jnp.abs(x)))


def _rms(x, g):
    ms = jnp.mean(x * x, axis=-1, keepdims=True)
    return x * lax.rsqrt(ms + EPS) * g


def _head_norm(y, g):
    mu = jnp.mean(y, axis=-1, keepdims=True)
    yc = y - mu
    var = jnp.mean(yc * yc, axis=-1, keepdims=True)
    return yc * lax.rsqrt(var + EPS) * g


def _mod_body(c_ref, w_ref, b_ref, o_ref):
    c = c_ref[...]
    o_ref[0] = _dot(_silu(c).astype(BF16), w_ref[0].astype(BF16)) + b_ref[0]


def _modulation(c_pad, ada_w, ada_b):
    depth, d, n = ada_w.shape
    m = c_pad.shape[0]
    tn = 1024
    return pl.pallas_call(
        _mod_body,
        out_shape=jax.ShapeDtypeStruct((depth, m, n), F32),
        grid=(depth, n // tn),
        in_specs=[
            pl.BlockSpec((m, d), lambda l, j: (0, 0)),
            pl.BlockSpec((1, d, tn), lambda l, j: (l, 0, j)),
            pl.BlockSpec((1, 1, tn), lambda l, j: (l, 0, j)),
        ],
        out_specs=pl.BlockSpec((1, m, tn), lambda l, j: (l, 0, j)),
        compiler_params=_cparams(("parallel", "parallel")),
        name="modulation",
    )(c_pad, ada_w, ada_b.reshape(depth, 1, n))


def _mm_body(*refs, prologue, epilogue, side, two_lhs):
    it = iter(refs)
    x_ref = next(it)
    x2_ref = next(it) if two_lhs else None
    g_ref = next(it) if prologue != "none" else None
    sc_ref = next(it) if prologue == "normmod" else None
    sh_ref = next(it) if prologue == "normmod" else None
    w_ref = next(it)
    ws_ref = next(it) if side else None
    res_ref = next(it) if epilogue == "residual" else None
    gate_ref = next(it) if epilogue == "residual" else None
    o_ref = next(it)
    os_ref = next(it) if side else None
    h_scr = next(it) if prologue != "none" else None

    if prologue != "none":
        @pl.when(pl.program_id(1) == 0)
        def _():
            rows_per = PROLOGUE_ROWS

            def chunk(i, carry):
                rows = pl.ds(pl.multiple_of(i * rows_per, rows_per), rows_per)
                y = _rms(x_ref[rows, :].astype(F32), g_ref[...])
                if prologue == "normmod":
                    y = y * (1.0 + sc_ref[0]) + sh_ref[0]
                hb = y.astype(BF16)
                h_scr[rows, :] = hb
                if side:
                    h_lo = (y - hb.astype(F32)).astype(BF16)
                    ws = ws_ref[...]
                    w_hi = ws.astype(BF16)
                    w_lo = (ws - w_hi.astype(F32)).astype(BF16)
                    os_ref[0, :, rows] = (_dot(hb, w_hi) + _dot(h_lo, w_hi) + _dot(hb, w_lo)).T
                return carry

            lax.fori_loop(0, x_ref.shape[0] // rows_per, chunk, 0)
        lhs = h_scr[...]
    else:
        lhs = x_ref[...]
    if two_lhs:
        k1 = x_ref.shape[1]
        acc = _dot(lhs, w_ref[:k1, :]) + _dot(x2_ref[...], w_ref[k1:, :])
    else:
        acc = _dot(lhs, w_ref[...])
    if epilogue == "residual":
        o_ref[...] = res_ref[...] + gate_ref[0] * acc
    else:
        o_ref[...] = acc.astype(o_ref.dtype)


def _fused_matmul(x, w, *, seq, x2=None, prologue="none", g=None, sc=None, sh=None, w_side=None,
                  res=None, gate=None, out_dtype=BF16, tm=1024, tn=1024, name="proj"):
    t, k = x.shape
    n = w.shape[1]
    tm = min(tm, seq)
    tn = min(tn, n)
    assert t % tm == 0 and seq % tm == 0 and n % tn == 0
    tps = seq // tm
    epilogue = "residual" if res is not None else "plain"
    side = w_side is not None
    two_lhs = x2 is not None
    assert not (two_lhs and prologue != "none")
    in_specs = [pl.BlockSpec((tm, k), lambda i, j: (i, 0))]
    args = [x]
    if two_lhs:
        in_specs.append(pl.BlockSpec((tm, x2.shape[1]), lambda i, j: (i, 0)))
        args.append(x2)
        k = k + x2.shape[1]
    if prologue != "none":
        in_specs.append(pl.BlockSpec((1, k), lambda i, j: (0, 0)))
        args.append(g.reshape(1, k))
    if prologue == "normmod":
        in_specs += [pl.BlockSpec((1, 1, k), lambda i, j: (i // tps, 0, 0))] * 2
        args += [sc, sh]
    in_specs.append(pl.BlockSpec((k, tn), lambda i, j: (0, j)))
    args.append(w)
    if side:
        in_specs.append(pl.BlockSpec((k, LANES), lambda i, j: (0, 0)))
        args.append(w_side)
    if epilogue == "residual":
        in_specs += [pl.BlockSpec((tm, tn), lambda i, j: (i, j)),
                     pl.BlockSpec((1, 1, tn), lambda i, j: (i // tps, 0, j))]
        args += [res, gate]
        out_dtype = F32
    out_shape = [jax.ShapeDtypeStruct((t, n), out_dtype)]
    out_specs = [pl.BlockSpec((tm, tn), lambda i, j: (i, j))]
    if side:
        out_shape.append(jax.ShapeDtypeStruct((t // seq, LANES, seq), F32))
        out_specs.append(pl.BlockSpec((1, LANES, tm), lambda i, j: (i // tps, 0, i % tps)))
    scratch = [pltpu.VMEM((tm, k), BF16)] if prologue != "none" else []
    outs = pl.pallas_call(
        functools.partial(_mm_body, prologue=prologue, epilogue=epilogue, side=side, two_lhs=two_lhs),
        out_shape=out_shape,
        grid=(t // tm, n // tn),
        in_specs=in_specs,
        out_specs=out_specs,
        scratch_shapes=scratch,
        compiler_params=_cparams(("parallel", "arbitrary")),
        name=name,
    )(*args)
    return outs if side else outs[0]


CONV_ROWS = 128


def _conv_fill(src_ref, pad_scr, seq):
    ch = pad_scr.shape[1]
    halo = CONV_HALO
    rows = CONV_ROWS
    zeros = jnp.zeros((halo, ch), F32)
    pad_scr[pl.ds(0, halo), :] = zeros
    pad_scr[pl.ds(seq + halo, halo), :] = zeros

    def fill(i, carry):
        r0 = pl.multiple_of(i * rows, rows)
        pad_scr[pl.ds(pl.multiple_of(r0 + halo, halo), rows), :] = src_ref[0, pl.ds(r0, rows), :].astype(F32)
        return carry

    lax.fori_loop(0, seq // rows, fill, 0)


def _conv_silu_rows(pad_scr, w_ref, b_ref, r0, emit):
    ch = pad_scr.shape[1]
    halo = CONV_HALO
    rows = CONV_ROWS
    win = rows + 2 * halo
    half = (CONV_W - 1) // 2
    for lane0 in range(0, ch, LANES):
        cols = slice(lane0, lane0 + LANES)
        window = pad_scr[pl.ds(r0, win), cols]
        acc = jnp.zeros((rows, LANES), F32) + b_ref[:, cols]
        for j in range(CONV_W):
            d = j - half
            shifted = window if d == 0 else pltpu.roll(window, (-d) % win, axis=0)
            acc = acc + w_ref[j:j + 1, cols] * shifted[halo:halo + rows, :]
        emit(lane0, _silu(acc))


def _two_ended_scan(n_chunks, prepare, step, finish):
    assert n_chunks % 2 == 0
    half = n_chunks // 2

    def first(i, carry):
        prepare(i)
        prepare(n_chunks - 1 - i)
        step(i)
        return carry

    def second(i, carry):
        step(i)
        finish(i)
        finish(n_chunks - 1 - i)
        return carry

    unroll = SCAN_UNROLL if half % SCAN_UNROLL == 0 else 1
    lax.fori_loop(0, half, first, 0, unroll=unroll)
    lax.fori_loop(half, n_chunks, second, 0, unroll=unroll)


def _ret_body(lg_ref, q_ref, k_ref, v_ref, g_ref, cos_ref, sin_ref, gn_ref, o_ref,
              qs_scr, ks_scr, acc_scr, st_scr, *, seq):
    c_len = RET_CHUNK
    n_chunks = seq // c_len
    h = pl.program_id(1)
    lgf = lg_ref[0, h]
    lgb = lg_ref[1, h]
    ri = lax.broadcasted_iota(I32, (c_len, c_len), 0)
    ci = lax.broadcasted_iota(I32, (c_len, c_len), 1)
    diff = (ri - ci).astype(F32)
    dmat = jnp.exp(jnp.where(diff >= 0, lgf * diff, -lgb * diff))
    pos = lax.broadcasted_iota(I32, (c_len, 1), 0).astype(F32)
    qdec_f = jnp.exp(lgf * (pos + 1.0))
    kdec_f = jnp.exp(lgf * (c_len - 1.0 - pos))
    cdec_f = jnp.exp(jnp.full((1, 1), c_len, F32) * lgf)
    qdec_b = jnp.exp(lgb * (c_len - pos))
    kdec_b = jnp.exp(lgb * pos)
    cdec_b = jnp.exp(jnp.full((1, 1), c_len, F32) * lgb)
    half = RET_DK // 2

    def rope(x, rows):
        return x * cos_ref[rows, :] + pltpu.roll(x, half, axis=1) * sin_ref[rows, :]

    def chunk_rows(c):
        return pl.ds(pl.multiple_of(c * c_len, c_len), c_len)

    def prepare(c):
        rows = chunk_rows(c)
        qs_scr[rows, :] = rope(q_ref[0, rows, :].astype(F32), rows).astype(BF16)
        ks_scr[rows, :] = rope(k_ref[0, rows, :].astype(F32), rows) * (RET_DK ** -0.5)

    st_scr[...] = jnp.zeros_like(st_scr)

    def step(i):
        rows = chunk_rows(i)
        qb = qs_scr[rows, :]
        k = ks_scr[rows, :]
        v = v_ref[0, rows, :]
        p = (_dot_nt(qb, k.astype(BF16)) * dmat).astype(BF16)
        acc_scr[0, rows, :] = _dot(p, v) + qdec_f * _dot(qb, st_scr[0].astype(BF16))
        st_scr[0] = cdec_f * st_scr[0] + _dot_tn((k * kdec_f).astype(BF16), v)

        rows = chunk_rows(n_chunks - 1 - i)
        k = ks_scr[rows, :]
        acc_scr[1, rows, :] = qdec_b * _dot(qs_scr[rows, :], st_scr[1].astype(BF16))
        st_scr[1] = cdec_b * st_scr[1] + _dot_tn((k * kdec_b).astype(BF16), v_ref[0, rows, :])

    def finish(c):
        rows = chunk_rows(c)
        o = acc_scr[0, rows, :] + acc_scr[1, rows, :]
        gate = g_ref[0, rows, :].astype(F32)
        o_ref[0, rows, :] = (_head_norm(o, gn_ref[0]) * _silu(gate)).astype(o_ref.dtype)

    _two_ended_scan(n_chunks, prepare, step, finish)


def _retention(proj, lg, cos_t, sin_t, ret_gn, nb, seq):
    kq, kv = RET_DK, RET_DV
    grid_spec = dict(
        grid=(nb, RET_HEADS),
        in_specs=[
            pl.BlockSpec(memory_space=pltpu.SMEM),
            pl.BlockSpec((1, seq, kq), lambda b, h: (b, 0, h)),
            pl.BlockSpec((1, seq, kq), lambda b, h: (b, 0, RET_QK // kq + h)),
            pl.BlockSpec((1, seq, kv), lambda b, h: (b, 0, 2 * RET_QK // kv + h)),
            pl.BlockSpec((1, seq, kv), lambda b, h: (b, 0, (2 * RET_QK + RET_V) // kv + h)),
            pl.BlockSpec((seq, kq), lambda b, h: (0, 0)),
            pl.BlockSpec((seq, kq), lambda b, h: (0, 0)),
            pl.BlockSpec((1, 1, kv), lambda b, h: (h, 0, 0)),
        ],
        out_specs=pl.BlockSpec((1, seq, kv), lambda b, h: (b, 0, h)),
        scratch_shapes=[
            pltpu.VMEM((seq, kq), BF16),
            pltpu.VMEM((seq, kq), F32),
            pltpu.VMEM((2, seq, kv), F32),
            pltpu.VMEM((2, kq, kv), F32),
        ],
    )
    return pl.pallas_call(
        functools.partial(_ret_body, seq=seq),
        out_shape=jax.ShapeDtypeStruct((nb, seq, RET_V), BF16),
        **grid_spec,
        compiler_params=_cparams(("parallel", "parallel")),
        name="retention",
    )(lg, proj, proj, proj, proj, cos_t, sin_t, ret_gn.reshape(RET_HEADS, 1, kv))


def _col_of(mat, c):
    lane = lax.broadcasted_iota(I32, mat.shape, 1)
    return jnp.sum(jnp.where(lane == c, mat, 0.0), axis=1, keepdims=True)


def _mlstm_body(gb_ref, q_ref, k_ref, v_ref, o_gate_ref, wq_ref, wk_ref, bq_ref, bk_ref,
                gr_ref, gn_ref, o_ref,
                padq_scr, padk_scr, qs_scr, ks_scr, acc_scr, c_scr, n_scr, m_scr, row_scr, col_scr,
                *, seq):
    ln = MLSTM_CHUNK
    assert ln == CONV_ROWS
    n_chunks = seq // ln
    h = pl.program_id(1)

    _conv_fill(q_ref, padq_scr, seq)
    _conv_fill(k_ref, padk_scr, seq)

    def prepare(c):
        r0 = pl.multiple_of(c * ln, ln)
        rows = pl.ds(r0, ln)

        def emit_q(lane0, y):
            qs_scr[rows, lane0:lane0 + LANES] = (y * (MLSTM_DK ** -0.5)).astype(BF16)

        def emit_k(lane0, y):
            ks_scr[rows, lane0:lane0 + LANES] = y

        _conv_silu_rows(padq_scr, wq_ref, bq_ref, r0, emit_q)
        _conv_silu_rows(padk_scr, wk_ref, bk_ref, r0, emit_k)

    row_scr[...] = jnp.zeros_like(row_scr)
    for t in range(4):
        for c in range(n_chunks):
            row_scr[t, c:c + 1, :] = gr_ref[0, t:t + 1, c * ln:(c + 1) * ln]
    tri_le = _tri(ln, "le")
    tri_ge = _tri(ln, "ge")
    for d in range(2):
        ig = row_scr[2 * d] + gb_ref[(2 * d) * MLSTM_HEADS + h]
        lf = _log_sigmoid(row_scr[2 * d + 1] + gb_ref[(2 * d + 1) * MLSTM_HEADS + h])
        bc = _dot01_right(lf, tri_le if d == 0 else tri_ge)
        row_scr[2 * d] = ig
        row_scr[2 * d + 1] = bc
        col_scr[2 * d] = ig.T
        col_scr[2 * d + 1] = bc.T

    ri = lax.broadcasted_iota(I32, (ln, ln), 0)
    ci = lax.broadcasted_iota(I32, (ln, ln), 1)

    def chunk_step(c, d):
        r0 = pl.multiple_of(c * ln, ln)
        rows = pl.ds(r0, ln)
        qb = qs_scr[rows, :]
        kf = ks_scr[rows, :]
        v = v_ref[0, rows, :]
        i_row = row_scr[2 * d, pl.ds(c, 1), :]
        b_row = row_scr[2 * d + 1, pl.ds(c, 1), :]
        i_col = _col_of(col_scr[2 * d], c)
        b_col = _col_of(col_scr[2 * d + 1], c)
        m_st = m_scr[d]
        mask = (ri >= ci) if d == 0 else (ri <= ci)
        logd = jnp.where(mask, b_col - b_row + i_row, NEG_INF)
        m_inter = b_col + m_st
        m_row = jnp.maximum(m_inter, jnp.max(logd, axis=1, keepdims=True))
        sc = _dot_nt(qb, kf.astype(BF16)) * jnp.exp(logd - m_row)
        inter = jnp.exp(m_inter - m_row)
        num = _dot(sc.astype(BF16), v) + inter * _dot(qb, c_scr[d].astype(BF16))
        den = jnp.sum(sc, axis=1, keepdims=True) + inter * jnp.sum(
            qb.astype(F32) * n_scr[d], axis=1, keepdims=True)
        hh = num / jnp.maximum(jnp.abs(den), jnp.exp(-m_row))
        b_end = b_row[:, ln - 1:ln] if d == 0 else b_row[:, 0:1]
        logw = b_end - b_col + i_col
        m_new = jnp.maximum(b_end + m_st, jnp.max(logw, axis=0, keepdims=True))
        kw = kf * jnp.exp(logw - m_new)
        dec = jnp.exp(b_end + m_st - m_new)
        c_scr[d] = dec * c_scr[d] + _dot_tn(kw.astype(BF16), v)
        n_scr[d] = dec * n_scr[d] + jnp.sum(kw, axis=0, keepdims=True)
        m_scr[d] = m_new
        acc_scr[d, rows, :] = hh

    c_scr[...] = jnp.zeros_like(c_scr)
    n_scr[...] = jnp.zeros_like(n_scr)
    m_scr[...] = jnp.zeros_like(m_scr)

    def step(i):
        chunk_step(i, 0)
        chunk_step(n_chunks - 1 - i, 1)

    def finish(c):
        rows = pl.ds(pl.multiple_of(c * ln, ln), ln)
        y = _head_norm(acc_scr[0, rows, :] + acc_scr[1, rows, :], gn_ref[0])
        o_ref[0, rows, :] = (y * _sigmoid(o_gate_ref[0, rows, :].astype(F32))).astype(o_ref.dtype)

    _two_ended_scan(n_chunks, prepare, step, finish)


def _mlstm(proj, gate_rows, gate_b, conv_w, conv_b, mlstm_gn, nb, seq):
    kq, kv = MLSTM_DK, MLSTM_DV
    q0 = 2 * RET_QK + 2 * RET_V
    k0 = q0 + MLSTM_QK
    v0 = k0 + MLSTM_QK
    o0 = v0 + MLSTM_V
    grid_spec = dict(
        grid=(nb, MLSTM_HEADS),
        in_specs=[
            pl.BlockSpec(memory_space=pltpu.SMEM),
            pl.BlockSpec((1, seq, kq), lambda b, h: (b, 0, q0 // kq + h)),
            pl.BlockSpec((1, seq, kq), lambda b, h: (b, 0, k0 // kq + h)),
            pl.BlockSpec((1, seq, kv), lambda b, h: (b, 0, v0 // kv + h)),
            pl.BlockSpec((1, seq, kv), lambda b, h: (b, 0, o0 // kv + h)),
            pl.BlockSpec((CONV_W, kq), lambda b, h: (0, h)),
            pl.BlockSpec((CONV_W, kq), lambda b, h: (0, MLSTM_QK // kq + h)),
            pl.BlockSpec((1, kq), lambda b, h: (0, h)),
            pl.BlockSpec((1, kq), lambda b, h: (0, MLSTM_QK // kq + h)),
            pl.BlockSpec((1, GATE_ROWS, seq), lambda b, h: (b, h, 0)),
            pl.BlockSpec((1, 1, kv), lambda b, h: (h, 0, 0)),
        ],
        out_specs=pl.BlockSpec((1, seq, kv), lambda b, h: (b, 0, h)),
        scratch_shapes=[
            pltpu.VMEM((seq + 2 * CONV_HALO, kq), F32),
            pltpu.VMEM((seq + 2 * CONV_HALO, kq), F32),
            pltpu.VMEM((seq, kq), BF16),
            pltpu.VMEM((seq, kq), F32),
            pltpu.VMEM((2, seq, kv), F32),
            pltpu.VMEM((2, kq, kv), F32),
            pltpu.VMEM((2, 1, kq), F32),
            pltpu.VMEM((2, 1, 1), F32),
            pltpu.VMEM((4, LANES, MLSTM_CHUNK), F32),
            pltpu.VMEM((4, MLSTM_CHUNK, LANES), F32),
        ],
    )
    return pl.pallas_call(
        functools.partial(_mlstm_body, seq=seq),
        out_shape=jax.ShapeDtypeStruct((nb, seq, MLSTM_V), BF16),
        **grid_spec,
        compiler_params=_cparams(("parallel", "parallel")),
        name="mlstm",
    )(gate_b, proj, proj, proj, proj, conv_w, conv_w, conv_b.reshape(1, -1), conv_b.reshape(1, -1),
      gate_rows, mlstm_gn.reshape(MLSTM_HEADS, 1, kv))


def _ssd_body(z_ref, x_ref, b_ref, c_ref, wx_ref, wb_ref, wc_ref, bx_ref, bb_ref, bc_ref,
              dtf_ref, dtb_ref, bias_ref, alog_ref, dskip_ref, o_ref,
              padx_scr, padb_scr, padc_scr, xs_scr, bs_scr, bst_scr, cs_scr, y_scr, st_scr,
              acr_scr, dtr_scr, er_scr, ur_scr, act_scr, *, seq):
    ln = SSD_CHUNK
    assert ln == CONV_ROWS
    n_chunks = seq // ln
    hp = SSD_HEADDIM
    n_pairs = SSD_HPG // 2

    _conv_fill(x_ref, padx_scr, seq)
    _conv_fill(b_ref, padb_scr, seq)
    _conv_fill(c_ref, padc_scr, seq)

    def prepare(c):
        r0 = pl.multiple_of(c * ln, ln)
        rows = pl.ds(r0, ln)

        def emit_x(lane0, y):
            xs_scr[rows, lane0:lane0 + LANES] = y

        def emit_b(lane0, y):
            bs_scr[rows, :] = y.astype(BF16)
            bst_scr[:, rows] = y.T.astype(BF16)

        def emit_c(lane0, y):
            cs_scr[rows, :] = y.astype(BF16)

        _conv_silu_rows(padx_scr, wx_ref, bx_ref, r0, emit_x)
        _conv_silu_rows(padb_scr, wb_ref, bb_ref, r0, emit_b)
        _conv_silu_rows(padc_scr, wc_ref, bc_ref, r0, emit_c)

    for d, dt_ref in enumerate((dtf_ref, dtb_ref)):
        acr_scr[d] = jnp.zeros((LANES, ln), F32)
        for c in range(n_chunks):
            acr_scr[d, c * SSD_HPG:(c + 1) * SSD_HPG, :] = dt_ref[0, :, c * ln:(c + 1) * ln]
        dt = _softplus(acr_scr[d] + bias_ref[0, d])
        adt = dt * (-jnp.exp(alog_ref[0, d]))
        acum = _dot01_right(adt, _tri(ln, "le" if d == 0 else "ge"))
        a_end = acum[:, ln - 1:ln] if d == 0 else acum[:, 0:1]
        acr_scr[d] = acum
        dtr_scr[d] = dt
        er_scr[d] = jnp.exp(acum)
        ur_scr[d] = dt * jnp.exp(a_end - acum)
        act_scr[d] = acum.T

    ri = lax.broadcasted_iota(I32, (ln, ln), 0)
    ci = lax.broadcasted_iota(I32, (ln, ln), 1)
    in_first = lax.broadcasted_iota(I32, (1, LANES), 1) < hp
    on_diag = ri == ci

    def chunk_rows(c):
        return pl.ds(pl.multiple_of(c * ln, ln), ln)

    def dir_step(d, c):
        rows = chunk_rows(c)
        xb = xs_scr[rows, :].astype(BF16)
        bcm = bs_scr[rows, :]
        bct = bst_scr[:, rows].astype(F32)
        ccm = cs_scr[rows, :]
        cb = _dot_nt(ccm, bcm)
        carried = _dot(ccm, st_scr[d].astype(BF16)).astype(BF16)
        head_rows = pl.ds(pl.multiple_of(c * SSD_HPG, SSD_HPG), SSD_HPG)
        arow = acr_scr[d, head_rows, :]
        dtrow = dtr_scr[d, head_rows, :]
        erow = er_scr[d, head_rows, :]
        urow = ur_scr[d, head_rows, :]
        acols = pltpu.roll(act_scr[d], (LANES - c * SSD_HPG) & (LANES - 1), axis=1)
        mask = (ri >= ci) if d == 0 else (ri <= ci)
        end = ln - 1 if d == 0 else 0
        pieces = []
        for pair in range(n_pairs):
            lanes = slice(pair * LANES, (pair + 1) * LANES)
            lhs, rhs, lhs_state, keep = [], [], [], []
            for sub in range(2):
                k = 2 * pair + sub
                sel = in_first if sub == 0 else ~in_first
                dec = jnp.exp(jnp.where(mask, acols[:, k:k + 1] - arow[k:k + 1, :], NEG_INF))
                lhs.append((cb * dec * dtrow[k:k + 1, :]).astype(BF16))
                rhs.append(jnp.where(sel, xb[:, lanes], jnp.zeros((ln, LANES), BF16)))
                lhs_state.append((bct * urow[k:k + 1, :]).astype(BF16))
                keep.append(jnp.exp(arow[k:k + 1, end:end + 1]))
            for sub in range(2):
                k = 2 * pair + sub
                sel = in_first if sub == 0 else ~in_first
                lhs.append(jnp.where(on_diag, erow[k:k + 1, :], 0.0).astype(BF16))
                rhs.append(jnp.where(sel, carried[:, lanes], jnp.zeros((ln, LANES), BF16)))
            pieces.append(_dot(jnp.concatenate(lhs, axis=1), jnp.concatenate(rhs, axis=0)))
            st_scr[d, :, lanes] = (jnp.where(in_first, keep[0], keep[1]) * st_scr[d, :, lanes]
                                   + _dot(jnp.concatenate(lhs_state, axis=1),
                                          jnp.concatenate(rhs[:2], axis=0)))
        y_scr[d, rows, :] = jnp.concatenate(pieces, axis=1)

    st_scr[...] = jnp.zeros_like(st_scr)

    def step(i):
        dir_step(0, i)
        dir_step(1, n_chunks - 1 - i)

    def finish(c):
        rows = chunk_rows(c)
        y = y_scr[0, rows, :] + y_scr[1, rows, :] + dskip_ref[0] * xs_scr[rows, :]
        o_ref[0, rows, :] = (y * _silu(z_ref[0, rows, :].astype(F32))).astype(o_ref.dtype)

    _two_ended_scan(n_chunks, prepare, step, finish)


def _ssd(proj, dt_t, bias_col, alog_col, dskip_x, conv_w, conv_b, nb, seq):
    width = SSD_HPG * SSD_HEADDIM
    ns = SSD_STATE
    x0 = SSD_INNER
    b0 = 2 * SSD_INNER
    c0 = b0 + SSD_BC
    cb = conv_b.reshape(1, -1)
    return pl.pallas_call(
        functools.partial(_ssd_body, seq=seq),
        out_shape=jax.ShapeDtypeStruct((nb, seq, SSD_INNER), BF16),
        grid=(nb, SSD_GROUPS),
        in_specs=[
            pl.BlockSpec((1, seq, width), lambda b, g: (b, 0, g)),
            pl.BlockSpec((1, seq, width), lambda b, g: (b, 0, x0 // width + g)),
            pl.BlockSpec((1, seq, ns), lambda b, g: (b, 0, b0 // ns + g)),
            pl.BlockSpec((1, seq, ns), lambda b, g: (b, 0, c0 // ns + g)),
            pl.BlockSpec((CONV_W, width), lambda b, g: (0, g)),
            pl.BlockSpec((CONV_W, ns), lambda b, g: (0, SSD_INNER // ns + g)),
            pl.BlockSpec((CONV_W, ns), lambda b, g: (0, (SSD_INNER + SSD_BC) // ns + g)),
            pl.BlockSpec((1, width), lambda b, g: (0, g)),
            pl.BlockSpec((1, ns), lambda b, g: (0, SSD_INNER // ns + g)),
            pl.BlockSpec((1, ns), lambda b, g: (0, (SSD_INNER + SSD_BC) // ns + g)),
            pl.BlockSpec((1, SSD_HPG, seq), lambda b, g: (b, g, 0)),
            pl.BlockSpec((1, SSD_HPG, seq), lambda b, g: (b, SSD_GROUPS + g, 0)),
            pl.BlockSpec((1, 2, LANES, 1), lambda b, g: (g, 0, 0, 0)),
            pl.BlockSpec((1, 2, LANES, 1), lambda b, g: (g, 0, 0, 0)),
            pl.BlockSpec((1, 1, width), lambda b, g: (g, 0, 0)),
        ],
        out_specs=pl.BlockSpec((1, seq, width), lambda b, g: (b, 0, g)),
        scratch_shapes=[
            pltpu.VMEM((seq + 2 * CONV_HALO, width), F32),
            pltpu.VMEM((seq + 2 * CONV_HALO, ns), F32),
            pltpu.VMEM((seq + 2 * CONV_HALO, ns), F32),
            pltpu.VMEM((seq, width), F32),
            pltpu.VMEM((seq, ns), BF16),
            pltpu.VMEM((ns, seq), BF16),
            pltpu.VMEM((seq, ns), BF16),
            pltpu.VMEM((2, seq, width), F32),
            pltpu.VMEM((2, ns, width), F32),
            pltpu.VMEM((2, LANES, SSD_CHUNK), F32),
            pltpu.VMEM((2, LANES, SSD_CHUNK), F32),
            pltpu.VMEM((2, LANES, SSD_CHUNK), F32),
            pltpu.VMEM((2, LANES, SSD_CHUNK), F32),
            pltpu.VMEM((2, SSD_CHUNK, LANES), F32),
        ],
        compiler_params=_cparams(("parallel", "parallel")),
        name="ssd",
    )(proj, proj, proj, proj, conv_w, conv_w, conv_w, cb, cb, cb,
      dt_t, dt_t, bias_col, alog_col, dskip_x)


META_E = 0
META_G = 2
META_R = 4
ROUTE_E0 = MOE_GROUPS


def _moe_input(x_ref, g_ref, sc_ref, sh_ref):
    return _rms(x_ref[...], g_ref[...]) * (1.0 + sc_ref[0]) + sh_ref[0]


def _router_body(x_ref, g_ref, sc_ref, sh_ref, w_ref, b_ref, meta_ref, cnt_ref, carry_scr):
    tm = x_ref.shape[0]

    @pl.when(pl.program_id(0) == 0)
    def _():
        carry_scr[...] = jnp.zeros_like(carry_scr)

    y = _moe_input(x_ref, g_ref, sc_ref, sh_ref)
    h_hi = y.astype(BF16)
    h_lo = (y - h_hi.astype(F32)).astype(BF16)
    w = w_ref[...]
    w_hi = w.astype(BF16)
    w_lo = (w - w_hi.astype(F32)).astype(BF16)
    logits = _dot(h_hi, w_hi) + _dot(h_lo, w_hi) + _dot(h_hi, w_lo) + b_ref[...]

    lane = lax.broadcasted_iota(I32, (tm, LANES), 1)
    lane_f = lane.astype(F32)
    big = float(LANES)
    is_grp = lane < MOE_GROUPS
    gl = jnp.where(is_grp, logits, NEG_INF)
    gmax = jnp.max(gl, axis=1, keepdims=True)
    gidx = jnp.min(jnp.where(gl == gmax, lane_f, big), axis=1, keepdims=True)
    gprob = 1.0 / jnp.sum(jnp.where(is_grp, jnp.exp(gl - gmax), 0.0), axis=1, keepdims=True)

    el = lane - ROUTE_E0
    el_f = el.astype(F32)
    valid = (el >= 0) & (el < MOE_EXPERTS)
    in_grp = valid & (_shr(el, MOE_EPG).astype(F32) == gidx)
    ev = jnp.where(in_grp, logits, NEG_INF)
    v1 = jnp.max(ev, axis=1, keepdims=True)
    i1 = jnp.min(jnp.where(ev == v1, el_f, big), axis=1, keepdims=True)
    ev2 = jnp.where(el_f == i1, NEG_INF, ev)
    v2 = jnp.max(ev2, axis=1, keepdims=True)
    i2 = jnp.min(jnp.where(ev2 == v2, el_f, big), axis=1, keepdims=True)
    p2 = jnp.exp(v2 - v1)
    s1 = 1.0 / (1.0 + p2)
    gate1 = s1 * gprob
    gate2 = p2 * s1 * gprob

    oh1 = jnp.where(el_f == i1, 1.0, 0.0)
    oh2 = jnp.where(el_f == i2, 1.0, 0.0)
    oh = oh1 + oh2
    before = _dot(_tri(tm, "gt"), oh.astype(BF16)) + carry_scr[...]
    rank1 = jnp.sum(oh1 * before, axis=1, keepdims=True)
    rank2 = jnp.sum(oh2 * before, axis=1, keepdims=True)
    carry_scr[...] = carry_scr[...] + jnp.sum(oh, axis=0, keepdims=True)
    cnt_ref[...] = jnp.broadcast_to(carry_scr[...], cnt_ref.shape)

    meta = jnp.zeros((tm, LANES), F32)
    for col, val in ((META_E, i1), (META_E + 1, i2), (META_G, gate1), (META_G + 1, gate2),
                     (META_R, rank1), (META_R + 1, rank2)):
        meta = jnp.where(lane == col, val, meta)
    meta_ref[...] = meta


def _router(x, g, sc, sh, w_route, b_route, seq, tm=512):
    t, k = x.shape
    tm = min(tm, seq)
    tps = seq // tm
    return pl.pallas_call(
        _router_body,
        out_shape=[jax.ShapeDtypeStruct((t, LANES), F32),
                   jax.ShapeDtypeStruct((8, LANES), F32)],
        grid=(t // tm,),
        in_specs=[
            pl.BlockSpec((tm, k), lambda i: (i, 0)),
            pl.BlockSpec((1, k), lambda i: (0, 0)),
            pl.BlockSpec((1, 1, k), lambda i: (i // tps, 0, 0)),
            pl.BlockSpec((1, 1, k), lambda i: (i // tps, 0, 0)),
            pl.BlockSpec((k, LANES), lambda i: (0, 0)),
            pl.BlockSpec((1, LANES), lambda i: (0, 0)),
        ],
        out_specs=[pl.BlockSpec((tm, LANES), lambda i: (i, 0)),
                   pl.BlockSpec((8, LANES), lambda i: (0, 0))],
        scratch_shapes=[pltpu.VMEM((1, LANES), F32)],
        compiler_params=_cparams(("arbitrary",)),
        name="router",
    )(x, g.reshape(1, k), sc, sh, w_route, b_route)


ZERO_BLOCKS = 2 * MOE_EXPERTS


def _dispatch_body(dest_ref, zero_ref, x_ref, g_ref, sc_ref, sh_ref, xs_hbm, h_scr, zero_scr, sem, zsem):
    i = pl.program_id(0)
    n_steps = pl.num_programs(0)
    tm = x_ref.shape[0]
    slot = i % 2

    def zero_copy(start):
        return pltpu.make_async_copy(zero_scr, xs_hbm.at[pl.ds(start, MOE_BLOCK)], zsem)

    @pl.when(i == 0)
    def _():
        zero_scr[...] = jnp.zeros_like(zero_scr)

        def issue(k, carry):
            @pl.when(zero_ref[k] >= 0)
            def _():
                zero_copy(pl.multiple_of(zero_ref[k], MOE_BLOCK)).start()
            return carry

        def drain(k, carry):
            @pl.when(zero_ref[k] >= 0)
            def _():
                zero_copy(0).wait()
            return carry

        lax.fori_loop(0, ZERO_BLOCKS, issue, 0)
        lax.fori_loop(0, ZERO_BLOCKS, drain, 0)

    def wait_rows(s):
        for _ in range(2):
            pltpu.make_async_copy(h_scr.at[s], xs_hbm.at[pl.ds(0, tm)], sem.at[s]).wait()

    @pl.when(i >= 2)
    def _():
        wait_rows(slot)

    h_scr[slot] = _moe_input(x_ref, g_ref, sc_ref, sh_ref)

    def body(r, carry):
        for j in range(2):
            pltpu.make_async_copy(h_scr.at[slot, pl.ds(r, 1)],
                                  xs_hbm.at[pl.ds(dest_ref[(i * tm + r) * 2 + j], 1)],
                                  sem.at[slot]).start()
        return carry

    lax.fori_loop(0, tm, body, 0, unroll=GATHER_UNROLL // 2)

    @pl.when(i == n_steps - 1)
    def _():
        @pl.when(n_steps >= 2)
        def _():
            wait_rows(1 - slot)
        wait_rows(slot)


def _dispatch(x, g, sc, sh, dest, zero_starts, n_rows, seq, tm=256):
    t, k = x.shape
    tm = min(tm, seq)
    tps = seq // tm
    grid_spec = pltpu.PrefetchScalarGridSpec(
        num_scalar_prefetch=2,
        grid=(t // tm,),
        in_specs=[
            pl.BlockSpec((tm, k), lambda i, ds, zs: (i, 0)),
            pl.BlockSpec((1, k), lambda i, ds, zs: (0, 0)),
            pl.BlockSpec((1, 1, k), lambda i, ds, zs: (i // tps, 0, 0)),
            pl.BlockSpec((1, 1, k), lambda i, ds, zs: (i // tps, 0, 0)),
        ],
        out_specs=pl.BlockSpec(memory_space=pl.ANY),
        scratch_shapes=[pltpu.VMEM((2, tm, k), F32), pltpu.VMEM((MOE_BLOCK, k), F32),
                        pltpu.SemaphoreType.DMA((2,)), pltpu.SemaphoreType.DMA],
    )
    return pl.pallas_call(
        _dispatch_body,
        out_shape=jax.ShapeDtypeStruct((n_rows, k), F32),
        grid_spec=grid_spec,
        compiler_params=_cparams(("arbitrary",)),
        name="moe_dispatch",
    )(dest, zero_starts, x, g.reshape(1, k), sc, sh)


def _row_copy(src_hbm, dst, sem, src_row, dst_row):
    return pltpu.make_async_copy(src_hbm.at[pl.ds(src_row, 1)], dst.at[pl.ds(dst_row, 1)], sem)


def _start_row_gather(idx_ref, base, n_rows, stride, src_hbm, dst, sem):
    def body(r, carry):
        _row_copy(src_hbm, dst, sem, idx_ref[base + r * stride], r).start()
        return carry
    lax.fori_loop(0, n_rows, body, 0, unroll=GATHER_UNROLL)


def _wait_row_gather(src_hbm, dst, sem, n_rows):
    pltpu.make_async_copy(src_hbm.at[pl.ds(0, n_rows)], dst, sem).wait()


def _expert_body(blk_exp_ref, n_used_ref, xs_ref, wg_ref, wu_ref, wd_ref, y_ref, wgu_scr, wdn_scr):
    i = pl.program_id(0)
    d = xs_ref.shape[1]
    used = i < n_used_ref[0]

    @pl.when(used & ((i == 0) | (blk_exp_ref[i] != blk_exp_ref[jnp.maximum(i - 1, 0)])))
    def _():
        def cast_up(c, carry):
            rows = pl.ds(pl.multiple_of(c * CAST_ROWS, CAST_ROWS), CAST_ROWS)
            wgu_scr[rows, :EXPERT_FF] = wg_ref[0, 0, rows, :].astype(BF16)
            wgu_scr[rows, EXPERT_FF:] = wu_ref[0, 0, rows, :].astype(BF16)
            return carry

        def cast_down(c, carry):
            rows = pl.ds(pl.multiple_of(c * CAST_ROWS, CAST_ROWS), CAST_ROWS)
            wdn_scr[rows, :] = wd_ref[0, 0, rows, :].astype(BF16)
            return carry

        lax.fori_loop(0, d // CAST_ROWS, cast_up, 0)
        lax.fori_loop(0, EXPERT_FF // CAST_ROWS, cast_down, 0)

    @pl.when(used)
    def _():
        a = _dot(xs_ref[...].astype(BF16), wgu_scr[...])
        hid = (_silu(a[:, :EXPERT_FF]) * a[:, EXPERT_FF:]).astype(BF16)
        y_ref[...] = _dot(hid, wdn_scr[...])

    @pl.when(jnp.logical_not(used))
    def _():
        y_ref[...] = jnp.zeros_like(y_ref)


def _experts(xs, w_gate, w_up, w_down, layer, blk_exp, n_used):
    n_rows, d = xs.shape
    n_blocks = n_rows // MOE_BLOCK
    grid_spec = pltpu.PrefetchScalarGridSpec(
        num_scalar_prefetch=2,
        grid=(n_blocks,),
        in_specs=[
            pl.BlockSpec((MOE_BLOCK, d), lambda i, be, nu: (i, 0)),
            pl.BlockSpec((1, 1, d, EXPERT_FF), lambda i, be, nu: (layer, be[i], 0, 0)),
            pl.BlockSpec((1, 1, d, EXPERT_FF), lambda i, be, nu: (layer, be[i], 0, 0)),
            pl.BlockSpec((1, 1, EXPERT_FF, d), lambda i, be, nu: (layer, be[i], 0, 0)),
        ],
        out_specs=pl.BlockSpec((MOE_BLOCK, d), lambda i, be, nu: (i, 0)),
        scratch_shapes=[pltpu.VMEM((d, 2 * EXPERT_FF), BF16),
                        pltpu.VMEM((EXPERT_FF, d), BF16)],
    )
    return pl.pallas_call(
        _expert_body,
        out_shape=jax.ShapeDtypeStruct((n_rows, d), F32),
        grid_spec=grid_spec,
        compiler_params=_cparams(("arbitrary",)),
        name="experts",
    )(blk_exp, n_used, xs, w_gate, w_up, w_down)


def _combine_body(dest_ref, y_hbm, x_ref, gate_ref, meta_ref, fn_ref, o_ref, ya_scr, sem,
                  *, final, tile0):
    i = pl.program_id(0)
    n_steps = pl.num_programs(0)
    tm = x_ref.shape[0]
    slot = i % 2

    def start(step, s):
        for j in range(2):
            _start_row_gather(dest_ref, (tile0 + step) * tm * 2 + j, tm, 2, y_hbm,
                              ya_scr.at[s, j], sem.at[s, j])

    @pl.when(i == 0)
    def _():
        start(0, 0)

    for j in range(2):
        _wait_row_gather(y_hbm, ya_scr.at[slot, j], sem.at[slot, j], tm)

    @pl.when(i + 1 < n_steps)
    def _():
        start(i + 1, 1 - slot)

    meta = meta_ref[...]
    moe = (ya_scr[slot, 0] * meta[:, META_G:META_G + 1]
           + ya_scr[slot, 1] * meta[:, META_G + 1:META_G + 2])
    out = x_ref[...] + gate_ref[0] * moe
    if final:
        out = _rms(out, fn_ref[...])
    o_ref[...] = out


def _combine(y, x, gate, meta, dest, final_norm, seq, final, tok0=0, n_tok=None, tm=256):
    t, d = x.shape
    n_tok = t if n_tok is None else n_tok
    tm = min(tm, seq)
    tps = seq // tm
    assert tok0 % seq == 0 and n_tok % seq == 0
    tile0 = tok0 // tm
    grid_spec = pltpu.PrefetchScalarGridSpec(
        num_scalar_prefetch=1,
        grid=(n_tok // tm,),
        in_specs=[
            pl.BlockSpec(memory_space=pl.ANY),
            pl.BlockSpec((tm, d), lambda i, ds: (tile0 + i, 0)),
            pl.BlockSpec((1, 1, d), lambda i, ds: ((tile0 + i) // tps, 0, 0)),
            pl.BlockSpec((tm, LANES), lambda i, ds: (tile0 + i, 0)),
            pl.BlockSpec((1, d), lambda i, ds: (0, 0)),
        ],
        out_specs=pl.BlockSpec((tm, d), lambda i, ds: (i, 0)),
        scratch_shapes=[pltpu.VMEM((2, 2, tm, d), F32), pltpu.SemaphoreType.DMA((2, 2))],
    )
    return pl.pallas_call(
        functools.partial(_combine_body, final=final, tile0=tile0),
        out_shape=jax.ShapeDtypeStruct((n_tok, d), F32),
        grid_spec=grid_spec,
        compiler_params=_cparams(("arbitrary",)),
        name="moe_combine",
    )(dest, y, x, gate, meta, final_norm.reshape(1, d))


def _moe_layer(x, g, sc, sh, gate, grp_w, grp_b, exp_w, exp_b, w_gate, w_up, w_down, layer,
               final_norm, seq, final, split=None):
    t, d = x.shape
    pad = LANES - MOE_GROUPS - MOE_EXPERTS
    w_route = jnp.concatenate([grp_w, exp_w, jnp.zeros((d, pad), F32)], axis=1)
    b_route = jnp.concatenate([grp_b, exp_b, jnp.zeros((pad,), F32)]).reshape(1, LANES)
    meta, cnt = _router(x, g, sc, sh, w_route, b_route, seq)

    expert = meta[:, META_E:META_E + 2].astype(I32)
    rank = meta[:, META_R:META_R + 2].astype(I32)
    counts = cnt[0, ROUTE_E0:ROUTE_E0 + MOE_EXPERTS].astype(I32)
    padded = (counts + MOE_BLOCK - 1) // MOE_BLOCK * MOE_BLOCK
    p_ends = jnp.cumsum(padded)
    p_starts = p_ends - padded
    dest = (p_starts[expert] + rank).reshape(-1)
    n_rows = t * 2 + MOE_EXPERTS * MOE_BLOCK
    n_blocks = n_rows // MOE_BLOCK
    blk_start = jnp.arange(n_blocks, dtype=I32) * MOE_BLOCK
    blk_exp = jnp.minimum(jnp.sum((p_ends[None, :] <= blk_start[:, None]).astype(I32), axis=1),
                          MOE_EXPERTS - 1)
    n_used = p_ends[-1:] // MOE_BLOCK
    tail = (n_used + jnp.arange(MOE_EXPERTS, dtype=I32)) * MOE_BLOCK
    zero_starts = jnp.concatenate([jnp.where(padded > counts, p_ends - MOE_BLOCK, -1),
                                   jnp.where(tail < n_rows, tail, -1)]).astype(I32)

    xs = _dispatch(x, g, sc, sh, dest, zero_starts, n_rows, seq)
    y = _experts(xs, w_gate, w_up, w_down, layer, blk_exp, n_used)
    if split is None:
        return _combine(y, x, gate, meta, dest, final_norm, seq, final)
    return tuple(_combine(y, x, gate, meta, dest, final_norm, seq, final, tok0=a, n_tok=b - a)
                 for a, b in ((0, split), (split, t)))


def _rope_tables(seq):
    half = RET_DK // 2
    inv = ROPE_BASE ** (-jnp.arange(half, dtype=F32) / half)
    ang = jnp.arange(seq, dtype=F32)[:, None] * inv[None, :]
    cos, sin = jnp.cos(ang), jnp.sin(ang)
    return jnp.concatenate([cos, cos], axis=1), jnp.concatenate([-sin, sin], axis=1)


def _pad_rows(a, axis, n):
    pad = [(0, 0)] * a.ndim
    pad[axis] = (0, n - a.shape[axis])
    return jnp.pad(a, pad)


def kernel(x_prompt, x_sample, c_prompt, c_sample, ada_w, ada_b, norm1, norm2, ev_w_in, ev_gate_b, ev_conv_w, ev_conv_b, ev_ret_gn, ev_mlstm_gn, ev_w_out, od_w_in, od_conv_w, od_conv_b, od_dt_bias, od_a_log, od_d_skip, od_norm, od_w_out, moe_grp_w, moe_grp_b, moe_exp_w, moe_exp_b, moe_w_gate, moe_w_up, moe_w_down, final_norm):
    n_prompt = x_prompt.shape[0]
    seq, d = x_prompt.shape[1], x_prompt.shape[2]
    assert x_sample.shape[1] == seq and d == D_MODEL
    assert seq % RET_CHUNK == 0 and seq // MLSTM_CHUNK <= LANES // SSD_HPG
    x = jnp.concatenate([x_prompt, x_sample], axis=0)
    nb = x.shape[0]
    t = nb * seq
    x = x.reshape(t, d)
    depth = ada_w.shape[0]

    c_all = jnp.concatenate([c_prompt, c_sample], axis=0)
    c_pad = _pad_rows(c_all, 0, -(-nb // 8) * 8)
    mod = _modulation(c_pad, ada_w, ada_b)[:, :nb].reshape(depth, nb, N_MOD, 1, d)

    heads = jnp.arange(RET_HEADS, dtype=F32)
    lg = jnp.stack([jnp.log1p(-jnp.exp2(-RET_DECAY_FWD - heads)),
                    jnp.log1p(-jnp.exp2(-RET_DECAY_BWD - heads))])
    cos_t, sin_t = _rope_tables(seq)

    for i in range(depth):
        sh1, sc1, g1, sh2, sc2, g2 = (mod[i, :, m] for m in range(N_MOD))
        j = i // 2
        if i % 2 == 0:
            w_in = ev_w_in[j]
            w_side = w_in[:, EVEN_MAIN:].reshape(d, 4, MLSTM_HEADS).transpose(0, 2, 1)
            w_side = _pad_rows(_pad_rows(w_side, 2, GATE_ROWS).reshape(d, -1), 1, LANES)
            proj, gates = _fused_matmul(x, w_in[:, :EVEN_MAIN].astype(BF16), seq=seq, prologue="normmod",
                                        g=norm1[i], sc=sc1, sh=sh1, w_side=w_side, name="even_in_proj")
            proj = proj.reshape(nb, seq, EVEN_MAIN)
            ret = _retention(proj, lg, cos_t, sin_t, ev_ret_gn[j], nb, seq)
            ml = _mlstm(proj, gates, ev_gate_b[j], ev_conv_w[j], ev_conv_b[j], ev_mlstm_gn[j], nb, seq)
            x = _fused_matmul(ret.reshape(t, RET_V), ev_w_out[j].astype(BF16), seq=seq,
                              x2=ml.reshape(t, MLSTM_V),
                              res=x, gate=g1, tn=512, name="even_out_proj")
        else:
            w_in = od_w_in[j]
            proj, dt_raw = _fused_matmul(x, w_in[:, :ODD_MAIN].astype(BF16), seq=seq, prologue="normmod",
                                         g=norm1[i], sc=sc1, sh=sh1, w_side=w_in[:, ODD_MAIN:],
                                         name="odd_in_proj")
            def per_row(p):
                p = p.reshape(2, SSD_GROUPS, SSD_HPG).transpose(1, 0, 2)
                return jnp.tile(p, (1, 1, LANES // SSD_HPG))[..., None]

            def per_lane(p):
                p = p.reshape(*p.shape[:-1], SSD_GROUPS, SSD_HPG)
                p = jnp.moveaxis(p, -2, 0)
                return jnp.repeat(p, SSD_HEADDIM, axis=-1)[..., None, :]

            y = _ssd(proj.reshape(nb, seq, ODD_MAIN), dt_raw, per_row(od_dt_bias[j]), per_row(od_a_log[j]),
                     per_lane(od_d_skip[j]), od_conv_w[j], od_conv_b[j], nb, seq)
            x = _fused_matmul(y.reshape(t, SSD_INNER), od_w_out[j].astype(BF16), seq=seq, prologue="norm",
                              g=od_norm[j], res=x, gate=g1, tn=512, name="odd_out_proj")
        last = i == depth - 1
        x = _moe_layer(x, norm2[i], sc2, sh2, g2, moe_grp_w[i], moe_grp_b[i], moe_exp_w[i], moe_exp_b[i],
                       moe_w_gate, moe_w_up, moe_w_down, i, final_norm, seq, final=last,
                       split=n_prompt * seq if last else None)
    y_prompt, y_sample = x
    return (y_prompt.reshape(n_prompt, seq, d), y_sample.reshape(nb - n_prompt, seq, d))
```

```python
import functools
import math

import jax
import jax.numpy as jnp
import numpy as np
from jax import lax
from jax.experimental import pallas as pl
from jax.experimental.pallas import tpu as pltpu

F32 = jnp.float32
BF16 = jnp.bfloat16
I32 = jnp.int32

D_MODEL = 2048
N_MOD = 6
EPS = 1e-6
CONV_W = 5
CONV_HALO = 8

RET_HEADS = 8
RET_DV = D_MODEL // RET_HEADS
RET_DK = RET_DV // 2
RET_DECAY_FWD = 5.0
RET_DECAY_BWD = 5.5
ROPE_BASE = 10000.0
RET_CHUNK = 256
MLSTM_HEADS = 4
MLSTM_DV = D_MODEL // MLSTM_HEADS
MLSTM_DK = MLSTM_DV // 2
MLSTM_CHUNK = 128
GATE_ROWS = 8
SSD_INNER = 2 * D_MODEL
SSD_HEADDIM = 64
SSD_HEADS = SSD_INNER // SSD_HEADDIM
SSD_GROUPS = 8
SSD_HPG = SSD_HEADS // SSD_GROUPS
SSD_STATE = 128
SSD_CHUNK = 128
MOE_GROUPS = 4
MOE_EPG = 8
MOE_EXPERTS = MOE_GROUPS * MOE_EPG
EXPERT_FF = D_MODEL // 4
MOE_BLOCK = 128

RET_QK = RET_HEADS * RET_DK
RET_V = RET_HEADS * RET_DV
MLSTM_QK = MLSTM_HEADS * MLSTM_DK
MLSTM_V = MLSTM_HEADS * MLSTM_DV
MLSTM_NGATE = 4 * MLSTM_HEADS
EVEN_MAIN = 2 * RET_QK + 2 * RET_V + 2 * MLSTM_QK + 2 * MLSTM_V
EVEN_MIX = RET_V + MLSTM_V
SSD_BC = SSD_GROUPS * SSD_STATE
SSD_CONV_CH = SSD_INNER + 2 * SSD_BC
ODD_MAIN = SSD_INNER + SSD_CONV_CH

PROLOGUE_ROWS = 256
CAST_ROWS = 256
GATHER_UNROLL = 32
SCAN_UNROLL = 2
LANES = 128
VMEM_LIMIT = 56 * 1024 * 1024

NEG_INF = float("-inf")


def _cparams(sem, vmem=VMEM_LIMIT):
    return pltpu.CompilerParams(dimension_semantics=sem, vmem_limit_bytes=vmem)


def _split3(x):
    hi = x.astype(BF16)
    r = x - hi.astype(F32)
    mid = r.astype(BF16)
    lo = (r - mid.astype(F32)).astype(BF16)
    return hi, mid, lo


def _dot(a, b):
    return jnp.dot(a, b, preferred_element_type=F32)


def _dot_nt(a, b):
    return lax.dot_general(a, b, (((1,), (1,)), ((), ())), preferred_element_type=F32)


def _dot_tn(a, b):
    return lax.dot_general(a, b, (((0,), (0,)), ((), ())), preferred_element_type=F32)


def _dot01_left(m01, x):
    hi, mid, lo = _split3(x)
    return _dot(m01, hi) + _dot(m01, mid) + _dot(m01, lo)


def _dot01_right(x, m01):
    hi, mid, lo = _split3(x)
    return _dot(hi, m01) + _dot(mid, m01) + _dot(lo, m01)


def _tri(n, kind):
    r = lax.broadcasted_iota(I32, (n, n), 0)
    c = lax.broadcasted_iota(I32, (n, n), 1)
    m = {"le": r <= c, "ge": r >= c, "gt": r > c}[kind]
    return jnp.where(m, 1.0, 0.0).astype(BF16)


def _shr(x, pow2):
    return lax.shift_right_arithmetic(x, jnp.int32(int(math.log2(pow2))))


def _sigmoid(x):
    return 1.0 / (1.0 + jnp.exp(-x))


def _silu(x):
    return x * _sigmoid(x)


def _softplus(x):
    return jnp.maximum(x, 0.0) + jnp.log1p(jnp.exp(-jnp.abs(x)))


def _log_sigmoid(x):
    return jnp.minimum(x, 0.0) - jnp.log1p(jnp.exp(-jnp.abs(x)))


def _rms(x, g):
    ms = jnp.mean(x * x, axis=-1, keepdims=True)
    return x * lax.rsqrt(ms + EPS) * g


def _head_norm(y, g):
    mu = jnp.mean(y, axis=-1, keepdims=True)
    yc = y - mu
    var = jnp.mean(yc * yc, axis=-1, keepdims=True)
    return yc * lax.rsqrt(var + EPS) * g


def _mod_body(c_ref, w_ref, b_ref, o_ref):
    c = c_ref[...]
    o_ref[0] = _dot(_silu(c).astype(BF16), w_ref[0].astype(BF16)) + b_ref[0]


def _modulation(c_pad, ada_w, ada_b):
    depth, d, n = ada_w.shape
    m = c_pad.shape[0]
    tn = 1024
    return pl.pallas_call(
        _mod_body,
        out_shape=jax.ShapeDtypeStruct((depth, m, n), F32),
        grid=(depth, n // tn),
        in_specs=[
            pl.BlockSpec((m, d), lambda l, j: (0, 0)),
            pl.BlockSpec((1, d, tn), lambda l, j: (l, 0, j)),
            pl.BlockSpec((1, 1, tn), lambda l, j: (l, 0, j)),
        ],
        out_specs=pl.BlockSpec((1, m, tn), lambda l, j: (l, 0, j)),
        compiler_params=_cparams(("parallel", "parallel")),
        name="modulation",
    )(c_pad, ada_w, ada_b.reshape(depth, 1, n))


def _mm_body(*refs, prologue, epilogue, side, two_lhs, n_head, split, res_split):
    it = iter(refs)
    x_ref = next(it)
    xt_ref = next(it) if split else None
    x2_ref = next(it) if two_lhs else None
    g_ref = next(it) if prologue != "none" else None
    sc_ref = next(it) if prologue == "normmod" else None
    sh_ref = next(it) if prologue == "normmod" else None
    w_ref = next(it)
    ws_ref = next(it) if side else None
    res_ref = next(it) if epilogue == "residual" else None
    rest_ref = next(it) if res_split else None
    gate_ref = next(it) if epilogue == "residual" else None
    o_ref = next(it)
    os_ref = next(it) if side else None
    h_scr = next(it) if prologue != "none" else None

    def in_head():
        return pl.program_id(0) < n_head

    if prologue != "none":
        def run_prologue(src_ref):
            rows_per = PROLOGUE_ROWS

            def chunk(i, carry):
                rows = pl.ds(pl.multiple_of(i * rows_per, rows_per), rows_per)
                y = _rms(src_ref[rows, :].astype(F32), g_ref[...])
                if prologue == "normmod":
                    y = y * (1.0 + sc_ref[0]) + sh_ref[0]
                hb = y.astype(BF16)
                h_scr[rows, :] = hb
                if side:
                    h_lo = (y - hb.astype(F32)).astype(BF16)
                    ws = ws_ref[...]
                    w_hi = ws.astype(BF16)
                    w_lo = (ws - w_hi.astype(F32)).astype(BF16)
                    os_ref[0, :, rows] = (_dot(hb, w_hi) + _dot(h_lo, w_hi) + _dot(hb, w_lo)).T
                return carry

            lax.fori_loop(0, src_ref.shape[0] // rows_per, chunk, 0)

        first_col = pl.program_id(1) == 0
        if split:
            pl.when(first_col & in_head())(lambda: run_prologue(x_ref))
            pl.when(first_col & jnp.logical_not(in_head()))(lambda: run_prologue(xt_ref))
        else:
            pl.when(first_col)(lambda: run_prologue(x_ref))
        lhs = h_scr[...]
    else:
        assert not split
        lhs = x_ref[...]
    if two_lhs:
        k1 = x_ref.shape[1]
        acc = _dot(lhs, w_ref[:k1, :]) + _dot(x2_ref[...], w_ref[k1:, :])
    else:
        acc = _dot(lhs, w_ref[...])
    if epilogue == "residual" and rest_ref is not None:
        @pl.when(in_head())
        def _():
            o_ref[...] = res_ref[...] + gate_ref[0] * acc

        @pl.when(jnp.logical_not(in_head()))
        def _():
            o_ref[...] = rest_ref[...] + gate_ref[0] * acc
    elif epilogue == "residual":
        o_ref[...] = res_ref[...] + gate_ref[0] * acc
    else:
        o_ref[...] = acc.astype(o_ref.dtype)


def _fused_matmul(x, w, *, seq, x_tail=None, x2=None, prologue="none", g=None, sc=None, sh=None,
                  w_side=None, res=None, res_tail=None, gate=None, out_dtype=BF16, tm=1024, tn=1024,
                  name="proj"):
    t, k = x.shape
    n = w.shape[1]
    tm = min(tm, seq)
    tn = min(tn, n)
    split = x_tail is not None
    res_split = res_tail is not None
    n_head = (x.shape[0] if split else res.shape[0] if res_split else t) // tm
    if split:
        t = t + x_tail.shape[0]
    assert t % tm == 0 and seq % tm == 0 and n % tn == 0
    tps = seq // tm
    epilogue = "residual" if res is not None else "plain"
    side = w_side is not None
    two_lhs = x2 is not None
    assert not (two_lhs and prologue != "none")

    def head_rows(i):
        return jnp.minimum(i, n_head - 1)

    def tail_rows(i):
        return jnp.maximum(i - n_head, 0)

    if split:
        in_specs = [pl.BlockSpec((tm, k), lambda i, j: (head_rows(i), 0)),
                    pl.BlockSpec((tm, k), lambda i, j: (tail_rows(i), 0))]
        args = [x, x_tail]
    else:
        in_specs = [pl.BlockSpec((tm, k), lambda i, j: (i, 0))]
        args = [x]
    if two_lhs:
        in_specs.append(pl.BlockSpec((tm, x2.shape[1]), lambda i, j: (i, 0)))
        args.append(x2)
        k = k + x2.shape[1]
    if prologue != "none":
        in_specs.append(pl.BlockSpec((1, k), lambda i, j: (0, 0)))
        args.append(g.reshape(1, k))
    if prologue == "normmod":
        in_specs += [pl.BlockSpec((1, 1, k), lambda i, j: (i // tps, 0, 0))] * 2
        args += [sc, sh]
    in_specs.append(pl.BlockSpec((k, tn), lambda i, j: (0, j)))
    args.append(w)
    if side:
        in_specs.append(pl.BlockSpec((k, LANES), lambda i, j: (0, 0)))
        args.append(w_side)
    if epilogue == "residual":
        if res_split:
            in_specs += [pl.BlockSpec((tm, tn), lambda i, j: (head_rows(i), jnp.where(i < n_head, j, 0))),
                         pl.BlockSpec((tm, tn), lambda i, j: (tail_rows(i), jnp.where(i < n_head, 0, j)))]
            args += [res, res_tail]
        else:
            in_specs.append(pl.BlockSpec((tm, tn), lambda i, j: (i, j)))
            args.append(res)
        in_specs.append(pl.BlockSpec((1, 1, tn), lambda i, j: (i // tps, 0, j)))
        args.append(gate)
        out_dtype = F32
    out_shape = [jax.ShapeDtypeStruct((t, n), out_dtype)]
    out_specs = [pl.BlockSpec((tm, tn), lambda i, j: (i, j))]
    if side:
        out_shape.append(jax.ShapeDtypeStruct((t // seq, LANES, seq), F32))
        out_specs.append(pl.BlockSpec((1, LANES, tm), lambda i, j: (i // tps, 0, i % tps)))
    scratch = [pltpu.VMEM((tm, k), BF16)] if prologue != "none" else []
    outs = pl.pallas_call(
        functools.partial(_mm_body, prologue=prologue, epilogue=epilogue, side=side, two_lhs=two_lhs,
                          n_head=n_head, split=split, res_split=res_split),
        out_shape=out_shape,
        grid=(t // tm, n // tn),
        in_specs=in_specs,
        out_specs=out_specs,
        scratch_shapes=scratch,
        compiler_params=_cparams(("parallel", "arbitrary")),
        name=name,
    )(*args)
    return outs if side else outs[0]


CONV_ROWS = 128


def _conv_fill(src_ref, pad_scr, seq):
    ch = pad_scr.shape[1]
    halo = CONV_HALO
    rows = CONV_ROWS
    zeros = jnp.zeros((halo, ch), F32)
    pad_scr[pl.ds(0, halo), :] = zeros
    pad_scr[pl.ds(seq + halo, halo), :] = zeros

    def fill(i, carry):
        r0 = pl.multiple_of(i * rows, rows)
        pad_scr[pl.ds(pl.multiple_of(r0 + halo, halo), rows), :] = src_ref[0, pl.ds(r0, rows), :].astype(F32)
        return carry

    lax.fori_loop(0, seq // rows, fill, 0)


def _conv_silu_rows(pad_scr, w_ref, b_ref, r0, emit):
    ch = pad_scr.shape[1]
    halo = CONV_HALO
    rows = CONV_ROWS
    win = rows + 2 * halo
    half = (CONV_W - 1) // 2
    for lane0 in range(0, ch, LANES):
        cols = slice(lane0, lane0 + LANES)
        window = pad_scr[pl.ds(r0, win), cols]
        acc = jnp.zeros((rows, LANES), F32) + b_ref[:, cols]
        for j in range(CONV_W):
            d = j - half
            shifted = window if d == 0 else pltpu.roll(window, (-d) % win, axis=0)
            acc = acc + w_ref[j:j + 1, cols] * shifted[halo:halo + rows, :]
        emit(lane0, _silu(acc))


def _two_ended_scan(n_chunks, prepare, step, finish):
    assert n_chunks % 2 == 0
    half = n_chunks // 2

    def first(i, carry):
        prepare(i)
        prepare(n_chunks - 1 - i)
        step(i)
        return carry

    def second(i, carry):
        step(i)
        finish(i)
        finish(n_chunks - 1 - i)
        return carry

    unroll = SCAN_UNROLL if half % SCAN_UNROLL == 0 else 1
    lax.fori_loop(0, half, first, 0, unroll=unroll)
    lax.fori_loop(half, n_chunks, second, 0, unroll=unroll)


def _ret_body(lg_ref, q_ref, k_ref, v_ref, g_ref, cos_ref, sin_ref, gn_ref, o_ref,
              qs_scr, ks_scr, acc_scr, st_scr, *, seq):
    c_len = RET_CHUNK
    n_chunks = seq // c_len
    h = pl.program_id(1)
    lgf = lg_ref[0, h]
    lgb = lg_ref[1, h]
    ri = lax.broadcasted_iota(I32, (c_len, c_len), 0)
    ci = lax.broadcasted_iota(I32, (c_len, c_len), 1)
    diff = (ri - ci).astype(F32)
    dmat = jnp.exp(jnp.where(diff >= 0, lgf * diff, -lgb * diff))
    pos = lax.broadcasted_iota(I32, (c_len, 1), 0).astype(F32)
    qdec_f = jnp.exp(lgf * (pos + 1.0))
    kdec_f = jnp.exp(lgf * (c_len - 1.0 - pos))
    cdec_f = jnp.exp(jnp.full((1, 1), c_len, F32) * lgf)
    qdec_b = jnp.exp(lgb * (c_len - pos))
    kdec_b = jnp.exp(lgb * pos)
    cdec_b = jnp.exp(jnp.full((1, 1), c_len, F32) * lgb)
    half = RET_DK // 2

    def rope(x, rows):
        return x * cos_ref[rows, :] + pltpu.roll(x, half, axis=1) * sin_ref[rows, :]

    def chunk_rows(c):
        return pl.ds(pl.multiple_of(c * c_len, c_len), c_len)

    def prepare(c):
        rows = chunk_rows(c)
        qs_scr[rows, :] = rope(q_ref[0, rows, :].astype(F32), rows).astype(BF16)
        ks_scr[rows, :] = rope(k_ref[0, rows, :].astype(F32), rows) * (RET_DK ** -0.5)

    st_scr[...] = jnp.zeros_like(st_scr)

    def step(i):
        rows = chunk_rows(i)
        qb = qs_scr[rows, :]
        k = ks_scr[rows, :]
        v = v_ref[0, rows, :]
        p = (_dot_nt(qb, k.astype(BF16)) * dmat).astype(BF16)
        acc_scr[0, rows, :] = _dot(p, v) + qdec_f * _dot(qb, st_scr[0].astype(BF16))
        st_scr[0] = cdec_f * st_scr[0] + _dot_tn((k * kdec_f).astype(BF16), v)

        rows = chunk_rows(n_chunks - 1 - i)
        k = ks_scr[rows, :]
        acc_scr[1, rows, :] = qdec_b * _dot(qs_scr[rows, :], st_scr[1].astype(BF16))
        st_scr[1] = cdec_b * st_scr[1] + _dot_tn((k * kdec_b).astype(BF16), v_ref[0, rows, :])

    def finish(c):
        rows = chunk_rows(c)
        o = acc_scr[0, rows, :] + acc_scr[1, rows, :]
        gate = g_ref[0, rows, :].astype(F32)
        o_ref[0, rows, :] = (_head_norm(o, gn_ref[0]) * _silu(gate)).astype(o_ref.dtype)

    _two_ended_scan(n_chunks, prepare, step, finish)


def _retention(proj, lg, cos_t, sin_t, ret_gn, nb, seq):
    kq, kv = RET_DK, RET_DV
    grid_spec = dict(
        grid=(nb, RET_HEADS),
        in_specs=[
            pl.BlockSpec(memory_space=pltpu.SMEM),
            pl.BlockSpec((1, seq, kq), lambda b, h: (b, 0, h)),
            pl.BlockSpec((1, seq, kq), lambda b, h: (b, 0, RET_QK // kq + h)),
            pl.BlockSpec((1, seq, kv), lambda b, h: (b, 0, 2 * RET_QK // kv + h)),
            pl.BlockSpec((1, seq, kv), lambda b, h: (b, 0, (2 * RET_QK + RET_V) // kv + h)),
            pl.BlockSpec((seq, kq), lambda b, h: (0, 0)),
            pl.BlockSpec((seq, kq), lambda b, h: (0, 0)),
            pl.BlockSpec((1, 1, kv), lambda b, h: (h, 0, 0)),
        ],
        out_specs=pl.BlockSpec((1, seq, kv), lambda b, h: (b, 0, h)),
        scratch_shapes=[
            pltpu.VMEM((seq, kq), BF16),
            pltpu.VMEM((seq, kq), F32),
            pltpu.VMEM((2, seq, kv), F32),
            pltpu.VMEM((2, kq, kv), F32),
        ],
    )
    return pl.pallas_call(
        functools.partial(_ret_body, seq=seq),
        out_shape=jax.ShapeDtypeStruct((nb, seq, RET_V), BF16),
        **grid_spec,
        compiler_params=_cparams(("parallel", "parallel")),
        name="retention",
    )(lg, proj, proj, proj, proj, cos_t, sin_t, ret_gn.reshape(RET_HEADS, 1, kv))


def _col_of(mat, c):
    lane = lax.broadcasted_iota(I32, mat.shape, 1)
    return jnp.sum(jnp.where(lane == c, mat, 0.0), axis=1, keepdims=True)


def _mlstm_body(gb_ref, q_ref, k_ref, v_ref, o_gate_ref, wq_ref, wk_ref, bq_ref, bk_ref,
                gr_ref, gn_ref, o_ref,
                padq_scr, padk_scr, qs_scr, ks_scr, acc_scr, c_scr, n_scr, m_scr, row_scr, col_scr,
                *, seq):
    ln = MLSTM_CHUNK
    assert ln == CONV_ROWS
    n_chunks = seq // ln
    h = pl.program_id(1)

    _conv_fill(q_ref, padq_scr, seq)
    _conv_fill(k_ref, padk_scr, seq)

    def prepare(c):
        r0 = pl.multiple_of(c * ln, ln)
        rows = pl.ds(r0, ln)

        def emit_q(lane0, y):
            qs_scr[rows, lane0:lane0 + LANES] = (y * (MLSTM_DK ** -0.5)).astype(BF16)

        def emit_k(lane0, y):
            ks_scr[rows, lane0:lane0 + LANES] = y

        _conv_silu_rows(padq_scr, wq_ref, bq_ref, r0, emit_q)
        _conv_silu_rows(padk_scr, wk_ref, bk_ref, r0, emit_k)

    row_scr[...] = jnp.zeros_like(row_scr)
    for t in range(4):
        for c in range(n_chunks):
            row_scr[t, c:c + 1, :] = gr_ref[0, t:t + 1, c * ln:(c + 1) * ln]
    tri_le = _tri(ln, "le")
    tri_ge = _tri(ln, "ge")
    for d in range(2):
        ig = row_scr[2 * d] + gb_ref[(2 * d) * MLSTM_HEADS + h]
        lf = _log_sigmoid(row_scr[2 * d + 1] + gb_ref[(2 * d + 1) * MLSTM_HEADS + h])
        bc = _dot01_right(lf, tri_le if d == 0 else tri_ge)
        row_scr[2 * d] = ig
        row_scr[2 * d + 1] = bc
        col_scr[2 * d] = ig.T
        col_scr[2 * d + 1] = bc.T

    ri = lax.broadcasted_iota(I32, (ln, ln), 0)
    ci = lax.broadcasted_iota(I32, (ln, ln), 1)

    def chunk_step(c, d):
        r0 = pl.multiple_of(c * ln, ln)
        rows = pl.ds(r0, ln)
        qb = qs_scr[rows, :]
        kf = ks_scr[rows, :]
        v = v_ref[0, rows, :]
        i_row = row_scr[2 * d, pl.ds(c, 1), :]
        b_row = row_scr[2 * d + 1, pl.ds(c, 1), :]
        i_col = _col_of(col_scr[2 * d], c)
        b_col = _col_of(col_scr[2 * d + 1], c)
        m_st = m_scr[d]
        mask = (ri >= ci) if d == 0 else (ri <= ci)
        logd = jnp.where(mask, b_col - b_row + i_row, NEG_INF)
        m_inter = b_col + m_st
        m_row = jnp.maximum(m_inter, jnp.max(logd, axis=1, keepdims=True))
        sc = _dot_nt(qb, kf.astype(BF16)) * jnp.exp(logd - m_row)
        inter = jnp.exp(m_inter - m_row)
        num = _dot(sc.astype(BF16), v) + inter * _dot(qb, c_scr[d].astype(BF16))
        den = jnp.sum(sc, axis=1, keepdims=True) + inter * jnp.sum(
            qb.astype(F32) * n_scr[d], axis=1, keepdims=True)
        hh = num / jnp.maximum(jnp.abs(den), jnp.exp(-m_row))
        b_end = b_row[:, ln - 1:ln] if d == 0 else b_row[:, 0:1]
        logw = b_end - b_col + i_col
        m_new = jnp.maximum(b_end + m_st, jnp.max(logw, axis=0, keepdims=True))
        kw = kf * jnp.exp(logw - m_new)
        dec = jnp.exp(b_end + m_st - m_new)
        c_scr[d] = dec * c_scr[d] + _dot_tn(kw.astype(BF16), v)
        n_scr[d] = dec * n_scr[d] + jnp.sum(kw, axis=0, keepdims=True)
        m_scr[d] = m_new
        acc_scr[d, rows, :] = hh

    c_scr[...] = jnp.zeros_like(c_scr)
    n_scr[...] = jnp.zeros_like(n_scr)
    m_scr[...] = jnp.zeros_like(m_scr)

    def step(i):
        chunk_step(i, 0)
        chunk_step(n_chunks - 1 - i, 1)

    def finish(c):
        rows = pl.ds(pl.multiple_of(c * ln, ln), ln)
        y = _head_norm(acc_scr[0, rows, :] + acc_scr[1, rows, :], gn_ref[0])
        o_ref[0, rows, :] = (y * _sigmoid(o_gate_ref[0, rows, :].astype(F32))).astype(o_ref.dtype)

    _two_ended_scan(n_chunks, prepare, step, finish)


def _mlstm(proj, gate_rows, gate_b, conv_w, conv_b, mlstm_gn, nb, seq):
    kq, kv = MLSTM_DK, MLSTM_DV
    q0 = 2 * RET_QK + 2 * RET_V
    k0 = q0 + MLSTM_QK
    v0 = k0 + MLSTM_QK
    o0 = v0 + MLSTM_V
    grid_spec = dict(
        grid=(nb, MLSTM_HEADS),
        in_specs=[
            pl.BlockSpec(memory_space=pltpu.SMEM),
            pl.BlockSpec((1, seq, kq), lambda b, h: (b, 0, q0 // kq + h)),
            pl.BlockSpec((1, seq, kq), lambda b, h: (b, 0, k0 // kq + h)),
            pl.BlockSpec((1, seq, kv), lambda b, h: (b, 0, v0 // kv + h)),
            pl.BlockSpec((1, seq, kv), lambda b, h: (b, 0, o0 // kv + h)),
            pl.BlockSpec((CONV_W, kq), lambda b, h: (0, h)),
            pl.BlockSpec((CONV_W, kq), lambda b, h: (0, MLSTM_QK // kq + h)),
            pl.BlockSpec((1, kq), lambda b, h: (0, h)),
            pl.BlockSpec((1, kq), lambda b, h: (0, MLSTM_QK // kq + h)),
            pl.BlockSpec((1, GATE_ROWS, seq), lambda b, h: (b, h, 0)),
            pl.BlockSpec((1, 1, kv), lambda b, h: (h, 0, 0)),
        ],
        out_specs=pl.BlockSpec((1, seq, kv), lambda b, h: (b, 0, h)),
        scratch_shapes=[
            pltpu.VMEM((seq + 2 * CONV_HALO, kq), F32),
            pltpu.VMEM((seq + 2 * CONV_HALO, kq), F32),
            pltpu.VMEM((seq, kq), BF16),
            pltpu.VMEM((seq, kq), F32),
            pltpu.VMEM((2, seq, kv), F32),
            pltpu.VMEM((2, kq, kv), F32),
            pltpu.VMEM((2, 1, kq), F32),
            pltpu.VMEM((2, 1, 1), F32),
            pltpu.VMEM((4, LANES, MLSTM_CHUNK), F32),
            pltpu.VMEM((4, MLSTM_CHUNK, LANES), F32),
        ],
    )
    return pl.pallas_call(
        functools.partial(_mlstm_body, seq=seq),
        out_shape=jax.ShapeDtypeStruct((nb, seq, MLSTM_V), BF16),
        **grid_spec,
        compiler_params=_cparams(("parallel", "parallel")),
        name="mlstm",
    )(gate_b, proj, proj, proj, proj, conv_w, conv_w, conv_b.reshape(1, -1), conv_b.reshape(1, -1),
      gate_rows, mlstm_gn.reshape(MLSTM_HEADS, 1, kv))


def _ssd_body(z_ref, x_ref, b_ref, c_ref, wx_ref, wb_ref, wc_ref, bx_ref, bb_ref, bc_ref,
              dtf_ref, dtb_ref, bias_ref, alog_ref, dskip_ref, o_ref,
              padx_scr, padb_scr, padc_scr, xs_scr, bs_scr, bst_scr, cs_scr, y_scr, st_scr,
              acr_scr, dtr_scr, er_scr, ur_scr, act_scr, *, seq):
    ln = SSD_CHUNK
    assert ln == CONV_ROWS
    n_chunks = seq // ln
    hp = SSD_HEADDIM
    n_pairs = SSD_HPG // 2

    _conv_fill(x_ref, padx_scr, seq)
    _conv_fill(b_ref, padb_scr, seq)
    _conv_fill(c_ref, padc_scr, seq)

    def prepare(c):
        r0 = pl.multiple_of(c * ln, ln)
        rows = pl.ds(r0, ln)

        def emit_x(lane0, y):
            xs_scr[rows, lane0:lane0 + LANES] = y

        def emit_b(lane0, y):
            bs_scr[rows, :] = y.astype(BF16)
            bst_scr[:, rows] = y.T.astype(BF16)

        def emit_c(lane0, y):
            cs_scr[rows, :] = y.astype(BF16)

        _conv_silu_rows(padx_scr, wx_ref, bx_ref, r0, emit_x)
        _conv_silu_rows(padb_scr, wb_ref, bb_ref, r0, emit_b)
        _conv_silu_rows(padc_scr, wc_ref, bc_ref, r0, emit_c)

    for d, dt_ref in enumerate((dtf_ref, dtb_ref)):
        acr_scr[d] = jnp.zeros((LANES, ln), F32)
        for c in range(n_chunks):
            acr_scr[d, c * SSD_HPG:(c + 1) * SSD_HPG, :] = dt_ref[0, :, c * ln:(c + 1) * ln]
        dt = _softplus(acr_scr[d] + bias_ref[0, d])
        adt = dt * (-jnp.exp(alog_ref[0, d]))
        acum = _dot01_right(adt, _tri(ln, "le" if d == 0 else "ge"))
        a_end = acum[:, ln - 1:ln] if d == 0 else acum[:, 0:1]
        acr_scr[d] = acum
        dtr_scr[d] = dt
        er_scr[d] = jnp.exp(acum)
        ur_scr[d] = dt * jnp.exp(a_end - acum)
        act_scr[d] = acum.T

    ri = lax.broadcasted_iota(I32, (ln, ln), 0)
    ci = lax.broadcasted_iota(I32, (ln, ln), 1)
    in_first = lax.broadcasted_iota(I32, (1, LANES), 1) < hp
    on_diag = ri == ci

    def chunk_rows(c):
        return pl.ds(pl.multiple_of(c * ln, ln), ln)

    def dir_step(d, c):
        rows = chunk_rows(c)
        xb = xs_scr[rows, :].astype(BF16)
        bcm = bs_scr[rows, :]
        bct = bst_scr[:, rows].astype(F32)
        ccm = cs_scr[rows, :]
        cb = _dot_nt(ccm, bcm)
        carried = _dot(ccm, st_scr[d].astype(BF16)).astype(BF16)
        head_rows = pl.ds(pl.multiple_of(c * SSD_HPG, SSD_HPG), SSD_HPG)
        arow = acr_scr[d, head_rows, :]
        dtrow = dtr_scr[d, head_rows, :]
        erow = er_scr[d, head_rows, :]
        urow = ur_scr[d, head_rows, :]
        acols = pltpu.roll(act_scr[d], (LANES - c * SSD_HPG) & (LANES - 1), axis=1)
        mask = (ri >= ci) if d == 0 else (ri <= ci)
        end = ln - 1 if d == 0 else 0
        pieces = []
        for pair in range(n_pairs):
            lanes = slice(pair * LANES, (pair + 1) * LANES)
            lhs, rhs, lhs_state, keep = [], [], [], []
            for sub in range(2):
                k = 2 * pair + sub
                sel = in_first if sub == 0 else ~in_first
                dec = jnp.exp(jnp.where(mask, acols[:, k:k + 1] - arow[k:k + 1, :], NEG_INF))
                lhs.append((cb * dec * dtrow[k:k + 1, :]).astype(BF16))
                rhs.append(jnp.where(sel, xb[:, lanes], jnp.zeros((ln, LANES), BF16)))
                lhs_state.append((bct * urow[k:k + 1, :]).astype(BF16))
                keep.append(jnp.exp(arow[k:k + 1, end:end + 1]))
            for sub in range(2):
                k = 2 * pair + sub
                sel = in_first if sub == 0 else ~in_first
                lhs.append(jnp.where(on_diag, erow[k:k + 1, :], 0.0).astype(BF16))
                rhs.append(jnp.where(sel, carried[:, lanes], jnp.zeros((ln, LANES), BF16)))
            pieces.append(_dot(jnp.concatenate(lhs, axis=1), jnp.concatenate(rhs, axis=0)))
            st_scr[d, :, lanes] = (jnp.where(in_first, keep[0], keep[1]) * st_scr[d, :, lanes]
                                   + _dot(jnp.concatenate(lhs_state, axis=1),
                                          jnp.concatenate(rhs[:2], axis=0)))
        y_scr[d, rows, :] = jnp.concatenate(pieces, axis=1)

    st_scr[...] = jnp.zeros_like(st_scr)

    def step(i):
        dir_step(0, i)
        dir_step(1, n_chunks - 1 - i)

    def finish(c):
        rows = chunk_rows(c)
        y = y_scr[0, rows, :] + y_scr[1, rows, :] + dskip_ref[0] * xs_scr[rows, :]
        o_ref[0, rows, :] = (y * _silu(z_ref[0, rows, :].astype(F32))).astype(o_ref.dtype)

    _two_ended_scan(n_chunks, prepare, step, finish)


def _ssd(proj, dt_t, bias_col, alog_col, dskip_x, conv_w, conv_b, nb, seq):
    width = SSD_HPG * SSD_HEADDIM
    ns = SSD_STATE
    x0 = SSD_INNER
    b0 = 2 * SSD_INNER
    c0 = b0 + SSD_BC
    cb = conv_b.reshape(1, -1)
    return pl.pallas_call(
        functools.partial(_ssd_body, seq=seq),
        out_shape=jax.ShapeDtypeStruct((nb, seq, SSD_INNER), BF16),
        grid=(nb, SSD_GROUPS),
        in_specs=[
            pl.BlockSpec((1, seq, width), lambda b, g: (b, 0, g)),
            pl.BlockSpec((1, seq, width), lambda b, g: (b, 0, x0 // width + g)),
            pl.BlockSpec((1, seq, ns), lambda b, g: (b, 0, b0 // ns + g)),
            pl.BlockSpec((1, seq, ns), lambda b, g: (b, 0, c0 // ns + g)),
            pl.BlockSpec((CONV_W, width), lambda b, g: (0, g)),
            pl.BlockSpec((CONV_W, ns), lambda b, g: (0, SSD_INNER // ns + g)),
            pl.BlockSpec((CONV_W, ns), lambda b, g: (0, (SSD_INNER + SSD_BC) // ns + g)),
            pl.BlockSpec((1, width), lambda b, g: (0, g)),
            pl.BlockSpec((1, ns), lambda b, g: (0, SSD_INNER // ns + g)),
            pl.BlockSpec((1, ns), lambda b, g: (0, (SSD_INNER + SSD_BC) // ns + g)),
            pl.BlockSpec((1, SSD_HPG, seq), lambda b, g: (b, g, 0)),
            pl.BlockSpec((1, SSD_HPG, seq), lambda b, g: (b, SSD_GROUPS + g, 0)),
            pl.BlockSpec((1, 2, LANES, 1), lambda b, g: (g, 0, 0, 0)),
            pl.BlockSpec((1, 2, LANES, 1), lambda b, g: (g, 0, 0, 0)),
            pl.BlockSpec((1, 1, width), lambda b, g: (g, 0, 0)),
        ],
        out_specs=pl.BlockSpec((1, seq, width), lambda b, g: (b, 0, g)),
        scratch_shapes=[
            pltpu.VMEM((seq + 2 * CONV_HALO, width), F32),
            pltpu.VMEM((seq + 2 * CONV_HALO, ns), F32),
            pltpu.VMEM((seq + 2 * CONV_HALO, ns), F32),
            pltpu.VMEM((seq, width), F32),
            pltpu.VMEM((seq, ns), BF16),
            pltpu.VMEM((ns, seq), BF16),
            pltpu.VMEM((seq, ns), BF16),
            pltpu.VMEM((2, seq, width), F32),
            pltpu.VMEM((2, ns, width), F32),
            pltpu.VMEM((2, LANES, SSD_CHUNK), F32),
            pltpu.VMEM((2, LANES, SSD_CHUNK), F32),
            pltpu.VMEM((2, LANES, SSD_CHUNK), F32),
            pltpu.VMEM((2, LANES, SSD_CHUNK), F32),
            pltpu.VMEM((2, SSD_CHUNK, LANES), F32),
        ],
        compiler_params=_cparams(("parallel", "parallel")),
        name="ssd",
    )(proj, proj, proj, proj, conv_w, conv_w, conv_w, cb, cb, cb,
      dt_t, dt_t, bias_col, alog_col, dskip_x)


META_E = 0
META_G = 2
META_R = 4
ROUTE_E0 = MOE_GROUPS


def _moe_input(x, g_ref, sc_ref, sh_ref):
    return _rms(x, g_ref[...]) * (1.0 + sc_ref[0]) + sh_ref[0]


def _router_body(x_ref, g_ref, sc_ref, sh_ref, w_ref, b_ref, meta_ref, cnt_ref,
                 carry_scr, whi_scr, wlo_scr):
    @pl.when(pl.program_id(0) == 0)
    def _():
        carry_scr[...] = jnp.zeros_like(carry_scr)
        w = w_ref[...]
        w_hi = w.astype(BF16)
        whi_scr[...] = w_hi
        wlo_scr[...] = (w - w_hi.astype(F32)).astype(BF16)

    meta_ref[...] = _route_rows(_moe_input(x_ref[...], g_ref, sc_ref, sh_ref),
                                whi_scr[...], wlo_scr[...], b_ref[...], carry_scr)
    cnt_ref[...] = jnp.broadcast_to(carry_scr[...], cnt_ref.shape)


def _route_rows(y, w_hi, w_lo, bias, carry_scr):
    tm = y.shape[0]
    h_hi = y.astype(BF16)
    h_lo = (y - h_hi.astype(F32)).astype(BF16)
    logits = _dot(h_hi, w_hi) + _dot(h_lo, w_hi) + _dot(h_hi, w_lo) + bias

    lane = lax.broadcasted_iota(I32, (tm, LANES), 1)
    lane_f = lane.astype(F32)
    big = float(LANES)
    is_grp = lane < MOE_GROUPS
    gl = jnp.where(is_grp, logits, NEG_INF)
    gmax = jnp.max(gl, axis=1, keepdims=True)
    gidx = jnp.min(jnp.where(gl == gmax, lane_f, big), axis=1, keepdims=True)
    gprob = 1.0 / jnp.sum(jnp.where(is_grp, jnp.exp(gl - gmax), 0.0), axis=1, keepdims=True)

    el = lane - ROUTE_E0
    el_f = el.astype(F32)
    valid = (el >= 0) & (el < MOE_EXPERTS)
    in_grp = valid & (_shr(el, MOE_EPG).astype(F32) == gidx)
    ev = jnp.where(in_grp, logits, NEG_INF)
    v1 = jnp.max(ev, axis=1, keepdims=True)
    i1 = jnp.min(jnp.where(ev == v1, el_f, big), axis=1, keepdims=True)
    ev2 = jnp.where(el_f == i1, NEG_INF, ev)
    v2 = jnp.max(ev2, axis=1, keepdims=True)
    i2 = jnp.min(jnp.where(ev2 == v2, el_f, big), axis=1, keepdims=True)
    p2 = jnp.exp(v2 - v1)
    s1 = 1.0 / (1.0 + p2)
    gate1 = s1 * gprob
    gate2 = p2 * s1 * gprob

    oh1 = jnp.where(el_f == i1, 1.0, 0.0)
    oh2 = jnp.where(el_f == i2, 1.0, 0.0)
    oh = oh1 + oh2
    before = _dot(_tri(tm, "gt"), oh.astype(BF16)) + carry_scr[...]
    rank1 = jnp.sum(oh1 * before, axis=1, keepdims=True)
    rank2 = jnp.sum(oh2 * before, axis=1, keepdims=True)
    carry_scr[...] = carry_scr[...] + jnp.sum(oh, axis=0, keepdims=True)

    meta = jnp.zeros((tm, LANES), F32)
    for col, val in ((META_E, i1), (META_E + 1, i2), (META_G, gate1), (META_G + 1, gate2),
                     (META_R, rank1), (META_R + 1, rank2)):
        meta = jnp.where(lane == col, val, meta)
    return meta


def _router(x, g, sc, sh, w_route, b_route, seq, tm=512):
    t, k = x.shape
    tm = min(tm, seq)
    tps = seq // tm
    return pl.pallas_call(
        _router_body,
        out_shape=[jax.ShapeDtypeStruct((t, LANES), F32),
                   jax.ShapeDtypeStruct((8, LANES), F32)],
        grid=(t // tm,),
        in_specs=[
            pl.BlockSpec((tm, k), lambda i: (i, 0)),
            pl.BlockSpec((1, k), lambda i: (0, 0)),
            pl.BlockSpec((1, 1, k), lambda i: (i // tps, 0, 0)),
            pl.BlockSpec((1, 1, k), lambda i: (i // tps, 0, 0)),
            pl.BlockSpec((k, LANES), lambda i: (0, 0)),
            pl.BlockSpec((1, LANES), lambda i: (0, 0)),
        ],
        out_specs=[pl.BlockSpec((tm, LANES), lambda i: (i, 0)),
                   pl.BlockSpec((8, LANES), lambda i: (0, 0))],
        scratch_shapes=[pltpu.VMEM((1, LANES), F32), pltpu.VMEM((k, LANES), BF16),
                        pltpu.VMEM((k, LANES), BF16)],
        compiler_params=_cparams(("arbitrary",)),
        name="router",
    )(x, g.reshape(1, k), sc, sh, w_route, b_route)


ZERO_BLOCKS = 2 * MOE_EXPERTS


def _dispatch_body(dest_ref, zero_ref, x_ref, g_ref, sc_ref, sh_ref, xs_hbm, h_scr, zero_scr, sem, zsem):
    i = pl.program_id(0)
    n_steps = pl.num_programs(0)
    tm = x_ref.shape[0]
    slot = i % 2

    def zero_copy(start):
        return pltpu.make_async_copy(zero_scr, xs_hbm.at[pl.ds(start, MOE_BLOCK)], zsem)

    @pl.when(i == 0)
    def _():
        zero_scr[...] = jnp.zeros_like(zero_scr)

        def issue(k, carry):
            @pl.when(zero_ref[k] >= 0)
            def _():
                zero_copy(pl.multiple_of(zero_ref[k], MOE_BLOCK)).start()
            return carry

        def drain(k, carry):
            @pl.when(zero_ref[k] >= 0)
            def _():
                zero_copy(0).wait()
            return carry

        lax.fori_loop(0, ZERO_BLOCKS, issue, 0)
        lax.fori_loop(0, ZERO_BLOCKS, drain, 0)

    def wait_rows(s):
        for _ in range(2):
            pltpu.make_async_copy(h_scr.at[s], xs_hbm.at[pl.ds(0, tm)], sem.at[s]).wait()

    @pl.when(i >= 2)
    def _():
        wait_rows(slot)

    h_scr[slot] = _moe_input(x_ref[...], g_ref, sc_ref, sh_ref)

    def body(r, carry):
        for j in range(2):
            pltpu.make_async_copy(h_scr.at[slot, pl.ds(r, 1)],
                                  xs_hbm.at[pl.ds(dest_ref[(i * tm + r) * 2 + j], 1)],
                                  sem.at[slot]).start()
        return carry

    lax.fori_loop(0, tm, body, 0, unroll=GATHER_UNROLL // 2)

    @pl.when(i == n_steps - 1)
    def _():
        @pl.when(n_steps >= 2)
        def _():
            wait_rows(1 - slot)
        wait_rows(slot)


def _dispatch(x, g, sc, sh, dest, zero_starts, n_rows, seq, tm=256):
    t, k = x.shape
    tm = min(tm, seq)
    tps = seq // tm
    grid_spec = pltpu.PrefetchScalarGridSpec(
        num_scalar_prefetch=2,
        grid=(t // tm,),
        in_specs=[
            pl.BlockSpec((tm, k), lambda i, ds, zs: (i, 0)),
            pl.BlockSpec((1, k), lambda i, ds, zs: (0, 0)),
            pl.BlockSpec((1, 1, k), lambda i, ds, zs: (i // tps, 0, 0)),
            pl.BlockSpec((1, 1, k), lambda i, ds, zs: (i // tps, 0, 0)),
        ],
        out_specs=pl.BlockSpec(memory_space=pl.ANY),
        scratch_shapes=[pltpu.VMEM((2, tm, k), F32), pltpu.VMEM((MOE_BLOCK, k), F32),
                        pltpu.SemaphoreType.DMA((2,)), pltpu.SemaphoreType.DMA],
    )
    return pl.pallas_call(
        _dispatch_body,
        out_shape=jax.ShapeDtypeStruct((n_rows, k), F32),
        grid_spec=grid_spec,
        compiler_params=_cparams(("arbitrary",)),
        name="moe_dispatch",
    )(dest, zero_starts, x, g.reshape(1, k), sc, sh)


def _row_copy(src_hbm, dst, sem, src_row, dst_row):
    return pltpu.make_async_copy(src_hbm.at[pl.ds(src_row, 1)], dst.at[pl.ds(dst_row, 1)], sem)


def _start_row_gather(idx_ref, base, n_rows, stride, src_hbm, dst, sem):
    def body(r, carry):
        _row_copy(src_hbm, dst, sem, idx_ref[base + r * stride], r).start()
        return carry
    lax.fori_loop(0, n_rows, body, 0, unroll=GATHER_UNROLL)


def _wait_row_gather(src_hbm, dst, sem, n_rows):
    pltpu.make_async_copy(src_hbm.at[pl.ds(0, n_rows)], dst, sem).wait()


def _expert_body(blk_exp_ref, n_used_ref, xs_ref, wg_ref, wu_ref, wd_ref, y_ref, wgu_scr, wdn_scr):
    i = pl.program_id(0)
    d = xs_ref.shape[1]
    used = i < n_used_ref[0]

    @pl.when(used & ((i == 0) | (blk_exp_ref[i] != blk_exp_ref[jnp.maximum(i - 1, 0)])))
    def _():
        def cast_up(c, carry):
            rows = pl.ds(pl.multiple_of(c * CAST_ROWS, CAST_ROWS), CAST_ROWS)
            wgu_scr[rows, :EXPERT_FF] = wg_ref[0, 0, rows, :].astype(BF16)
            wgu_scr[rows, EXPERT_FF:] = wu_ref[0, 0, rows, :].astype(BF16)
            return carry

        def cast_down(c, carry):
            rows = pl.ds(pl.multiple_of(c * CAST_ROWS, CAST_ROWS), CAST_ROWS)
            wdn_scr[rows, :] = wd_ref[0, 0, rows, :].astype(BF16)
            return carry

        lax.fori_loop(0, d // CAST_ROWS, cast_up, 0)
        lax.fori_loop(0, EXPERT_FF // CAST_ROWS, cast_down, 0)

    @pl.when(used)
    def _():
        a = _dot(xs_ref[...].astype(BF16), wgu_scr[...])
        hid = (_silu(a[:, :EXPERT_FF]) * a[:, EXPERT_FF:]).astype(BF16)
        y_ref[...] = _dot(hid, wdn_scr[...])

    @pl.when(jnp.logical_not(used))
    def _():
        y_ref[...] = jnp.zeros_like(y_ref)


def _experts(xs, w_gate, w_up, w_down, layer, blk_exp, n_used):
    n_rows, d = xs.shape
    n_blocks = n_rows // MOE_BLOCK
    grid_spec = pltpu.PrefetchScalarGridSpec(
        num_scalar_prefetch=2,
        grid=(n_blocks,),
        in_specs=[
            pl.BlockSpec((MOE_BLOCK, d), lambda i, be, nu: (i, 0)),
            pl.BlockSpec((1, 1, d, EXPERT_FF), lambda i, be, nu: (layer, be[i], 0, 0)),
            pl.BlockSpec((1, 1, d, EXPERT_FF), lambda i, be, nu: (layer, be[i], 0, 0)),
            pl.BlockSpec((1, 1, EXPERT_FF, d), lambda i, be, nu: (layer, be[i], 0, 0)),
        ],
        out_specs=pl.BlockSpec((MOE_BLOCK, d), lambda i, be, nu: (i, 0)),
        scratch_shapes=[pltpu.VMEM((d, 2 * EXPERT_FF), BF16),
                        pltpu.VMEM((EXPERT_FF, d), BF16)],
    )
    return pl.pallas_call(
        _expert_body,
        out_shape=jax.ShapeDtypeStruct((n_rows, d), F32),
        grid_spec=grid_spec,
        compiler_params=_cparams(("arbitrary",)),
        name="experts",
    )(blk_exp, n_used, xs, w_gate, w_up, w_down)


def _combine_body(dest_ref, y_hbm, x_ref, gate_ref, meta_ref, fn_ref, o_ref, ya_scr, sem,
                  *, final, tile0):
    i = pl.program_id(0)
    n_steps = pl.num_programs(0)
    tm = x_ref.shape[0]
    slot = i % 2

    def start(step, s):
        for j in range(2):
            _start_row_gather(dest_ref, (tile0 + step) * tm * 2 + j, tm, 2, y_hbm,
                              ya_scr.at[s, j], sem.at[s, j])

    @pl.when(i == 0)
    def _():
        start(0, 0)

    for j in range(2):
        _wait_row_gather(y_hbm, ya_scr.at[slot, j], sem.at[slot, j], tm)

    @pl.when(i + 1 < n_steps)
    def _():
        start(i + 1, 1 - slot)

    meta = meta_ref[...]
    moe = (ya_scr[slot, 0] * meta[:, META_G:META_G + 1]
           + ya_scr[slot, 1] * meta[:, META_G + 1:META_G + 2])
    out = x_ref[...] + gate_ref[0] * moe
    if final:
        out = _rms(out, fn_ref[...])
    o_ref[...] = out


def _combine(y, x, gate, meta, dest, final_norm, seq, final, tok0=0, n_tok=None, tm=256):
    t, d = x.shape
    n_tok = t if n_tok is None else n_tok
    tm = min(tm, seq)
    tps = seq // tm
    assert tok0 % seq == 0 and n_tok % seq == 0
    tile0 = tok0 // tm
    grid_spec = pltpu.PrefetchScalarGridSpec(
        num_scalar_prefetch=1,
        grid=(n_tok // tm,),
        in_specs=[
            pl.BlockSpec(memory_space=pl.ANY),
            pl.BlockSpec((tm, d), lambda i, ds: (tile0 + i, 0)),
            pl.BlockSpec((1, 1, d), lambda i, ds: ((tile0 + i) // tps, 0, 0)),
            pl.BlockSpec((tm, LANES), lambda i, ds: (tile0 + i, 0)),
            pl.BlockSpec((1, d), lambda i, ds: (0, 0)),
        ],
        out_specs=pl.BlockSpec((tm, d), lambda i, ds: (i, 0)),
        scratch_shapes=[pltpu.VMEM((2, 2, tm, d), F32), pltpu.SemaphoreType.DMA((2, 2))],
    )
    return pl.pallas_call(
        functools.partial(_combine_body, final=final, tile0=tile0),
        out_shape=jax.ShapeDtypeStruct((n_tok, d), F32),
        grid_spec=grid_spec,
        compiler_params=_cparams(("arbitrary",)),
        name="moe_combine",
    )(dest, y, x, gate, meta, final_norm.reshape(1, d))


def _moe_layer(x, g, sc, sh, gate, grp_w, grp_b, exp_w, exp_b, w_gate, w_up, w_down, layer,
               final_norm, seq, final, split=None):
    t, d = x.shape
    pad = LANES - MOE_GROUPS - MOE_EXPERTS
    w_route = jnp.concatenate([grp_w, exp_w, jnp.zeros((d, pad), F32)], axis=1)
    b_route = jnp.concatenate([grp_b, exp_b, jnp.zeros((pad,), F32)]).reshape(1, LANES)
    meta, cnt = _router(x, g, sc, sh, w_route, b_route, seq)

    expert = meta[:, META_E:META_E + 2].astype(I32)
    rank = meta[:, META_R:META_R + 2].astype(I32)
    counts = cnt[0, ROUTE_E0:ROUTE_E0 + MOE_EXPERTS].astype(I32)
    padded = (counts + MOE_BLOCK - 1) // MOE_BLOCK * MOE_BLOCK
    p_ends = jnp.cumsum(padded)
    p_starts = p_ends - padded
    dest = (p_starts[expert] + rank).reshape(-1)
    n_rows = t * 2 + MOE_EXPERTS * MOE_BLOCK
    n_blocks = n_rows // MOE_BLOCK
    blk_start = jnp.arange(n_blocks, dtype=I32) * MOE_BLOCK
    blk_exp = jnp.minimum(jnp.sum((p_ends[None, :] <= blk_start[:, None]).astype(I32), axis=1),
                          MOE_EXPERTS - 1)
    n_used = p_ends[-1:] // MOE_BLOCK
    tail = (n_used + jnp.arange(MOE_EXPERTS, dtype=I32)) * MOE_BLOCK
    zero_starts = jnp.concatenate([jnp.where(padded > counts, p_ends - MOE_BLOCK, -1),
                                   jnp.where(tail < n_rows, tail, -1)]).astype(I32)

    xs = _dispatch(x, g, sc, sh, dest, zero_starts, n_rows, seq)
    y = _experts(xs, w_gate, w_up, w_down, layer, blk_exp, n_used)
    if split is None:
        return _combine(y, x, gate, meta, dest, final_norm, seq, final)
    return tuple(_combine(y, x, gate, meta, dest, final_norm, seq, final, tok0=a, n_tok=b - a)
                 for a, b in ((0, split), (split, t)))


def _rope_tables(seq):
    half = RET_DK // 2
    inv = ROPE_BASE ** (-jnp.arange(half, dtype=F32) / half)
    ang = jnp.arange(seq, dtype=F32)[:, None] * inv[None, :]
    cos, sin = jnp.cos(ang), jnp.sin(ang)
    return jnp.concatenate([cos, cos], axis=1), jnp.concatenate([-sin, sin], axis=1)


def _pad_rows(a, axis, n):
    pad = [(0, 0)] * a.ndim
    pad[axis] = (0, n - a.shape[axis])
    return jnp.pad(a, pad)


def kernel(x_prompt, x_sample, c_prompt, c_sample, ada_w, ada_b, norm1, norm2, ev_w_in, ev_gate_b, ev_conv_w, ev_conv_b, ev_ret_gn, ev_mlstm_gn, ev_w_out, od_w_in, od_conv_w, od_conv_b, od_dt_bias, od_a_log, od_d_skip, od_norm, od_w_out, moe_grp_w, moe_grp_b, moe_exp_w, moe_exp_b, moe_w_gate, moe_w_up, moe_w_down, final_norm):
    n_prompt = x_prompt.shape[0]
    seq, d = x_prompt.shape[1], x_prompt.shape[2]
    assert x_sample.shape[1] == seq and d == D_MODEL
    assert seq % RET_CHUNK == 0 and seq // MLSTM_CHUNK <= LANES // SSD_HPG
    nb = n_prompt + x_sample.shape[0]
    t = nb * seq
    x = x_prompt.reshape(n_prompt * seq, d)
    x_tail = x_sample.reshape(t - n_prompt * seq, d)
    depth = ada_w.shape[0]

    c_all = jnp.concatenate([c_prompt, c_sample], axis=0)
    c_pad = _pad_rows(c_all, 0, -(-nb // 8) * 8)
    mod = _modulation(c_pad, ada_w, ada_b)[:, :nb].reshape(depth, nb, N_MOD, 1, d)

    heads = jnp.arange(RET_HEADS, dtype=F32)
    lg = jnp.stack([jnp.log1p(-jnp.exp2(-RET_DECAY_FWD - heads)),
                    jnp.log1p(-jnp.exp2(-RET_DECAY_BWD - heads))])
    cos_t, sin_t = _rope_tables(seq)

    for i in range(depth):
        sh1, sc1, g1, sh2, sc2, g2 = (mod[i, :, m] for m in range(N_MOD))
        j = i // 2
        if i % 2 == 0:
            w_in = ev_w_in[j]
            w_side = w_in[:, EVEN_MAIN:].reshape(d, 4, MLSTM_HEADS).transpose(0, 2, 1)
            w_side = _pad_rows(_pad_rows(w_side, 2, GATE_ROWS).reshape(d, -1), 1, LANES)
            proj, gates = _fused_matmul(x, w_in[:, :EVEN_MAIN].astype(BF16), seq=seq, x_tail=x_tail,
                                        prologue="normmod", g=norm1[i], sc=sc1, sh=sh1, w_side=w_side,
                                        name="even_in_proj")
            proj = proj.reshape(nb, seq, EVEN_MAIN)
            ret = _retention(proj, lg, cos_t, sin_t, ev_ret_gn[j], nb, seq)
            ml = _mlstm(proj, gates, ev_gate_b[j], ev_conv_w[j], ev_conv_b[j], ev_mlstm_gn[j], nb, seq)
            x = _fused_matmul(ret.reshape(t, RET_V), ev_w_out[j].astype(BF16), seq=seq,
                              x2=ml.reshape(t, MLSTM_V),
                              res=x, res_tail=x_tail, gate=g1, tn=512, name="even_out_proj")
        else:
            w_in = od_w_in[j]
            proj, dt_raw = _fused_matmul(x, w_in[:, :ODD_MAIN].astype(BF16), seq=seq, x_tail=x_tail,
                                         prologue="normmod", g=norm1[i], sc=sc1, sh=sh1,
                                         w_side=w_in[:, ODD_MAIN:], name="odd_in_proj")
            def per_row(p):
                p = p.reshape(2, SSD_GROUPS, SSD_HPG).transpose(1, 0, 2)
                return jnp.tile(p, (1, 1, LANES // SSD_HPG))[..., None]

            def per_lane(p):
                p = p.reshape(*p.shape[:-1], SSD_GROUPS, SSD_HPG)
                p = jnp.moveaxis(p, -2, 0)
                return jnp.repeat(p, SSD_HEADDIM, axis=-1)[..., None, :]

            y = _ssd(proj.reshape(nb, seq, ODD_MAIN), dt_raw, per_row(od_dt_bias[j]), per_row(od_a_log[j]),
                     per_lane(od_d_skip[j]), od_conv_w[j], od_conv_b[j], nb, seq)
            x = _fused_matmul(y.reshape(t, SSD_INNER), od_w_out[j].astype(BF16), seq=seq, prologue="norm",
                              g=od_norm[j], res=x, res_tail=x_tail, gate=g1, tn=512, name="odd_out_proj")
        x_tail = None
        last = i == depth - 1
        x = _moe_layer(x, norm2[i], sc2, sh2, g2, moe_grp_w[i], moe_grp_b[i], moe_exp_w[i], moe_exp_b[i],
                       moe_w_gate, moe_w_up, moe_w_down, i, final_norm, seq, final=last,
                       split=n_prompt * seq if last else None)
    y_prompt, y_sample = x
    return (y_prompt.reshape(n_prompt, seq, d), y_sample.reshape(nb - n_prompt, seq, d))
```

```python
import functools
import math

import jax
import jax.numpy as jnp
import numpy as np
from jax import lax
from jax.experimental import pallas as pl
from jax.experimental.pallas import tpu as pltpu

F32 = jnp.float32
BF16 = jnp.bfloat16
I32 = jnp.int32

D_MODEL = 2048
N_MOD = 6
EPS = 1e-6
CONV_W = 5
CONV_HALO = 8

RET_HEADS = 8
RET_DV = D_MODEL // RET_HEADS
RET_DK = RET_DV // 2
RET_DECAY_FWD = 5.0
RET_DECAY_BWD = 5.5
ROPE_BASE = 10000.0
RET_CHUNK = 256
MLSTM_HEADS = 4
MLSTM_DV = D_MODEL // MLSTM_HEADS
MLSTM_DK = MLSTM_DV // 2
MLSTM_CHUNK = 128
GATE_ROWS = 8
SSD_INNER = 2 * D_MODEL
SSD_HEADDIM = 64
SSD_HEADS = SSD_INNER // SSD_HEADDIM
SSD_GROUPS = 8
SSD_HPG = SSD_HEADS // SSD_GROUPS
SSD_STATE = 128
SSD_CHUNK = 128
MOE_GROUPS = 4
MOE_EPG = 8
MOE_EXPERTS = MOE_GROUPS * MOE_EPG
EXPERT_FF = D_MODEL // 4
MOE_BLOCK = 256

RET_QK = RET_HEADS * RET_DK
RET_V = RET_HEADS * RET_DV
MLSTM_QK = MLSTM_HEADS * MLSTM_DK
MLSTM_V = MLSTM_HEADS * MLSTM_DV
MLSTM_NGATE = 4 * MLSTM_HEADS
EVEN_MAIN = 2 * RET_QK + 2 * RET_V + 2 * MLSTM_QK + 2 * MLSTM_V
EVEN_MIX = RET_V + MLSTM_V
SSD_BC = SSD_GROUPS * SSD_STATE
SSD_CONV_CH = SSD_INNER + 2 * SSD_BC
ODD_MAIN = SSD_INNER + SSD_CONV_CH

PROLOGUE_ROWS = 256
CAST_ROWS = 256
GATHER_UNROLL = 32
SCAN_UNROLL = 2
LANES = 128
VMEM_LIMIT = 56 * 1024 * 1024

NEG_INF = float("-inf")


def _cparams(sem, vmem=VMEM_LIMIT):
    return pltpu.CompilerParams(dimension_semantics=sem, vmem_limit_bytes=vmem)


def _split3(x):
    hi = x.astype(BF16)
    r = x - hi.astype(F32)
    mid = r.astype(BF16)
    lo = (r - mid.astype(F32)).astype(BF16)
    return hi, mid, lo


def _dot(a, b):
    return jnp.dot(a, b, preferred_element_type=F32)


def _dot_nt(a, b):
    return lax.dot_general(a, b, (((1,), (1,)), ((), ())), preferred_element_type=F32)


def _dot_tn(a, b):
    return lax.dot_general(a, b, (((0,), (0,)), ((), ())), preferred_element_type=F32)


def _dot01_left(m01, x):
    hi, mid, lo = _split3(x)
    return _dot(m01, hi) + _dot(m01, mid) + _dot(m01, lo)


def _dot01_right(x, m01):
    hi, mid, lo = _split3(x)
    return _dot(hi, m01) + _dot(mid, m01) + _dot(lo, m01)


def _tri(n, kind):
    r = lax.broadcasted_iota(I32, (n, n), 0)
    c = lax.broadcasted_iota(I32, (n, n), 1)
    m = {"le": r <= c, "ge": r >= c, "gt": r > c}[kind]
    return jnp.where(m, 1.0, 0.0).astype(BF16)


def _shr(x, pow2):
    return lax.shift_right_arithmetic(x, jnp.int32(int(math.log2(pow2))))


def _sigmoid(x):
    return 1.0 / (1.0 + jnp.exp(-x))


def _silu(x):
    return x * _sigmoid(x)


def _softplus(x):
    return jnp.maximum(x, 0.0) + jnp.log1p(jnp.exp(-jnp.abs(x)))


def _log_sigmoid(x):
    return jnp.minimum(x, 0.0) - jnp.log1p(jnp.exp(-jnp.abs(x)))


def _rms(x, g):
    ms = jnp.mean(x * x, axis=-1, keepdims=True)
    return x * lax.rsqrt(ms + EPS) * g


def _head_norm(y, g):
    mu = jnp.mean(y, axis=-1, keepdims=True)
    yc = y - mu
    var = jnp.mean(yc * yc, axis=-1, keepdims=True)
    return yc * lax.rsqrt(var + EPS) * g


def _mod_body(c_ref, w_ref, b_ref, o_ref):
    c = c_ref[...]
    o_ref[0] = _dot(_silu(c).astype(BF16), w_ref[0].astype(BF16)) + b_ref[0]


def _modulation(c_pad, ada_w, ada_b):
    depth, d, n = ada_w.shape
    m = c_pad.shape[0]
    tn = 1024
    return pl.pallas_call(
        _mod_body,
        out_shape=jax.ShapeDtypeStruct((depth, m, n), F32),
        grid=(depth, n // tn),
        in_specs=[
            pl.BlockSpec((m, d), lambda l, j: (0, 0)),
            pl.BlockSpec((1, d, tn), lambda l, j: (l, 0, j)),
            pl.BlockSpec((1, 1, tn), lambda l, j: (l, 0, j)),
        ],
        out_specs=pl.BlockSpec((1, m, tn), lambda l, j: (l, 0, j)),
        compiler_params=_cparams(("parallel", "parallel")),
        name="modulation",
    )(c_pad, ada_w, ada_b.reshape(depth, 1, n))


def _mm_body(*refs, prologue, epilogue, side, two_lhs, n_head, split, res_split):
    it = iter(refs)
    x_ref = next(it)
    xt_ref = next(it) if split else None
    x2_ref = next(it) if two_lhs else None
    g_ref = next(it) if prologue != "none" else None
    sc_ref = next(it) if prologue == "normmod" else None
    sh_ref = next(it) if prologue == "normmod" else None
    w_ref = next(it)
    ws_ref = next(it) if side else None
    res_ref = next(it) if epilogue == "residual" else None
    rest_ref = next(it) if res_split else None
    gate_ref = next(it) if epilogue == "residual" else None
    o_ref = next(it)
    os_ref = next(it) if side else None
    h_scr = next(it) if prologue != "none" else None

    def in_head():
        return pl.program_id(0) < n_head

    if prologue != "none":
        def run_prologue(src_ref):
            rows_per = PROLOGUE_ROWS

            def chunk(i, carry):
                rows = pl.ds(pl.multiple_of(i * rows_per, rows_per), rows_per)
                y = _rms(src_ref[rows, :].astype(F32), g_ref[...])
                if prologue == "normmod":
                    y = y * (1.0 + sc_ref[0]) + sh_ref[0]
                hb = y.astype(BF16)
                h_scr[rows, :] = hb
                if side:
                    h_lo = (y - hb.astype(F32)).astype(BF16)
                    ws = ws_ref[...]
                    w_hi = ws.astype(BF16)
                    w_lo = (ws - w_hi.astype(F32)).astype(BF16)
                    os_ref[0, :, rows] = (_dot(hb, w_hi) + _dot(h_lo, w_hi) + _dot(hb, w_lo)).T
                return carry

            lax.fori_loop(0, src_ref.shape[0] // rows_per, chunk, 0)

        first_col = pl.program_id(1) == 0
        if split:
            pl.when(first_col & in_head())(lambda: run_prologue(x_ref))
            pl.when(first_col & jnp.logical_not(in_head()))(lambda: run_prologue(xt_ref))
        else:
            pl.when(first_col)(lambda: run_prologue(x_ref))
        lhs = h_scr[...]
    else:
        assert not split
        lhs = x_ref[...]
    if two_lhs:
        k1 = x_ref.shape[1]
        acc = _dot(lhs, w_ref[:k1, :]) + _dot(x2_ref[...], w_ref[k1:, :])
    else:
        acc = _dot(lhs, w_ref[...])
    if epilogue == "residual" and rest_ref is not None:
        @pl.when(in_head())
        def _():
            o_ref[...] = res_ref[...] + gate_ref[0] * acc

        @pl.when(jnp.logical_not(in_head()))
        def _():
            o_ref[...] = rest_ref[...] + gate_ref[0] * acc
    elif epilogue == "residual":
        o_ref[...] = res_ref[...] + gate_ref[0] * acc
    else:
        o_ref[...] = acc.astype(o_ref.dtype)


def _fused_matmul(x, w, *, seq, x_tail=None, x2=None, prologue="none", g=None, sc=None, sh=None,
                  w_side=None, res=None, res_tail=None, gate=None, out_dtype=BF16, tm=1024, tn=1024,
                  name="proj"):
    t, k = x.shape
    n = w.shape[1]
    tm = min(tm, seq)
    tn = min(tn, n)
    split = x_tail is not None
    res_split = res_tail is not None
    n_head = (x.shape[0] if split else res.shape[0] if res_split else t) // tm
    if split:
        t = t + x_tail.shape[0]
    assert t % tm == 0 and seq % tm == 0 and n % tn == 0
    tps = seq // tm
    epilogue = "residual" if res is not None else "plain"
    side = w_side is not None
    two_lhs = x2 is not None
    assert not (two_lhs and prologue != "none")

    def head_rows(i):
        return jnp.minimum(i, n_head - 1)

    def tail_rows(i):
        return jnp.maximum(i - n_head, 0)

    if split:
        in_specs = [pl.BlockSpec((tm, k), lambda i, j: (head_rows(i), 0)),
                    pl.BlockSpec((tm, k), lambda i, j: (tail_rows(i), 0))]
        args = [x, x_tail]
    else:
        in_specs = [pl.BlockSpec((tm, k), lambda i, j: (i, 0))]
        args = [x]
    if two_lhs:
        in_specs.append(pl.BlockSpec((tm, x2.shape[1]), lambda i, j: (i, 0)))
        args.append(x2)
        k = k + x2.shape[1]
    if prologue != "none":
        in_specs.append(pl.BlockSpec((1, k), lambda i, j: (0, 0)))
        args.append(g.reshape(1, k))
    if prologue == "normmod":
        in_specs += [pl.BlockSpec((1, 1, k), lambda i, j: (i // tps, 0, 0))] * 2
        args += [sc, sh]
    in_specs.append(pl.BlockSpec((k, tn), lambda i, j: (0, j)))
    args.append(w)
    if side:
        in_specs.append(pl.BlockSpec((k, LANES), lambda i, j: (0, 0)))
        args.append(w_side)
    if epilogue == "residual":
        if res_split:
            in_specs += [pl.BlockSpec((tm, tn), lambda i, j: (head_rows(i), jnp.where(i < n_head, j, 0))),
                         pl.BlockSpec((tm, tn), lambda i, j: (tail_rows(i), jnp.where(i < n_head, 0, j)))]
            args += [res, res_tail]
        else:
            in_specs.append(pl.BlockSpec((tm, tn), lambda i, j: (i, j)))
            args.append(res)
        in_specs.append(pl.BlockSpec((1, 1, tn), lambda i, j: (i // tps, 0, j)))
        args.append(gate)
        out_dtype = F32
    out_shape = [jax.ShapeDtypeStruct((t, n), out_dtype)]
    out_specs = [pl.BlockSpec((tm, tn), lambda i, j: (i, j))]
    if side:
        out_shape.append(jax.ShapeDtypeStruct((t // seq, LANES, seq), F32))
        out_specs.append(pl.BlockSpec((1, LANES, tm), lambda i, j: (i // tps, 0, i % tps)))
    scratch = [pltpu.VMEM((tm, k), BF16)] if prologue != "none" else []
    outs = pl.pallas_call(
        functools.partial(_mm_body, prologue=prologue, epilogue=epilogue, side=side, two_lhs=two_lhs,
                          n_head=n_head, split=split, res_split=res_split),
        out_shape=out_shape,
        grid=(t // tm, n // tn),
        in_specs=in_specs,
        out_specs=out_specs,
        scratch_shapes=scratch,
        compiler_params=_cparams(("parallel", "arbitrary")),
        name=name,
    )(*args)
    return outs if side else outs[0]


CONV_ROWS = 128


def _conv_fill(src_ref, pad_scr, seq):
    ch = pad_scr.shape[1]
    halo = CONV_HALO
    rows = CONV_ROWS
    zeros = jnp.zeros((halo, ch), F32)
    pad_scr[pl.ds(0, halo), :] = zeros
    pad_scr[pl.ds(seq + halo, halo), :] = zeros

    def fill(i, carry):
        r0 = pl.multiple_of(i * rows, rows)
        pad_scr[pl.ds(pl.multiple_of(r0 + halo, halo), rows), :] = src_ref[0, pl.ds(r0, rows), :].astype(F32)
        return carry

    lax.fori_loop(0, seq // rows, fill, 0)


def _conv_silu_rows(pad_scr, w_ref, b_ref, r0, emit):
    ch = pad_scr.shape[1]
    halo = CONV_HALO
    rows = CONV_ROWS
    win = rows + 2 * halo
    half = (CONV_W - 1) // 2
    for lane0 in range(0, ch, LANES):
        cols = slice(lane0, lane0 + LANES)
        window = pad_scr[pl.ds(r0, win), cols]
        acc = jnp.zeros((rows, LANES), F32) + b_ref[:, cols]
        for j in range(CONV_W):
            d = j - half
            shifted = window if d == 0 else pltpu.roll(window, (-d) % win, axis=0)
            acc = acc + w_ref[j:j + 1, cols] * shifted[halo:halo + rows, :]
        emit(lane0, _silu(acc))


def _two_ended_scan(n_chunks, prepare, step, finish):
    assert n_chunks % 2 == 0
    half = n_chunks // 2

    def first(i, carry):
        prepare(i)
        prepare(n_chunks - 1 - i)
        step(i)
        return carry

    def second(i, carry):
        step(i)
        finish(i)
        finish(n_chunks - 1 - i)
        return carry

    unroll = SCAN_UNROLL if half % SCAN_UNROLL == 0 else 1
    lax.fori_loop(0, half, first, 0, unroll=unroll)
    lax.fori_loop(half, n_chunks, second, 0, unroll=unroll)


def _ret_body(lg_ref, q_ref, k_ref, v_ref, g_ref, cos_ref, sin_ref, gn_ref, o_ref,
              qs_scr, ks_scr, acc_scr, st_scr, *, seq):
    c_len = RET_CHUNK
    n_chunks = seq // c_len
    h = pl.program_id(1)
    lgf = lg_ref[0, h]
    lgb = lg_ref[1, h]
    ri = lax.broadcasted_iota(I32, (c_len, c_len), 0)
    ci = lax.broadcasted_iota(I32, (c_len, c_len), 1)
    diff = (ri - ci).astype(F32)
    dmat = jnp.exp(jnp.where(diff >= 0, lgf * diff, -lgb * diff))
    pos = lax.broadcasted_iota(I32, (c_len, 1), 0).astype(F32)
    qdec_f = jnp.exp(lgf * (pos + 1.0))
    kdec_f = jnp.exp(lgf * (c_len - 1.0 - pos))
    cdec_f = jnp.exp(jnp.full((1, 1), c_len, F32) * lgf)
    qdec_b = jnp.exp(lgb * (c_len - pos))
    kdec_b = jnp.exp(lgb * pos)
    cdec_b = jnp.exp(jnp.full((1, 1), c_len, F32) * lgb)
    half = RET_DK // 2

    def rope(x, rows):
        return x * cos_ref[rows, :] + pltpu.roll(x, half, axis=1) * sin_ref[rows, :]

    def chunk_rows(c):
        return pl.ds(pl.multiple_of(c * c_len, c_len), c_len)

    def prepare(c):
        rows = chunk_rows(c)
        qs_scr[rows, :] = rope(q_ref[0, rows, :].astype(F32), rows).astype(BF16)
        ks_scr[rows, :] = rope(k_ref[0, rows, :].astype(F32), rows) * (RET_DK ** -0.5)

    st_scr[...] = jnp.zeros_like(st_scr)

    def step(i):
        rows = chunk_rows(i)
        qb = qs_scr[rows, :]
        k = ks_scr[rows, :]
        v = v_ref[0, rows, :]
        p = (_dot_nt(qb, k.astype(BF16)) * dmat).astype(BF16)
        acc_scr[0, rows, :] = _dot(p, v) + qdec_f * _dot(qb, st_scr[0].astype(BF16))
        st_scr[0] = cdec_f * st_scr[0] + _dot_tn((k * kdec_f).astype(BF16), v)

        rows = chunk_rows(n_chunks - 1 - i)
        k = ks_scr[rows, :]
        acc_scr[1, rows, :] = qdec_b * _dot(qs_scr[rows, :], st_scr[1].astype(BF16))
        st_scr[1] = cdec_b * st_scr[1] + _dot_tn((k * kdec_b).astype(BF16), v_ref[0, rows, :])

    def finish(c):
        rows = chunk_rows(c)
        o = acc_scr[0, rows, :] + acc_scr[1, rows, :]
        gate = g_ref[0, rows, :].astype(F32)
        o_ref[0, rows, :] = (_head_norm(o, gn_ref[0]) * _silu(gate)).astype(o_ref.dtype)

    _two_ended_scan(n_chunks, prepare, step, finish)


def _retention(proj, lg, cos_t, sin_t, ret_gn, nb, seq):
    kq, kv = RET_DK, RET_DV
    grid_spec = dict(
        grid=(nb, RET_HEADS),
        in_specs=[
            pl.BlockSpec(memory_space=pltpu.SMEM),
            pl.BlockSpec((1, seq, kq), lambda b, h: (b, 0, h)),
            pl.BlockSpec((1, seq, kq), lambda b, h: (b, 0, RET_QK // kq + h)),
            pl.BlockSpec((1, seq, kv), lambda b, h: (b, 0, 2 * RET_QK // kv + h)),
            pl.BlockSpec((1, seq, kv), lambda b, h: (b, 0, (2 * RET_QK + RET_V) // kv + h)),
            pl.BlockSpec((seq, kq), lambda b, h: (0, 0)),
            pl.BlockSpec((seq, kq), lambda b, h: (0, 0)),
            pl.BlockSpec((1, 1, kv), lambda b, h: (h, 0, 0)),
        ],
        out_specs=pl.BlockSpec((1, seq, kv), lambda b, h: (b, 0, h)),
        scratch_shapes=[
            pltpu.VMEM((seq, kq), BF16),
            pltpu.VMEM((seq, kq), F32),
            pltpu.VMEM((2, seq, kv), F32),
            pltpu.VMEM((2, kq, kv), F32),
        ],
    )
    return pl.pallas_call(
        functools.partial(_ret_body, seq=seq),
        out_shape=jax.ShapeDtypeStruct((nb, seq, RET_V), BF16),
        **grid_spec,
        compiler_params=_cparams(("parallel", "parallel")),
        name="retention",
    )(lg, proj, proj, proj, proj, cos_t, sin_t, ret_gn.reshape(RET_HEADS, 1, kv))


def _col_of(mat, c):
    lane = lax.broadcasted_iota(I32, mat.shape, 1)
    return jnp.sum(jnp.where(lane == c, mat, 0.0), axis=1, keepdims=True)


def _mlstm_body(gb_ref, q_ref, k_ref, v_ref, o_gate_ref, wq_ref, wk_ref, bq_ref, bk_ref,
                gr_ref, gn_ref, o_ref,
                padq_scr, padk_scr, qs_scr, ks_scr, acc_scr, c_scr, n_scr, m_scr, row_scr, col_scr,
                *, seq):
    ln = MLSTM_CHUNK
    assert ln == CONV_ROWS
    n_chunks = seq // ln
    h = pl.program_id(1)

    _conv_fill(q_ref, padq_scr, seq)
    _conv_fill(k_ref, padk_scr, seq)

    def prepare(c):
        r0 = pl.multiple_of(c * ln, ln)
        rows = pl.ds(r0, ln)

        def emit_q(lane0, y):
            qs_scr[rows, lane0:lane0 + LANES] = (y * (MLSTM_DK ** -0.5)).astype(BF16)

        def emit_k(lane0, y):
            ks_scr[rows, lane0:lane0 + LANES] = y

        _conv_silu_rows(padq_scr, wq_ref, bq_ref, r0, emit_q)
        _conv_silu_rows(padk_scr, wk_ref, bk_ref, r0, emit_k)

    row_scr[...] = jnp.zeros_like(row_scr)
    for t in range(4):
        for c in range(n_chunks):
            row_scr[t, c:c + 1, :] = gr_ref[0, t:t + 1, c * ln:(c + 1) * ln]
    tri_le = _tri(ln, "le")
    tri_ge = _tri(ln, "ge")
    for d in range(2):
        ig = row_scr[2 * d] + gb_ref[(2 * d) * MLSTM_HEADS + h]
        lf = _log_sigmoid(row_scr[2 * d + 1] + gb_ref[(2 * d + 1) * MLSTM_HEADS + h])
        bc = _dot01_right(lf, tri_le if d == 0 else tri_ge)
        row_scr[2 * d] = ig
        row_scr[2 * d + 1] = bc
        col_scr[2 * d] = ig.T
        col_scr[2 * d + 1] = bc.T

    ri = lax.broadcasted_iota(I32, (ln, ln), 0)
    ci = lax.broadcasted_iota(I32, (ln, ln), 1)

    def chunk_step(c, d):
        r0 = pl.multiple_of(c * ln, ln)
        rows = pl.ds(r0, ln)
        qb = qs_scr[rows, :]
        kf = ks_scr[rows, :]
        v = v_ref[0, rows, :]
        i_row = row_scr[2 * d, pl.ds(c, 1), :]
        b_row = row_scr[2 * d + 1, pl.ds(c, 1), :]
        i_col = _col_of(col_scr[2 * d], c)
        b_col = _col_of(col_scr[2 * d + 1], c)
        m_st = m_scr[d]
        mask = (ri >= ci) if d == 0 else (ri <= ci)
        logd = jnp.where(mask, b_col - b_row + i_row, NEG_INF)
        m_inter = b_col + m_st
        m_row = jnp.maximum(m_inter, jnp.max(logd, axis=1, keepdims=True))
        sc = _dot_nt(qb, kf.astype(BF16)) * jnp.exp(logd - m_row)
        inter = jnp.exp(m_inter - m_row)
        num = _dot(sc.astype(BF16), v) + inter * _dot(qb, c_scr[d].astype(BF16))
        den = jnp.sum(sc, axis=1, keepdims=True) + inter * jnp.sum(
            qb.astype(F32) * n_scr[d], axis=1, keepdims=True)
        hh = num / jnp.maximum(jnp.abs(den), jnp.exp(-m_row))
        b_end = b_row[:, ln - 1:ln] if d == 0 else b_row[:, 0:1]
        logw = b_end - b_col + i_col
        m_new = jnp.maximum(b_end + m_st, jnp.max(logw, axis=0, keepdims=True))
        kw = kf * jnp.exp(logw - m_new)
        dec = jnp.exp(b_end + m_st - m_new)
        c_scr[d] = dec * c_scr[d] + _dot_tn(kw.astype(BF16), v)
        n_scr[d] = dec * n_scr[d] + jnp.sum(kw, axis=0, keepdims=True)
        m_scr[d] = m_new
        acc_scr[d, rows, :] = hh

    c_scr[...] = jnp.zeros_like(c_scr)
    n_scr[...] = jnp.zeros_like(n_scr)
    m_scr[...] = jnp.zeros_like(m_scr)

    def step(i):
        chunk_step(i, 0)
        chunk_step(n_chunks - 1 - i, 1)

    def finish(c):
        rows = pl.ds(pl.multiple_of(c * ln, ln), ln)
        y = _head_norm(acc_scr[0, rows, :] + acc_scr[1, rows, :], gn_ref[0])
        o_ref[0, rows, :] = (y * _sigmoid(o_gate_ref[0, rows, :].astype(F32))).astype(o_ref.dtype)

    _two_ended_scan(n_chunks, prepare, step, finish)


def _mlstm(proj, gate_rows, gate_b, conv_w, conv_b, mlstm_gn, nb, seq):
    kq, kv = MLSTM_DK, MLSTM_DV
    q0 = 2 * RET_QK + 2 * RET_V
    k0 = q0 + MLSTM_QK
    v0 = k0 + MLSTM_QK
    o0 = v0 + MLSTM_V
    grid_spec = dict(
        grid=(nb, MLSTM_HEADS),
        in_specs=[
            pl.BlockSpec(memory_space=pltpu.SMEM),
            pl.BlockSpec((1, seq, kq), lambda b, h: (b, 0, q0 // kq + h)),
            pl.BlockSpec((1, seq, kq), lambda b, h: (b, 0, k0 // kq + h)),
            pl.BlockSpec((1, seq, kv), lambda b, h: (b, 0, v0 // kv + h)),
            pl.BlockSpec((1, seq, kv), lambda b, h: (b, 0, o0 // kv + h)),
            pl.BlockSpec((CONV_W, kq), lambda b, h: (0, h)),
            pl.BlockSpec((CONV_W, kq), lambda b, h: (0, MLSTM_QK // kq + h)),
            pl.BlockSpec((1, kq), lambda b, h: (0, h)),
            pl.BlockSpec((1, kq), lambda b, h: (0, MLSTM_QK // kq + h)),
            pl.BlockSpec((1, GATE_ROWS, seq), lambda b, h: (b, h, 0)),
            pl.BlockSpec((1, 1, kv), lambda b, h: (h, 0, 0)),
        ],
        out_specs=pl.BlockSpec((1, seq, kv), lambda b, h: (b, 0, h)),
        scratch_shapes=[
            pltpu.VMEM((seq + 2 * CONV_HALO, kq), F32),
            pltpu.VMEM((seq + 2 * CONV_HALO, kq), F32),
            pltpu.VMEM((seq, kq), BF16),
            pltpu.VMEM((seq, kq), F32),
            pltpu.VMEM((2, seq, kv), F32),
            pltpu.VMEM((2, kq, kv), F32),
            pltpu.VMEM((2, 1, kq), F32),
            pltpu.VMEM((2, 1, 1), F32),
            pltpu.VMEM((4, LANES, MLSTM_CHUNK), F32),
            pltpu.VMEM((4, MLSTM_CHUNK, LANES), F32),
        ],
    )
    return pl.pallas_call(
        functools.partial(_mlstm_body, seq=seq),
        out_shape=jax.ShapeDtypeStruct((nb, seq, MLSTM_V), BF16),
        **grid_spec,
        compiler_params=_cparams(("parallel", "parallel")),
        name="mlstm",
    )(gate_b, proj, proj, proj, proj, conv_w, conv_w, conv_b.reshape(1, -1), conv_b.reshape(1, -1),
      gate_rows, mlstm_gn.reshape(MLSTM_HEADS, 1, kv))


def _ssd_body(z_ref, x_ref, b_ref, c_ref, wx_ref, wb_ref, wc_ref, bx_ref, bb_ref, bc_ref,
              dtf_ref, dtb_ref, bias_ref, alog_ref, dskip_ref, o_ref,
              padx_scr, padb_scr, padc_scr, xs_scr, bs_scr, bst_scr, cs_scr, y_scr, st_scr,
              acr_scr, dtr_scr, er_scr, ur_scr, act_scr, *, seq):
    ln = SSD_CHUNK
    assert ln == CONV_ROWS
    n_chunks = seq // ln
    hp = SSD_HEADDIM
    n_pairs = SSD_HPG // 2

    _conv_fill(x_ref, padx_scr, seq)
    _conv_fill(b_ref, padb_scr, seq)
    _conv_fill(c_ref, padc_scr, seq)

    def prepare(c):
        r0 = pl.multiple_of(c * ln, ln)
        rows = pl.ds(r0, ln)

        def emit_x(lane0, y):
            xs_scr[rows, lane0:lane0 + LANES] = y

        def emit_b(lane0, y):
            bs_scr[rows, :] = y.astype(BF16)
            bst_scr[:, rows] = y.T.astype(BF16)

        def emit_c(lane0, y):
            cs_scr[rows, :] = y.astype(BF16)

        _conv_silu_rows(padx_scr, wx_ref, bx_ref, r0, emit_x)
        _conv_silu_rows(padb_scr, wb_ref, bb_ref, r0, emit_b)
        _conv_silu_rows(padc_scr, wc_ref, bc_ref, r0, emit_c)

    for d, dt_ref in enumerate((dtf_ref, dtb_ref)):
        acr_scr[d] = jnp.zeros((LANES, ln), F32)
        for c in range(n_chunks):
            acr_scr[d, c * SSD_HPG:(c + 1) * SSD_HPG, :] = dt_ref[0, :, c * ln:(c + 1) * ln]
        dt = _softplus(acr_scr[d] + bias_ref[0, d])
        adt = dt * (-jnp.exp(alog_ref[0, d]))
        acum = _dot01_right(adt, _tri(ln, "le" if d == 0 else "ge"))
        a_end = acum[:, ln - 1:ln] if d == 0 else acum[:, 0:1]
        acr_scr[d] = acum
        dtr_scr[d] = dt
        er_scr[d] = jnp.exp(acum)
        ur_scr[d] = dt * jnp.exp(a_end - acum)
        act_scr[d] = acum.T

    ri = lax.broadcasted_iota(I32, (ln, ln), 0)
    ci = lax.broadcasted_iota(I32, (ln, ln), 1)
    in_first = lax.broadcasted_iota(I32, (1, LANES), 1) < hp
    on_diag = ri == ci

    def chunk_rows(c):
        return pl.ds(pl.multiple_of(c * ln, ln), ln)

    def dir_step(d, c):
        rows = chunk_rows(c)
        xb = xs_scr[rows, :].astype(BF16)
        bcm = bs_scr[rows, :]
        bct = bst_scr[:, rows].astype(F32)
        ccm = cs_scr[rows, :]
        cb = _dot_nt(ccm, bcm)
        carried = _dot(ccm, st_scr[d].astype(BF16)).astype(BF16)
        head_rows = pl.ds(pl.multiple_of(c * SSD_HPG, SSD_HPG), SSD_HPG)
        arow = acr_scr[d, head_rows, :]
        dtrow = dtr_scr[d, head_rows, :]
        erow = er_scr[d, head_rows, :]
        urow = ur_scr[d, head_rows, :]
        acols = pltpu.roll(act_scr[d], (LANES - c * SSD_HPG) & (LANES - 1), axis=1)
        mask = (ri >= ci) if d == 0 else (ri <= ci)
        end = ln - 1 if d == 0 else 0
        pieces = []
        for pair in range(n_pairs):
            lanes = slice(pair * LANES, (pair + 1) * LANES)
            lhs, rhs, lhs_state, keep = [], [], [], []
            for sub in range(2):
                k = 2 * pair + sub
                sel = in_first if sub == 0 else ~in_first
                dec = jnp.exp(jnp.where(mask, acols[:, k:k + 1] - arow[k:k + 1, :], NEG_INF))
                lhs.append((cb * dec * dtrow[k:k + 1, :]).astype(BF16))
                rhs.append(jnp.where(sel, xb[:, lanes], jnp.zeros((ln, LANES), BF16)))
                lhs_state.append((bct * urow[k:k + 1, :]).astype(BF16))
                keep.append(jnp.exp(arow[k:k + 1, end:end + 1]))
            for sub in range(2):
                k = 2 * pair + sub
                sel = in_first if sub == 0 else ~in_first
                lhs.append(jnp.where(on_diag, erow[k:k + 1, :], 0.0).astype(BF16))
                rhs.append(jnp.where(sel, carried[:, lanes], jnp.zeros((ln, LANES), BF16)))
            pieces.append(_dot(jnp.concatenate(lhs, axis=1), jnp.concatenate(rhs, axis=0)))
            st_scr[d, :, lanes] = (jnp.where(in_first, keep[0], keep[1]) * st_scr[d, :, lanes]
                                   + _dot(jnp.concatenate(lhs_state, axis=1),
                                          jnp.concatenate(rhs[:2], axis=0)))
        y_scr[d, rows, :] = jnp.concatenate(pieces, axis=1)

    st_scr[...] = jnp.zeros_like(st_scr)

    def step(i):
        dir_step(0, i)
        dir_step(1, n_chunks - 1 - i)

    def finish(c):
        rows = chunk_rows(c)
        y = y_scr[0, rows, :] + y_scr[1, rows, :] + dskip_ref[0] * xs_scr[rows, :]
        o_ref[0, rows, :] = (y * _silu(z_ref[0, rows, :].astype(F32))).astype(o_ref.dtype)

    _two_ended_scan(n_chunks, prepare, step, finish)


def _ssd(proj, dt_t, bias_col, alog_col, dskip_x, conv_w, conv_b, nb, seq):
    width = SSD_HPG * SSD_HEADDIM
    ns = SSD_STATE
    x0 = SSD_INNER
    b0 = 2 * SSD_INNER
    c0 = b0 + SSD_BC
    cb = conv_b.reshape(1, -1)
    return pl.pallas_call(
        functools.partial(_ssd_body, seq=seq),
        out_shape=jax.ShapeDtypeStruct((nb, seq, SSD_INNER), BF16),
        grid=(nb, SSD_GROUPS),
        in_specs=[
            pl.BlockSpec((1, seq, width), lambda b, g: (b, 0, g)),
            pl.BlockSpec((1, seq, width), lambda b, g: (b, 0, x0 // width + g)),
            pl.BlockSpec((1, seq, ns), lambda b, g: (b, 0, b0 // ns + g)),
            pl.BlockSpec((1, seq, ns), lambda b, g: (b, 0, c0 // ns + g)),
            pl.BlockSpec((CONV_W, width), lambda b, g: (0, g)),
            pl.BlockSpec((CONV_W, ns), lambda b, g: (0, SSD_INNER // ns + g)),
            pl.BlockSpec((CONV_W, ns), lambda b, g: (0, (SSD_INNER + SSD_BC) // ns + g)),
            pl.BlockSpec((1, width), lambda b, g: (0, g)),
            pl.BlockSpec((1, ns), lambda b, g: (0, SSD_INNER // ns + g)),
            pl.BlockSpec((1, ns), lambda b, g: (0, (SSD_INNER + SSD_BC) // ns + g)),
            pl.BlockSpec((1, SSD_HPG, seq), lambda b, g: (b, g, 0)),
            pl.BlockSpec((1, SSD_HPG, seq), lambda b, g: (b, SSD_GROUPS + g, 0)),
            pl.BlockSpec((1, 2, LANES, 1), lambda b, g: (g, 0, 0, 0)),
            pl.BlockSpec((1, 2, LANES, 1), lambda b, g: (g, 0, 0, 0)),
            pl.BlockSpec((1, 1, width), lambda b, g: (g, 0, 0)),
        ],
        out_specs=pl.BlockSpec((1, seq, width), lambda b, g: (b, 0, g)),
        scratch_shapes=[
            pltpu.VMEM((seq + 2 * CONV_HALO, width), F32),
            pltpu.VMEM((seq + 2 * CONV_HALO, ns), F32),
            pltpu.VMEM((seq + 2 * CONV_HALO, ns), F32),
            pltpu.VMEM((seq, width), F32),
            pltpu.VMEM((seq, ns), BF16),
            pltpu.VMEM((ns, seq), BF16),
            pltpu.VMEM((seq, ns), BF16),
            pltpu.VMEM((2, seq, width), F32),
            pltpu.VMEM((2, ns, width), F32),
            pltpu.VMEM((2, LANES, SSD_CHUNK), F32),
            pltpu.VMEM((2, LANES, SSD_CHUNK), F32),
            pltpu.VMEM((2, LANES, SSD_CHUNK), F32),
            pltpu.VMEM((2, LANES, SSD_CHUNK), F32),
            pltpu.VMEM((2, SSD_CHUNK, LANES), F32),
        ],
        compiler_params=_cparams(("parallel", "parallel")),
        name="ssd",
    )(proj, proj, proj, proj, conv_w, conv_w, conv_w, cb, cb, cb,
      dt_t, dt_t, bias_col, alog_col, dskip_x)


META_E = 0
META_G = 2
META_R = 4
ROUTE_E0 = MOE_GROUPS


def _moe_input(x, g_ref, sc_ref, sh_ref):
    return _rms(x, g_ref[...]) * (1.0 + sc_ref[0]) + sh_ref[0]


def _router_body(x_ref, g_ref, sc_ref, sh_ref, w_ref, b_ref, meta_ref, cnt_ref,
                 carry_scr, whi_scr, wlo_scr):
    @pl.when(pl.program_id(0) == 0)
    def _():
        carry_scr[...] = jnp.zeros_like(carry_scr)
        w = w_ref[...]
        w_hi = w.astype(BF16)
        whi_scr[...] = w_hi
        wlo_scr[...] = (w - w_hi.astype(F32)).astype(BF16)

    meta_ref[...] = _route_rows(_moe_input(x_ref[...], g_ref, sc_ref, sh_ref),
                                whi_scr[...], wlo_scr[...], b_ref[...], carry_scr)
    cnt_ref[...] = jnp.broadcast_to(carry_scr[...], cnt_ref.shape)


def _route_rows(y, w_hi, w_lo, bias, carry_scr):
    tm = y.shape[0]
    h_hi = y.astype(BF16)
    h_lo = (y - h_hi.astype(F32)).astype(BF16)
    logits = _dot(h_hi, w_hi) + _dot(h_lo, w_hi) + _dot(h_hi, w_lo) + bias

    lane = lax.broadcasted_iota(I32, (tm, LANES), 1)
    lane_f = lane.astype(F32)
    big = float(LANES)
    is_grp = lane < MOE_GROUPS
    gl = jnp.where(is_grp, logits, NEG_INF)
    gmax = jnp.max(gl, axis=1, keepdims=True)
    gidx = jnp.min(jnp.where(gl == gmax, lane_f, big), axis=1, keepdims=True)
    gprob = 1.0 / jnp.sum(jnp.where(is_grp, jnp.exp(gl - gmax), 0.0), axis=1, keepdims=True)

    el = lane - ROUTE_E0
    el_f = el.astype(F32)
    valid = (el >= 0) & (el < MOE_EXPERTS)
    in_grp = valid & (_shr(el, MOE_EPG).astype(F32) == gidx)
    ev = jnp.where(in_grp, logits, NEG_INF)
    v1 = jnp.max(ev, axis=1, keepdims=True)
    i1 = jnp.min(jnp.where(ev == v1, el_f, big), axis=1, keepdims=True)
    ev2 = jnp.where(el_f == i1, NEG_INF, ev)
    v2 = jnp.max(ev2, axis=1, keepdims=True)
    i2 = jnp.min(jnp.where(ev2 == v2, el_f, big), axis=1, keepdims=True)
    p2 = jnp.exp(v2 - v1)
    s1 = 1.0 / (1.0 + p2)
    gate1 = s1 * gprob
    gate2 = p2 * s1 * gprob

    oh1 = jnp.where(el_f == i1, 1.0, 0.0)
    oh2 = jnp.where(el_f == i2, 1.0, 0.0)
    oh = oh1 + oh2
    before = _dot(_tri(tm, "gt"), oh.astype(BF16)) + carry_scr[...]
    rank1 = jnp.sum(oh1 * before, axis=1, keepdims=True)
    rank2 = jnp.sum(oh2 * before, axis=1, keepdims=True)
    carry_scr[...] = carry_scr[...] + jnp.sum(oh, axis=0, keepdims=True)

    meta = jnp.zeros((tm, LANES), F32)
    for col, val in ((META_E, i1), (META_E + 1, i2), (META_G, gate1), (META_G + 1, gate2),
                     (META_R, rank1), (META_R + 1, rank2)):
        meta = jnp.where(lane == col, val, meta)
    return meta


def _router(x, g, sc, sh, w_route, b_route, seq, tm=512):
    t, k = x.shape
    tm = min(tm, seq)
    tps = seq // tm
    return pl.pallas_call(
        _router_body,
        out_shape=[jax.ShapeDtypeStruct((t, LANES), F32),
                   jax.ShapeDtypeStruct((8, LANES), F32)],
        grid=(t // tm,),
        in_specs=[
            pl.BlockSpec((tm, k), lambda i: (i, 0)),
            pl.BlockSpec((1, k), lambda i: (0, 0)),
            pl.BlockSpec((1, 1, k), lambda i: (i // tps, 0, 0)),
            pl.BlockSpec((1, 1, k), lambda i: (i // tps, 0, 0)),
            pl.BlockSpec((k, LANES), lambda i: (0, 0)),
            pl.BlockSpec((1, LANES), lambda i: (0, 0)),
        ],
        out_specs=[pl.BlockSpec((tm, LANES), lambda i: (i, 0)),
                   pl.BlockSpec((8, LANES), lambda i: (0, 0))],
        scratch_shapes=[pltpu.VMEM((1, LANES), F32), pltpu.VMEM((k, LANES), BF16),
                        pltpu.VMEM((k, LANES), BF16)],
        compiler_params=_cparams(("arbitrary",)),
        name="router",
    )(x, g.reshape(1, k), sc, sh, w_route, b_route)


ZERO_BLOCKS = 2 * MOE_EXPERTS


def _dispatch_body(dest_ref, zero_ref, x_ref, g_ref, sc_ref, sh_ref, xs_hbm, h_scr, zero_scr, sem, zsem):
    i = pl.program_id(0)
    n_steps = pl.num_programs(0)
    tm = x_ref.shape[0]
    slot = i % 2

    def zero_copy(start):
        return pltpu.make_async_copy(zero_scr, xs_hbm.at[pl.ds(start, MOE_BLOCK)], zsem)

    @pl.when(i == 0)
    def _():
        zero_scr[...] = jnp.zeros_like(zero_scr)

        def issue(k, carry):
            @pl.when(zero_ref[k] >= 0)
            def _():
                zero_copy(pl.multiple_of(zero_ref[k], MOE_BLOCK)).start()
            return carry

        def drain(k, carry):
            @pl.when(zero_ref[k] >= 0)
            def _():
                zero_copy(0).wait()
            return carry

        lax.fori_loop(0, ZERO_BLOCKS, issue, 0)
        lax.fori_loop(0, ZERO_BLOCKS, drain, 0)

    def wait_rows(s):
        for _ in range(2):
            pltpu.make_async_copy(h_scr.at[s], xs_hbm.at[pl.ds(0, tm)], sem.at[s]).wait()

    @pl.when(i >= 2)
    def _():
        wait_rows(slot)

    h_scr[slot] = _moe_input(x_ref[...], g_ref, sc_ref, sh_ref)

    def body(r, carry):
        for j in range(2):
            pltpu.make_async_copy(h_scr.at[slot, pl.ds(r, 1)],
                                  xs_hbm.at[pl.ds(dest_ref[(i * tm + r) * 2 + j], 1)],
                                  sem.at[slot]).start()
        return carry

    lax.fori_loop(0, tm, body, 0, unroll=GATHER_UNROLL // 2)

    @pl.when(i == n_steps - 1)
    def _():
        @pl.when(n_steps >= 2)
        def _():
            wait_rows(1 - slot)
        wait_rows(slot)


def _dispatch(x, g, sc, sh, dest, zero_starts, n_rows, seq, tm=256):
    t, k = x.shape
    tm = min(tm, seq)
    tps = seq // tm
    grid_spec = pltpu.PrefetchScalarGridSpec(
        num_scalar_prefetch=2,
        grid=(t // tm,),
        in_specs=[
            pl.BlockSpec((tm, k), lambda i, ds, zs: (i, 0)),
            pl.BlockSpec((1, k), lambda i, ds, zs: (0, 0)),
            pl.BlockSpec((1, 1, k), lambda i, ds, zs: (i // tps, 0, 0)),
            pl.BlockSpec((1, 1, k), lambda i, ds, zs: (i // tps, 0, 0)),
        ],
        out_specs=pl.BlockSpec(memory_space=pl.ANY),
        scratch_shapes=[pltpu.VMEM((2, tm, k), F32), pltpu.VMEM((MOE_BLOCK, k), F32),
                        pltpu.SemaphoreType.DMA((2,)), pltpu.SemaphoreType.DMA],
    )
    return pl.pallas_call(
        _dispatch_body,
        out_shape=jax.ShapeDtypeStruct((n_rows, k), F32),
        grid_spec=grid_spec,
        compiler_params=_cparams(("arbitrary",)),
        name="moe_dispatch",
    )(dest, zero_starts, x, g.reshape(1, k), sc, sh)


def _row_copy(src_hbm, dst, sem, src_row, dst_row):
    return pltpu.make_async_copy(src_hbm.at[pl.ds(src_row, 1)], dst.at[pl.ds(dst_row, 1)], sem)


def _start_row_gather(idx_ref, base, n_rows, stride, src_hbm, dst, sem):
    def body(r, carry):
        _row_copy(src_hbm, dst, sem, idx_ref[base + r * stride], r).start()
        return carry
    lax.fori_loop(0, n_rows, body, 0, unroll=GATHER_UNROLL)


def _wait_row_gather(src_hbm, dst, sem, n_rows):
    pltpu.make_async_copy(src_hbm.at[pl.ds(0, n_rows)], dst, sem).wait()


def _expert_body(blk_exp_ref, n_used_ref, xs_ref, wg_ref, wu_ref, wd_ref, y_ref, wgu_scr, wdn_scr):
    i = pl.program_id(0)
    d = xs_ref.shape[1]
    used = i < n_used_ref[0]

    @pl.when(used & ((i == 0) | (blk_exp_ref[i] != blk_exp_ref[jnp.maximum(i - 1, 0)])))
    def _():
        def cast_up(c, carry):
            rows = pl.ds(pl.multiple_of(c * CAST_ROWS, CAST_ROWS), CAST_ROWS)
            wgu_scr[rows, :EXPERT_FF] = wg_ref[0, 0, rows, :].astype(BF16)
            wgu_scr[rows, EXPERT_FF:] = wu_ref[0, 0, rows, :].astype(BF16)
            return carry

        def cast_down(c, carry):
            rows = pl.ds(pl.multiple_of(c * CAST_ROWS, CAST_ROWS), CAST_ROWS)
            wdn_scr[rows, :] = wd_ref[0, 0, rows, :].astype(BF16)
            return carry

        lax.fori_loop(0, d // CAST_ROWS, cast_up, 0)
        lax.fori_loop(0, EXPERT_FF // CAST_ROWS, cast_down, 0)

    @pl.when(used)
    def _():
        a = _dot(xs_ref[...].astype(BF16), wgu_scr[...])
        hid = (_silu(a[:, :EXPERT_FF]) * a[:, EXPERT_FF:]).astype(BF16)
        y_ref[...] = _dot(hid, wdn_scr[...])

    @pl.when(jnp.logical_not(used))
    def _():
        y_ref[...] = jnp.zeros_like(y_ref)


def _experts(xs, w_gate, w_up, w_down, layer, blk_exp, n_used):
    n_rows, d = xs.shape
    n_blocks = n_rows // MOE_BLOCK
    grid_spec = pltpu.PrefetchScalarGridSpec(
        num_scalar_prefetch=2,
        grid=(n_blocks,),
        in_specs=[
            pl.BlockSpec((MOE_BLOCK, d), lambda i, be, nu: (i, 0)),
            pl.BlockSpec((1, 1, d, EXPERT_FF), lambda i, be, nu: (layer, be[i], 0, 0)),
            pl.BlockSpec((1, 1, d, EXPERT_FF), lambda i, be, nu: (layer, be[i], 0, 0)),
            pl.BlockSpec((1, 1, EXPERT_FF, d), lambda i, be, nu: (layer, be[i], 0, 0)),
        ],
        out_specs=pl.BlockSpec((MOE_BLOCK, d), lambda i, be, nu: (i, 0)),
        scratch_shapes=[pltpu.VMEM((d, 2 * EXPERT_FF), BF16),
                        pltpu.VMEM((EXPERT_FF, d), BF16)],
    )
    return pl.pallas_call(
        _expert_body,
        out_shape=jax.ShapeDtypeStruct((n_rows, d), F32),
        grid_spec=grid_spec,
        compiler_params=_cparams(("arbitrary",)),
        name="experts",
    )(blk_exp, n_used, xs, w_gate, w_up, w_down)


def _combine_body(dest_ref, y_hbm, x_ref, gate_ref, meta_ref, fn_ref, o_ref, ya_scr, sem,
                  *, final, tile0):
    i = pl.program_id(0)
    n_steps = pl.num_programs(0)
    tm = x_ref.shape[0]
    slot = i % 2

    def start(step, s):
        for j in range(2):
            _start_row_gather(dest_ref, (tile0 + step) * tm * 2 + j, tm, 2, y_hbm,
                              ya_scr.at[s, j], sem.at[s, j])

    @pl.when(i == 0)
    def _():
        start(0, 0)

    for j in range(2):
        _wait_row_gather(y_hbm, ya_scr.at[slot, j], sem.at[slot, j], tm)

    @pl.when(i + 1 < n_steps)
    def _():
        start(i + 1, 1 - slot)

    meta = meta_ref[...]
    moe = (ya_scr[slot, 0] * meta[:, META_G:META_G + 1]
           + ya_scr[slot, 1] * meta[:, META_G + 1:META_G + 2])
    out = x_ref[...] + gate_ref[0] * moe
    if final:
        out = _rms(out, fn_ref[...])
    o_ref[...] = out


def _combine(y, x, gate, meta, dest, final_norm, seq, final, tok0=0, n_tok=None, tm=256):
    t, d = x.shape
    n_tok = t if n_tok is None else n_tok
    tm = min(tm, seq)
    tps = seq // tm
    assert tok0 % seq == 0 and n_tok % seq == 0
    tile0 = tok0 // tm
    grid_spec = pltpu.PrefetchScalarGridSpec(
        num_scalar_prefetch=1,
        grid=(n_tok // tm,),
        in_specs=[
            pl.BlockSpec(memory_space=pl.ANY),
            pl.BlockSpec((tm, d), lambda i, ds: (tile0 + i, 0)),
            pl.BlockSpec((1, 1, d), lambda i, ds: ((tile0 + i) // tps, 0, 0)),
            pl.BlockSpec((tm, LANES), lambda i, ds: (tile0 + i, 0)),
            pl.BlockSpec((1, d), lambda i, ds: (0, 0)),
        ],
        out_specs=pl.BlockSpec((tm, d), lambda i, ds: (i, 0)),
        scratch_shapes=[pltpu.VMEM((2, 2, tm, d), F32), pltpu.SemaphoreType.DMA((2, 2))],
    )
    return pl.pallas_call(
        functools.partial(_combine_body, final=final, tile0=tile0),
        out_shape=jax.ShapeDtypeStruct((n_tok, d), F32),
        grid_spec=grid_spec,
        compiler_params=_cparams(("arbitrary",)),
        name="moe_combine",
    )(dest, y, x, gate, meta, final_norm.reshape(1, d))


def _moe_layer(x, g, sc, sh, gate, grp_w, grp_b, exp_w, exp_b, w_gate, w_up, w_down, layer,
               final_norm, seq, final, split=None):
    t, d = x.shape
    pad = LANES - MOE_GROUPS - MOE_EXPERTS
    w_route = jnp.concatenate([grp_w, exp_w, jnp.zeros((d, pad), F32)], axis=1)
    b_route = jnp.concatenate([grp_b, exp_b, jnp.zeros((pad,), F32)]).reshape(1, LANES)
    meta, cnt = _router(x, g, sc, sh, w_route, b_route, seq)

    expert = meta[:, META_E:META_E + 2].astype(I32)
    rank = meta[:, META_R:META_R + 2].astype(I32)
    counts = cnt[0, ROUTE_E0:ROUTE_E0 + MOE_EXPERTS].astype(I32)
    padded = (counts + MOE_BLOCK - 1) // MOE_BLOCK * MOE_BLOCK
    p_ends = jnp.cumsum(padded)
    p_starts = p_ends - padded
    dest = (p_starts[expert] + rank).reshape(-1)
    n_rows = t * 2 + MOE_EXPERTS * MOE_BLOCK
    n_blocks = n_rows // MOE_BLOCK
    blk_start = jnp.arange(n_blocks, dtype=I32) * MOE_BLOCK
    blk_exp = jnp.minimum(jnp.sum((p_ends[None, :] <= blk_start[:, None]).astype(I32), axis=1),
                          MOE_EXPERTS - 1)
    n_used = p_ends[-1:] // MOE_BLOCK
    tail = (n_used + jnp.arange(MOE_EXPERTS, dtype=I32)) * MOE_BLOCK
    zero_starts = jnp.concatenate([jnp.where(padded > counts, p_ends - MOE_BLOCK, -1),
                                   jnp.where(tail < n_rows, tail, -1)]).astype(I32)

    xs = _dispatch(x, g, sc, sh, dest, zero_starts, n_rows, seq)
    y = _experts(xs, w_gate, w_up, w_down, layer, blk_exp, n_used)
    if split is None:
        return _combine(y, x, gate, meta, dest, final_norm, seq, final)
    return tuple(_combine(y, x, gate, meta, dest, final_norm, seq, final, tok0=a, n_tok=b - a)
                 for a, b in ((0, split), (split, t)))


def _rope_tables(seq):
    half = RET_DK // 2
    inv = ROPE_BASE ** (-jnp.arange(half, dtype=F32) / half)
    ang = jnp.arange(seq, dtype=F32)[:, None] * inv[None, :]
    cos, sin = jnp.cos(ang), jnp.sin(ang)
    return jnp.concatenate([cos, cos], axis=1), jnp.concatenate([-sin, sin], axis=1)


def _pad_rows(a, axis, n):
    pad = [(0, 0)] * a.ndim
    pad[axis] = (0, n - a.shape[axis])
    return jnp.pad(a, pad)


def kernel(x_prompt, x_sample, c_prompt, c_sample, ada_w, ada_b, norm1, norm2, ev_w_in, ev_gate_b, ev_conv_w, ev_conv_b, ev_ret_gn, ev_mlstm_gn, ev_w_out, od_w_in, od_conv_w, od_conv_b, od_dt_bias, od_a_log, od_d_skip, od_norm, od_w_out, moe_grp_w, moe_grp_b, moe_exp_w, moe_exp_b, moe_w_gate, moe_w_up, moe_w_down, final_norm):
    n_prompt = x_prompt.shape[0]
    seq, d = x_prompt.shape[1], x_prompt.shape[2]
    assert x_sample.shape[1] == seq and d == D_MODEL
    assert seq % RET_CHUNK == 0 and seq // MLSTM_CHUNK <= LANES // SSD_HPG
    nb = n_prompt + x_sample.shape[0]
    t = nb * seq
    x = x_prompt.reshape(n_prompt * seq, d)
    x_tail = x_sample.reshape(t - n_prompt * seq, d)
    depth = ada_w.shape[0]

    c_all = jnp.concatenate([c_prompt, c_sample], axis=0)
    c_pad = _pad_rows(c_all, 0, -(-nb // 8) * 8)
    mod = _modulation(c_pad, ada_w, ada_b)[:, :nb].reshape(depth, nb, N_MOD, 1, d)

    heads = jnp.arange(RET_HEADS, dtype=F32)
    lg = jnp.stack([jnp.log1p(-jnp.exp2(-RET_DECAY_FWD - heads)),
                    jnp.log1p(-jnp.exp2(-RET_DECAY_BWD - heads))])
    cos_t, sin_t = _rope_tables(seq)

    for i in range(depth):
        sh1, sc1, g1, sh2, sc2, g2 = (mod[i, :, m] for m in range(N_MOD))
        j = i // 2
        if i % 2 == 0:
            w_in = ev_w_in[j]
            w_side = w_in[:, EVEN_MAIN:].reshape(d, 4, MLSTM_HEADS).transpose(0, 2, 1)
            w_side = _pad_rows(_pad_rows(w_side, 2, GATE_ROWS).reshape(d, -1), 1, LANES)
            proj, gates = _fused_matmul(x, w_in[:, :EVEN_MAIN].astype(BF16), seq=seq, x_tail=x_tail,
                                        prologue="normmod", g=norm1[i], sc=sc1, sh=sh1, w_side=w_side,
                                        name="even_in_proj")
            proj = proj.reshape(nb, seq, EVEN_MAIN)
            ret = _retention(proj, lg, cos_t, sin_t, ev_ret_gn[j], nb, seq)
            ml = _mlstm(proj, gates, ev_gate_b[j], ev_conv_w[j], ev_conv_b[j], ev_mlstm_gn[j], nb, seq)
            x = _fused_matmul(ret.reshape(t, RET_V), ev_w_out[j].astype(BF16), seq=seq,
                              x2=ml.reshape(t, MLSTM_V),
                              res=x, res_tail=x_tail, gate=g1, tn=512, name="even_out_proj")
        else:
            w_in = od_w_in[j]
            proj, dt_raw = _fused_matmul(x, w_in[:, :ODD_MAIN].astype(BF16), seq=seq, x_tail=x_tail,
                                         prologue="normmod", g=norm1[i], sc=sc1, sh=sh1,
                                         w_side=w_in[:, ODD_MAIN:], name="odd_in_proj")
            def per_row(p):
                p = p.reshape(2, SSD_GROUPS, SSD_HPG).transpose(1, 0, 2)
                return jnp.tile(p, (1, 1, LANES // SSD_HPG))[..., None]

            def per_lane(p):
                p = p.reshape(*p.shape[:-1], SSD_GROUPS, SSD_HPG)
                p = jnp.moveaxis(p, -2, 0)
                return jnp.repeat(p, SSD_HEADDIM, axis=-1)[..., None, :]

            y = _ssd(proj.reshape(nb, seq, ODD_MAIN), dt_raw, per_row(od_dt_bias[j]), per_row(od_a_log[j]),
                     per_lane(od_d_skip[j]), od_conv_w[j], od_conv_b[j], nb, seq)
            x = _fused_matmul(y.reshape(t, SSD_INNER), od_w_out[j].astype(BF16), seq=seq, prologue="norm",
                              g=od_norm[j], res=x, res_tail=x_tail, gate=g1, tn=512, name="odd_out_proj")
        x_tail = None
        last = i == depth - 1
        x = _moe_layer(x, norm2[i], sc2, sh2, g2, moe_grp_w[i], moe_grp_b[i], moe_exp_w[i], moe_exp_b[i],
                       moe_w_gate, moe_w_up, moe_w_down, i, final_norm, seq, final=last,
                       split=n_prompt * seq if last else None)
    y_prompt, y_sample = x
    return (y_prompt.reshape(n_prompt, seq, d), y_sample.reshape(nb - n_prompt, seq, d))
```

```python
import functools
import math

import jax
import jax.numpy as jnp
import numpy as np
from jax import lax
from jax.experimental import pallas as pl
from jax.experimental.pallas import tpu as pltpu

F32 = jnp.float32
BF16 = jnp.bfloat16
I32 = jnp.int32

D_MODEL = 2048
N_MOD = 6
EPS = 1e-6
CONV_W = 5
CONV_HALO = 8

RET_HEADS = 8
RET_DV = D_MODEL // RET_HEADS
RET_DK = RET_DV // 2
RET_DECAY_FWD = 5.0
RET_DECAY_BWD = 5.5
ROPE_BASE = 10000.0
RET_CHUNK = 256
MLSTM_HEADS = 4
MLSTM_DV = D_MODEL // MLSTM_HEADS
MLSTM_DK = MLSTM_DV // 2
MLSTM_CHUNK = 128
GATE_ROWS = 8
SSD_INNER = 2 * D_MODEL
SSD_HEADDIM = 64
SSD_HEADS = SSD_INNER // SSD_HEADDIM
SSD_GROUPS = 8
SSD_HPG = SSD_HEADS // SSD_GROUPS
SSD_STATE = 128
SSD_CHUNK = 128
MOE_GROUPS = 4
MOE_EPG = 8
MOE_EXPERTS = MOE_GROUPS * MOE_EPG
EXPERT_FF = D_MODEL // 4
MOE_BLOCK = 256

RET_QK = RET_HEADS * RET_DK
RET_V = RET_HEADS * RET_DV
MLSTM_QK = MLSTM_HEADS * MLSTM_DK
MLSTM_V = MLSTM_HEADS * MLSTM_DV
MLSTM_NGATE = 4 * MLSTM_HEADS
EVEN_MAIN = 2 * RET_QK + 2 * RET_V + 2 * MLSTM_QK + 2 * MLSTM_V
EVEN_MIX = RET_V + MLSTM_V
SSD_BC = SSD_GROUPS * SSD_STATE
SSD_CONV_CH = SSD_INNER + 2 * SSD_BC
ODD_MAIN = SSD_INNER + SSD_CONV_CH

PROLOGUE_ROWS = 256
CAST_ROWS = 256
GATHER_UNROLL = 32
SCAN_UNROLL = 2
LANES = 128
VMEM_LIMIT = 56 * 1024 * 1024

NEG_INF = float("-inf")
LOG2_E = math.log2(math.e)


def _cparams(sem, vmem=VMEM_LIMIT):
    return pltpu.CompilerParams(dimension_semantics=sem, vmem_limit_bytes=vmem)


def _split3(x):
    hi = x.astype(BF16)
    r = x - hi.astype(F32)
    mid = r.astype(BF16)
    lo = (r - mid.astype(F32)).astype(BF16)
    return hi, mid, lo


def _dot(a, b):
    return jnp.dot(a, b, preferred_element_type=F32)


def _dot_nt(a, b):
    return lax.dot_general(a, b, (((1,), (1,)), ((), ())), preferred_element_type=F32)


def _dot_tn(a, b):
    return lax.dot_general(a, b, (((0,), (0,)), ((), ())), preferred_element_type=F32)


def _dot01_left(m01, x):
    hi, mid, lo = _split3(x)
    return _dot(m01, hi) + _dot(m01, mid) + _dot(m01, lo)


def _dot01_right(x, m01):
    hi, mid, lo = _split3(x)
    return _dot(hi, m01) + _dot(mid, m01) + _dot(lo, m01)


def _tri(n, kind):
    r = lax.broadcasted_iota(I32, (n, n), 0)
    c = lax.broadcasted_iota(I32, (n, n), 1)
    m = {"le": r <= c, "ge": r >= c, "gt": r > c}[kind]
    return jnp.where(m, 1.0, 0.0).astype(BF16)


def _shr(x, pow2):
    return lax.shift_right_arithmetic(x, jnp.int32(int(math.log2(pow2))))


def _sigmoid(x):
    return 1.0 / (1.0 + jnp.exp(-x))


def _silu(x):
    return x * _sigmoid(x)


def _softplus(x):
    return jnp.maximum(x, 0.0) + jnp.log1p(jnp.exp(-jnp.abs(x)))


def _log_sigmoid(x):
    return jnp.minimum(x, 0.0) - jnp.log1p(jnp.exp(-jnp.abs(x)))


def _rms(x, g):
    ms = jnp.mean(x * x, axis=-1, keepdims=True)
    return x * lax.rsqrt(ms + EPS) * g


def _head_norm(y, g):
    mu = jnp.mean(y, axis=-1, keepdims=True)
    yc = y - mu
    var = jnp.mean(yc * yc, axis=-1, keepdims=True)
    return yc * lax.rsqrt(var + EPS) * g


def _mod_body(c_ref, w_ref, b_ref, o_ref):
    c = c_ref[...]
    o_ref[0] = _dot(_silu(c).astype(BF16), w_ref[0].astype(BF16)) + b_ref[0]


def _modulation(c_pad, ada_w, ada_b):
    depth, d, n = ada_w.shape
    m = c_pad.shape[0]
    tn = 1024
    return pl.pallas_call(
        _mod_body,
        out_shape=jax.ShapeDtypeStruct((depth, m, n), F32),
        grid=(depth, n // tn),
        in_specs=[
            pl.BlockSpec((m, d), lambda l, j: (0, 0)),
            pl.BlockSpec((1, d, tn), lambda l, j: (l, 0, j)),
            pl.BlockSpec((1, 1, tn), lambda l, j: (l, 0, j)),
        ],
        out_specs=pl.BlockSpec((1, m, tn), lambda l, j: (l, 0, j)),
        compiler_params=_cparams(("parallel", "parallel")),
        name="modulation",
    )(c_pad, ada_w, ada_b.reshape(depth, 1, n))


def _mm_body(*refs, prologue, epilogue, side, two_lhs, n_head, split, res_split):
    it = iter(refs)
    x_ref = next(it)
    xt_ref = next(it) if split else None
    x2_ref = next(it) if two_lhs else None
    g_ref = next(it) if prologue != "none" else None
    sc_ref = next(it) if prologue == "normmod" else None
    sh_ref = next(it) if prologue == "normmod" else None
    w_ref = next(it)
    ws_ref = next(it) if side else None
    res_ref = next(it) if epilogue == "residual" else None
    rest_ref = next(it) if res_split else None
    gate_ref = next(it) if epilogue == "residual" else None
    o_ref = next(it)
    os_ref = next(it) if side else None
    h_scr = next(it) if prologue != "none" else None

    def in_head():
        return pl.program_id(0) < n_head

    if prologue != "none":
        def run_prologue(src_ref):
            rows_per = PROLOGUE_ROWS

            def chunk(i, carry):
                rows = pl.ds(pl.multiple_of(i * rows_per, rows_per), rows_per)
                y = _rms(src_ref[rows, :].astype(F32), g_ref[...])
                if prologue == "normmod":
                    y = y * (1.0 + sc_ref[0]) + sh_ref[0]
                hb = y.astype(BF16)
                h_scr[rows, :] = hb
                if side:
                    h_lo = (y - hb.astype(F32)).astype(BF16)
                    ws = ws_ref[...]
                    w_hi = ws.astype(BF16)
                    w_lo = (ws - w_hi.astype(F32)).astype(BF16)
                    os_ref[0, :, rows] = (_dot(hb, w_hi) + _dot(h_lo, w_hi) + _dot(hb, w_lo)).T
                return carry

            lax.fori_loop(0, src_ref.shape[0] // rows_per, chunk, 0)

        first_col = pl.program_id(1) == 0
        if split:
            pl.when(first_col & in_head())(lambda: run_prologue(x_ref))
            pl.when(first_col & jnp.logical_not(in_head()))(lambda: run_prologue(xt_ref))
        else:
            pl.when(first_col)(lambda: run_prologue(x_ref))
        lhs = h_scr[...]
    else:
        assert not split
        lhs = x_ref[...]
    if two_lhs:
        k1 = x_ref.shape[1]
        acc = _dot(lhs, w_ref[:k1, :]) + _dot(x2_ref[...], w_ref[k1:, :])
    else:
        acc = _dot(lhs, w_ref[...])
    if epilogue == "residual" and rest_ref is not None:
        @pl.when(in_head())
        def _():
            o_ref[...] = res_ref[...] + gate_ref[0] * acc

        @pl.when(jnp.logical_not(in_head()))
        def _():
            o_ref[...] = rest_ref[...] + gate_ref[0] * acc
    elif epilogue == "residual":
        o_ref[...] = res_ref[...] + gate_ref[0] * acc
    else:
        o_ref[...] = acc.astype(o_ref.dtype)


def _fused_matmul(x, w, *, seq, x_tail=None, x2=None, prologue="none", g=None, sc=None, sh=None,
                  w_side=None, res=None, res_tail=None, gate=None, out_dtype=BF16, tm=1024, tn=1024,
                  name="proj"):
    t, k = x.shape
    n = w.shape[1]
    tm = min(tm, seq)
    tn = min(tn, n)
    split = x_tail is not None
    res_split = res_tail is not None
    n_head = (x.shape[0] if split else res.shape[0] if res_split else t) // tm
    if split:
        t = t + x_tail.shape[0]
    assert t % tm == 0 and seq % tm == 0 and n % tn == 0
    tps = seq // tm
    epilogue = "residual" if res is not None else "plain"
    side = w_side is not None
    two_lhs = x2 is not None
    assert not (two_lhs and prologue != "none")

    def head_rows(i):
        return jnp.minimum(i, n_head - 1)

    def tail_rows(i):
        return jnp.maximum(i - n_head, 0)

    if split:
        in_specs = [pl.BlockSpec((tm, k), lambda i, j: (head_rows(i), 0)),
                    pl.BlockSpec((tm, k), lambda i, j: (tail_rows(i), 0))]
        args = [x, x_tail]
    else:
        in_specs = [pl.BlockSpec((tm, k), lambda i, j: (i, 0))]
        args = [x]
    if two_lhs:
        in_specs.append(pl.BlockSpec((tm, x2.shape[1]), lambda i, j: (i, 0)))
        args.append(x2)
        k = k + x2.shape[1]
    if prologue != "none":
        in_specs.append(pl.BlockSpec((1, k), lambda i, j: (0, 0)))
        args.append(g.reshape(1, k))
    if prologue == "normmod":
        in_specs += [pl.BlockSpec((1, 1, k), lambda i, j: (i // tps, 0, 0))] * 2
        args += [sc, sh]
    in_specs.append(pl.BlockSpec((k, tn), lambda i, j: (0, j)))
    args.append(w)
    if side:
        in_specs.append(pl.BlockSpec((k, LANES), lambda i, j: (0, 0)))
        args.append(w_side)
    if epilogue == "residual":
        if res_split:
            in_specs += [pl.BlockSpec((tm, tn), lambda i, j: (head_rows(i), jnp.where(i < n_head, j, 0))),
                         pl.BlockSpec((tm, tn), lambda i, j: (tail_rows(i), jnp.where(i < n_head, 0, j)))]
            args += [res, res_tail]
        else:
            in_specs.append(pl.BlockSpec((tm, tn), lambda i, j: (i, j)))
            args.append(res)
        in_specs.append(pl.BlockSpec((1, 1, tn), lambda i, j: (i // tps, 0, j)))
        args.append(gate)
        out_dtype = F32
    out_shape = [jax.ShapeDtypeStruct((t, n), out_dtype)]
    out_specs = [pl.BlockSpec((tm, tn), lambda i, j: (i, j))]
    if side:
        out_shape.append(jax.ShapeDtypeStruct((t // seq, LANES, seq), F32))
        out_specs.append(pl.BlockSpec((1, LANES, tm), lambda i, j: (i // tps, 0, i % tps)))
    scratch = [pltpu.VMEM((tm, k), BF16)] if prologue != "none" else []
    outs = pl.pallas_call(
        functools.partial(_mm_body, prologue=prologue, epilogue=epilogue, side=side, two_lhs=two_lhs,
                          n_head=n_head, split=split, res_split=res_split),
        out_shape=out_shape,
        grid=(t // tm, n // tn),
        in_specs=in_specs,
        out_specs=out_specs,
        scratch_shapes=scratch,
        compiler_params=_cparams(("parallel", "arbitrary")),
        name=name,
    )(*args)
    return outs if side else outs[0]


CONV_ROWS = 128


def _conv_fill(src_ref, pad_scr, seq):
    ch = pad_scr.shape[1]
    halo = CONV_HALO
    rows = CONV_ROWS
    zeros = jnp.zeros((halo, ch), F32)
    pad_scr[pl.ds(0, halo), :] = zeros
    pad_scr[pl.ds(seq + halo, halo), :] = zeros

    def fill(i, carry):
        r0 = pl.multiple_of(i * rows, rows)
        pad_scr[pl.ds(pl.multiple_of(r0 + halo, halo), rows), :] = src_ref[0, pl.ds(r0, rows), :].astype(F32)
        return carry

    lax.fori_loop(0, seq // rows, fill, 0)


def _conv_silu_rows(pad_scr, w_ref, b_ref, r0, emit):
    ch = pad_scr.shape[1]
    halo = CONV_HALO
    rows = CONV_ROWS
    win = rows + 2 * halo
    half = (CONV_W - 1) // 2
    for lane0 in range(0, ch, LANES):
        cols = slice(lane0, lane0 + LANES)
        window = pad_scr[pl.ds(r0, win), cols]
        acc = jnp.zeros((rows, LANES), F32) + b_ref[:, cols]
        for j in range(CONV_W):
            d = j - half
            shifted = window if d == 0 else pltpu.roll(window, (-d) % win, axis=0)
            acc = acc + w_ref[j:j + 1, cols] * shifted[halo:halo + rows, :]
        emit(lane0, _silu(acc))


def _two_ended_scan(n_chunks, prepare, step, finish):
    assert n_chunks % 2 == 0
    half = n_chunks // 2

    def first(i, carry):
        prepare(i)
        prepare(n_chunks - 1 - i)
        step(i)
        return carry

    def second(i, carry):
        step(i)
        finish(i)
        finish(n_chunks - 1 - i)
        return carry

    unroll = SCAN_UNROLL if half % SCAN_UNROLL == 0 else 1
    lax.fori_loop(0, half, first, 0, unroll=unroll)
    lax.fori_loop(half, n_chunks, second, 0, unroll=unroll)


def _ret_body(lg_ref, q_ref, k_ref, v_ref, g_ref, cos_ref, sin_ref, gn_ref, o_ref,
              qs_scr, ks_scr, acc_scr, st_scr, *, seq):
    c_len = RET_CHUNK
    n_chunks = seq // c_len
    h = pl.program_id(1)
    lgf = lg_ref[0, h]
    lgb = lg_ref[1, h]
    ri = lax.broadcasted_iota(I32, (c_len, c_len), 0)
    ci = lax.broadcasted_iota(I32, (c_len, c_len), 1)
    diff = (ri - ci).astype(F32)
    dmat = jnp.exp(jnp.where(diff >= 0, lgf * diff, -lgb * diff))
    pos = lax.broadcasted_iota(I32, (c_len, 1), 0).astype(F32)
    qdec_f = jnp.exp(lgf * (pos + 1.0))
    kdec_f = jnp.exp(lgf * (c_len - 1.0 - pos))
    cdec_f = jnp.exp(jnp.full((1, 1), c_len, F32) * lgf)
    qdec_b = jnp.exp(lgb * (c_len - pos))
    kdec_b = jnp.exp(lgb * pos)
    cdec_b = jnp.exp(jnp.full((1, 1), c_len, F32) * lgb)
    half = RET_DK // 2

    def rope(x, rows):
        return x * cos_ref[rows, :] + pltpu.roll(x, half, axis=1) * sin_ref[rows, :]

    def chunk_rows(c):
        return pl.ds(pl.multiple_of(c * c_len, c_len), c_len)

    def prepare(c):
        rows = chunk_rows(c)
        qs_scr[rows, :] = rope(q_ref[0, rows, :].astype(F32), rows).astype(BF16)
        ks_scr[rows, :] = rope(k_ref[0, rows, :].astype(F32), rows) * (RET_DK ** -0.5)

    st_scr[...] = jnp.zeros_like(st_scr)

    def step(i):
        rows = chunk_rows(i)
        qb = qs_scr[rows, :]
        k = ks_scr[rows, :]
        v = v_ref[0, rows, :]
        p = (_dot_nt(qb, k.astype(BF16)) * dmat).astype(BF16)
        acc_scr[0, rows, :] = _dot(p, v) + qdec_f * _dot(qb, st_scr[0].astype(BF16))
        st_scr[0] = cdec_f * st_scr[0] + _dot_tn((k * kdec_f).astype(BF16), v)

        rows = chunk_rows(n_chunks - 1 - i)
        k = ks_scr[rows, :]
        acc_scr[1, rows, :] = qdec_b * _dot(qs_scr[rows, :], st_scr[1].astype(BF16))
        st_scr[1] = cdec_b * st_scr[1] + _dot_tn((k * kdec_b).astype(BF16), v_ref[0, rows, :])

    def finish(c):
        rows = chunk_rows(c)
        o = acc_scr[0, rows, :] + acc_scr[1, rows, :]
        gate = g_ref[0, rows, :].astype(F32)
        o_ref[0, rows, :] = (_head_norm(o, gn_ref[0]) * _silu(gate)).astype(o_ref.dtype)

    _two_ended_scan(n_chunks, prepare, step, finish)


def _retention(proj, lg, cos_t, sin_t, ret_gn, nb, seq):
    kq, kv = RET_DK, RET_DV
    grid_spec = dict(
        grid=(nb, RET_HEADS),
        in_specs=[
            pl.BlockSpec(memory_space=pltpu.SMEM),
            pl.BlockSpec((1, seq, kq), lambda b, h: (b, 0, h)),
            pl.BlockSpec((1, seq, kq), lambda b, h: (b, 0, RET_QK // kq + h)),
            pl.BlockSpec((1, seq, kv), lambda b, h: (b, 0, 2 * RET_QK // kv + h)),
            pl.BlockSpec((1, seq, kv), lambda b, h: (b, 0, (2 * RET_QK + RET_V) // kv + h)),
            pl.BlockSpec((seq, kq), lambda b, h: (0, 0)),
            pl.BlockSpec((seq, kq), lambda b, h: (0, 0)),
            pl.BlockSpec((1, 1, kv), lambda b, h: (h, 0, 0)),
        ],
        out_specs=pl.BlockSpec((1, seq, kv), lambda b, h: (b, 0, h)),
        scratch_shapes=[
            pltpu.VMEM((seq, kq), BF16),
            pltpu.VMEM((seq, kq), F32),
            pltpu.VMEM((2, seq, kv), F32),
            pltpu.VMEM((2, kq, kv), F32),
        ],
    )
    return pl.pallas_call(
        functools.partial(_ret_body, seq=seq),
        out_shape=jax.ShapeDtypeStruct((nb, seq, RET_V), BF16),
        **grid_spec,
        compiler_params=_cparams(("parallel", "parallel")),
        name="retention",
    )(lg, proj, proj, proj, proj, cos_t, sin_t, ret_gn.reshape(RET_HEADS, 1, kv))


def _col_of(mat, c):
    lane = lax.broadcasted_iota(I32, mat.shape, 1)
    return jnp.sum(jnp.where(lane == c, mat, 0.0), axis=1, keepdims=True)


def _mlstm_body(gb_ref, q_ref, k_ref, v_ref, o_gate_ref, wq_ref, wk_ref, bq_ref, bk_ref,
                gr_ref, gn_ref, o_ref,
                padq_scr, padk_scr, qs_scr, ks_scr, acc_scr, c_scr, n_scr, m_scr, row_scr, col_scr,
                *, seq):
    ln = MLSTM_CHUNK
    assert ln == CONV_ROWS
    n_chunks = seq // ln
    h = pl.program_id(1)

    _conv_fill(q_ref, padq_scr, seq)
    _conv_fill(k_ref, padk_scr, seq)

    def prepare(c):
        r0 = pl.multiple_of(c * ln, ln)
        rows = pl.ds(r0, ln)

        def emit_q(lane0, y):
            qs_scr[rows, lane0:lane0 + LANES] = (y * (MLSTM_DK ** -0.5)).astype(BF16)

        def emit_k(lane0, y):
            ks_scr[rows, lane0:lane0 + LANES] = y

        _conv_silu_rows(padq_scr, wq_ref, bq_ref, r0, emit_q)
        _conv_silu_rows(padk_scr, wk_ref, bk_ref, r0, emit_k)

    row_scr[...] = jnp.zeros_like(row_scr)
    for t in range(4):
        for c in range(n_chunks):
            row_scr[t, c:c + 1, :] = gr_ref[0, t:t + 1, c * ln:(c + 1) * ln]
    tri_le = _tri(ln, "le")
    tri_ge = _tri(ln, "ge")
    for d in range(2):
        ig = row_scr[2 * d] + gb_ref[(2 * d) * MLSTM_HEADS + h]
        lf = _log_sigmoid(row_scr[2 * d + 1] + gb_ref[(2 * d + 1) * MLSTM_HEADS + h])
        bc = _dot01_right(lf, tri_le if d == 0 else tri_ge)
        row_scr[2 * d] = ig
        row_scr[2 * d + 1] = bc
        col_scr[2 * d] = ig.T
        col_scr[2 * d + 1] = bc.T

    ri = lax.broadcasted_iota(I32, (ln, ln), 0)
    ci = lax.broadcasted_iota(I32, (ln, ln), 1)

    def chunk_step(c, d):
        r0 = pl.multiple_of(c * ln, ln)
        rows = pl.ds(r0, ln)
        qb = qs_scr[rows, :]
        kf = ks_scr[rows, :]
        v = v_ref[0, rows, :]
        i_row = row_scr[2 * d, pl.ds(c, 1), :]
        b_row = row_scr[2 * d + 1, pl.ds(c, 1), :]
        i_col = _col_of(col_scr[2 * d], c)
        b_col = _col_of(col_scr[2 * d + 1], c)
        m_st = m_scr[d]
        mask = (ri >= ci) if d == 0 else (ri <= ci)
        logd = jnp.where(mask, b_col - b_row + i_row, NEG_INF)
        m_inter = b_col + m_st
        m_row = jnp.maximum(m_inter, jnp.max(logd, axis=1, keepdims=True))
        sc = _dot_nt(qb, kf.astype(BF16)) * jnp.exp(logd - m_row)
        inter = jnp.exp(m_inter - m_row)
        num = _dot(sc.astype(BF16), v) + inter * _dot(qb, c_scr[d].astype(BF16))
        den = jnp.sum(sc, axis=1, keepdims=True) + inter * jnp.sum(
            qb.astype(F32) * n_scr[d], axis=1, keepdims=True)
        hh = num / jnp.maximum(jnp.abs(den), jnp.exp(-m_row))
        b_end = b_row[:, ln - 1:ln] if d == 0 else b_row[:, 0:1]
        logw = b_end - b_col + i_col
        m_new = jnp.maximum(b_end + m_st, jnp.max(logw, axis=0, keepdims=True))
        kw = kf * jnp.exp(logw - m_new)
        dec = jnp.exp(b_end + m_st - m_new)
        c_scr[d] = dec * c_scr[d] + _dot_tn(kw.astype(BF16), v)
        n_scr[d] = dec * n_scr[d] + jnp.sum(kw, axis=0, keepdims=True)
        m_scr[d] = m_new
        acc_scr[d, rows, :] = hh

    c_scr[...] = jnp.zeros_like(c_scr)
    n_scr[...] = jnp.zeros_like(n_scr)
    m_scr[...] = jnp.zeros_like(m_scr)

    def step(i):
        chunk_step(i, 0)
        chunk_step(n_chunks - 1 - i, 1)

    def finish(c):
        rows = pl.ds(pl.multiple_of(c * ln, ln), ln)
        y = _head_norm(acc_scr[0, rows, :] + acc_scr[1, rows, :], gn_ref[0])
        o_ref[0, rows, :] = (y * _sigmoid(o_gate_ref[0, rows, :].astype(F32))).astype(o_ref.dtype)

    _two_ended_scan(n_chunks, prepare, step, finish)


def _mlstm(proj, gate_rows, gate_b, conv_w, conv_b, mlstm_gn, nb, seq):
    kq, kv = MLSTM_DK, MLSTM_DV
    q0 = 2 * RET_QK + 2 * RET_V
    k0 = q0 + MLSTM_QK
    v0 = k0 + MLSTM_QK
    o0 = v0 + MLSTM_V
    grid_spec = dict(
        grid=(nb, MLSTM_HEADS),
        in_specs=[
            pl.BlockSpec(memory_space=pltpu.SMEM),
            pl.BlockSpec((1, seq, kq), lambda b, h: (b, 0, q0 // kq + h)),
            pl.BlockSpec((1, seq, kq), lambda b, h: (b, 0, k0 // kq + h)),
            pl.BlockSpec((1, seq, kv), lambda b, h: (b, 0, v0 // kv + h)),
            pl.BlockSpec((1, seq, kv), lambda b, h: (b, 0, o0 // kv + h)),
            pl.BlockSpec((CONV_W, kq), lambda b, h: (0, h)),
            pl.BlockSpec((CONV_W, kq), lambda b, h: (0, MLSTM_QK // kq + h)),
            pl.BlockSpec((1, kq), lambda b, h: (0, h)),
            pl.BlockSpec((1, kq), lambda b, h: (0, MLSTM_QK // kq + h)),
            pl.BlockSpec((1, GATE_ROWS, seq), lambda b, h: (b, h, 0)),
            pl.BlockSpec((1, 1, kv), lambda b, h: (h, 0, 0)),
        ],
        out_specs=pl.BlockSpec((1, seq, kv), lambda b, h: (b, 0, h)),
        scratch_shapes=[
            pltpu.VMEM((seq + 2 * CONV_HALO, kq), F32),
            pltpu.VMEM((seq + 2 * CONV_HALO, kq), F32),
            pltpu.VMEM((seq, kq), BF16),
            pltpu.VMEM((seq, kq), F32),
            pltpu.VMEM((2, seq, kv), F32),
            pltpu.VMEM((2, kq, kv), F32),
            pltpu.VMEM((2, 1, kq), F32),
            pltpu.VMEM((2, 1, 1), F32),
            pltpu.VMEM((4, LANES, MLSTM_CHUNK), F32),
            pltpu.VMEM((4, MLSTM_CHUNK, LANES), F32),
        ],
    )
    return pl.pallas_call(
        functools.partial(_mlstm_body, seq=seq),
        out_shape=jax.ShapeDtypeStruct((nb, seq, MLSTM_V), BF16),
        **grid_spec,
        compiler_params=_cparams(("parallel", "parallel")),
        name="mlstm",
    )(gate_b, proj, proj, proj, proj, conv_w, conv_w, conv_b.reshape(1, -1), conv_b.reshape(1, -1),
      gate_rows, mlstm_gn.reshape(MLSTM_HEADS, 1, kv))


def _ssd_body(z_ref, x_ref, b_ref, c_ref, wx_ref, wb_ref, wc_ref, bx_ref, bb_ref, bc_ref,
              dtf_ref, dtb_ref, bias_ref, alog_ref, dskip_ref, o_ref,
              padx_scr, padb_scr, padc_scr, xs_scr, xb_scr, bs_scr, bst_scr, cs_scr, y_scr, st_scr,
              acr_scr, dtr_scr, er_scr, ur_scr, act_scr, *, seq):
    ln = SSD_CHUNK
    assert ln == CONV_ROWS
    n_chunks = seq // ln
    hp = SSD_HEADDIM
    n_pairs = SSD_HPG // 2

    _conv_fill(x_ref, padx_scr, seq)
    _conv_fill(b_ref, padb_scr, seq)
    _conv_fill(c_ref, padc_scr, seq)

    def prepare(c):
        r0 = pl.multiple_of(c * ln, ln)
        rows = pl.ds(r0, ln)

        def emit_x(lane0, y):
            xs_scr[rows, lane0:lane0 + LANES] = y
            xb_scr[rows, lane0:lane0 + LANES] = y.astype(BF16)

        def emit_b(lane0, y):
            bs_scr[rows, :] = y.astype(BF16)
            bst_scr[:, rows] = y.T.astype(BF16)

        def emit_c(lane0, y):
            cs_scr[rows, :] = y.astype(BF16)

        _conv_silu_rows(padx_scr, wx_ref, bx_ref, r0, emit_x)
        _conv_silu_rows(padb_scr, wb_ref, bb_ref, r0, emit_b)
        _conv_silu_rows(padc_scr, wc_ref, bc_ref, r0, emit_c)

    for d, dt_ref in enumerate((dtf_ref, dtb_ref)):
        acr_scr[d] = jnp.zeros((LANES, ln), F32)
        for c in range(n_chunks):
            acr_scr[d, c * SSD_HPG:(c + 1) * SSD_HPG, :] = dt_ref[0, :, c * ln:(c + 1) * ln]
        dt = _softplus(acr_scr[d] + bias_ref[0, d])
        adt = dt * (-jnp.exp(alog_ref[0, d]))
        acum = _dot01_right(adt, _tri(ln, "le" if d == 0 else "ge"))
        a_end = acum[:, ln - 1:ln] if d == 0 else acum[:, 0:1]
        acum2 = acum * LOG2_E
        acr_scr[d] = acum2
        dtr_scr[d] = dt
        er_scr[d] = jnp.exp(acum)
        ur_scr[d] = dt * jnp.exp(a_end - acum)
        act_scr[d] = acum2.T

    ri = lax.broadcasted_iota(I32, (ln, ln), 0)
    ci = lax.broadcasted_iota(I32, (ln, ln), 1)
    in_first = lax.broadcasted_iota(I32, (1, LANES), 1) < hp
    on_diag = ri == ci

    def chunk_rows(c):
        return pl.ds(pl.multiple_of(c * ln, ln), ln)

    def dir_step(d, c):
        rows = chunk_rows(c)
        xb = xb_scr[rows, :]
        bcm = bs_scr[rows, :]
        bct = bst_scr[:, rows].astype(F32)
        ccm = cs_scr[rows, :]
        cb = _dot_nt(ccm, bcm)
        carried = _dot(ccm, st_scr[d].astype(BF16)).astype(BF16)
        head_rows = pl.ds(pl.multiple_of(c * SSD_HPG, SSD_HPG), SSD_HPG)
        arow = acr_scr[d, head_rows, :]
        dtrow = dtr_scr[d, head_rows, :]
        erow = er_scr[d, head_rows, :]
        urow = ur_scr[d, head_rows, :]
        acols = pltpu.roll(act_scr[d], (LANES - c * SSD_HPG) & (LANES - 1), axis=1)
        mask = (ri >= ci) if d == 0 else (ri <= ci)
        end = ln - 1 if d == 0 else 0
        pieces = []
        for pair in range(n_pairs):
            lanes = slice(pair * LANES, (pair + 1) * LANES)
            lhs, rhs, lhs_state, keep = [], [], [], []
            for sub in range(2):
                k = 2 * pair + sub
                sel = in_first if sub == 0 else ~in_first
                dec = jnp.exp2(jnp.where(mask, acols[:, k:k + 1] - arow[k:k + 1, :], NEG_INF))
                lhs.append((cb * dec * dtrow[k:k + 1, :]).astype(BF16))
                rhs.append(jnp.where(sel, xb[:, lanes], jnp.zeros((ln, LANES), BF16)))
                lhs_state.append((bct * urow[k:k + 1, :]).astype(BF16))
                keep.append(erow[k:k + 1, end:end + 1])
            for sub in range(2):
                k = 2 * pair + sub
                sel = in_first if sub == 0 else ~in_first
                lhs.append(jnp.where(on_diag, erow[k:k + 1, :], 0.0).astype(BF16))
                rhs.append(jnp.where(sel, carried[:, lanes], jnp.zeros((ln, LANES), BF16)))
            pieces.append(_dot(jnp.concatenate(lhs, axis=1), jnp.concatenate(rhs, axis=0)))
            st_scr[d, :, lanes] = (jnp.where(in_first, keep[0], keep[1]) * st_scr[d, :, lanes]
                                   + _dot(jnp.concatenate(lhs_state, axis=1),
                                          jnp.concatenate(rhs[:2], axis=0)))
        y_scr[d, rows, :] = jnp.concatenate(pieces, axis=1)

    st_scr[...] = jnp.zeros_like(st_scr)

    def step(i):
        dir_step(0, i)
        dir_step(1, n_chunks - 1 - i)

    def finish(c):
        rows = chunk_rows(c)
        y = y_scr[0, rows, :] + y_scr[1, rows, :] + dskip_ref[0] * xs_scr[rows, :]
        o_ref[0, rows, :] = (y * _silu(z_ref[0, rows, :].astype(F32))).astype(o_ref.dtype)

    _two_ended_scan(n_chunks, prepare, step, finish)


def _ssd(proj, dt_t, bias_col, alog_col, dskip_x, conv_w, conv_b, nb, seq):
    width = SSD_HPG * SSD_HEADDIM
    ns = SSD_STATE
    x0 = SSD_INNER
    b0 = 2 * SSD_INNER
    c0 = b0 + SSD_BC
    cb = conv_b.reshape(1, -1)
    return pl.pallas_call(
        functools.partial(_ssd_body, seq=seq),
        out_shape=jax.ShapeDtypeStruct((nb, seq, SSD_INNER), BF16),
        grid=(nb, SSD_GROUPS),
        in_specs=[
            pl.BlockSpec((1, seq, width), lambda b, g: (b, 0, g)),
            pl.BlockSpec((1, seq, width), lambda b, g: (b, 0, x0 // width + g)),
            pl.BlockSpec((1, seq, ns), lambda b, g: (b, 0, b0 // ns + g)),
            pl.BlockSpec((1, seq, ns), lambda b, g: (b, 0, c0 // ns + g)),
            pl.BlockSpec((CONV_W, width), lambda b, g: (0, g)),
            pl.BlockSpec((CONV_W, ns), lambda b, g: (0, SSD_INNER // ns + g)),
            pl.BlockSpec((CONV_W, ns), lambda b, g: (0, (SSD_INNER + SSD_BC) // ns + g)),
            pl.BlockSpec((1, width), lambda b, g: (0, g)),
            pl.BlockSpec((1, ns), lambda b, g: (0, SSD_INNER // ns + g)),
            pl.BlockSpec((1, ns), lambda b, g: (0, (SSD_INNER + SSD_BC) // ns + g)),
            pl.BlockSpec((1, SSD_HPG, seq), lambda b, g: (b, g, 0)),
            pl.BlockSpec((1, SSD_HPG, seq), lambda b, g: (b, SSD_GROUPS + g, 0)),
            pl.BlockSpec((1, 2, LANES, 1), lambda b, g: (g, 0, 0, 0)),
            pl.BlockSpec((1, 2, LANES, 1), lambda b, g: (g, 0, 0, 0)),
            pl.BlockSpec((1, 1, width), lambda b, g: (g, 0, 0)),
        ],
        out_specs=pl.BlockSpec((1, seq, width), lambda b, g: (b, 0, g)),
        scratch_shapes=[
            pltpu.VMEM((seq + 2 * CONV_HALO, width), F32),
            pltpu.VMEM((seq + 2 * CONV_HALO, ns), F32),
            pltpu.VMEM((seq + 2 * CONV_HALO, ns), F32),
            pltpu.VMEM((seq, width), F32),
            pltpu.VMEM((seq, width), BF16),
            pltpu.VMEM((seq, ns), BF16),
            pltpu.VMEM((ns, seq), BF16),
            pltpu.VMEM((seq, ns), BF16),
            pltpu.VMEM((2, seq, width), F32),
            pltpu.VMEM((2, ns, width), F32),
            pltpu.VMEM((2, LANES, SSD_CHUNK), F32),
            pltpu.VMEM((2, LANES, SSD_CHUNK), F32),
            pltpu.VMEM((2, LANES, SSD_CHUNK), F32),
            pltpu.VMEM((2, LANES, SSD_CHUNK), F32),
            pltpu.VMEM((2, SSD_CHUNK, LANES), F32),
        ],
        compiler_params=_cparams(("parallel", "parallel")),
        name="ssd",
    )(proj, proj, proj, proj, conv_w, conv_w, conv_w, cb, cb, cb,
      dt_t, dt_t, bias_col, alog_col, dskip_x)


META_E = 0
META_G = 2
META_R = 4
ROUTE_E0 = MOE_GROUPS


def _moe_input(x, g_ref, sc_ref, sh_ref):
    return _rms(x, g_ref[...]) * (1.0 + sc_ref[0]) + sh_ref[0]


def _router_body(x_ref, g_ref, sc_ref, sh_ref, w_ref, b_ref, meta_ref, cnt_ref,
                 carry_scr, whi_scr, wlo_scr):
    @pl.when(pl.program_id(0) == 0)
    def _():
        carry_scr[...] = jnp.zeros_like(carry_scr)
        w = w_ref[...]
        w_hi = w.astype(BF16)
        whi_scr[...] = w_hi
        wlo_scr[...] = (w - w_hi.astype(F32)).astype(BF16)

    meta_ref[...] = _route_rows(_moe_input(x_ref[...], g_ref, sc_ref, sh_ref),
                                whi_scr[...], wlo_scr[...], b_ref[...], carry_scr)
    cnt_ref[...] = jnp.broadcast_to(carry_scr[...], cnt_ref.shape)


def _route_rows(y, w_hi, w_lo, bias, carry_scr):
    tm = y.shape[0]
    h_hi = y.astype(BF16)
    h_lo = (y - h_hi.astype(F32)).astype(BF16)
    logits = _dot(h_hi, w_hi) + _dot(h_lo, w_hi) + _dot(h_hi, w_lo) + bias

    lane = lax.broadcasted_iota(I32, (tm, LANES), 1)
    lane_f = lane.astype(F32)
    big = float(LANES)
    is_grp = lane < MOE_GROUPS
    gl = jnp.where(is_grp, logits, NEG_INF)
    gmax = jnp.max(gl, axis=1, keepdims=True)
    gidx = jnp.min(jnp.where(gl == gmax, lane_f, big), axis=1, keepdims=True)
    gprob = 1.0 / jnp.sum(jnp.where(is_grp, jnp.exp(gl - gmax), 0.0), axis=1, keepdims=True)

    el = lane - ROUTE_E0
    el_f = el.astype(F32)
    valid = (el >= 0) & (el < MOE_EXPERTS)
    in_grp = valid & (_shr(el, MOE_EPG).astype(F32) == gidx)
    ev = jnp.where(in_grp, logits, NEG_INF)
    v1 = jnp.max(ev, axis=1, keepdims=True)
    i1 = jnp.min(jnp.where(ev == v1, el_f, big), axis=1, keepdims=True)
    ev2 = jnp.where(el_f == i1, NEG_INF, ev)
    v2 = jnp.max(ev2, axis=1, keepdims=True)
    i2 = jnp.min(jnp.where(ev2 == v2, el_f, big), axis=1, keepdims=True)
    p2 = jnp.exp(v2 - v1)
    s1 = 1.0 / (1.0 + p2)
    gate1 = s1 * gprob
    gate2 = p2 * s1 * gprob

    oh1 = jnp.where(el_f == i1, 1.0, 0.0)
    oh2 = jnp.where(el_f == i2, 1.0, 0.0)
    oh = oh1 + oh2
    before = _dot(_tri(tm, "gt"), oh.astype(BF16)) + carry_scr[...]
    rank1 = jnp.sum(oh1 * before, axis=1, keepdims=True)
    rank2 = jnp.sum(oh2 * before, axis=1, keepdims=True)
    carry_scr[...] = carry_scr[...] + jnp.sum(oh, axis=0, keepdims=True)

    meta = jnp.zeros((tm, LANES), F32)
    for col, val in ((META_E, i1), (META_E + 1, i2), (META_G, gate1), (META_G + 1, gate2),
                     (META_R, rank1), (META_R + 1, rank2)):
        meta = jnp.where(lane == col, val, meta)
    return meta


def _router(x, g, sc, sh, w_route, b_route, seq, tm=512):
    t, k = x.shape
    tm = min(tm, seq)
    tps = seq // tm
    return pl.pallas_call(
        _router_body,
        out_shape=[jax.ShapeDtypeStruct((t, LANES), F32),
                   jax.ShapeDtypeStruct((8, LANES), F32)],
        grid=(t // tm,),
        in_specs=[
            pl.BlockSpec((tm, k), lambda i: (i, 0)),
            pl.BlockSpec((1, k), lambda i: (0, 0)),
            pl.BlockSpec((1, 1, k), lambda i: (i // tps, 0, 0)),
            pl.BlockSpec((1, 1, k), lambda i: (i // tps, 0, 0)),
            pl.BlockSpec((k, LANES), lambda i: (0, 0)),
            pl.BlockSpec((1, LANES), lambda i: (0, 0)),
        ],
        out_specs=[pl.BlockSpec((tm, LANES), lambda i: (i, 0)),
                   pl.BlockSpec((8, LANES), lambda i: (0, 0))],
        scratch_shapes=[pltpu.VMEM((1, LANES), F32), pltpu.VMEM((k, LANES), BF16),
                        pltpu.VMEM((k, LANES), BF16)],
        compiler_params=_cparams(("arbitrary",)),
        name="router",
    )(x, g.reshape(1, k), sc, sh, w_route, b_route)


ZERO_BLOCKS = 2 * MOE_EXPERTS


def _dispatch_body(dest_ref, zero_ref, x_ref, g_ref, sc_ref, sh_ref, xs_hbm, h_scr, zero_scr, sem, zsem):
    i = pl.program_id(0)
    n_steps = pl.num_programs(0)
    tm = x_ref.shape[0]
    slot = i % 2

    def zero_copy(start):
        return pltpu.make_async_copy(zero_scr, xs_hbm.at[pl.ds(start, MOE_BLOCK)], zsem)

    @pl.when(i == 0)
    def _():
        zero_scr[...] = jnp.zeros_like(zero_scr)

        def issue(k, carry):
            @pl.when(zero_ref[k] >= 0)
            def _():
                zero_copy(pl.multiple_of(zero_ref[k], MOE_BLOCK)).start()
            return carry

        def drain(k, carry):
            @pl.when(zero_ref[k] >= 0)
            def _():
                zero_copy(0).wait()
            return carry

        lax.fori_loop(0, ZERO_BLOCKS, issue, 0)
        lax.fori_loop(0, ZERO_BLOCKS, drain, 0)

    def wait_rows(s):
        for _ in range(2):
            pltpu.make_async_copy(h_scr.at[s], xs_hbm.at[pl.ds(0, tm)], sem.at[s]).wait()

    @pl.when(i >= 2)
    def _():
        wait_rows(slot)

    h_scr[slot] = _moe_input(x_ref[...], g_ref, sc_ref, sh_ref)

    for r in range(tm):
        for j in range(2):
            pltpu.make_async_copy(h_scr.at[slot, pl.ds(r, 1)],
                                  xs_hbm.at[pl.ds(dest_ref[(i * tm + r) * 2 + j], 1)],
                                  sem.at[slot]).start()

    @pl.when(i == n_steps - 1)
    def _():
        @pl.when(n_steps >= 2)
        def _():
            wait_rows(1 - slot)
        wait_rows(slot)


def _dispatch(x, g, sc, sh, dest, zero_starts, n_rows, seq, tm=256):
    t, k = x.shape
    tm = min(tm, seq)
    tps = seq // tm
    grid_spec = pltpu.PrefetchScalarGridSpec(
        num_scalar_prefetch=2,
        grid=(t // tm,),
        in_specs=[
            pl.BlockSpec((tm, k), lambda i, ds, zs: (i, 0)),
            pl.BlockSpec((1, k), lambda i, ds, zs: (0, 0)),
            pl.BlockSpec((1, 1, k), lambda i, ds, zs: (i // tps, 0, 0)),
            pl.BlockSpec((1, 1, k), lambda i, ds, zs: (i // tps, 0, 0)),
        ],
        out_specs=pl.BlockSpec(memory_space=pl.ANY),
        scratch_shapes=[pltpu.VMEM((2, tm, k), F32), pltpu.VMEM((MOE_BLOCK, k), F32),
                        pltpu.SemaphoreType.DMA((2,)), pltpu.SemaphoreType.DMA],
    )
    return pl.pallas_call(
        _dispatch_body,
        out_shape=jax.ShapeDtypeStruct((n_rows, k), F32),
        grid_spec=grid_spec,
        compiler_params=_cparams(("arbitrary",)),
        name="moe_dispatch",
    )(dest, zero_starts, x, g.reshape(1, k), sc, sh)


def _row_copy(src_hbm, dst, sem, src_row, dst_row):
    return pltpu.make_async_copy(src_hbm.at[pl.ds(src_row, 1)], dst.at[pl.ds(dst_row, 1)], sem)


def _start_row_gather(idx_ref, base, n_rows, stride, src_hbm, dst, sem, straight_line=False):
    def body(r, carry):
        _row_copy(src_hbm, dst, sem, idx_ref[base + r * stride], r).start()
        return carry
    if straight_line:
        for r in range(n_rows):
            body(r, 0)
    else:
        lax.fori_loop(0, n_rows, body, 0, unroll=GATHER_UNROLL)


def _wait_row_gather(src_hbm, dst, sem, n_rows):
    pltpu.make_async_copy(src_hbm.at[pl.ds(0, n_rows)], dst, sem).wait()


def _expert_body(blk_exp_ref, n_used_ref, xs_ref, wg_ref, wu_ref, wd_ref, y_ref, wgu_scr, wdn_scr):
    i = pl.program_id(0)
    d = xs_ref.shape[1]
    used = i < n_used_ref[0]

    @pl.when(used & ((i == 0) | (blk_exp_ref[i] != blk_exp_ref[jnp.maximum(i - 1, 0)])))
    def _():
        def cast_up(c, carry):
            rows = pl.ds(pl.multiple_of(c * CAST_ROWS, CAST_ROWS), CAST_ROWS)
            wgu_scr[rows, :EXPERT_FF] = wg_ref[0, 0, rows, :].astype(BF16)
            wgu_scr[rows, EXPERT_FF:] = wu_ref[0, 0, rows, :].astype(BF16)
            return carry

        def cast_down(c, carry):
            rows = pl.ds(pl.multiple_of(c * CAST_ROWS, CAST_ROWS), CAST_ROWS)
            wdn_scr[rows, :] = wd_ref[0, 0, rows, :].astype(BF16)
            return carry

        lax.fori_loop(0, d // CAST_ROWS, cast_up, 0)
        lax.fori_loop(0, EXPERT_FF // CAST_ROWS, cast_down, 0)

    @pl.when(used)
    def _():
        a = _dot(xs_ref[...].astype(BF16), wgu_scr[...])
        hid = (_silu(a[:, :EXPERT_FF]) * a[:, EXPERT_FF:]).astype(BF16)
        y_ref[...] = _dot(hid, wdn_scr[...])

    @pl.when(jnp.logical_not(used))
    def _():
        y_ref[...] = jnp.zeros_like(y_ref)


def _experts(xs, w_gate, w_up, w_down, layer, blk_exp, n_used):
    n_rows, d = xs.shape
    n_blocks = n_rows // MOE_BLOCK
    grid_spec = pltpu.PrefetchScalarGridSpec(
        num_scalar_prefetch=2,
        grid=(n_blocks,),
        in_specs=[
            pl.BlockSpec((MOE_BLOCK, d), lambda i, be, nu: (i, 0)),
            pl.BlockSpec((1, 1, d, EXPERT_FF), lambda i, be, nu: (layer, be[i], 0, 0)),
            pl.BlockSpec((1, 1, d, EXPERT_FF), lambda i, be, nu: (layer, be[i], 0, 0)),
            pl.BlockSpec((1, 1, EXPERT_FF, d), lambda i, be, nu: (layer, be[i], 0, 0)),
        ],
        out_specs=pl.BlockSpec((MOE_BLOCK, d), lambda i, be, nu: (i, 0)),
        scratch_shapes=[pltpu.VMEM((d, 2 * EXPERT_FF), BF16),
                        pltpu.VMEM((EXPERT_FF, d), BF16)],
    )
    return pl.pallas_call(
        _expert_body,
        out_shape=jax.ShapeDtypeStruct((n_rows, d), F32),
        grid_spec=grid_spec,
        compiler_params=_cparams(("arbitrary",)),
        name="experts",
    )(blk_exp, n_used, xs, w_gate, w_up, w_down)


def _combine_body(dest_ref, y_hbm, x_ref, gate_ref, meta_ref, fn_ref, o_ref, ya_scr, sem,
                  *, final, tile0):
    i = pl.program_id(0)
    n_steps = pl.num_programs(0)
    tm = x_ref.shape[0]
    slot = i % 2

    def start(step, s, straight_line):
        for j in range(2):
            _start_row_gather(dest_ref, (tile0 + step) * tm * 2 + j, tm, 2, y_hbm,
                              ya_scr.at[s, j], sem.at[s, j], straight_line)

    @pl.when(i == 0)
    def _():
        start(0, 0, False)

    for j in range(2):
        _wait_row_gather(y_hbm, ya_scr.at[slot, j], sem.at[slot, j], tm)

    @pl.when(i + 1 < n_steps)
    def _():
        start(i + 1, 1 - slot, True)

    meta = meta_ref[...]
    moe = (ya_scr[slot, 0] * meta[:, META_G:META_G + 1]
           + ya_scr[slot, 1] * meta[:, META_G + 1:META_G + 2])
    out = x_ref[...] + gate_ref[0] * moe
    if final:
        out = _rms(out, fn_ref[...])
    o_ref[...] = out


def _combine(y, x, gate, meta, dest, final_norm, seq, final, tok0=0, n_tok=None, tm=256):
    t, d = x.shape
    n_tok = t if n_tok is None else n_tok
    tm = min(tm, seq)
    tps = seq // tm
    assert tok0 % seq == 0 and n_tok % seq == 0
    tile0 = tok0 // tm
    grid_spec = pltpu.PrefetchScalarGridSpec(
        num_scalar_prefetch=1,
        grid=(n_tok // tm,),
        in_specs=[
            pl.BlockSpec(memory_space=pl.ANY),
            pl.BlockSpec((tm, d), lambda i, ds: (tile0 + i, 0)),
            pl.BlockSpec((1, 1, d), lambda i, ds: ((tile0 + i) // tps, 0, 0)),
            pl.BlockSpec((tm, LANES), lambda i, ds: (tile0 + i, 0)),
            pl.BlockSpec((1, d), lambda i, ds: (0, 0)),
        ],
        out_specs=pl.BlockSpec((tm, d), lambda i, ds: (i, 0)),
        scratch_shapes=[pltpu.VMEM((2, 2, tm, d), F32), pltpu.SemaphoreType.DMA((2, 2))],
    )
    return pl.pallas_call(
        functools.partial(_combine_body, final=final, tile0=tile0),
        out_shape=jax.ShapeDtypeStruct((n_tok, d), F32),
        grid_spec=grid_spec,
        compiler_params=_cparams(("arbitrary",)),
        name="moe_combine",
    )(dest, y, x, gate, meta, final_norm.reshape(1, d))


def _moe_layer(x, g, sc, sh, gate, grp_w, grp_b, exp_w, exp_b, w_gate, w_up, w_down, layer,
               final_norm, seq, final, split=None):
    t, d = x.shape
    pad = LANES - MOE_GROUPS - MOE_EXPERTS
    w_route = jnp.concatenate([grp_w, exp_w, jnp.zeros((d, pad), F32)], axis=1)
    b_route = jnp.concatenate([grp_b, exp_b, jnp.zeros((pad,), F32)]).reshape(1, LANES)
    meta, cnt = _router(x, g, sc, sh, w_route, b_route, seq)

    expert = meta[:, META_E:META_E + 2].astype(I32)
    rank = meta[:, META_R:META_R + 2].astype(I32)
    counts = cnt[0, ROUTE_E0:ROUTE_E0 + MOE_EXPERTS].astype(I32)
    padded = (counts + MOE_BLOCK - 1) // MOE_BLOCK * MOE_BLOCK
    p_ends = jnp.cumsum(padded)
    p_starts = p_ends - padded
    dest = (p_starts[expert] + rank).reshape(-1)
    n_rows = t * 2 + MOE_EXPERTS * MOE_BLOCK
    n_blocks = n_rows // MOE_BLOCK
    blk_start = jnp.arange(n_blocks, dtype=I32) * MOE_BLOCK
    blk_exp = jnp.minimum(jnp.sum((p_ends[None, :] <= blk_start[:, None]).astype(I32), axis=1),
                          MOE_EXPERTS - 1)
    n_used = p_ends[-1:] // MOE_BLOCK
    tail = (n_used + jnp.arange(MOE_EXPERTS, dtype=I32)) * MOE_BLOCK
    zero_starts = jnp.concatenate([jnp.where(padded > counts, p_ends - MOE_BLOCK, -1),
                                   jnp.where(tail < n_rows, tail, -1)]).astype(I32)

    xs = _dispatch(x, g, sc, sh, dest, zero_starts, n_rows, seq)
    y = _experts(xs, w_gate, w_up, w_down, layer, blk_exp, n_used)
    if split is None:
        return _combine(y, x, gate, meta, dest, final_norm, seq, final)
    return tuple(_combine(y, x, gate, meta, dest, final_norm, seq, final, tok0=a, n_tok=b - a)
                 for a, b in ((0, split), (split, t)))


def _rope_tables(seq):
    half = RET_DK // 2
    inv = ROPE_BASE ** (-jnp.arange(half, dtype=F32) / half)
    ang = jnp.arange(seq, dtype=F32)[:, None] * inv[None, :]
    cos, sin = jnp.cos(ang), jnp.sin(ang)
    return jnp.concatenate([cos, cos], axis=1), jnp.concatenate([-sin, sin], axis=1)


def _pad_rows(a, axis, n):
    pad = [(0, 0)] * a.ndim
    pad[axis] = (0, n - a.shape[axis])
    return jnp.pad(a, pad)


def kernel(x_prompt, x_sample, c_prompt, c_sample, ada_w, ada_b, norm1, norm2, ev_w_in, ev_gate_b, ev_conv_w, ev_conv_b, ev_ret_gn, ev_mlstm_gn, ev_w_out, od_w_in, od_conv_w, od_conv_b, od_dt_bias, od_a_log, od_d_skip, od_norm, od_w_out, moe_grp_w, moe_grp_b, moe_exp_w, moe_exp_b, moe_w_gate, moe_w_up, moe_w_down, final_norm):
    n_prompt = x_prompt.shape[0]
    seq, d = x_prompt.shape[1], x_prompt.shape[2]
    assert x_sample.shape[1] == seq and d == D_MODEL
    assert seq % RET_CHUNK == 0 and seq // MLSTM_CHUNK <= LANES // SSD_HPG
    nb = n_prompt + x_sample.shape[0]
    t = nb * seq
    x = x_prompt.reshape(n_prompt * seq, d)
    x_tail = x_sample.reshape(t - n_prompt * seq, d)
    depth = ada_w.shape[0]

    c_all = jnp.concatenate([c_prompt, c_sample], axis=0)
    c_pad = _pad_rows(c_all, 0, -(-nb // 8) * 8)
    mod = _modulation(c_pad, ada_w, ada_b)[:, :nb].reshape(depth, nb, N_MOD, 1, d)

    heads = jnp.arange(RET_HEADS, dtype=F32)
    lg = jnp.stack([jnp.log1p(-jnp.exp2(-RET_DECAY_FWD - heads)),
                    jnp.log1p(-jnp.exp2(-RET_DECAY_BWD - heads))])
    cos_t, sin_t = _rope_tables(seq)

    for i in range(depth):
        sh1, sc1, g1, sh2, sc2, g2 = (mod[i, :, m] for m in range(N_MOD))
        j = i // 2
        if i % 2 == 0:
            w_in = ev_w_in[j]
            w_side = w_in[:, EVEN_MAIN:].reshape(d, 4, MLSTM_HEADS).transpose(0, 2, 1)
            w_side = _pad_rows(_pad_rows(w_side, 2, GATE_ROWS).reshape(d, -1), 1, LANES)
            proj, gates = _fused_matmul(x, w_in[:, :EVEN_MAIN].astype(BF16), seq=seq, x_tail=x_tail,
                                        prologue="normmod", g=norm1[i], sc=sc1, sh=sh1, w_side=w_side,
                                        name="even_in_proj")
            proj = proj.reshape(nb, seq, EVEN_MAIN)
            ret = _retention(proj, lg, cos_t, sin_t, ev_ret_gn[j], nb, seq)
            ml = _mlstm(proj, gates, ev_gate_b[j], ev_conv_w[j], ev_conv_b[j], ev_mlstm_gn[j], nb, seq)
            x = _fused_matmul(ret.reshape(t, RET_V), ev_w_out[j].astype(BF16), seq=seq,
                              x2=ml.reshape(t, MLSTM_V),
                              res=x, res_tail=x_tail, gate=g1, tn=512, name="even_out_proj")
        else:
            w_in = od_w_in[j]
            proj, dt_raw = _fused_matmul(x, w_in[:, :ODD_MAIN].astype(BF16), seq=seq, x_tail=x_tail,
                                         prologue="normmod", g=norm1[i], sc=sc1, sh=sh1,
                                         w_side=w_in[:, ODD_MAIN:], name="odd_in_proj")
            def per_row(p):
                p = p.reshape(2, SSD_GROUPS, SSD_HPG).transpose(1, 0, 2)
                return jnp.tile(p, (1, 1, LANES // SSD_HPG))[..., None]

            def per_lane(p):
                p = p.reshape(*p.shape[:-1], SSD_GROUPS, SSD_HPG)
                p = jnp.moveaxis(p, -2, 0)
                return jnp.repeat(p, SSD_HEADDIM, axis=-1)[..., None, :]

            y = _ssd(proj.reshape(nb, seq, ODD_MAIN), dt_raw, per_row(od_dt_bias[j]), per_row(od_a_log[j]),
                     per_lane(od_d_skip[j]), od_conv_w[j], od_conv_b[j], nb, seq)
            x = _fused_matmul(y.reshape(t, SSD_INNER), od_w_out[j].astype(BF16), seq=seq, prologue="norm",
                              g=od_norm[j], res=x, res_tail=x_tail, gate=g1, tn=512, name="odd_out_proj")
        x_tail = None
        last = i == depth - 1
        x = _moe_layer(x, norm2[i], sc2, sh2, g2, moe_grp_w[i], moe_grp_b[i], moe_exp_w[i], moe_exp_b[i],
                       moe_w_gate, moe_w_up, moe_w_down, i, final_norm, seq, final=last,
                       split=n_prompt * seq if last else None)
    y_prompt, y_sample = x
    return (y_prompt.reshape(n_prompt, seq, d), y_sample.reshape(nb - n_prompt, seq, d))
```

```python
import functools
import math

import jax
import jax.numpy as jnp
import numpy as np
from jax import lax
from jax.experimental import pallas as pl
from jax.experimental.pallas import tpu as pltpu

F32 = jnp.float32
BF16 = jnp.bfloat16
I32 = jnp.int32

D_MODEL = 2048
N_MOD = 6
EPS = 1e-6
CONV_W = 5
CONV_HALO = 8

RET_HEADS = 8
RET_DV = D_MODEL // RET_HEADS
RET_DK = RET_DV // 2
RET_DECAY_FWD = 5.0
RET_DECAY_BWD = 5.5
ROPE_BASE = 10000.0
RET_CHUNK = 256
MLSTM_HEADS = 4
MLSTM_DV = D_MODEL // MLSTM_HEADS
MLSTM_DK = MLSTM_DV // 2
MLSTM_CHUNK = 128
GATE_ROWS = 8
SSD_INNER = 2 * D_MODEL
SSD_HEADDIM = 64
SSD_HEADS = SSD_INNER // SSD_HEADDIM
SSD_GROUPS = 8
SSD_HPG = SSD_HEADS // SSD_GROUPS
SSD_STATE = 128
SSD_CHUNK = 128
MOE_GROUPS = 4
MOE_EPG = 8
MOE_EXPERTS = MOE_GROUPS * MOE_EPG
EXPERT_FF = D_MODEL // 4
MOE_BLOCK = 256

RET_QK = RET_HEADS * RET_DK
RET_V = RET_HEADS * RET_DV
MLSTM_QK = MLSTM_HEADS * MLSTM_DK
MLSTM_V = MLSTM_HEADS * MLSTM_DV
MLSTM_NGATE = 4 * MLSTM_HEADS
EVEN_MAIN = 2 * RET_QK + 2 * RET_V + 2 * MLSTM_QK + 2 * MLSTM_V
EVEN_MIX = RET_V + MLSTM_V
SSD_BC = SSD_GROUPS * SSD_STATE
SSD_CONV_CH = SSD_INNER + 2 * SSD_BC
ODD_MAIN = SSD_INNER + SSD_CONV_CH

PROLOGUE_ROWS = 256
CAST_ROWS = 256
GATHER_UNROLL = 32
SCAN_UNROLL = 2
LANES = 128
VMEM_LIMIT = 56 * 1024 * 1024

NEG_INF = float("-inf")


def _cparams(sem, vmem=VMEM_LIMIT):
    return pltpu.CompilerParams(dimension_semantics=sem, vmem_limit_bytes=vmem)


def _split3(x):
    hi = x.astype(BF16)
    r = x - hi.astype(F32)
    mid = r.astype(BF16)
    lo = (r - mid.astype(F32)).astype(BF16)
    return hi, mid, lo


def _dot(a, b):
    return jnp.dot(a, b, preferred_element_type=F32)


def _dot_nt(a, b):
    return lax.dot_general(a, b, (((1,), (1,)), ((), ())), preferred_element_type=F32)


def _dot_tn(a, b):
    return lax.dot_general(a, b, (((0,), (0,)), ((), ())), preferred_element_type=F32)


def _dot01_left(m01, x):
    hi, mid, lo = _split3(x)
    return _dot(m01, hi) + _dot(m01, mid) + _dot(m01, lo)


def _dot01_right(x, m01):
    hi, mid, lo = _split3(x)
    return _dot(hi, m01) + _dot(mid, m01) + _dot(lo, m01)


def _tri(n, kind):
    r = lax.broadcasted_iota(I32, (n, n), 0)
    c = lax.broadcasted_iota(I32, (n, n), 1)
    m = {"le": r <= c, "ge": r >= c, "gt": r > c}[kind]
    return jnp.where(m, 1.0, 0.0).astype(BF16)


def _shr(x, pow2):
    return lax.shift_right_arithmetic(x, jnp.int32(int(math.log2(pow2))))


def _sigmoid(x):
    return 1.0 / (1.0 + jnp.exp(-x))


def _silu(x):
    return x * _sigmoid(x)


def _softplus(x):
    return jnp.maximum(x, 0.0) + jnp.log1p(jnp.exp(-jnp.abs(x)))


def _log_sigmoid(x):
    return jnp.minimum(x, 0.0) - jnp.log1p(jnp.exp(-jnp.abs(x)))


def _rms(x, g):
    ms = jnp.mean(x * x, axis=-1, keepdims=True)
    return x * lax.rsqrt(ms + EPS) * g


def _head_norm(y, g):
    mu = jnp.mean(y, axis=-1, keepdims=True)
    yc = y - mu
    var = jnp.mean(yc * yc, axis=-1, keepdims=True)
    return yc * lax.rsqrt(var + EPS) * g


def _mod_body(c_ref, w_ref, b_ref, o_ref):
    c = c_ref[...]
    o_ref[0] = _dot(_silu(c).astype(BF16), w_ref[0].astype(BF16)) + b_ref[0]


def _modulation(c_pad, ada_w, ada_b):
    depth, d, n = ada_w.shape
    m = c_pad.shape[0]
    tn = 1024
    return pl.pallas_call(
        _mod_body,
        out_shape=jax.ShapeDtypeStruct((depth, m, n), F32),
        grid=(depth, n // tn),
        in_specs=[
            pl.BlockSpec((m, d), lambda l, j: (0, 0)),
            pl.BlockSpec((1, d, tn), lambda l, j: (l, 0, j)),
            pl.BlockSpec((1, 1, tn), lambda l, j: (l, 0, j)),
        ],
        out_specs=pl.BlockSpec((1, m, tn), lambda l, j: (l, 0, j)),
        compiler_params=_cparams(("parallel", "parallel")),
        name="modulation",
    )(c_pad, ada_w, ada_b.reshape(depth, 1, n))


def _mm_body(*refs, prologue, epilogue, side, two_lhs, n_head, split, res_split):
    it = iter(refs)
    x_ref = next(it)
    xt_ref = next(it) if split else None
    x2_ref = next(it) if two_lhs else None
    g_ref = next(it) if prologue != "none" else None
    sc_ref = next(it) if prologue == "normmod" else None
    sh_ref = next(it) if prologue == "normmod" else None
    w_ref = next(it)
    ws_ref = next(it) if side else None
    res_ref = next(it) if epilogue == "residual" else None
    rest_ref = next(it) if res_split else None
    gate_ref = next(it) if epilogue == "residual" else None
    o_ref = next(it)
    os_ref = next(it) if side else None
    h_scr = next(it) if prologue != "none" else None

    def in_head():
        return pl.program_id(0) < n_head

    if prologue != "none":
        def run_prologue(src_ref):
            rows_per = PROLOGUE_ROWS

            def chunk(i, carry):
                rows = pl.ds(pl.multiple_of(i * rows_per, rows_per), rows_per)
                y = _rms(src_ref[rows, :].astype(F32), g_ref[...])
                if prologue == "normmod":
                    y = y * (1.0 + sc_ref[0]) + sh_ref[0]
                hb = y.astype(BF16)
                h_scr[rows, :] = hb
                if side:
                    h_lo = (y - hb.astype(F32)).astype(BF16)
                    w_hi = ws_ref[0]
                    w_lo = ws_ref[1]
                    os_ref[0, :, rows] = (_dot(hb, w_hi) + _dot(h_lo, w_hi) + _dot(hb, w_lo)).T
                return carry

            lax.fori_loop(0, src_ref.shape[0] // rows_per, chunk, 0)

        first_col = pl.program_id(1) == 0
        if split:
            pl.when(first_col & in_head())(lambda: run_prologue(x_ref))
            pl.when(first_col & jnp.logical_not(in_head()))(lambda: run_prologue(xt_ref))
        else:
            pl.when(first_col)(lambda: run_prologue(x_ref))
        lhs = h_scr[...]
    else:
        assert not split
        lhs = x_ref[...]
    if two_lhs:
        k1 = x_ref.shape[1]
        acc = _dot(lhs, w_ref[:k1, :]) + _dot(x2_ref[...], w_ref[k1:, :])
    else:
        acc = _dot(lhs, w_ref[...])
    if epilogue == "residual" and rest_ref is not None:
        @pl.when(in_head())
        def _():
            o_ref[...] = res_ref[...] + gate_ref[0] * acc

        @pl.when(jnp.logical_not(in_head()))
        def _():
            o_ref[...] = rest_ref[...] + gate_ref[0] * acc
    elif epilogue == "residual":
        o_ref[...] = res_ref[...] + gate_ref[0] * acc
    else:
        o_ref[...] = acc.astype(o_ref.dtype)


def _fused_matmul(x, w, *, seq, x_tail=None, x2=None, prologue="none", g=None, sc=None, sh=None,
                  w_side=None, res=None, res_tail=None, gate=None, out_dtype=BF16, tm=1024, tn=1024,
                  name="proj"):
    t, k = x.shape
    n = w.shape[1]
    tm = min(tm, seq)
    tn = min(tn, n)
    split = x_tail is not None
    res_split = res_tail is not None
    n_head = (x.shape[0] if split else res.shape[0] if res_split else t) // tm
    if split:
        t = t + x_tail.shape[0]
    assert t % tm == 0 and seq % tm == 0 and n % tn == 0
    tps = seq // tm
    epilogue = "residual" if res is not None else "plain"
    side = w_side is not None
    two_lhs = x2 is not None
    assert not (two_lhs and prologue != "none")

    def head_rows(i):
        return jnp.minimum(i, n_head - 1)

    def tail_rows(i):
        return jnp.maximum(i - n_head, 0)

    if split:
        in_specs = [pl.BlockSpec((tm, k), lambda i, j: (head_rows(i), 0)),
                    pl.BlockSpec((tm, k), lambda i, j: (tail_rows(i), 0))]
        args = [x, x_tail]
    else:
        in_specs = [pl.BlockSpec((tm, k), lambda i, j: (i, 0))]
        args = [x]
    if two_lhs:
        in_specs.append(pl.BlockSpec((tm, x2.shape[1]), lambda i, j: (i, 0)))
        args.append(x2)
        k = k + x2.shape[1]
    if prologue != "none":
        in_specs.append(pl.BlockSpec((1, k), lambda i, j: (0, 0)))
        args.append(g.reshape(1, k))
    if prologue == "normmod":
        in_specs += [pl.BlockSpec((1, 1, k), lambda i, j: (i // tps, 0, 0))] * 2
        args += [sc, sh]
    in_specs.append(pl.BlockSpec((k, tn), lambda i, j: (0, j)))
    args.append(w)
    if side:
        ws_hi = w_side.astype(BF16)
        ws_lo = (w_side - ws_hi.astype(F32)).astype(BF16)
        in_specs.append(pl.BlockSpec((2, k, LANES), lambda i, j: (0, 0, 0)))
        args.append(jnp.stack([ws_hi, ws_lo]))
    if epilogue == "residual":
        if res_split:
            in_specs += [pl.BlockSpec((tm, tn), lambda i, j: (head_rows(i), jnp.where(i < n_head, j, 0))),
                         pl.BlockSpec((tm, tn), lambda i, j: (tail_rows(i), jnp.where(i < n_head, 0, j)))]
            args += [res, res_tail]
        else:
            in_specs.append(pl.BlockSpec((tm, tn), lambda i, j: (i, j)))
            args.append(res)
        in_specs.append(pl.BlockSpec((1, 1, tn), lambda i, j: (i // tps, 0, j)))
        args.append(gate)
        out_dtype = F32
    out_shape = [jax.ShapeDtypeStruct((t, n), out_dtype)]
    out_specs = [pl.BlockSpec((tm, tn), lambda i, j: (i, j))]
    if side:
        out_shape.append(jax.ShapeDtypeStruct((t // seq, LANES, seq), F32))
        out_specs.append(pl.BlockSpec((1, LANES, tm), lambda i, j: (i // tps, 0, i % tps)))
    scratch = [pltpu.VMEM((tm, k), BF16)] if prologue != "none" else []
    outs = pl.pallas_call(
        functools.partial(_mm_body, prologue=prologue, epilogue=epilogue, side=side, two_lhs=two_lhs,
                          n_head=n_head, split=split, res_split=res_split),
        out_shape=out_shape,
        grid=(t // tm, n // tn),
        in_specs=in_specs,
        out_specs=out_specs,
        scratch_shapes=scratch,
        compiler_params=_cparams(("parallel", "arbitrary")),
        name=name,
    )(*args)
    return outs if side else outs[0]


CONV_ROWS = 128


def _conv_fill(src_ref, pad_scr, seq):
    ch = pad_scr.shape[1]
    halo = CONV_HALO
    rows = CONV_ROWS
    zeros = jnp.zeros((halo, ch), F32)
    pad_scr[pl.ds(0, halo), :] = zeros
    pad_scr[pl.ds(seq + halo, halo), :] = zeros

    def fill(i, carry):
        r0 = pl.multiple_of(i * rows, rows)
        pad_scr[pl.ds(pl.multiple_of(r0 + halo, halo), rows), :] = src_ref[0, pl.ds(r0, rows), :].astype(F32)
        return carry

    lax.fori_loop(0, seq // rows, fill, 0)


def _conv_silu_rows(pad_scr, w_ref, b_ref, r0, emit):
    ch = pad_scr.shape[1]
    halo = CONV_HALO
    rows = CONV_ROWS
    win = rows + 2 * halo
    half = (CONV_W - 1) // 2
    for lane0 in range(0, ch, LANES):
        cols = slice(lane0, lane0 + LANES)
        window = pad_scr[pl.ds(r0, win), cols]
        acc = jnp.zeros((rows, LANES), F32) + b_ref[:, cols]
        for j in range(CONV_W):
            d = j - half
            shifted = window if d == 0 else pltpu.roll(window, (-d) % win, axis=0)
            acc = acc + w_ref[j:j + 1, cols] * shifted[halo:halo + rows, :]
        emit(lane0, _silu(acc))


def _two_ended_scan(n_chunks, prepare, step, finish):
    assert n_chunks % 2 == 0
    half = n_chunks // 2

    def first(i, carry):
        prepare(i)
        prepare(n_chunks - 1 - i)
        step(i)
        return carry

    def second(i, carry):
        step(i)
        finish(i)
        finish(n_chunks - 1 - i)
        return carry

    unroll = SCAN_UNROLL if half % SCAN_UNROLL == 0 else 1
    lax.fori_loop(0, half, first, 0, unroll=unroll)
    lax.fori_loop(half, n_chunks, second, 0, unroll=unroll)


def _ret_body(lg_ref, q_ref, k_ref, v_ref, g_ref, cos_ref, sin_ref, gn_ref, o_ref,
              qs_scr, ks_scr, acc_scr, st_scr, *, seq):
    c_len = RET_CHUNK
    n_chunks = seq // c_len
    h = pl.program_id(1)
    lgf = lg_ref[0, h]
    lgb = lg_ref[1, h]
    ri = lax.broadcasted_iota(I32, (c_len, c_len), 0)
    ci = lax.broadcasted_iota(I32, (c_len, c_len), 1)
    diff = (ri - ci).astype(F32)
    dmat = jnp.exp(jnp.where(diff >= 0, lgf * diff, -lgb * diff))
    pos = lax.broadcasted_iota(I32, (c_len, 1), 0).astype(F32)
    qdec_f = jnp.exp(lgf * (pos + 1.0))
    kdec_f = jnp.exp(lgf * (c_len - 1.0 - pos))
    cdec_f = jnp.exp(jnp.full((1, 1), c_len, F32) * lgf)
    qdec_b = jnp.exp(lgb * (c_len - pos))
    kdec_b = jnp.exp(lgb * pos)
    cdec_b = jnp.exp(jnp.full((1, 1), c_len, F32) * lgb)
    half = RET_DK // 2

    def rope(x, rows):
        return x * cos_ref[rows, :] + pltpu.roll(x, half, axis=1) * sin_ref[rows, :]

    def chunk_rows(c):
        return pl.ds(pl.multiple_of(c * c_len, c_len), c_len)

    def prepare(c):
        rows = chunk_rows(c)
        qs_scr[rows, :] = rope(q_ref[0, rows, :].astype(F32), rows).astype(BF16)
        ks_scr[rows, :] = rope(k_ref[0, rows, :].astype(F32), rows) * (RET_DK ** -0.5)

    st_scr[...] = jnp.zeros_like(st_scr)

    def step(i):
        rows = chunk_rows(i)
        qb = qs_scr[rows, :]
        k = ks_scr[rows, :]
        v = v_ref[0, rows, :]
        p = (_dot_nt(qb, k.astype(BF16)) * dmat).astype(BF16)
        acc_scr[0, rows, :] = _dot(p, v) + qdec_f * _dot(qb, st_scr[0].astype(BF16))
        st_scr[0] = cdec_f * st_scr[0] + _dot_tn((k * kdec_f).astype(BF16), v)

        rows = chunk_rows(n_chunks - 1 - i)
        k = ks_scr[rows, :]
        acc_scr[1, rows, :] = qdec_b * _dot(qs_scr[rows, :], st_scr[1].astype(BF16))
        st_scr[1] = cdec_b * st_scr[1] + _dot_tn((k * kdec_b).astype(BF16), v_ref[0, rows, :])

    def finish(c):
        rows = chunk_rows(c)
        o = acc_scr[0, rows, :] + acc_scr[1, rows, :]
        gate = g_ref[0, rows, :].astype(F32)
        o_ref[0, rows, :] = (_head_norm(o, gn_ref[0]) * _silu(gate)).astype(o_ref.dtype)

    _two_ended_scan(n_chunks, prepare, step, finish)


def _retention(proj, lg, cos_t, sin_t, ret_gn, nb, seq):
    kq, kv = RET_DK, RET_DV
    grid_spec = dict(
        grid=(nb, RET_HEADS),
        in_specs=[
            pl.BlockSpec(memory_space=pltpu.SMEM),
            pl.BlockSpec((1, seq, kq), lambda b, h: (b, 0, h)),
            pl.BlockSpec((1, seq, kq), lambda b, h: (b, 0, RET_QK // kq + h)),
            pl.BlockSpec((1, seq, kv), lambda b, h: (b, 0, 2 * RET_QK // kv + h)),
            pl.BlockSpec((1, seq, kv), lambda b, h: (b, 0, (2 * RET_QK + RET_V) // kv + h)),
            pl.BlockSpec((seq, kq), lambda b, h: (0, 0)),
            pl.BlockSpec((seq, kq), lambda b, h: (0, 0)),
            pl.BlockSpec((1, 1, kv), lambda b, h: (h, 0, 0)),
        ],
        out_specs=pl.BlockSpec((1, seq, kv), lambda b, h: (b, 0, h)),
        scratch_shapes=[
            pltpu.VMEM((seq, kq), BF16),
            pltpu.VMEM((seq, kq), F32),
            pltpu.VMEM((2, seq, kv), F32),
            pltpu.VMEM((2, kq, kv), F32),
        ],
    )
    return pl.pallas_call(
        functools.partial(_ret_body, seq=seq),
        out_shape=jax.ShapeDtypeStruct((nb, seq, RET_V), BF16),
        **grid_spec,
        compiler_params=_cparams(("parallel", "parallel")),
        name="retention",
    )(lg, proj, proj, proj, proj, cos_t, sin_t, ret_gn.reshape(RET_HEADS, 1, kv))


def _col_of(mat, c):
    lane = lax.broadcasted_iota(I32, mat.shape, 1)
    return jnp.sum(jnp.where(lane == c, mat, 0.0), axis=1, keepdims=True)


def _mlstm_body(gb_ref, q_ref, k_ref, v_ref, o_gate_ref, wq_ref, wk_ref, bq_ref, bk_ref,
                gr_ref, gn_ref, o_ref,
                padq_scr, padk_scr, qs_scr, ks_scr, acc_scr, c_scr, n_scr, m_scr, row_scr, col_scr,
                *, seq):
    ln = MLSTM_CHUNK
    assert ln == CONV_ROWS
    n_chunks = seq // ln
    h = pl.program_id(1)

    _conv_fill(q_ref, padq_scr, seq)
    _conv_fill(k_ref, padk_scr, seq)

    def prepare(c):
        r0 = pl.multiple_of(c * ln, ln)
        rows = pl.ds(r0, ln)

        def emit_q(lane0, y):
            qs_scr[rows, lane0:lane0 + LANES] = (y * (MLSTM_DK ** -0.5)).astype(BF16)

        def emit_k(lane0, y):
            ks_scr[rows, lane0:lane0 + LANES] = y

        _conv_silu_rows(padq_scr, wq_ref, bq_ref, r0, emit_q)
        _conv_silu_rows(padk_scr, wk_ref, bk_ref, r0, emit_k)

    row_scr[...] = jnp.zeros_like(row_scr)
    for t in range(4):
        for c in range(n_chunks):
            row_scr[t, c:c + 1, :] = gr_ref[0, t:t + 1, c * ln:(c + 1) * ln]
    tri_le = _tri(ln, "le")
    tri_ge = _tri(ln, "ge")
    for d in range(2):
        ig = row_scr[2 * d] + gb_ref[(2 * d) * MLSTM_HEADS + h]
        lf = _log_sigmoid(row_scr[2 * d + 1] + gb_ref[(2 * d + 1) * MLSTM_HEADS + h])
        bc = _dot01_right(lf, tri_le if d == 0 else tri_ge)
        row_scr[2 * d] = ig
        row_scr[2 * d + 1] = bc
        col_scr[2 * d] = ig.T
        col_scr[2 * d + 1] = bc.T

    ri = lax.broadcasted_iota(I32, (ln, ln), 0)
    ci = lax.broadcasted_iota(I32, (ln, ln), 1)

    def chunk_step(c, d):
        r0 = pl.multiple_of(c * ln, ln)
        rows = pl.ds(r0, ln)
        qb = qs_scr[rows, :]
        kf = ks_scr[rows, :]
        v = v_ref[0, rows, :]
        i_row = row_scr[2 * d, pl.ds(c, 1), :]
        b_row = row_scr[2 * d + 1, pl.ds(c, 1), :]
        i_col = _col_of(col_scr[2 * d], c)
        b_col = _col_of(col_scr[2 * d + 1], c)
        m_st = m_scr[d]
        mask = (ri >= ci) if d == 0 else (ri <= ci)
        logd = jnp.where(mask, b_col - b_row + i_row, NEG_INF)
        m_inter = b_col + m_st
        m_row = jnp.maximum(m_inter, jnp.max(logd, axis=1, keepdims=True))
        sc = _dot_nt(qb, kf.astype(BF16)) * jnp.exp(logd - m_row)
        inter = jnp.exp(m_inter - m_row)
        num = _dot(sc.astype(BF16), v) + inter * _dot(qb, c_scr[d].astype(BF16))
        den = jnp.sum(sc, axis=1, keepdims=True) + inter * jnp.sum(
            qb.astype(F32) * n_scr[d], axis=1, keepdims=True)
        hh = num / jnp.maximum(jnp.abs(den), jnp.exp(-m_row))
        b_end = b_row[:, ln - 1:ln] if d == 0 else b_row[:, 0:1]
        logw = b_end - b_col + i_col
        m_new = jnp.maximum(b_end + m_st, jnp.max(logw, axis=0, keepdims=True))
        kw = kf * jnp.exp(logw - m_new)
        dec = jnp.exp(b_end + m_st - m_new)
        c_scr[d] = dec * c_scr[d] + _dot_tn(kw.astype(BF16), v)
        n_scr[d] = dec * n_scr[d] + jnp.sum(kw, axis=0, keepdims=True)
        m_scr[d] = m_new
        acc_scr[d, rows, :] = hh

    c_scr[...] = jnp.zeros_like(c_scr)
    n_scr[...] = jnp.zeros_like(n_scr)
    m_scr[...] = jnp.zeros_like(m_scr)

    def step(i):
        chunk_step(i, 0)
        chunk_step(n_chunks - 1 - i, 1)

    def finish(c):
        rows = pl.ds(pl.multiple_of(c * ln, ln), ln)
        y = _head_norm(acc_scr[0, rows, :] + acc_scr[1, rows, :], gn_ref[0])
        o_ref[0, rows, :] = (y * _sigmoid(o_gate_ref[0, rows, :].astype(F32))).astype(o_ref.dtype)

    _two_ended_scan(n_chunks, prepare, step, finish)


def _mlstm(proj, gate_rows, gate_b, conv_w, conv_b, mlstm_gn, nb, seq):
    kq, kv = MLSTM_DK, MLSTM_DV
    q0 = 2 * RET_QK + 2 * RET_V
    k0 = q0 + MLSTM_QK
    v0 = k0 + MLSTM_QK
    o0 = v0 + MLSTM_V
    grid_spec = dict(
        grid=(nb, MLSTM_HEADS),
        in_specs=[
            pl.BlockSpec(memory_space=pltpu.SMEM),
            pl.BlockSpec((1, seq, kq), lambda b, h: (b, 0, q0 // kq + h)),
            pl.BlockSpec((1, seq, kq), lambda b, h: (b, 0, k0 // kq + h)),
            pl.BlockSpec((1, seq, kv), lambda b, h: (b, 0, v0 // kv + h)),
            pl.BlockSpec((1, seq, kv), lambda b, h: (b, 0, o0 // kv + h)),
            pl.BlockSpec((CONV_W, kq), lambda b, h: (0, h)),
            pl.BlockSpec((CONV_W, kq), lambda b, h: (0, MLSTM_QK // kq + h)),
            pl.BlockSpec((1, kq), lambda b, h: (0, h)),
            pl.BlockSpec((1, kq), lambda b, h: (0, MLSTM_QK // kq + h)),
            pl.BlockSpec((1, GATE_ROWS, seq), lambda b, h: (b, h, 0)),
            pl.BlockSpec((1, 1, kv), lambda b, h: (h, 0, 0)),
        ],
        out_specs=pl.BlockSpec((1, seq, kv), lambda b, h: (b, 0, h)),
        scratch_shapes=[
            pltpu.VMEM((seq + 2 * CONV_HALO, kq), F32),
            pltpu.VMEM((seq + 2 * CONV_HALO, kq), F32),
            pltpu.VMEM((seq, kq), BF16),
            pltpu.VMEM((seq, kq), F32),
            pltpu.VMEM((2, seq, kv), F32),
            pltpu.VMEM((2, kq, kv), F32),
            pltpu.VMEM((2, 1, kq), F32),
            pltpu.VMEM((2, 1, 1), F32),
            pltpu.VMEM((4, LANES, MLSTM_CHUNK), F32),
            pltpu.VMEM((4, MLSTM_CHUNK, LANES), F32),
        ],
    )
    return pl.pallas_call(
        functools.partial(_mlstm_body, seq=seq),
        out_shape=jax.ShapeDtypeStruct((nb, seq, MLSTM_V), BF16),
        **grid_spec,
        compiler_params=_cparams(("parallel", "parallel")),
        name="mlstm",
    )(gate_b, proj, proj, proj, proj, conv_w, conv_w, conv_b.reshape(1, -1), conv_b.reshape(1, -1),
      gate_rows, mlstm_gn.reshape(MLSTM_HEADS, 1, kv))


def _ssd_body(z_ref, x_ref, b_ref, c_ref, wx_ref, wb_ref, wc_ref, bx_ref, bb_ref, bc_ref,
              dtf_ref, dtb_ref, bias_ref, alog_ref, dskip_ref, o_ref,
              padx_scr, padb_scr, padc_scr, xs_scr, bs_scr, bst_scr, cs_scr, y_scr, st_scr,
              acr_scr, dtr_scr, er_scr, ur_scr, act_scr, *, seq):
    ln = SSD_CHUNK
    assert ln == CONV_ROWS
    n_chunks = seq // ln
    hp = SSD_HEADDIM
    n_pairs = SSD_HPG // 2

    _conv_fill(x_ref, padx_scr, seq)
    _conv_fill(b_ref, padb_scr, seq)
    _conv_fill(c_ref, padc_scr, seq)

    def prepare(c):
        r0 = pl.multiple_of(c * ln, ln)
        rows = pl.ds(r0, ln)

        def emit_x(lane0, y):
            xs_scr[rows, lane0:lane0 + LANES] = y

        def emit_b(lane0, y):
            bs_scr[rows, :] = y.astype(BF16)
            bst_scr[:, rows] = y.T.astype(BF16)

        def emit_c(lane0, y):
            cs_scr[rows, :] = y.astype(BF16)

        _conv_silu_rows(padx_scr, wx_ref, bx_ref, r0, emit_x)
        _conv_silu_rows(padb_scr, wb_ref, bb_ref, r0, emit_b)
        _conv_silu_rows(padc_scr, wc_ref, bc_ref, r0, emit_c)

    for d, dt_ref in enumerate((dtf_ref, dtb_ref)):
        acr_scr[d] = jnp.zeros((LANES, ln), F32)
        for c in range(n_chunks):
            acr_scr[d, c * SSD_HPG:(c + 1) * SSD_HPG, :] = dt_ref[0, :, c * ln:(c + 1) * ln]
        dt = _softplus(acr_scr[d] + bias_ref[0, d])
        adt = dt * (-jnp.exp(alog_ref[0, d]))
        acum = _dot01_right(adt, _tri(ln, "le" if d == 0 else "ge"))
        a_end = acum[:, ln - 1:ln] if d == 0 else acum[:, 0:1]
        acr_scr[d] = acum
        dtr_scr[d] = dt
        er_scr[d] = jnp.exp(acum)
        ur_scr[d] = dt * jnp.exp(a_end - acum)
        act_scr[d] = acum.T

    ri = lax.broadcasted_iota(I32, (ln, ln), 0)
    ci = lax.broadcasted_iota(I32, (ln, ln), 1)
    in_first = lax.broadcasted_iota(I32, (1, LANES), 1) < hp
    on_diag = ri == ci

    def chunk_rows(c):
        return pl.ds(pl.multiple_of(c * ln, ln), ln)

    def dir_step(d, c):
        rows = chunk_rows(c)
        xb = xs_scr[rows, :].astype(BF16)
        bcm = bs_scr[rows, :]
        bct = bst_scr[:, rows].astype(F32)
        ccm = cs_scr[rows, :]
        cb = _dot_nt(ccm, bcm)
        carried = _dot(ccm, st_scr[d].astype(BF16)).astype(BF16)
        head_rows = pl.ds(pl.multiple_of(c * SSD_HPG, SSD_HPG), SSD_HPG)
        arow = acr_scr[d, head_rows, :]
        dtrow = dtr_scr[d, head_rows, :]
        erow = er_scr[d, head_rows, :]
        urow = ur_scr[d, head_rows, :]
        acols = pltpu.roll(act_scr[d], (LANES - c * SSD_HPG) & (LANES - 1), axis=1)
        mask = (ri >= ci) if d == 0 else (ri <= ci)
        end = ln - 1 if d == 0 else 0
        pieces = []
        for pair in range(n_pairs):
            lanes = slice(pair * LANES, (pair + 1) * LANES)
            lhs, rhs, lhs_state, keep = [], [], [], []
            for sub in range(2):
                k = 2 * pair + sub
                sel = in_first if sub == 0 else ~in_first
                dec = jnp.exp(jnp.where(mask, acols[:, k:k + 1] - arow[k:k + 1, :], NEG_INF))
                lhs.append((cb * dec * dtrow[k:k + 1, :]).astype(BF16))
                rhs.append(jnp.where(sel, xb[:, lanes], jnp.zeros((ln, LANES), BF16)))
                lhs_state.append((bct * urow[k:k + 1, :]).astype(BF16))
                keep.append(erow[k:k + 1, end:end + 1])
            for sub in range(2):
                k = 2 * pair + sub
                sel = in_first if sub == 0 else ~in_first
                lhs.append(jnp.where(on_diag, erow[k:k + 1, :], 0.0).astype(BF16))
                rhs.append(jnp.where(sel, carried[:, lanes], jnp.zeros((ln, LANES), BF16)))
            pieces.append(_dot(jnp.concatenate(lhs, axis=1), jnp.concatenate(rhs, axis=0)))
            st_scr[d, :, lanes] = (jnp.where(in_first, keep[0], keep[1]) * st_scr[d, :, lanes]
                                   + _dot(jnp.concatenate(lhs_state, axis=1),
                                          jnp.concatenate(rhs[:2], axis=0)))
        y_scr[d, rows, :] = jnp.concatenate(pieces, axis=1)

    st_scr[...] = jnp.zeros_like(st_scr)

    def step(i):
        dir_step(0, i)
        dir_step(1, n_chunks - 1 - i)

    def finish(c):
        rows = chunk_rows(c)
        y = y_scr[0, rows, :] + y_scr[1, rows, :] + dskip_ref[0] * xs_scr[rows, :]
        o_ref[0, rows, :] = (y * _silu(z_ref[0, rows, :].astype(F32))).astype(o_ref.dtype)

    _two_ended_scan(n_chunks, prepare, step, finish)


def _ssd(proj, dt_t, bias_col, alog_col, dskip_x, conv_w, conv_b, nb, seq):
    width = SSD_HPG * SSD_HEADDIM
    ns = SSD_STATE
    x0 = SSD_INNER
    b0 = 2 * SSD_INNER
    c0 = b0 + SSD_BC
    cb = conv_b.reshape(1, -1)
    return pl.pallas_call(
        functools.partial(_ssd_body, seq=seq),
        out_shape=jax.ShapeDtypeStruct((nb, seq, SSD_INNER), BF16),
        grid=(nb, SSD_GROUPS),
        in_specs=[
            pl.BlockSpec((1, seq, width), lambda b, g: (b, 0, g)),
            pl.BlockSpec((1, seq, width), lambda b, g: (b, 0, x0 // width + g)),
            pl.BlockSpec((1, seq, ns), lambda b, g: (b, 0, b0 // ns + g)),
            pl.BlockSpec((1, seq, ns), lambda b, g: (b, 0, c0 // ns + g)),
            pl.BlockSpec((CONV_W, width), lambda b, g: (0, g)),
            pl.BlockSpec((CONV_W, ns), lambda b, g: (0, SSD_INNER // ns + g)),
            pl.BlockSpec((CONV_W, ns), lambda b, g: (0, (SSD_INNER + SSD_BC) // ns + g)),
            pl.BlockSpec((1, width), lambda b, g: (0, g)),
            pl.BlockSpec((1, ns), lambda b, g: (0, SSD_INNER // ns + g)),
            pl.BlockSpec((1, ns), lambda b, g: (0, (SSD_INNER + SSD_BC) // ns + g)),
            pl.BlockSpec((1, SSD_HPG, seq), lambda b, g: (b, g, 0)),
            pl.BlockSpec((1, SSD_HPG, seq), lambda b, g: (b, SSD_GROUPS + g, 0)),
            pl.BlockSpec((1, 2, LANES, 1), lambda b, g: (g, 0, 0, 0)),
            pl.BlockSpec((1, 2, LANES, 1), lambda b, g: (g, 0, 0, 0)),
            pl.BlockSpec((1, 1, width), lambda b, g: (g, 0, 0)),
        ],
        out_specs=pl.BlockSpec((1, seq, width), lambda b, g: (b, 0, g)),
        scratch_shapes=[
            pltpu.VMEM((seq + 2 * CONV_HALO, width), F32),
            pltpu.VMEM((seq + 2 * CONV_HALO, ns), F32),
            pltpu.VMEM((seq + 2 * CONV_HALO, ns), F32),
            pltpu.VMEM((seq, width), F32),
            pltpu.VMEM((seq, ns), BF16),
            pltpu.VMEM((ns, seq), BF16),
            pltpu.VMEM((seq, ns), BF16),
            pltpu.VMEM((2, seq, width), F32),
            pltpu.VMEM((2, ns, width), F32),
            pltpu.VMEM((2, LANES, SSD_CHUNK), F32),
            pltpu.VMEM((2, LANES, SSD_CHUNK), F32),
            pltpu.VMEM((2, LANES, SSD_CHUNK), F32),
            pltpu.VMEM((2, LANES, SSD_CHUNK), F32),
            pltpu.VMEM((2, SSD_CHUNK, LANES), F32),
        ],
        compiler_params=_cparams(("parallel", "parallel")),
        name="ssd",
    )(proj, proj, proj, proj, conv_w, conv_w, conv_w, cb, cb, cb,
      dt_t, dt_t, bias_col, alog_col, dskip_x)


META_E = 0
META_G = 2
META_R = 4
ROUTE_E0 = MOE_GROUPS


def _moe_input(x, g_ref, sc_ref, sh_ref):
    return _rms(x, g_ref[...]) * (1.0 + sc_ref[0]) + sh_ref[0]


def _router_body(x_ref, g_ref, sc_ref, sh_ref, w_ref, b_ref, meta_ref, cnt_ref,
                 carry_scr, whi_scr, wlo_scr):
    @pl.when(pl.program_id(0) == 0)
    def _():
        carry_scr[...] = jnp.zeros_like(carry_scr)
        w = w_ref[...]
        w_hi = w.astype(BF16)
        whi_scr[...] = w_hi
        wlo_scr[...] = (w - w_hi.astype(F32)).astype(BF16)

    meta_ref[...] = _route_rows(_moe_input(x_ref[...], g_ref, sc_ref, sh_ref),
                                whi_scr[...], wlo_scr[...], b_ref[...], carry_scr)
    cnt_ref[...] = jnp.broadcast_to(carry_scr[...], cnt_ref.shape)


def _route_rows(y, w_hi, w_lo, bias, carry_scr):
    tm = y.shape[0]
    h_hi = y.astype(BF16)
    h_lo = (y - h_hi.astype(F32)).astype(BF16)
    logits = _dot(h_hi, w_hi) + _dot(h_lo, w_hi) + _dot(h_hi, w_lo) + bias

    lane = lax.broadcasted_iota(I32, (tm, LANES), 1)
    lane_f = lane.astype(F32)
    big = float(LANES)
    is_grp = lane < MOE_GROUPS
    gl = jnp.where(is_grp, logits, NEG_INF)
    gmax = jnp.max(gl, axis=1, keepdims=True)
    gidx = jnp.min(jnp.where(gl == gmax, lane_f, big), axis=1, keepdims=True)
    gprob = 1.0 / jnp.sum(jnp.where(is_grp, jnp.exp(gl - gmax), 0.0), axis=1, keepdims=True)

    el = lane - ROUTE_E0
    el_f = el.astype(F32)
    valid = (el >= 0) & (el < MOE_EXPERTS)
    in_grp = valid & (_shr(el, MOE_EPG).astype(F32) == gidx)
    ev = jnp.where(in_grp, logits, NEG_INF)
    v1 = jnp.max(ev, axis=1, keepdims=True)
    i1 = jnp.min(jnp.where(ev == v1, el_f, big), axis=1, keepdims=True)
    ev2 = jnp.where(el_f == i1, NEG_INF, ev)
    v2 = jnp.max(ev2, axis=1, keepdims=True)
    i2 = jnp.min(jnp.where(ev2 == v2, el_f, big), axis=1, keepdims=True)
    p2 = jnp.exp(v2 - v1)
    s1 = 1.0 / (1.0 + p2)
    gate1 = s1 * gprob
    gate2 = p2 * s1 * gprob

    oh1 = jnp.where(el_f == i1, 1.0, 0.0)
    oh2 = jnp.where(el_f == i2, 1.0, 0.0)
    oh = oh1 + oh2
    before = _dot(_tri(tm, "gt"), oh.astype(BF16)) + carry_scr[...]
    rank1 = jnp.sum(oh1 * before, axis=1, keepdims=True)
    rank2 = jnp.sum(oh2 * before, axis=1, keepdims=True)
    carry_scr[...] = carry_scr[...] + jnp.sum(oh, axis=0, keepdims=True)

    meta = jnp.zeros((tm, LANES), F32)
    for col, val in ((META_E, i1), (META_E + 1, i2), (META_G, gate1), (META_G + 1, gate2),
                     (META_R, rank1), (META_R + 1, rank2)):
        meta = jnp.where(lane == col, val, meta)
    return meta


def _router(x, g, sc, sh, w_route, b_route, seq, tm=512):
    t, k = x.shape
    tm = min(tm, seq)
    tps = seq // tm
    return pl.pallas_call(
        _router_body,
        out_shape=[jax.ShapeDtypeStruct((t, LANES), F32),
                   jax.ShapeDtypeStruct((8, LANES), F32)],
        grid=(t // tm,),
        in_specs=[
            pl.BlockSpec((tm, k), lambda i: (i, 0)),
            pl.BlockSpec((1, k), lambda i: (0, 0)),
            pl.BlockSpec((1, 1, k), lambda i: (i // tps, 0, 0)),
            pl.BlockSpec((1, 1, k), lambda i: (i // tps, 0, 0)),
            pl.BlockSpec((k, LANES), lambda i: (0, 0)),
            pl.BlockSpec((1, LANES), lambda i: (0, 0)),
        ],
        out_specs=[pl.BlockSpec((tm, LANES), lambda i: (i, 0)),
                   pl.BlockSpec((8, LANES), lambda i: (0, 0))],
        scratch_shapes=[pltpu.VMEM((1, LANES), F32), pltpu.VMEM((k, LANES), BF16),
                        pltpu.VMEM((k, LANES), BF16)],
        compiler_params=_cparams(("arbitrary",)),
        name="router",
    )(x, g.reshape(1, k), sc, sh, w_route, b_route)


ZERO_BLOCKS = 2 * MOE_EXPERTS


def _dispatch_body(dest_ref, zero_ref, x_ref, g_ref, sc_ref, sh_ref, xs_hbm, h_scr, zero_scr, sem, zsem):
    i = pl.program_id(0)
    n_steps = pl.num_programs(0)
    tm = x_ref.shape[0]
    slot = i % 2

    def zero_copy(start):
        return pltpu.make_async_copy(zero_scr, xs_hbm.at[pl.ds(start, MOE_BLOCK)], zsem)

    @pl.when(i == 0)
    def _():
        zero_scr[...] = jnp.zeros_like(zero_scr)

        def issue(k, carry):
            @pl.when(zero_ref[k] >= 0)
            def _():
                zero_copy(pl.multiple_of(zero_ref[k], MOE_BLOCK)).start()
            return carry

        def drain(k, carry):
            @pl.when(zero_ref[k] >= 0)
            def _():
                zero_copy(0).wait()
            return carry

        lax.fori_loop(0, ZERO_BLOCKS, issue, 0)
        lax.fori_loop(0, ZERO_BLOCKS, drain, 0)

    def wait_rows(s):
        for _ in range(2):
            pltpu.make_async_copy(h_scr.at[s], xs_hbm.at[pl.ds(0, tm)], sem.at[s]).wait()

    @pl.when(i >= 2)
    def _():
        wait_rows(slot)

    h_scr[slot] = _moe_input(x_ref[...], g_ref, sc_ref, sh_ref)

    for r in range(tm):
        for j in range(2):
            pltpu.make_async_copy(h_scr.at[slot, pl.ds(r, 1)],
                                  xs_hbm.at[pl.ds(dest_ref[(i * tm + r) * 2 + j], 1)],
                                  sem.at[slot]).start()

    @pl.when(i == n_steps - 1)
    def _():
        @pl.when(n_steps >= 2)
        def _():
            wait_rows(1 - slot)
        wait_rows(slot)


def _dispatch(x, g, sc, sh, dest, zero_starts, n_rows, seq, tm=256):
    t, k = x.shape
    tm = min(tm, seq)
    tps = seq // tm
    grid_spec = pltpu.PrefetchScalarGridSpec(
        num_scalar_prefetch=2,
        grid=(t // tm,),
        in_specs=[
            pl.BlockSpec((tm, k), lambda i, ds, zs: (i, 0)),
            pl.BlockSpec((1, k), lambda i, ds, zs: (0, 0)),
            pl.BlockSpec((1, 1, k), lambda i, ds, zs: (i // tps, 0, 0)),
            pl.BlockSpec((1, 1, k), lambda i, ds, zs: (i // tps, 0, 0)),
        ],
        out_specs=pl.BlockSpec(memory_space=pl.ANY),
        scratch_shapes=[pltpu.VMEM((2, tm, k), F32), pltpu.VMEM((MOE_BLOCK, k), F32),
                        pltpu.SemaphoreType.DMA((2,)), pltpu.SemaphoreType.DMA],
    )
    return pl.pallas_call(
        _dispatch_body,
        out_shape=jax.ShapeDtypeStruct((n_rows, k), F32),
        grid_spec=grid_spec,
        compiler_params=_cparams(("arbitrary",)),
        name="moe_dispatch",
    )(dest, zero_starts, x, g.reshape(1, k), sc, sh)


def _row_copy(src_hbm, dst, sem, src_row, dst_row):
    return pltpu.make_async_copy(src_hbm.at[pl.ds(src_row, 1)], dst.at[pl.ds(dst_row, 1)], sem)


def _start_row_gather(idx_ref, base, n_rows, stride, src_hbm, dst, sem, straight_line=False):
    def body(r, carry):
        _row_copy(src_hbm, dst, sem, idx_ref[base + r * stride], r).start()
        return carry
    if straight_line:
        for r in range(n_rows):
            body(r, 0)
    else:
        lax.fori_loop(0, n_rows, body, 0, unroll=GATHER_UNROLL)


def _wait_row_gather(src_hbm, dst, sem, n_rows):
    pltpu.make_async_copy(src_hbm.at[pl.ds(0, n_rows)], dst, sem).wait()


def _expert_body(blk_exp_ref, n_used_ref, xs_ref, wg_ref, wu_ref, wd_ref, y_ref, wgu_scr, wdn_scr):
    i = pl.program_id(0)
    d = xs_ref.shape[1]
    used = i < n_used_ref[0]

    @pl.when(used & ((i == 0) | (blk_exp_ref[i] != blk_exp_ref[jnp.maximum(i - 1, 0)])))
    def _():
        def cast_up(c, carry):
            rows = pl.ds(pl.multiple_of(c * CAST_ROWS, CAST_ROWS), CAST_ROWS)
            wgu_scr[rows, :EXPERT_FF] = wg_ref[0, 0, rows, :].astype(BF16)
            wgu_scr[rows, EXPERT_FF:] = wu_ref[0, 0, rows, :].astype(BF16)
            return carry

        def cast_down(c, carry):
            rows = pl.ds(pl.multiple_of(c * CAST_ROWS, CAST_ROWS), CAST_ROWS)
            wdn_scr[rows, :] = wd_ref[0, 0, rows, :].astype(BF16)
            return carry

        lax.fori_loop(0, d // CAST_ROWS, cast_up, 0)
        lax.fori_loop(0, EXPERT_FF // CAST_ROWS, cast_down, 0)

    @pl.when(used)
    def _():
        a = _dot(xs_ref[...].astype(BF16), wgu_scr[...])
        hid = (_silu(a[:, :EXPERT_FF]) * a[:, EXPERT_FF:]).astype(BF16)
        y_ref[...] = _dot(hid, wdn_scr[...])

    @pl.when(jnp.logical_not(used))
    def _():
        y_ref[...] = jnp.zeros_like(y_ref)


def _experts(xs, w_gate, w_up, w_down, layer, blk_exp, n_used):
    n_rows, d = xs.shape
    n_blocks = n_rows // MOE_BLOCK
    grid_spec = pltpu.PrefetchScalarGridSpec(
        num_scalar_prefetch=2,
        grid=(n_blocks,),
        in_specs=[
            pl.BlockSpec((MOE_BLOCK, d), lambda i, be, nu: (i, 0)),
            pl.BlockSpec((1, 1, d, EXPERT_FF), lambda i, be, nu: (layer, be[i], 0, 0)),
            pl.BlockSpec((1, 1, d, EXPERT_FF), lambda i, be, nu: (layer, be[i], 0, 0)),
            pl.BlockSpec((1, 1, EXPERT_FF, d), lambda i, be, nu: (layer, be[i], 0, 0)),
        ],
        out_specs=pl.BlockSpec((MOE_BLOCK, d), lambda i, be, nu: (i, 0)),
        scratch_shapes=[pltpu.VMEM((d, 2 * EXPERT_FF), BF16),
                        pltpu.VMEM((EXPERT_FF, d), BF16)],
    )
    return pl.pallas_call(
        _expert_body,
        out_shape=jax.ShapeDtypeStruct((n_rows, d), F32),
        grid_spec=grid_spec,
        compiler_params=_cparams(("arbitrary",)),
        name="experts",
    )(blk_exp, n_used, xs, w_gate, w_up, w_down)


def _combine_body(dest_ref, y_hbm, x_ref, gate_ref, meta_ref, fn_ref, o_ref, ya_scr, sem,
                  *, final, tile0):
    i = pl.program_id(0)
    n_steps = pl.num_programs(0)
    tm = x_ref.shape[0]
    slot = i % 2

    def start(step, s, straight_line):
        for j in range(2):
            _start_row_gather(dest_ref, (tile0 + step) * tm * 2 + j, tm, 2, y_hbm,
                              ya_scr.at[s, j], sem.at[s, j], straight_line)

    @pl.when(i == 0)
    def _():
        start(0, 0, False)

    for j in range(2):
        _wait_row_gather(y_hbm, ya_scr.at[slot, j], sem.at[slot, j], tm)

    @pl.when(i + 1 < n_steps)
    def _():
        start(i + 1, 1 - slot, True)

    meta = meta_ref[...]
    moe = (ya_scr[slot, 0] * meta[:, META_G:META_G + 1]
           + ya_scr[slot, 1] * meta[:, META_G + 1:META_G + 2])
    out = x_ref[...] + gate_ref[0] * moe
    if final:
        out = _rms(out, fn_ref[...])
    o_ref[...] = out


def _combine(y, x, gate, meta, dest, final_norm, seq, final, tok0=0, n_tok=None, tm=256):
    t, d = x.shape
    n_tok = t if n_tok is None else n_tok
    tm = min(tm, seq)
    tps = seq // tm
    assert tok0 % seq == 0 and n_tok % seq == 0
    tile0 = tok0 // tm
    grid_spec = pltpu.PrefetchScalarGridSpec(
        num_scalar_prefetch=1,
        grid=(n_tok // tm,),
        in_specs=[
            pl.BlockSpec(memory_space=pl.ANY),
            pl.BlockSpec((tm, d), lambda i, ds: (tile0 + i, 0)),
            pl.BlockSpec((1, 1, d), lambda i, ds: ((tile0 + i) // tps, 0, 0)),
            pl.BlockSpec((tm, LANES), lambda i, ds: (tile0 + i, 0)),
            pl.BlockSpec((1, d), lambda i, ds: (0, 0)),
        ],
        out_specs=pl.BlockSpec((tm, d), lambda i, ds: (i, 0)),
        scratch_shapes=[pltpu.VMEM((2, 2, tm, d), F32), pltpu.SemaphoreType.DMA((2, 2))],
    )
    return pl.pallas_call(
        functools.partial(_combine_body, final=final, tile0=tile0),
        out_shape=jax.ShapeDtypeStruct((n_tok, d), F32),
        grid_spec=grid_spec,
        compiler_params=_cparams(("arbitrary",)),
        name="moe_combine",
    )(dest, y, x, gate, meta, final_norm.reshape(1, d))


def _moe_layer(x, g, sc, sh, gate, grp_w, grp_b, exp_w, exp_b, w_gate, w_up, w_down, layer,
               final_norm, seq, final, split=None):
    t, d = x.shape
    pad = LANES - MOE_GROUPS - MOE_EXPERTS
    w_route = jnp.concatenate([grp_w, exp_w, jnp.zeros((d, pad), F32)], axis=1)
    b_route = jnp.concatenate([grp_b, exp_b, jnp.zeros((pad,), F32)]).reshape(1, LANES)
    meta, cnt = _router(x, g, sc, sh, w_route, b_route, seq)

    expert = meta[:, META_E:META_E + 2].astype(I32)
    rank = meta[:, META_R:META_R + 2].astype(I32)
    counts = cnt[0, ROUTE_E0:ROUTE_E0 + MOE_EXPERTS].astype(I32)
    padded = (counts + MOE_BLOCK - 1) // MOE_BLOCK * MOE_BLOCK
    p_ends = jnp.cumsum(padded)
    p_starts = p_ends - padded
    dest = (p_starts[expert] + rank).reshape(-1)
    n_rows = t * 2 + MOE_EXPERTS * MOE_BLOCK
    n_blocks = n_rows // MOE_BLOCK
    blk_start = jnp.arange(n_blocks, dtype=I32) * MOE_BLOCK
    blk_exp = jnp.minimum(jnp.sum((p_ends[None, :] <= blk_start[:, None]).astype(I32), axis=1),
                          MOE_EXPERTS - 1)
    n_used = p_ends[-1:] // MOE_BLOCK
    tail = (n_used + jnp.arange(MOE_EXPERTS, dtype=I32)) * MOE_BLOCK
    zero_starts = jnp.concatenate([jnp.where(padded > counts, p_ends - MOE_BLOCK, -1),
                                   jnp.where(tail < n_rows, tail, -1)]).astype(I32)

    xs = _dispatch(x, g, sc, sh, dest, zero_starts, n_rows, seq)
    y = _experts(xs, w_gate, w_up, w_down, layer, blk_exp, n_used)
    if split is None:
        return _combine(y, x, gate, meta, dest, final_norm, seq, final)
    return tuple(_combine(y, x, gate, meta, dest, final_norm, seq, final, tok0=a, n_tok=b - a)
                 for a, b in ((0, split), (split, t)))


def _rope_tables(seq):
    half = RET_DK // 2
    inv = ROPE_BASE ** (-jnp.arange(half, dtype=F32) / half)
    ang = jnp.arange(seq, dtype=F32)[:, None] * inv[None, :]
    cos, sin = jnp.cos(ang), jnp.sin(ang)
    return jnp.concatenate([cos, cos], axis=1), jnp.concatenate([-sin, sin], axis=1)


def _pad_rows(a, axis, n):
    pad = [(0, 0)] * a.ndim
    pad[axis] = (0, n - a.shape[axis])
    return jnp.pad(a, pad)


def kernel(x_prompt, x_sample, c_prompt, c_sample, ada_w, ada_b, norm1, norm2, ev_w_in, ev_gate_b, ev_conv_w, ev_conv_b, ev_ret_gn, ev_mlstm_gn, ev_w_out, od_w_in, od_conv_w, od_conv_b, od_dt_bias, od_a_log, od_d_skip, od_norm, od_w_out, moe_grp_w, moe_grp_b, moe_exp_w, moe_exp_b, moe_w_gate, moe_w_up, moe_w_down, final_norm):
    n_prompt = x_prompt.shape[0]
    seq, d = x_prompt.shape[1], x_prompt.shape[2]
    assert x_sample.shape[1] == seq and d == D_MODEL
    assert seq % RET_CHUNK == 0 and seq // MLSTM_CHUNK <= LANES // SSD_HPG
    nb = n_prompt + x_sample.shape[0]
    t = nb * seq
    x = x_prompt.reshape(n_prompt * seq, d)
    x_tail = x_sample.reshape(t - n_prompt * seq, d)
    depth = ada_w.shape[0]

    c_all = jnp.concatenate([c_prompt, c_sample], axis=0)
    c_pad = _pad_rows(c_all, 0, -(-nb // 8) * 8)
    mod = _modulation(c_pad, ada_w, ada_b)[:, :nb].reshape(depth, nb, N_MOD, 1, d)

    heads = jnp.arange(RET_HEADS, dtype=F32)
    lg = jnp.stack([jnp.log1p(-jnp.exp2(-RET_DECAY_FWD - heads)),
                    jnp.log1p(-jnp.exp2(-RET_DECAY_BWD - heads))])
    cos_t, sin_t = _rope_tables(seq)

    for i in range(depth):
        sh1, sc1, g1, sh2, sc2, g2 = (mod[i, :, m] for m in range(N_MOD))
        j = i // 2
        if i % 2 == 0:
            w_in = ev_w_in[j]
            w_side = w_in[:, EVEN_MAIN:].reshape(d, 4, MLSTM_HEADS).transpose(0, 2, 1)
            w_side = _pad_rows(_pad_rows(w_side, 2, GATE_ROWS).reshape(d, -1), 1, LANES)
            proj, gates = _fused_matmul(x, w_in[:, :EVEN_MAIN].astype(BF16), seq=seq, x_tail=x_tail,
                                        prologue="normmod", g=norm1[i], sc=sc1, sh=sh1, w_side=w_side,
                                        name="even_in_proj")
            proj = proj.reshape(nb, seq, EVEN_MAIN)
            ret = _retention(proj, lg, cos_t, sin_t, ev_ret_gn[j], nb, seq)
            ml = _mlstm(proj, gates, ev_gate_b[j], ev_conv_w[j], ev_conv_b[j], ev_mlstm_gn[j], nb, seq)
            x = _fused_matmul(ret.reshape(t, RET_V), ev_w_out[j].astype(BF16), seq=seq,
                              x2=ml.reshape(t, MLSTM_V),
                              res=x, res_tail=x_tail, gate=g1, tn=512, name="even_out_proj")
        else:
            w_in = od_w_in[j]
            proj, dt_raw = _fused_matmul(x, w_in[:, :ODD_MAIN].astype(BF16), seq=seq, x_tail=x_tail,
                                         prologue="normmod", g=norm1[i], sc=sc1, sh=sh1,
                                         w_side=w_in[:, ODD_MAIN:], name="odd_in_proj")
            def per_row(p):
                p = p.reshape(2, SSD_GROUPS, SSD_HPG).transpose(1, 0, 2)
                return jnp.tile(p, (1, 1, LANES // SSD_HPG))[..., None]

            def per_lane(p):
                p = p.reshape(*p.shape[:-1], SSD_GROUPS, SSD_HPG)
                p = jnp.moveaxis(p, -2, 0)
                return jnp.repeat(p, SSD_HEADDIM, axis=-1)[..., None, :]

            y = _ssd(proj.reshape(nb, seq, ODD_MAIN), dt_raw, per_row(od_dt_bias[j]), per_row(od_a_log[j]),
                     per_lane(od_d_skip[j]), od_conv_w[j], od_conv_b[j], nb, seq)
            x = _fused_matmul(y.reshape(t, SSD_INNER), od_w_out[j].astype(BF16), seq=seq, prologue="norm",
                              g=od_norm[j], res=x, res_tail=x_tail, gate=g1, tn=512, name="odd_out_proj")
        x_tail = None
        last = i == depth - 1
        x = _moe_layer(x, norm2[i], sc2, sh2, g2, moe_grp_w[i], moe_grp_b[i], moe_exp_w[i], moe_exp_b[i],
                       moe_w_gate, moe_w_up, moe_w_down, i, final_norm, seq, final=last,
                       split=n_prompt * seq if last else None)
    y_prompt, y_sample = x
    return (y_prompt.reshape(n_prompt, seq, d), y_sample.reshape(nb - n_prompt, seq, d))
```

```python
import functools
import math

import jax
import jax.numpy as jnp
import numpy as np
from jax import lax
from jax.experimental import pallas as pl
from jax.experimental.pallas import tpu as pltpu

F32 = jnp.float32
BF16 = jnp.bfloat16
I32 = jnp.int32

D_MODEL = 2048
N_MOD = 6
EPS = 1e-6
CONV_W = 5
CONV_HALO = 8

RET_HEADS = 8
RET_DV = D_MODEL // RET_HEADS
RET_DK = RET_DV // 2
RET_DECAY_FWD = 5.0
RET_DECAY_BWD = 5.5
ROPE_BASE = 10000.0
RET_CHUNK = 256
MLSTM_HEADS = 4
MLSTM_DV = D_MODEL // MLSTM_HEADS
MLSTM_DK = MLSTM_DV // 2
MLSTM_CHUNK = 128
GATE_ROWS = 8
SSD_INNER = 2 * D_MODEL
SSD_HEADDIM = 64
SSD_HEADS = SSD_INNER // SSD_HEADDIM
SSD_GROUPS = 8
SSD_HPG = SSD_HEADS // SSD_GROUPS
SSD_STATE = 128
SSD_CHUNK = 128
MOE_GROUPS = 4
MOE_EPG = 8
MOE_EXPERTS = MOE_GROUPS * MOE_EPG
EXPERT_FF = D_MODEL // 4
MOE_BLOCK = 512

RET_QK = RET_HEADS * RET_DK
RET_V = RET_HEADS * RET_DV
MLSTM_QK = MLSTM_HEADS * MLSTM_DK
MLSTM_V = MLSTM_HEADS * MLSTM_DV
MLSTM_NGATE = 4 * MLSTM_HEADS
EVEN_MAIN = 2 * RET_QK + 2 * RET_V + 2 * MLSTM_QK + 2 * MLSTM_V
EVEN_MIX = RET_V + MLSTM_V
SSD_BC = SSD_GROUPS * SSD_STATE
SSD_CONV_CH = SSD_INNER + 2 * SSD_BC
ODD_MAIN = SSD_INNER + SSD_CONV_CH

PROLOGUE_ROWS = 256
CAST_ROWS = 256
GATHER_UNROLL = 32
SCAN_UNROLL = 2
LANES = 128
VMEM_LIMIT = 56 * 1024 * 1024

NEG_INF = float("-inf")


def _cparams(sem, vmem=VMEM_LIMIT):
    return pltpu.CompilerParams(dimension_semantics=sem, vmem_limit_bytes=vmem)


def _split3(x):
    hi = x.astype(BF16)
    r = x - hi.astype(F32)
    mid = r.astype(BF16)
    lo = (r - mid.astype(F32)).astype(BF16)
    return hi, mid, lo


def _dot(a, b):
    return jnp.dot(a, b, preferred_element_type=F32)


def _dot_nt(a, b):
    return lax.dot_general(a, b, (((1,), (1,)), ((), ())), preferred_element_type=F32)


def _dot_tn(a, b):
    return lax.dot_general(a, b, (((0,), (0,)), ((), ())), preferred_element_type=F32)


def _dot01_left(m01, x):
    hi, mid, lo = _split3(x)
    return _dot(m01, hi) + _dot(m01, mid) + _dot(m01, lo)


def _dot01_right(x, m01):
    hi, mid, lo = _split3(x)
    return _dot(hi, m01) + _dot(mid, m01) + _dot(lo, m01)


def _tri(n, kind):
    r = lax.broadcasted_iota(I32, (n, n), 0)
    c = lax.broadcasted_iota(I32, (n, n), 1)
    m = {"le": r <= c, "ge": r >= c, "gt": r > c}[kind]
    return jnp.where(m, 1.0, 0.0).astype(BF16)


def _shr(x, pow2):
    return lax.shift_right_arithmetic(x, jnp.int32(int(math.log2(pow2))))


def _sigmoid(x):
    return 1.0 / (1.0 + jnp.exp(-x))


def _silu(x):
    return x * _sigmoid(x)


def _softplus(x):
    return jnp.maximum(x, 0.0) + jnp.log1p(jnp.exp(-jnp.abs(x)))


def _log_sigmoid(x):
    return jnp.minimum(x, 0.0) - jnp.log1p(jnp.exp(-jnp.abs(x)))


def _rms(x, g):
    ms = jnp.mean(x * x, axis=-1, keepdims=True)
    return x * lax.rsqrt(ms + EPS) * g


def _head_norm(y, g):
    mu = jnp.mean(y, axis=-1, keepdims=True)
    yc = y - mu
    var = jnp.mean(yc * yc, axis=-1, keepdims=True)
    return yc * lax.rsqrt(var + EPS) * g


def _mod_body(c_ref, w_ref, b_ref, o_ref):
    c = c_ref[...]
    o_ref[0] = _dot(_silu(c).astype(BF16), w_ref[0].astype(BF16)) + b_ref[0]


def _modulation(c_pad, ada_w, ada_b):
    depth, d, n = ada_w.shape
    m = c_pad.shape[0]
    tn = 1024
    return pl.pallas_call(
        _mod_body,
        out_shape=jax.ShapeDtypeStruct((depth, m, n), F32),
        grid=(depth, n // tn),
        in_specs=[
            pl.BlockSpec((m, d), lambda l, j: (0, 0)),
            pl.BlockSpec((1, d, tn), lambda l, j: (l, 0, j)),
            pl.BlockSpec((1, 1, tn), lambda l, j: (l, 0, j)),
        ],
        out_specs=pl.BlockSpec((1, m, tn), lambda l, j: (l, 0, j)),
        compiler_params=_cparams(("parallel", "parallel")),
        name="modulation",
    )(c_pad, ada_w, ada_b.reshape(depth, 1, n))


def _mm_body(*refs, prologue, epilogue, side, two_lhs, n_head, split, res_split):
    it = iter(refs)
    x_ref = next(it)
    xt_ref = next(it) if split else None
    x2_ref = next(it) if two_lhs else None
    g_ref = next(it) if prologue != "none" else None
    sc_ref = next(it) if prologue == "normmod" else None
    sh_ref = next(it) if prologue == "normmod" else None
    w_ref = next(it)
    ws_ref = next(it) if side else None
    res_ref = next(it) if epilogue == "residual" else None
    rest_ref = next(it) if res_split else None
    gate_ref = next(it) if epilogue == "residual" else None
    o_ref = next(it)
    os_ref = next(it) if side else None
    h_scr = next(it) if prologue != "none" else None

    def in_head():
        return pl.program_id(0) < n_head

    if prologue != "none":
        def run_prologue(src_ref):
            rows_per = PROLOGUE_ROWS

            def chunk(i, carry):
                rows = pl.ds(pl.multiple_of(i * rows_per, rows_per), rows_per)
                y = _rms(src_ref[rows, :].astype(F32), g_ref[...])
                if prologue == "normmod":
                    y = y * (1.0 + sc_ref[0]) + sh_ref[0]
                hb = y.astype(BF16)
                h_scr[rows, :] = hb
                if side:
                    h_lo = (y - hb.astype(F32)).astype(BF16)
                    w_hi = ws_ref[0]
                    w_lo = ws_ref[1]
                    os_ref[0, :, rows] = (_dot(hb, w_hi) + _dot(h_lo, w_hi) + _dot(hb, w_lo)).T
                return carry

            lax.fori_loop(0, src_ref.shape[0] // rows_per, chunk, 0)

        first_col = pl.program_id(1) == 0
        if split:
            pl.when(first_col & in_head())(lambda: run_prologue(x_ref))
            pl.when(first_col & jnp.logical_not(in_head()))(lambda: run_prologue(xt_ref))
        else:
            pl.when(first_col)(lambda: run_prologue(x_ref))
        lhs = h_scr[...]
    else:
        assert not split
        lhs = x_ref[...]
    if two_lhs:
        k1 = x_ref.shape[1]
        acc = _dot(lhs, w_ref[:k1, :]) + _dot(x2_ref[...], w_ref[k1:, :])
    else:
        acc = _dot(lhs, w_ref[...])
    if epilogue == "residual" and rest_ref is not None:
        @pl.when(in_head())
        def _():
            o_ref[...] = res_ref[...] + gate_ref[0] * acc

        @pl.when(jnp.logical_not(in_head()))
        def _():
            o_ref[...] = rest_ref[...] + gate_ref[0] * acc
    elif epilogue == "residual":
        o_ref[...] = res_ref[...] + gate_ref[0] * acc
    else:
        o_ref[...] = acc.astype(o_ref.dtype)


def _fused_matmul(x, w, *, seq, x_tail=None, x2=None, prologue="none", g=None, sc=None, sh=None,
                  w_side=None, res=None, res_tail=None, gate=None, out_dtype=BF16, tm=1024, tn=1024,
                  name="proj"):
    t, k = x.shape
    n = w.shape[1]
    tm = min(tm, seq)
    tn = min(tn, n)
    split = x_tail is not None
    res_split = res_tail is not None
    n_head = (x.shape[0] if split else res.shape[0] if res_split else t) // tm
    if split:
        t = t + x_tail.shape[0]
    assert t % tm == 0 and seq % tm == 0 and n % tn == 0
    tps = seq // tm
    epilogue = "residual" if res is not None else "plain"
    side = w_side is not None
    two_lhs = x2 is not None
    assert not (two_lhs and prologue != "none")

    def head_rows(i):
        return jnp.minimum(i, n_head - 1)

    def tail_rows(i):
        return jnp.maximum(i - n_head, 0)

    if split:
        in_specs = [pl.BlockSpec((tm, k), lambda i, j: (head_rows(i), 0)),
                    pl.BlockSpec((tm, k), lambda i, j: (tail_rows(i), 0))]
        args = [x, x_tail]
    else:
        in_specs = [pl.BlockSpec((tm, k), lambda i, j: (i, 0))]
        args = [x]
    if two_lhs:
        in_specs.append(pl.BlockSpec((tm, x2.shape[1]), lambda i, j: (i, 0)))
        args.append(x2)
        k = k + x2.shape[1]
    if prologue != "none":
        in_specs.append(pl.BlockSpec((1, k), lambda i, j: (0, 0)))
        args.append(g.reshape(1, k))
    if prologue == "normmod":
        in_specs += [pl.BlockSpec((1, 1, k), lambda i, j: (i // tps, 0, 0))] * 2
        args += [sc, sh]
    in_specs.append(pl.BlockSpec((k, tn), lambda i, j: (0, j)))
    args.append(w)
    if side:
        ws_hi = w_side.astype(BF16)
        ws_lo = (w_side - ws_hi.astype(F32)).astype(BF16)
        in_specs.append(pl.BlockSpec((2, k, LANES), lambda i, j: (0, 0, 0)))
        args.append(jnp.stack([ws_hi, ws_lo]))
    if epilogue == "residual":
        if res_split:
            in_specs += [pl.BlockSpec((tm, tn), lambda i, j: (head_rows(i), jnp.where(i < n_head, j, 0))),
                         pl.BlockSpec((tm, tn), lambda i, j: (tail_rows(i), jnp.where(i < n_head, 0, j)))]
            args += [res, res_tail]
        else:
            in_specs.append(pl.BlockSpec((tm, tn), lambda i, j: (i, j)))
            args.append(res)
        in_specs.append(pl.BlockSpec((1, 1, tn), lambda i, j: (i // tps, 0, j)))
        args.append(gate)
        out_dtype = F32
    out_shape = [jax.ShapeDtypeStruct((t, n), out_dtype)]
    out_specs = [pl.BlockSpec((tm, tn), lambda i, j: (i, j))]
    if side:
        out_shape.append(jax.ShapeDtypeStruct((t // seq, LANES, seq), F32))
        out_specs.append(pl.BlockSpec((1, LANES, tm), lambda i, j: (i // tps, 0, i % tps)))
    scratch = [pltpu.VMEM((tm, k), BF16)] if prologue != "none" else []
    outs = pl.pallas_call(
        functools.partial(_mm_body, prologue=prologue, epilogue=epilogue, side=side, two_lhs=two_lhs,
                          n_head=n_head, split=split, res_split=res_split),
        out_shape=out_shape,
        grid=(t // tm, n // tn),
        in_specs=in_specs,
        out_specs=out_specs,
        scratch_shapes=scratch,
        compiler_params=_cparams(("parallel", "arbitrary")),
        name=name,
    )(*args)
    return outs if side else outs[0]


CONV_ROWS = 128


def _conv_fill(src_ref, pad_scr, seq):
    ch = pad_scr.shape[1]
    halo = CONV_HALO
    rows = CONV_ROWS
    zeros = jnp.zeros((halo, ch), F32)
    pad_scr[pl.ds(0, halo), :] = zeros
    pad_scr[pl.ds(seq + halo, halo), :] = zeros

    def fill(i, carry):
        r0 = pl.multiple_of(i * rows, rows)
        pad_scr[pl.ds(pl.multiple_of(r0 + halo, halo), rows), :] = src_ref[0, pl.ds(r0, rows), :].astype(F32)
        return carry

    lax.fori_loop(0, seq // rows, fill, 0)


def _conv_silu_rows(pad_scr, w_ref, b_ref, r0, emit):
    ch = pad_scr.shape[1]
    halo = CONV_HALO
    rows = CONV_ROWS
    win = rows + 2 * halo
    half = (CONV_W - 1) // 2
    for lane0 in range(0, ch, LANES):
        cols = slice(lane0, lane0 + LANES)
        window = pad_scr[pl.ds(r0, win), cols]
        acc = jnp.zeros((rows, LANES), F32) + b_ref[:, cols]
        for j in range(CONV_W):
            d = j - half
            shifted = window if d == 0 else pltpu.roll(window, (-d) % win, axis=0)
            acc = acc + w_ref[j:j + 1, cols] * shifted[halo:halo + rows, :]
        emit(lane0, _silu(acc))


def _two_ended_scan(n_chunks, prepare, step, finish):
    assert n_chunks % 2 == 0
    half = n_chunks // 2

    def first(i, carry):
        prepare(i)
        prepare(n_chunks - 1 - i)
        step(i)
        return carry

    def second(i, carry):
        step(i)
        finish(i)
        finish(n_chunks - 1 - i)
        return carry

    unroll = SCAN_UNROLL if half % SCAN_UNROLL == 0 else 1
    lax.fori_loop(0, half, first, 0, unroll=unroll)
    lax.fori_loop(half, n_chunks, second, 0, unroll=unroll)


def _ret_body(lg_ref, q_ref, k_ref, v_ref, g_ref, cos_ref, sin_ref, gn_ref, o_ref,
              qs_scr, ks_scr, acc_scr, st_scr, *, seq):
    c_len = RET_CHUNK
    n_chunks = seq // c_len
    h = pl.program_id(1)
    lgf = lg_ref[0, h]
    lgb = lg_ref[1, h]
    ri = lax.broadcasted_iota(I32, (c_len, c_len), 0)
    ci = lax.broadcasted_iota(I32, (c_len, c_len), 1)
    diff = (ri - ci).astype(F32)
    dmat = jnp.exp(jnp.where(diff >= 0, lgf * diff, -lgb * diff))
    pos = lax.broadcasted_iota(I32, (c_len, 1), 0).astype(F32)
    qdec_f = jnp.exp(lgf * (pos + 1.0))
    kdec_f = jnp.exp(lgf * (c_len - 1.0 - pos))
    cdec_f = jnp.exp(jnp.full((1, 1), c_len, F32) * lgf)
    qdec_b = jnp.exp(lgb * (c_len - pos))
    kdec_b = jnp.exp(lgb * pos)
    cdec_b = jnp.exp(jnp.full((1, 1), c_len, F32) * lgb)
    half = RET_DK // 2

    def rope(x, rows):
        return x * cos_ref[rows, :] + pltpu.roll(x, half, axis=1) * sin_ref[rows, :]

    def chunk_rows(c):
        return pl.ds(pl.multiple_of(c * c_len, c_len), c_len)

    def prepare(c):
        rows = chunk_rows(c)
        qs_scr[rows, :] = rope(q_ref[0, rows, :].astype(F32), rows).astype(BF16)
        ks_scr[rows, :] = rope(k_ref[0, rows, :].astype(F32), rows) * (RET_DK ** -0.5)

    st_scr[...] = jnp.zeros_like(st_scr)

    def step(i):
        rows = chunk_rows(i)
        qb = qs_scr[rows, :]
        k = ks_scr[rows, :]
        v = v_ref[0, rows, :]
        p = (_dot_nt(qb, k.astype(BF16)) * dmat).astype(BF16)
        acc_scr[0, rows, :] = _dot(p, v) + qdec_f * _dot(qb, st_scr[0].astype(BF16))
        st_scr[0] = cdec_f * st_scr[0] + _dot_tn((k * kdec_f).astype(BF16), v)

        rows = chunk_rows(n_chunks - 1 - i)
        k = ks_scr[rows, :]
        acc_scr[1, rows, :] = qdec_b * _dot(qs_scr[rows, :], st_scr[1].astype(BF16))
        st_scr[1] = cdec_b * st_scr[1] + _dot_tn((k * kdec_b).astype(BF16), v_ref[0, rows, :])

    def finish(c):
        rows = chunk_rows(c)
        o = acc_scr[0, rows, :] + acc_scr[1, rows, :]
        gate = g_ref[0, rows, :].astype(F32)
        o_ref[0, rows, :] = (_head_norm(o, gn_ref[0]) * _silu(gate)).astype(o_ref.dtype)

    _two_ended_scan(n_chunks, prepare, step, finish)


def _retention(proj, lg, cos_t, sin_t, ret_gn, nb, seq):
    kq, kv = RET_DK, RET_DV
    grid_spec = dict(
        grid=(nb, RET_HEADS),
        in_specs=[
            pl.BlockSpec(memory_space=pltpu.SMEM),
            pl.BlockSpec((1, seq, kq), lambda b, h: (b, 0, h)),
            pl.BlockSpec((1, seq, kq), lambda b, h: (b, 0, RET_QK // kq + h)),
            pl.BlockSpec((1, seq, kv), lambda b, h: (b, 0, 2 * RET_QK // kv + h)),
            pl.BlockSpec((1, seq, kv), lambda b, h: (b, 0, (2 * RET_QK + RET_V) // kv + h)),
            pl.BlockSpec((seq, kq), lambda b, h: (0, 0)),
            pl.BlockSpec((seq, kq), lambda b, h: (0, 0)),
            pl.BlockSpec((1, 1, kv), lambda b, h: (h, 0, 0)),
        ],
        out_specs=pl.BlockSpec((1, seq, kv), lambda b, h: (b, 0, h)),
        scratch_shapes=[
            pltpu.VMEM((seq, kq), BF16),
            pltpu.VMEM((seq, kq), F32),
            pltpu.VMEM((2, seq, kv), F32),
            pltpu.VMEM((2, kq, kv), F32),
        ],
    )
    return pl.pallas_call(
        functools.partial(_ret_body, seq=seq),
        out_shape=jax.ShapeDtypeStruct((nb, seq, RET_V), BF16),
        **grid_spec,
        compiler_params=_cparams(("parallel", "parallel")),
        name="retention",
    )(lg, proj, proj, proj, proj, cos_t, sin_t, ret_gn.reshape(RET_HEADS, 1, kv))


def _col_of(mat, c):
    lane = lax.broadcasted_iota(I32, mat.shape, 1)
    return jnp.sum(jnp.where(lane == c, mat, 0.0), axis=1, keepdims=True)


def _mlstm_body(gb_ref, q_ref, k_ref, v_ref, o_gate_ref, wq_ref, wk_ref, bq_ref, bk_ref,
                gr_ref, gn_ref, o_ref,
                padq_scr, padk_scr, qs_scr, ks_scr, acc_scr, c_scr, n_scr, m_scr, row_scr, col_scr,
                *, seq):
    ln = MLSTM_CHUNK
    assert ln == CONV_ROWS
    n_chunks = seq // ln
    h = pl.program_id(1)

    _conv_fill(q_ref, padq_scr, seq)
    _conv_fill(k_ref, padk_scr, seq)

    def prepare(c):
        r0 = pl.multiple_of(c * ln, ln)
        rows = pl.ds(r0, ln)

        def emit_q(lane0, y):
            qs_scr[rows, lane0:lane0 + LANES] = (y * (MLSTM_DK ** -0.5)).astype(BF16)

        def emit_k(lane0, y):
            ks_scr[rows, lane0:lane0 + LANES] = y

        _conv_silu_rows(padq_scr, wq_ref, bq_ref, r0, emit_q)
        _conv_silu_rows(padk_scr, wk_ref, bk_ref, r0, emit_k)

    row_scr[...] = jnp.zeros_like(row_scr)
    for t in range(4):
        for c in range(n_chunks):
            row_scr[t, c:c + 1, :] = gr_ref[0, t:t + 1, c * ln:(c + 1) * ln]
    tri_le = _tri(ln, "le")
    tri_ge = _tri(ln, "ge")
    for d in range(2):
        ig = row_scr[2 * d] + gb_ref[(2 * d) * MLSTM_HEADS + h]
        lf = _log_sigmoid(row_scr[2 * d + 1] + gb_ref[(2 * d + 1) * MLSTM_HEADS + h])
        bc = _dot01_right(lf, tri_le if d == 0 else tri_ge)
        row_scr[2 * d] = ig
        row_scr[2 * d + 1] = bc
        col_scr[2 * d] = ig.T
        col_scr[2 * d + 1] = bc.T

    ri = lax.broadcasted_iota(I32, (ln, ln), 0)
    ci = lax.broadcasted_iota(I32, (ln, ln), 1)

    def chunk_step(c, d):
        r0 = pl.multiple_of(c * ln, ln)
        rows = pl.ds(r0, ln)
        qb = qs_scr[rows, :]
        kf = ks_scr[rows, :]
        v = v_ref[0, rows, :]
        i_row = row_scr[2 * d, pl.ds(c, 1), :]
        b_row = row_scr[2 * d + 1, pl.ds(c, 1), :]
        i_col = _col_of(col_scr[2 * d], c)
        b_col = _col_of(col_scr[2 * d + 1], c)
        m_st = m_scr[d]
        mask = (ri >= ci) if d == 0 else (ri <= ci)
        logd = jnp.where(mask, b_col - b_row + i_row, NEG_INF)
        m_inter = b_col + m_st
        m_row = jnp.maximum(m_inter, jnp.max(logd, axis=1, keepdims=True))
        sc = _dot_nt(qb, kf.astype(BF16)) * jnp.exp(logd - m_row)
        inter = jnp.exp(m_inter - m_row)
        num = _dot(sc.astype(BF16), v) + inter * _dot(qb, c_scr[d].astype(BF16))
        den = jnp.sum(sc, axis=1, keepdims=True) + inter * jnp.sum(
            qb.astype(F32) * n_scr[d], axis=1, keepdims=True)
        hh = num / jnp.maximum(jnp.abs(den), jnp.exp(-m_row))
        b_end = b_row[:, ln - 1:ln] if d == 0 else b_row[:, 0:1]
        logw = b_end - b_col + i_col
        m_new = jnp.maximum(b_end + m_st, jnp.max(logw, axis=0, keepdims=True))
        kw = kf * jnp.exp(logw - m_new)
        dec = jnp.exp(b_end + m_st - m_new)
        c_scr[d] = dec * c_scr[d] + _dot_tn(kw.astype(BF16), v)
        n_scr[d] = dec * n_scr[d] + jnp.sum(kw, axis=0, keepdims=True)
        m_scr[d] = m_new
        acc_scr[d, rows, :] = hh

    c_scr[...] = jnp.zeros_like(c_scr)
    n_scr[...] = jnp.zeros_like(n_scr)
    m_scr[...] = jnp.zeros_like(m_scr)

    def step(i):
        chunk_step(i, 0)
        chunk_step(n_chunks - 1 - i, 1)

    def finish(c):
        rows = pl.ds(pl.multiple_of(c * ln, ln), ln)
        y = _head_norm(acc_scr[0, rows, :] + acc_scr[1, rows, :], gn_ref[0])
        o_ref[0, rows, :] = (y * _sigmoid(o_gate_ref[0, rows, :].astype(F32))).astype(o_ref.dtype)

    _two_ended_scan(n_chunks, prepare, step, finish)


def _mlstm(proj, gate_rows, gate_b, conv_w, conv_b, mlstm_gn, nb, seq):
    kq, kv = MLSTM_DK, MLSTM_DV
    q0 = 2 * RET_QK + 2 * RET_V
    k0 = q0 + MLSTM_QK
    v0 = k0 + MLSTM_QK
    o0 = v0 + MLSTM_V
    grid_spec = dict(
        grid=(nb, MLSTM_HEADS),
        in_specs=[
            pl.BlockSpec(memory_space=pltpu.SMEM),
            pl.BlockSpec((1, seq, kq), lambda b, h: (b, 0, q0 // kq + h)),
            pl.BlockSpec((1, seq, kq), lambda b, h: (b, 0, k0 // kq + h)),
            pl.BlockSpec((1, seq, kv), lambda b, h: (b, 0, v0 // kv + h)),
            pl.BlockSpec((1, seq, kv), lambda b, h: (b, 0, o0 // kv + h)),
            pl.BlockSpec((CONV_W, kq), lambda b, h: (0, h)),
            pl.BlockSpec((CONV_W, kq), lambda b, h: (0, MLSTM_QK // kq + h)),
            pl.BlockSpec((1, kq), lambda b, h: (0, h)),
            pl.BlockSpec((1, kq), lambda b, h: (0, MLSTM_QK // kq + h)),
            pl.BlockSpec((1, GATE_ROWS, seq), lambda b, h: (b, h, 0)),
            pl.BlockSpec((1, 1, kv), lambda b, h: (h, 0, 0)),
        ],
        out_specs=pl.BlockSpec((1, seq, kv), lambda b, h: (b, 0, h)),
        scratch_shapes=[
            pltpu.VMEM((seq + 2 * CONV_HALO, kq), F32),
            pltpu.VMEM((seq + 2 * CONV_HALO, kq), F32),
            pltpu.VMEM((seq, kq), BF16),
            pltpu.VMEM((seq, kq), F32),
            pltpu.VMEM((2, seq, kv), F32),
            pltpu.VMEM((2, kq, kv), F32),
            pltpu.VMEM((2, 1, kq), F32),
            pltpu.VMEM((2, 1, 1), F32),
            pltpu.VMEM((4, LANES, MLSTM_CHUNK), F32),
            pltpu.VMEM((4, MLSTM_CHUNK, LANES), F32),
        ],
    )
    return pl.pallas_call(
        functools.partial(_mlstm_body, seq=seq),
        out_shape=jax.ShapeDtypeStruct((nb, seq, MLSTM_V), BF16),
        **grid_spec,
        compiler_params=_cparams(("parallel", "parallel")),
        name="mlstm",
    )(gate_b, proj, proj, proj, proj, conv_w, conv_w, conv_b.reshape(1, -1), conv_b.reshape(1, -1),
      gate_rows, mlstm_gn.reshape(MLSTM_HEADS, 1, kv))


def _ssd_body(z_ref, x_ref, b_ref, c_ref, wx_ref, wb_ref, wc_ref, bx_ref, bb_ref, bc_ref,
              dtf_ref, dtb_ref, bias_ref, alog_ref, dskip_ref, o_ref,
              padx_scr, padb_scr, padc_scr, xs_scr, bs_scr, bst_scr, cs_scr, y_scr, st_scr,
              acr_scr, dtr_scr, er_scr, ur_scr, act_scr, *, seq):
    ln = SSD_CHUNK
    assert ln == CONV_ROWS
    n_chunks = seq // ln
    hp = SSD_HEADDIM
    n_pairs = SSD_HPG // 2

    _conv_fill(x_ref, padx_scr, seq)
    _conv_fill(b_ref, padb_scr, seq)
    _conv_fill(c_ref, padc_scr, seq)

    def prepare(c):
        r0 = pl.multiple_of(c * ln, ln)
        rows = pl.ds(r0, ln)

        def emit_x(lane0, y):
            xs_scr[rows, lane0:lane0 + LANES] = y

        def emit_b(lane0, y):
            bs_scr[rows, :] = y.astype(BF16)
            bst_scr[:, rows] = y.T.astype(BF16)

        def emit_c(lane0, y):
            cs_scr[rows, :] = y.astype(BF16)

        _conv_silu_rows(padx_scr, wx_ref, bx_ref, r0, emit_x)
        _conv_silu_rows(padb_scr, wb_ref, bb_ref, r0, emit_b)
        _conv_silu_rows(padc_scr, wc_ref, bc_ref, r0, emit_c)

    for d, dt_ref in enumerate((dtf_ref, dtb_ref)):
        acr_scr[d] = jnp.zeros((LANES, ln), F32)
        for c in range(n_chunks):
            acr_scr[d, c * SSD_HPG:(c + 1) * SSD_HPG, :] = dt_ref[0, :, c * ln:(c + 1) * ln]
        dt = _softplus(acr_scr[d] + bias_ref[0, d])
        adt = dt * (-jnp.exp(alog_ref[0, d]))
        acum = _dot01_right(adt, _tri(ln, "le" if d == 0 else "ge"))
        a_end = acum[:, ln - 1:ln] if d == 0 else acum[:, 0:1]
        acr_scr[d] = acum
        dtr_scr[d] = dt
        er_scr[d] = jnp.exp(acum)
        ur_scr[d] = dt * jnp.exp(a_end - acum)
        act_scr[d] = acum.T

    ri = lax.broadcasted_iota(I32, (ln, ln), 0)
    ci = lax.broadcasted_iota(I32, (ln, ln), 1)
    in_first = lax.broadcasted_iota(I32, (1, LANES), 1) < hp
    on_diag = ri == ci

    def chunk_rows(c):
        return pl.ds(pl.multiple_of(c * ln, ln), ln)

    def dir_step(d, c):
        rows = chunk_rows(c)
        xb = xs_scr[rows, :].astype(BF16)
        bcm = bs_scr[rows, :]
        bct = bst_scr[:, rows].astype(F32)
        ccm = cs_scr[rows, :]
        cb = _dot_nt(ccm, bcm)
        carried = _dot(ccm, st_scr[d].astype(BF16)).astype(BF16)
        head_rows = pl.ds(pl.multiple_of(c * SSD_HPG, SSD_HPG), SSD_HPG)
        arow = acr_scr[d, head_rows, :]
        dtrow = dtr_scr[d, head_rows, :]
        erow = er_scr[d, head_rows, :]
        urow = ur_scr[d, head_rows, :]
        acols = pltpu.roll(act_scr[d], (LANES - c * SSD_HPG) & (LANES - 1), axis=1)
        mask = (ri >= ci) if d == 0 else (ri <= ci)
        end = ln - 1 if d == 0 else 0
        pieces = []
        for pair in range(n_pairs):
            lanes = slice(pair * LANES, (pair + 1) * LANES)
            lhs, rhs, lhs_state, keep = [], [], [], []
            for sub in range(2):
                k = 2 * pair + sub
                sel = in_first if sub == 0 else ~in_first
                dec = jnp.exp(jnp.where(mask, acols[:, k:k + 1] - arow[k:k + 1, :], NEG_INF))
                lhs.append((cb * dec * dtrow[k:k + 1, :]).astype(BF16))
                rhs.append(jnp.where(sel, xb[:, lanes], jnp.zeros((ln, LANES), BF16)))
                lhs_state.append((bct * urow[k:k + 1, :]).astype(BF16))
                keep.append(erow[k:k + 1, end:end + 1])
            for sub in range(2):
                k = 2 * pair + sub
                sel = in_first if sub == 0 else ~in_first
                lhs.append(jnp.where(on_diag, erow[k:k + 1, :], 0.0).astype(BF16))
                rhs.append(jnp.where(sel, carried[:, lanes], jnp.zeros((ln, LANES), BF16)))
            pieces.append(_dot(jnp.concatenate(lhs, axis=1), jnp.concatenate(rhs, axis=0)))
            st_scr[d, :, lanes] = (jnp.where(in_first, keep[0], keep[1]) * st_scr[d, :, lanes]
                                   + _dot(jnp.concatenate(lhs_state, axis=1),
                                          jnp.concatenate(rhs[:2], axis=0)))
        y_scr[d, rows, :] = jnp.concatenate(pieces, axis=1)

    st_scr[...] = jnp.zeros_like(st_scr)

    def step(i):
        dir_step(0, i)
        dir_step(1, n_chunks - 1 - i)

    def finish(c):
        rows = chunk_rows(c)
        y = y_scr[0, rows, :] + y_scr[1, rows, :] + dskip_ref[0] * xs_scr[rows, :]
        o_ref[0, rows, :] = (y * _silu(z_ref[0, rows, :].astype(F32))).astype(o_ref.dtype)

    _two_ended_scan(n_chunks, prepare, step, finish)


def _ssd(proj, dt_t, bias_col, alog_col, dskip_x, conv_w, conv_b, nb, seq):
    width = SSD_HPG * SSD_HEADDIM
    ns = SSD_STATE
    x0 = SSD_INNER
    b0 = 2 * SSD_INNER
    c0 = b0 + SSD_BC
    cb = conv_b.reshape(1, -1)
    return pl.pallas_call(
        functools.partial(_ssd_body, seq=seq),
        out_shape=jax.ShapeDtypeStruct((nb, seq, SSD_INNER), BF16),
        grid=(nb, SSD_GROUPS),
        in_specs=[
            pl.BlockSpec((1, seq, width), lambda b, g: (b, 0, g)),
            pl.BlockSpec((1, seq, width), lambda b, g: (b, 0, x0 // width + g)),
            pl.BlockSpec((1, seq, ns), lambda b, g: (b, 0, b0 // ns + g)),
            pl.BlockSpec((1, seq, ns), lambda b, g: (b, 0, c0 // ns + g)),
            pl.BlockSpec((CONV_W, width), lambda b, g: (0, g)),
            pl.BlockSpec((CONV_W, ns), lambda b, g: (0, SSD_INNER // ns + g)),
            pl.BlockSpec((CONV_W, ns), lambda b, g: (0, (SSD_INNER + SSD_BC) // ns + g)),
            pl.BlockSpec((1, width), lambda b, g: (0, g)),
            pl.BlockSpec((1, ns), lambda b, g: (0, SSD_INNER // ns + g)),
            pl.BlockSpec((1, ns), lambda b, g: (0, (SSD_INNER + SSD_BC) // ns + g)),
            pl.BlockSpec((1, SSD_HPG, seq), lambda b, g: (b, g, 0)),
            pl.BlockSpec((1, SSD_HPG, seq), lambda b, g: (b, SSD_GROUPS + g, 0)),
            pl.BlockSpec((1, 2, LANES, 1), lambda b, g: (g, 0, 0, 0)),
            pl.BlockSpec((1, 2, LANES, 1), lambda b, g: (g, 0, 0, 0)),
            pl.BlockSpec((1, 1, width), lambda b, g: (g, 0, 0)),
        ],
        out_specs=pl.BlockSpec((1, seq, width), lambda b, g: (b, 0, g)),
        scratch_shapes=[
            pltpu.VMEM((seq + 2 * CONV_HALO, width), F32),
            pltpu.VMEM((seq + 2 * CONV_HALO, ns), F32),
            pltpu.VMEM((seq + 2 * CONV_HALO, ns), F32),
            pltpu.VMEM((seq, width), F32),
            pltpu.VMEM((seq, ns), BF16),
            pltpu.VMEM((ns, seq), BF16),
            pltpu.VMEM((seq, ns), BF16),
            pltpu.VMEM((2, seq, width), F32),
            pltpu.VMEM((2, ns, width), F32),
            pltpu.VMEM((2, LANES, SSD_CHUNK), F32),
            pltpu.VMEM((2, LANES, SSD_CHUNK), F32),
            pltpu.VMEM((2, LANES, SSD_CHUNK), F32),
            pltpu.VMEM((2, LANES, SSD_CHUNK), F32),
            pltpu.VMEM((2, SSD_CHUNK, LANES), F32),
        ],
        compiler_params=_cparams(("parallel", "parallel")),
        name="ssd",
    )(proj, proj, proj, proj, conv_w, conv_w, conv_w, cb, cb, cb,
      dt_t, dt_t, bias_col, alog_col, dskip_x)


META_E = 0
META_G = 2
META_R = 4
ROUTE_E0 = MOE_GROUPS


def _moe_input(x, g_ref, sc_ref, sh_ref):
    return _rms(x, g_ref[...]) * (1.0 + sc_ref[0]) + sh_ref[0]


def _router_body(x_ref, g_ref, sc_ref, sh_ref, w_ref, b_ref, meta_ref, cnt_ref,
                 carry_scr, whi_scr, wlo_scr):
    @pl.when(pl.program_id(0) == 0)
    def _():
        carry_scr[...] = jnp.zeros_like(carry_scr)
        w = w_ref[...]
        w_hi = w.astype(BF16)
        whi_scr[...] = w_hi
        wlo_scr[...] = (w - w_hi.astype(F32)).astype(BF16)

    meta_ref[...] = _route_rows(_moe_input(x_ref[...], g_ref, sc_ref, sh_ref),
                                whi_scr[...], wlo_scr[...], b_ref[...], carry_scr)
    cnt_ref[...] = jnp.broadcast_to(carry_scr[...], cnt_ref.shape)


def _route_rows(y, w_hi, w_lo, bias, carry_scr):
    tm = y.shape[0]
    h_hi = y.astype(BF16)
    h_lo = (y - h_hi.astype(F32)).astype(BF16)
    logits = _dot(h_hi, w_hi) + _dot(h_lo, w_hi) + _dot(h_hi, w_lo) + bias

    lane = lax.broadcasted_iota(I32, (tm, LANES), 1)
    lane_f = lane.astype(F32)
    big = float(LANES)
    is_grp = lane < MOE_GROUPS
    gl = jnp.where(is_grp, logits, NEG_INF)
    gmax = jnp.max(gl, axis=1, keepdims=True)
    gidx = jnp.min(jnp.where(gl == gmax, lane_f, big), axis=1, keepdims=True)
    gprob = 1.0 / jnp.sum(jnp.where(is_grp, jnp.exp(gl - gmax), 0.0), axis=1, keepdims=True)

    el = lane - ROUTE_E0
    el_f = el.astype(F32)
    valid = (el >= 0) & (el < MOE_EXPERTS)
    in_grp = valid & (_shr(el, MOE_EPG).astype(F32) == gidx)
    ev = jnp.where(in_grp, logits, NEG_INF)
    v1 = jnp.max(ev, axis=1, keepdims=True)
    i1 = jnp.min(jnp.where(ev == v1, el_f, big), axis=1, keepdims=True)
    ev2 = jnp.where(el_f == i1, NEG_INF, ev)
    v2 = jnp.max(ev2, axis=1, keepdims=True)
    i2 = jnp.min(jnp.where(ev2 == v2, el_f, big), axis=1, keepdims=True)
    p2 = jnp.exp(v2 - v1)
    s1 = 1.0 / (1.0 + p2)
    gate1 = s1 * gprob
    gate2 = p2 * s1 * gprob

    oh1 = jnp.where(el_f == i1, 1.0, 0.0)
    oh2 = jnp.where(el_f == i2, 1.0, 0.0)
    oh = oh1 + oh2
    before = _dot(_tri(tm, "gt"), oh.astype(BF16)) + carry_scr[...]
    rank1 = jnp.sum(oh1 * before, axis=1, keepdims=True)
    rank2 = jnp.sum(oh2 * before, axis=1, keepdims=True)
    carry_scr[...] = carry_scr[...] + jnp.sum(oh, axis=0, keepdims=True)

    meta = jnp.zeros((tm, LANES), F32)
    for col, val in ((META_E, i1), (META_E + 1, i2), (META_G, gate1), (META_G + 1, gate2),
                     (META_R, rank1), (META_R + 1, rank2)):
        meta = jnp.where(lane == col, val, meta)
    return meta


def _router(x, g, sc, sh, w_route, b_route, seq, tm=512):
    t, k = x.shape
    tm = min(tm, seq)
    tps = seq // tm
    return pl.pallas_call(
        _router_body,
        out_shape=[jax.ShapeDtypeStruct((t, LANES), F32),
                   jax.ShapeDtypeStruct((8, LANES), F32)],
        grid=(t // tm,),
        in_specs=[
            pl.BlockSpec((tm, k), lambda i: (i, 0)),
            pl.BlockSpec((1, k), lambda i: (0, 0)),
            pl.BlockSpec((1, 1, k), lambda i: (i // tps, 0, 0)),
            pl.BlockSpec((1, 1, k), lambda i: (i // tps, 0, 0)),
            pl.BlockSpec((k, LANES), lambda i: (0, 0)),
            pl.BlockSpec((1, LANES), lambda i: (0, 0)),
        ],
        out_specs=[pl.BlockSpec((tm, LANES), lambda i: (i, 0)),
                   pl.BlockSpec((8, LANES), lambda i: (0, 0))],
        scratch_shapes=[pltpu.VMEM((1, LANES), F32), pltpu.VMEM((k, LANES), BF16),
                        pltpu.VMEM((k, LANES), BF16)],
        compiler_params=_cparams(("arbitrary",)),
        name="router",
    )(x, g.reshape(1, k), sc, sh, w_route, b_route)


ZERO_BLOCKS = 2 * MOE_EXPERTS


def _dispatch_body(dest_ref, zero_ref, x_ref, g_ref, sc_ref, sh_ref, xs_hbm, h_scr, zero_scr, sem, zsem):
    i = pl.program_id(0)
    n_steps = pl.num_programs(0)
    tm = x_ref.shape[0]
    slot = i % 2

    def zero_copy(start):
        return pltpu.make_async_copy(zero_scr, xs_hbm.at[pl.ds(start, MOE_BLOCK)], zsem)

    @pl.when(i == 0)
    def _():
        zero_scr[...] = jnp.zeros_like(zero_scr)

        def issue(k, carry):
            @pl.when(zero_ref[k] >= 0)
            def _():
                zero_copy(pl.multiple_of(zero_ref[k], MOE_BLOCK)).start()
            return carry

        def drain(k, carry):
            @pl.when(zero_ref[k] >= 0)
            def _():
                zero_copy(0).wait()
            return carry

        lax.fori_loop(0, ZERO_BLOCKS, issue, 0)
        lax.fori_loop(0, ZERO_BLOCKS, drain, 0)

    def wait_rows(s):
        for _ in range(2):
            pltpu.make_async_copy(h_scr.at[s], xs_hbm.at[pl.ds(0, tm)], sem.at[s]).wait()

    @pl.when(i >= 2)
    def _():
        wait_rows(slot)

    h_scr[slot] = _moe_input(x_ref[...], g_ref, sc_ref, sh_ref)

    for r in range(tm):
        for j in range(2):
            pltpu.make_async_copy(h_scr.at[slot, pl.ds(r, 1)],
                                  xs_hbm.at[pl.ds(dest_ref[(i * tm + r) * 2 + j], 1)],
                                  sem.at[slot]).start()

    @pl.when(i == n_steps - 1)
    def _():
        @pl.when(n_steps >= 2)
        def _():
            wait_rows(1 - slot)
        wait_rows(slot)


def _dispatch(x, g, sc, sh, dest, zero_starts, n_rows, seq, tm=512):
    t, k = x.shape
    tm = min(tm, seq)
    tps = seq // tm
    grid_spec = pltpu.PrefetchScalarGridSpec(
        num_scalar_prefetch=2,
        grid=(t // tm,),
        in_specs=[
            pl.BlockSpec((tm, k), lambda i, ds, zs: (i, 0)),
            pl.BlockSpec((1, k), lambda i, ds, zs: (0, 0)),
            pl.BlockSpec((1, 1, k), lambda i, ds, zs: (i // tps, 0, 0)),
            pl.BlockSpec((1, 1, k), lambda i, ds, zs: (i // tps, 0, 0)),
        ],
        out_specs=pl.BlockSpec(memory_space=pl.ANY),
        scratch_shapes=[pltpu.VMEM((2, tm, k), F32), pltpu.VMEM((MOE_BLOCK, k), F32),
                        pltpu.SemaphoreType.DMA((2,)), pltpu.SemaphoreType.DMA],
    )
    return pl.pallas_call(
        _dispatch_body,
        out_shape=jax.ShapeDtypeStruct((n_rows, k), F32),
        grid_spec=grid_spec,
        compiler_params=_cparams(("arbitrary",)),
        name="moe_dispatch",
    )(dest, zero_starts, x, g.reshape(1, k), sc, sh)


def _row_copy(src_hbm, dst, sem, src_row, dst_row):
    return pltpu.make_async_copy(src_hbm.at[pl.ds(src_row, 1)], dst.at[pl.ds(dst_row, 1)], sem)


def _start_row_gather(idx_ref, base, n_rows, stride, src_hbm, dst, sem, straight_line=False):
    def body(r, carry):
        _row_copy(src_hbm, dst, sem, idx_ref[base + r * stride], r).start()
        return carry
    if straight_line:
        for r in range(n_rows):
            body(r, 0)
    else:
        lax.fori_loop(0, n_rows, body, 0, unroll=GATHER_UNROLL)


def _wait_row_gather(src_hbm, dst, sem, n_rows):
    pltpu.make_async_copy(src_hbm.at[pl.ds(0, n_rows)], dst, sem).wait()


def _expert_body(blk_exp_ref, n_used_ref, xs_ref, wg_ref, wu_ref, wd_ref, y_ref, wgu_scr, wdn_scr):
    i = pl.program_id(0)
    d = xs_ref.shape[1]
    used = i < n_used_ref[0]

    @pl.when(used & ((i == 0) | (blk_exp_ref[i] != blk_exp_ref[jnp.maximum(i - 1, 0)])))
    def _():
        def cast_up(c, carry):
            rows = pl.ds(pl.multiple_of(c * CAST_ROWS, CAST_ROWS), CAST_ROWS)
            wgu_scr[rows, :EXPERT_FF] = wg_ref[0, 0, rows, :].astype(BF16)
            wgu_scr[rows, EXPERT_FF:] = wu_ref[0, 0, rows, :].astype(BF16)
            return carry

        def cast_down(c, carry):
            rows = pl.ds(pl.multiple_of(c * CAST_ROWS, CAST_ROWS), CAST_ROWS)
            wdn_scr[rows, :] = wd_ref[0, 0, rows, :].astype(BF16)
            return carry

        lax.fori_loop(0, d // CAST_ROWS, cast_up, 0)
        lax.fori_loop(0, EXPERT_FF // CAST_ROWS, cast_down, 0)

    @pl.when(used)
    def _():
        a = _dot(xs_ref[...].astype(BF16), wgu_scr[...])
        hid = (_silu(a[:, :EXPERT_FF]) * a[:, EXPERT_FF:]).astype(BF16)
        y_ref[...] = _dot(hid, wdn_scr[...])

    @pl.when(jnp.logical_not(used))
    def _():
        y_ref[...] = jnp.zeros_like(y_ref)


def _experts(xs, w_gate, w_up, w_down, layer, blk_exp, n_used):
    n_rows, d = xs.shape
    n_blocks = n_rows // MOE_BLOCK
    grid_spec = pltpu.PrefetchScalarGridSpec(
        num_scalar_prefetch=2,
        grid=(n_blocks,),
        in_specs=[
            pl.BlockSpec((MOE_BLOCK, d), lambda i, be, nu: (i, 0)),
            pl.BlockSpec((1, 1, d, EXPERT_FF), lambda i, be, nu: (layer, be[i], 0, 0)),
            pl.BlockSpec((1, 1, d, EXPERT_FF), lambda i, be, nu: (layer, be[i], 0, 0)),
            pl.BlockSpec((1, 1, EXPERT_FF, d), lambda i, be, nu: (layer, be[i], 0, 0)),
        ],
        out_specs=pl.BlockSpec((MOE_BLOCK, d), lambda i, be, nu: (i, 0)),
        scratch_shapes=[pltpu.VMEM((d, 2 * EXPERT_FF), BF16),
                        pltpu.VMEM((EXPERT_FF, d), BF16)],
    )
    return pl.pallas_call(
        _expert_body,
        out_shape=jax.ShapeDtypeStruct((n_rows, d), F32),
        grid_spec=grid_spec,
        compiler_params=_cparams(("arbitrary",)),
        name="experts",
    )(blk_exp, n_used, xs, w_gate, w_up, w_down)


def _combine_body(dest_ref, y_hbm, x_ref, gate_ref, meta_ref, fn_ref, o_ref, ya_scr, sem,
                  *, final, tile0):
    i = pl.program_id(0)
    n_steps = pl.num_programs(0)
    tm = x_ref.shape[0]
    slot = i % 2

    def start(step, s, straight_line):
        for j in range(2):
            _start_row_gather(dest_ref, (tile0 + step) * tm * 2 + j, tm, 2, y_hbm,
                              ya_scr.at[s, j], sem.at[s, j], straight_line)

    @pl.when(i == 0)
    def _():
        start(0, 0, False)

    for j in range(2):
        _wait_row_gather(y_hbm, ya_scr.at[slot, j], sem.at[slot, j], tm)

    @pl.when(i + 1 < n_steps)
    def _():
        start(i + 1, 1 - slot, True)

    meta = meta_ref[...]
    moe = (ya_scr[slot, 0] * meta[:, META_G:META_G + 1]
           + ya_scr[slot, 1] * meta[:, META_G + 1:META_G + 2])
    out = x_ref[...] + gate_ref[0] * moe
    if final:
        out = _rms(out, fn_ref[...])
    o_ref[...] = out


def _combine(y, x, gate, meta, dest, final_norm, seq, final, tok0=0, n_tok=None, tm=512):
    t, d = x.shape
    n_tok = t if n_tok is None else n_tok
    tm = min(tm, seq)
    tps = seq // tm
    assert tok0 % seq == 0 and n_tok % seq == 0
    tile0 = tok0 // tm
    grid_spec = pltpu.PrefetchScalarGridSpec(
        num_scalar_prefetch=1,
        grid=(n_tok // tm,),
        in_specs=[
            pl.BlockSpec(memory_space=pl.ANY),
            pl.BlockSpec((tm, d), lambda i, ds: (tile0 + i, 0)),
            pl.BlockSpec((1, 1, d), lambda i, ds: ((tile0 + i) // tps, 0, 0)),
            pl.BlockSpec((tm, LANES), lambda i, ds: (tile0 + i, 0)),
            pl.BlockSpec((1, d), lambda i, ds: (0, 0)),
        ],
        out_specs=pl.BlockSpec((tm, d), lambda i, ds: (i, 0)),
        scratch_shapes=[pltpu.VMEM((2, 2, tm, d), F32), pltpu.SemaphoreType.DMA((2, 2))],
    )
    return pl.pallas_call(
        functools.partial(_combine_body, final=final, tile0=tile0),
        out_shape=jax.ShapeDtypeStruct((n_tok, d), F32),
        grid_spec=grid_spec,
        compiler_params=_cparams(("arbitrary",)),
        name="moe_combine",
    )(dest, y, x, gate, meta, final_norm.reshape(1, d))


def _moe_layer(x, g, sc, sh, gate, grp_w, grp_b, exp_w, exp_b, w_gate, w_up, w_down, layer,
               final_norm, seq, final, split=None):
    t, d = x.shape
    pad = LANES - MOE_GROUPS - MOE_EXPERTS
    w_route = jnp.concatenate([grp_w, exp_w, jnp.zeros((d, pad), F32)], axis=1)
    b_route = jnp.concatenate([grp_b, exp_b, jnp.zeros((pad,), F32)]).reshape(1, LANES)
    meta, cnt = _router(x, g, sc, sh, w_route, b_route, seq)

    expert = meta[:, META_E:META_E + 2].astype(I32)
    rank = meta[:, META_R:META_R + 2].astype(I32)
    counts = cnt[0, ROUTE_E0:ROUTE_E0 + MOE_EXPERTS].astype(I32)
    padded = (counts + MOE_BLOCK - 1) // MOE_BLOCK * MOE_BLOCK
    p_ends = jnp.cumsum(padded)
    p_starts = p_ends - padded
    dest = (p_starts[expert] + rank).reshape(-1)
    n_rows = t * 2 + MOE_EXPERTS * MOE_BLOCK
    n_blocks = n_rows // MOE_BLOCK
    blk_start = jnp.arange(n_blocks, dtype=I32) * MOE_BLOCK
    blk_exp = jnp.minimum(jnp.sum((p_ends[None, :] <= blk_start[:, None]).astype(I32), axis=1),
                          MOE_EXPERTS - 1)
    n_used = p_ends[-1:] // MOE_BLOCK
    tail = (n_used + jnp.arange(MOE_EXPERTS, dtype=I32)) * MOE_BLOCK
    zero_starts = jnp.concatenate([jnp.where(padded > counts, p_ends - MOE_BLOCK, -1),
                                   jnp.where(tail < n_rows, tail, -1)]).astype(I32)

    xs = _dispatch(x, g, sc, sh, dest, zero_starts, n_rows, seq)
    y = _experts(xs, w_gate, w_up, w_down, layer, blk_exp, n_used)
    if split is None:
        return _combine(y, x, gate, meta, dest, final_norm, seq, final)
    return tuple(_combine(y, x, gate, meta, dest, final_norm, seq, final, tok0=a, n_tok=b - a)
                 for a, b in ((0, split), (split, t)))


def _rope_tables(seq):
    half = RET_DK // 2
    inv = ROPE_BASE ** (-jnp.arange(half, dtype=F32) / half)
    ang = jnp.arange(seq, dtype=F32)[:, None] * inv[None, :]
    cos, sin = jnp.cos(ang), jnp.sin(ang)
    return jnp.concatenate([cos, cos], axis=1), jnp.concatenate([-sin, sin], axis=1)


def _pad_rows(a, axis, n):
    pad = [(0, 0)] * a.ndim
    pad[axis] = (0, n - a.shape[axis])
    return jnp.pad(a, pad)


def kernel(x_prompt, x_sample, c_prompt, c_sample, ada_w, ada_b, norm1, norm2, ev_w_in, ev_gate_b, ev_conv_w, ev_conv_b, ev_ret_gn, ev_mlstm_gn, ev_w_out, od_w_in, od_conv_w, od_conv_b, od_dt_bias, od_a_log, od_d_skip, od_norm, od_w_out, moe_grp_w, moe_grp_b, moe_exp_w, moe_exp_b, moe_w_gate, moe_w_up, moe_w_down, final_norm):
    n_prompt = x_prompt.shape[0]
    seq, d = x_prompt.shape[1], x_prompt.shape[2]
    assert x_sample.shape[1] == seq and d == D_MODEL
    assert seq % RET_CHUNK == 0 and seq // MLSTM_CHUNK <= LANES // SSD_HPG
    nb = n_prompt + x_sample.shape[0]
    t = nb * seq
    x = x_prompt.reshape(n_prompt * seq, d)
    x_tail = x_sample.reshape(t - n_prompt * seq, d)
    depth = ada_w.shape[0]

    c_all = jnp.concatenate([c_prompt, c_sample], axis=0)
    c_pad = _pad_rows(c_all, 0, -(-nb // 8) * 8)
    mod = _modulation(c_pad, ada_w, ada_b)[:, :nb].reshape(depth, nb, N_MOD, 1, d)

    heads = jnp.arange(RET_HEADS, dtype=F32)
    lg = jnp.stack([jnp.log1p(-jnp.exp2(-RET_DECAY_FWD - heads)),
                    jnp.log1p(-jnp.exp2(-RET_DECAY_BWD - heads))])
    cos_t, sin_t = _rope_tables(seq)

    for i in range(depth):
        sh1, sc1, g1, sh2, sc2, g2 = (mod[i, :, m] for m in range(N_MOD))
        j = i // 2
        if i % 2 == 0:
            w_in = ev_w_in[j]
            w_side = w_in[:, EVEN_MAIN:].reshape(d, 4, MLSTM_HEADS).transpose(0, 2, 1)
            w_side = _pad_rows(_pad_rows(w_side, 2, GATE_ROWS).reshape(d, -1), 1, LANES)
            proj, gates = _fused_matmul(x, w_in[:, :EVEN_MAIN].astype(BF16), seq=seq, x_tail=x_tail,
                                        prologue="normmod", g=norm1[i], sc=sc1, sh=sh1, w_side=w_side,
                                        name="even_in_proj")
            proj = proj.reshape(nb, seq, EVEN_MAIN)
            ret = _retention(proj, lg, cos_t, sin_t, ev_ret_gn[j], nb, seq)
            ml = _mlstm(proj, gates, ev_gate_b[j], ev_conv_w[j], ev_conv_b[j], ev_mlstm_gn[j], nb, seq)
            x = _fused_matmul(ret.reshape(t, RET_V), ev_w_out[j].astype(BF16), seq=seq,
                              x2=ml.reshape(t, MLSTM_V),
                              res=x, res_tail=x_tail, gate=g1, tn=512, name="even_out_proj")
        else:
            w_in = od_w_in[j]
            proj, dt_raw = _fused_matmul(x, w_in[:, :ODD_MAIN].astype(BF16), seq=seq, x_tail=x_tail,
                                         prologue="normmod", g=norm1[i], sc=sc1, sh=sh1,
                                         w_side=w_in[:, ODD_MAIN:], name="odd_in_proj")
            def per_row(p):
                p = p.reshape(2, SSD_GROUPS, SSD_HPG).transpose(1, 0, 2)
                return jnp.tile(p, (1, 1, LANES // SSD_HPG))[..., None]

            def per_lane(p):
                p = p.reshape(*p.shape[:-1], SSD_GROUPS, SSD_HPG)
                p = jnp.moveaxis(p, -2, 0)
                return jnp.repeat(p, SSD_HEADDIM, axis=-1)[..., None, :]

            y = _ssd(proj.reshape(nb, seq, ODD_MAIN), dt_raw, per_row(od_dt_bias[j]), per_row(od_a_log[j]),
                     per_lane(od_d_skip[j]), od_conv_w[j], od_conv_b[j], nb, seq)
            x = _fused_matmul(y.reshape(t, SSD_INNER), od_w_out[j].astype(BF16), seq=seq, prologue="norm",
                              g=od_norm[j], res=x, res_tail=x_tail, gate=g1, tn=512, name="odd_out_proj")
        x_tail = None
        last = i == depth - 1
        x = _moe_layer(x, norm2[i], sc2, sh2, g2, moe_grp_w[i], moe_grp_b[i], moe_exp_w[i], moe_exp_b[i],
                       moe_w_gate, moe_w_up, moe_w_down, i, final_norm, seq, final=last,
                       split=n_prompt * seq if last else None)
    y_prompt, y_sample = x
    return (y_prompt.reshape(n_prompt, seq, d), y_sample.reshape(nb - n_prompt, seq, d))
```

```python
import functools
import math

import jax
import jax.numpy as jnp
import numpy as np
from jax import lax
from jax.experimental import pallas as pl
from jax.experimental.pallas import tpu as pltpu

F32 = jnp.float32
BF16 = jnp.bfloat16
I32 = jnp.int32

D_MODEL = 2048
N_MOD = 6
EPS = 1e-6
CONV_W = 5
CONV_HALO = 8

RET_HEADS = 8
RET_DV = D_MODEL // RET_HEADS
RET_DK = RET_DV // 2
RET_DECAY_FWD = 5.0
RET_DECAY_BWD = 5.5
ROPE_BASE = 10000.0
RET_CHUNK = 256
MLSTM_HEADS = 4
MLSTM_DV = D_MODEL // MLSTM_HEADS
MLSTM_DK = MLSTM_DV // 2
MLSTM_CHUNK = 128
GATE_ROWS = 8
SSD_INNER = 2 * D_MODEL
SSD_HEADDIM = 64
SSD_HEADS = SSD_INNER // SSD_HEADDIM
SSD_GROUPS = 8
SSD_HPG = SSD_HEADS // SSD_GROUPS
SSD_STATE = 128
SSD_CHUNK = 128
MOE_GROUPS = 4
MOE_EPG = 8
MOE_EXPERTS = MOE_GROUPS * MOE_EPG
EXPERT_FF = D_MODEL // 4
MOE_BLOCK = 512

RET_QK = RET_HEADS * RET_DK
RET_V = RET_HEADS * RET_DV
MLSTM_QK = MLSTM_HEADS * MLSTM_DK
MLSTM_V = MLSTM_HEADS * MLSTM_DV
MLSTM_NGATE = 4 * MLSTM_HEADS
EVEN_MAIN = 2 * RET_QK + 2 * RET_V + 2 * MLSTM_QK + 2 * MLSTM_V
EVEN_MIX = RET_V + MLSTM_V
SSD_BC = SSD_GROUPS * SSD_STATE
SSD_CONV_CH = SSD_INNER + 2 * SSD_BC
ODD_MAIN = SSD_INNER + SSD_CONV_CH

PROLOGUE_ROWS = 256
CAST_ROWS = 256
GATHER_UNROLL = 32
SCAN_UNROLL = 2
RET_SCAN_UNROLL = 4
LANES = 128
VMEM_LIMIT = 56 * 1024 * 1024

NEG_INF = float("-inf")


def _cparams(sem, vmem=VMEM_LIMIT):
    return pltpu.CompilerParams(dimension_semantics=sem, vmem_limit_bytes=vmem)


def _split3(x):
    hi = x.astype(BF16)
    r = x - hi.astype(F32)
    mid = r.astype(BF16)
    lo = (r - mid.astype(F32)).astype(BF16)
    return hi, mid, lo


def _dot(a, b):
    return jnp.dot(a, b, preferred_element_type=F32)


def _dot_nt(a, b):
    return lax.dot_general(a, b, (((1,), (1,)), ((), ())), preferred_element_type=F32)


def _dot_tn(a, b):
    return lax.dot_general(a, b, (((0,), (0,)), ((), ())), preferred_element_type=F32)


def _dot01_left(m01, x):
    hi, mid, lo = _split3(x)
    return _dot(m01, hi) + _dot(m01, mid) + _dot(m01, lo)


def _dot01_right(x, m01):
    hi, mid, lo = _split3(x)
    return _dot(hi, m01) + _dot(mid, m01) + _dot(lo, m01)


def _tri(n, kind):
    r = lax.broadcasted_iota(I32, (n, n), 0)
    c = lax.broadcasted_iota(I32, (n, n), 1)
    m = {"le": r <= c, "ge": r >= c, "gt": r > c}[kind]
    return jnp.where(m, 1.0, 0.0).astype(BF16)


def _shr(x, pow2):
    return lax.shift_right_arithmetic(x, jnp.int32(int(math.log2(pow2))))


def _sigmoid(x):
    return 1.0 / (1.0 + jnp.exp(-x))


def _silu(x):
    return x * _sigmoid(x)


def _softplus(x):
    return jnp.maximum(x, 0.0) + jnp.log1p(jnp.exp(-jnp.abs(x)))


def _log_sigmoid(x):
    return jnp.minimum(x, 0.0) - jnp.log1p(jnp.exp(-jnp.abs(x)))


def _rms(x, g):
    ms = jnp.mean(x * x, axis=-1, keepdims=True)
    return x * lax.rsqrt(ms + EPS) * g


def _head_norm(y, g):
    mu = jnp.mean(y, axis=-1, keepdims=True)
    yc = y - mu
    var = jnp.mean(yc * yc, axis=-1, keepdims=True)
    return yc * lax.rsqrt(var + EPS) * g


def _mod_body(c_ref, w_ref, b_ref, o_ref):
    c = c_ref[...]
    o_ref[0] = _dot(_silu(c).astype(BF16), w_ref[0].astype(BF16)) + b_ref[0]


def _modulation(c_pad, ada_w, ada_b):
    depth, d, n = ada_w.shape
    m = c_pad.shape[0]
    tn = 1024
    return pl.pallas_call(
        _mod_body,
        out_shape=jax.ShapeDtypeStruct((depth, m, n), F32),
        grid=(depth, n // tn),
        in_specs=[
            pl.BlockSpec((m, d), lambda l, j: (0, 0)),
            pl.BlockSpec((1, d, tn), lambda l, j: (l, 0, j)),
            pl.BlockSpec((1, 1, tn), lambda l, j: (l, 0, j)),
        ],
        out_specs=pl.BlockSpec((1, m, tn), lambda l, j: (l, 0, j)),
        compiler_params=_cparams(("parallel", "parallel")),
        name="modulation",
    )(c_pad, ada_w, ada_b.reshape(depth, 1, n))


def _mm_body(*refs, prologue, epilogue, side, two_lhs, n_head, split, res_split):
    it = iter(refs)
    x_ref = next(it)
    xt_ref = next(it) if split else None
    x2_ref = next(it) if two_lhs else None
    g_ref = next(it) if prologue != "none" else None
    sc_ref = next(it) if prologue == "normmod" else None
    sh_ref = next(it) if prologue == "normmod" else None
    w_ref = next(it)
    ws_ref = next(it) if side else None
    res_ref = next(it) if epilogue == "residual" else None
    rest_ref = next(it) if res_split else None
    gate_ref = next(it) if epilogue == "residual" else None
    o_ref = next(it)
    os_ref = next(it) if side else None
    h_scr = next(it) if prologue != "none" else None

    def in_head():
        return pl.program_id(0) < n_head

    if prologue != "none":
        def run_prologue(src_ref):
            rows_per = PROLOGUE_ROWS

            def chunk(i, carry):
                rows = pl.ds(pl.multiple_of(i * rows_per, rows_per), rows_per)
                y = _rms(src_ref[rows, :].astype(F32), g_ref[...])
                if prologue == "normmod":
                    y = y * (1.0 + sc_ref[0]) + sh_ref[0]
                hb = y.astype(BF16)
                h_scr[rows, :] = hb
                if side:
                    h_lo = (y - hb.astype(F32)).astype(BF16)
                    w_hi = ws_ref[0]
                    w_lo = ws_ref[1]
                    os_ref[0, :, rows] = (_dot(hb, w_hi) + _dot(h_lo, w_hi) + _dot(hb, w_lo)).T
                return carry

            lax.fori_loop(0, src_ref.shape[0] // rows_per, chunk, 0)

        first_col = pl.program_id(1) == 0
        if split:
            pl.when(first_col & in_head())(lambda: run_prologue(x_ref))
            pl.when(first_col & jnp.logical_not(in_head()))(lambda: run_prologue(xt_ref))
        else:
            pl.when(first_col)(lambda: run_prologue(x_ref))
        lhs = h_scr[...]
    else:
        assert not split
        lhs = x_ref[...]
    if two_lhs:
        k1 = x_ref.shape[1]
        acc = _dot(lhs, w_ref[:k1, :]) + _dot(x2_ref[...], w_ref[k1:, :])
    else:
        acc = _dot(lhs, w_ref[...])
    if epilogue == "residual" and rest_ref is not None:
        @pl.when(in_head())
        def _():
            o_ref[...] = res_ref[...] + gate_ref[0] * acc

        @pl.when(jnp.logical_not(in_head()))
        def _():
            o_ref[...] = rest_ref[...] + gate_ref[0] * acc
    elif epilogue == "residual":
        o_ref[...] = res_ref[...] + gate_ref[0] * acc
    else:
        o_ref[...] = acc.astype(o_ref.dtype)


def _fused_matmul(x, w, *, seq, x_tail=None, x2=None, prologue="none", g=None, sc=None, sh=None,
                  w_side=None, res=None, res_tail=None, gate=None, out_dtype=BF16, tm=1024, tn=1024,
                  name="proj"):
    t, k = x.shape
    n = w.shape[1]
    tm = min(tm, seq)
    tn = min(tn, n)
    split = x_tail is not None
    res_split = res_tail is not None
    n_head = (x.shape[0] if split else res.shape[0] if res_split else t) // tm
    if split:
        t = t + x_tail.shape[0]
    assert t % tm == 0 and seq % tm == 0 and n % tn == 0
    tps = seq // tm
    epilogue = "residual" if res is not None else "plain"
    side = w_side is not None
    two_lhs = x2 is not None
    assert not (two_lhs and prologue != "none")

    def head_rows(i):
        return jnp.minimum(i, n_head - 1)

    def tail_rows(i):
        return jnp.maximum(i - n_head, 0)

    if split:
        in_specs = [pl.BlockSpec((tm, k), lambda i, j: (head_rows(i), 0)),
                    pl.BlockSpec((tm, k), lambda i, j: (tail_rows(i), 0))]
        args = [x, x_tail]
    else:
        in_specs = [pl.BlockSpec((tm, k), lambda i, j: (i, 0))]
        args = [x]
    if two_lhs:
        in_specs.append(pl.BlockSpec((tm, x2.shape[1]), lambda i, j: (i, 0)))
        args.append(x2)
        k = k + x2.shape[1]
    if prologue != "none":
        in_specs.append(pl.BlockSpec((1, k), lambda i, j: (0, 0)))
        args.append(g.reshape(1, k))
    if prologue == "normmod":
        in_specs += [pl.BlockSpec((1, 1, k), lambda i, j: (i // tps, 0, 0))] * 2
        args += [sc, sh]
    in_specs.append(pl.BlockSpec((k, tn), lambda i, j: (0, j)))
    args.append(w)
    if side:
        ws_hi = w_side.astype(BF16)
        ws_lo = (w_side - ws_hi.astype(F32)).astype(BF16)
        in_specs.append(pl.BlockSpec((2, k, LANES), lambda i, j: (0, 0, 0)))
        args.append(jnp.stack([ws_hi, ws_lo]))
    if epilogue == "residual":
        if res_split:
            in_specs += [pl.BlockSpec((tm, tn), lambda i, j: (head_rows(i), jnp.where(i < n_head, j, 0))),
                         pl.BlockSpec((tm, tn), lambda i, j: (tail_rows(i), jnp.where(i < n_head, 0, j)))]
            args += [res, res_tail]
        else:
            in_specs.append(pl.BlockSpec((tm, tn), lambda i, j: (i, j)))
            args.append(res)
        in_specs.append(pl.BlockSpec((1, 1, tn), lambda i, j: (i // tps, 0, j)))
        args.append(gate)
        out_dtype = F32
    out_shape = [jax.ShapeDtypeStruct((t, n), out_dtype)]
    out_specs = [pl.BlockSpec((tm, tn), lambda i, j: (i, j))]
    if side:
        out_shape.append(jax.ShapeDtypeStruct((t // seq, LANES, seq), F32))
        out_specs.append(pl.BlockSpec((1, LANES, tm), lambda i, j: (i // tps, 0, i % tps)))
    scratch = [pltpu.VMEM((tm, k), BF16)] if prologue != "none" else []
    outs = pl.pallas_call(
        functools.partial(_mm_body, prologue=prologue, epilogue=epilogue, side=side, two_lhs=two_lhs,
                          n_head=n_head, split=split, res_split=res_split),
        out_shape=out_shape,
        grid=(t // tm, n // tn),
        in_specs=in_specs,
        out_specs=out_specs,
        scratch_shapes=scratch,
        compiler_params=_cparams(("parallel", "arbitrary")),
        name=name,
    )(*args)
    return outs if side else outs[0]


CONV_ROWS = 128


def _conv_fill(src_ref, pad_scr, seq):
    ch = pad_scr.shape[1]
    halo = CONV_HALO
    rows = CONV_ROWS
    zeros = jnp.zeros((halo, ch), F32)
    pad_scr[pl.ds(0, halo), :] = zeros
    pad_scr[pl.ds(seq + halo, halo), :] = zeros

    def fill(i, carry):
        r0 = pl.multiple_of(i * rows, rows)
        pad_scr[pl.ds(pl.multiple_of(r0 + halo, halo), rows), :] = src_ref[0, pl.ds(r0, rows), :].astype(F32)
        return carry

    lax.fori_loop(0, seq // rows, fill, 0)


def _conv_silu_rows(pad_scr, w_ref, b_ref, r0, emit):
    ch = pad_scr.shape[1]
    halo = CONV_HALO
    rows = CONV_ROWS
    win = rows + 2 * halo
    half = (CONV_W - 1) // 2
    for lane0 in range(0, ch, LANES):
        cols = slice(lane0, lane0 + LANES)
        window = pad_scr[pl.ds(r0, win), cols]
        acc = jnp.zeros((rows, LANES), F32) + b_ref[:, cols]
        for j in range(CONV_W):
            d = j - half
            shifted = window if d == 0 else pltpu.roll(window, (-d) % win, axis=0)
            acc = acc + w_ref[j:j + 1, cols] * shifted[halo:halo + rows, :]
        emit(lane0, _silu(acc))


def _two_ended_scan(n_chunks, prepare, step, finish, scan_unroll=SCAN_UNROLL):
    assert n_chunks % 2 == 0
    half = n_chunks // 2

    def first(i, carry):
        prepare(i)
        prepare(n_chunks - 1 - i)
        step(i)
        return carry

    def second(i, carry):
        step(i)
        finish(i)
        finish(n_chunks - 1 - i)
        return carry

    unroll = scan_unroll if half % scan_unroll == 0 else 1
    lax.fori_loop(0, half, first, 0, unroll=unroll)
    lax.fori_loop(half, n_chunks, second, 0, unroll=unroll)


def _ret_body(lg_ref, q_ref, k_ref, v_ref, g_ref, cos_ref, sin_ref, gn_ref, o_ref,
              qs_scr, ks_scr, acc_scr, st_scr, *, seq):
    c_len = RET_CHUNK
    n_chunks = seq // c_len
    h = pl.program_id(1)
    lgf = lg_ref[0, h]
    lgb = lg_ref[1, h]
    ri = lax.broadcasted_iota(I32, (c_len, c_len), 0)
    ci = lax.broadcasted_iota(I32, (c_len, c_len), 1)
    diff = (ri - ci).astype(F32)
    dmat = jnp.exp(jnp.where(diff >= 0, lgf * diff, -lgb * diff))
    pos = lax.broadcasted_iota(I32, (c_len, 1), 0).astype(F32)
    qdec_f = jnp.exp(lgf * (pos + 1.0))
    kdec_f = jnp.exp(lgf * (c_len - 1.0 - pos))
    cdec_f = jnp.exp(jnp.full((1, 1), c_len, F32) * lgf)
    qdec_b = jnp.exp(lgb * (c_len - pos))
    kdec_b = jnp.exp(lgb * pos)
    cdec_b = jnp.exp(jnp.full((1, 1), c_len, F32) * lgb)
    half = RET_DK // 2

    def rope(x, rows):
        return x * cos_ref[rows, :] + pltpu.roll(x, half, axis=1) * sin_ref[rows, :]

    def chunk_rows(c):
        return pl.ds(pl.multiple_of(c * c_len, c_len), c_len)

    def prepare(c):
        rows = chunk_rows(c)
        qs_scr[rows, :] = rope(q_ref[0, rows, :].astype(F32), rows).astype(BF16)
        ks_scr[rows, :] = rope(k_ref[0, rows, :].astype(F32), rows) * (RET_DK ** -0.5)

    st_scr[...] = jnp.zeros_like(st_scr)

    def step(i):
        rows = chunk_rows(i)
        qb = qs_scr[rows, :]
        k = ks_scr[rows, :]
        v = v_ref[0, rows, :]
        p = (_dot_nt(qb, k.astype(BF16)) * dmat).astype(BF16)
        acc_scr[0, rows, :] = _dot(p, v) + qdec_f * _dot(qb, st_scr[0].astype(BF16))
        st_scr[0] = cdec_f * st_scr[0] + _dot_tn((k * kdec_f).astype(BF16), v)

        rows = chunk_rows(n_chunks - 1 - i)
        k = ks_scr[rows, :]
        acc_scr[1, rows, :] = qdec_b * _dot(qs_scr[rows, :], st_scr[1].astype(BF16))
        st_scr[1] = cdec_b * st_scr[1] + _dot_tn((k * kdec_b).astype(BF16), v_ref[0, rows, :])

    def finish(c):
        rows = chunk_rows(c)
        o = acc_scr[0, rows, :] + acc_scr[1, rows, :]
        gate = g_ref[0, rows, :].astype(F32)
        o_ref[0, rows, :] = (_head_norm(o, gn_ref[0]) * _silu(gate)).astype(o_ref.dtype)

    _two_ended_scan(n_chunks, prepare, step, finish, scan_unroll=RET_SCAN_UNROLL)


def _retention(proj, lg, cos_t, sin_t, ret_gn, nb, seq):
    kq, kv = RET_DK, RET_DV
    grid_spec = dict(
        grid=(nb, RET_HEADS),
        in_specs=[
            pl.BlockSpec(memory_space=pltpu.SMEM),
            pl.BlockSpec((1, seq, kq), lambda b, h: (b, 0, h)),
            pl.BlockSpec((1, seq, kq), lambda b, h: (b, 0, RET_QK // kq + h)),
            pl.BlockSpec((1, seq, kv), lambda b, h: (b, 0, 2 * RET_QK // kv + h)),
            pl.BlockSpec((1, seq, kv), lambda b, h: (b, 0, (2 * RET_QK + RET_V) // kv + h)),
            pl.BlockSpec((seq, kq), lambda b, h: (0, 0)),
            pl.BlockSpec((seq, kq), lambda b, h: (0, 0)),
            pl.BlockSpec((1, 1, kv), lambda b, h: (h, 0, 0)),
        ],
        out_specs=pl.BlockSpec((1, seq, kv), lambda b, h: (b, 0, h)),
        scratch_shapes=[
            pltpu.VMEM((seq, kq), BF16),
            pltpu.VMEM((seq, kq), F32),
            pltpu.VMEM((2, seq, kv), F32),
            pltpu.VMEM((2, kq, kv), F32),
        ],
    )
    return pl.pallas_call(
        functools.partial(_ret_body, seq=seq),
        out_shape=jax.ShapeDtypeStruct((nb, seq, RET_V), BF16),
        **grid_spec,
        compiler_params=_cparams(("parallel", "parallel")),
        name="retention",
    )(lg, proj, proj, proj, proj, cos_t, sin_t, ret_gn.reshape(RET_HEADS, 1, kv))


def _col_of(mat, c):
    lane = lax.broadcasted_iota(I32, mat.shape, 1)
    return jnp.sum(jnp.where(lane == c, mat, 0.0), axis=1, keepdims=True)


def _mlstm_body(gb_ref, q_ref, k_ref, v_ref, o_gate_ref, wq_ref, wk_ref, bq_ref, bk_ref,
                gr_ref, gn_ref, o_ref,
                padq_scr, padk_scr, qs_scr, ks_scr, acc_scr, c_scr, n_scr, m_scr, row_scr, col_scr,
                *, seq):
    ln = MLSTM_CHUNK
    assert ln == CONV_ROWS
    n_chunks = seq // ln
    h = pl.program_id(1)

    _conv_fill(q_ref, padq_scr, seq)
    _conv_fill(k_ref, padk_scr, seq)

    def prepare(c):
        r0 = pl.multiple_of(c * ln, ln)
        rows = pl.ds(r0, ln)

        def emit_q(lane0, y):
            qs_scr[rows, lane0:lane0 + LANES] = (y * (MLSTM_DK ** -0.5)).astype(BF16)

        def emit_k(lane0, y):
            ks_scr[rows, lane0:lane0 + LANES] = y

        _conv_silu_rows(padq_scr, wq_ref, bq_ref, r0, emit_q)
        _conv_silu_rows(padk_scr, wk_ref, bk_ref, r0, emit_k)

    row_scr[...] = jnp.zeros_like(row_scr)
    for t in range(4):
        for c in range(n_chunks):
            row_scr[t, c:c + 1, :] = gr_ref[0, t:t + 1, c * ln:(c + 1) * ln]
    tri_le = _tri(ln, "le")
    tri_ge = _tri(ln, "ge")
    for d in range(2):
        ig = row_scr[2 * d] + gb_ref[(2 * d) * MLSTM_HEADS + h]
        lf = _log_sigmoid(row_scr[2 * d + 1] + gb_ref[(2 * d + 1) * MLSTM_HEADS + h])
        bc = _dot01_right(lf, tri_le if d == 0 else tri_ge)
        row_scr[2 * d] = ig
        row_scr[2 * d + 1] = bc
        col_scr[2 * d] = ig.T
        col_scr[2 * d + 1] = bc.T

    ri = lax.broadcasted_iota(I32, (ln, ln), 0)
    ci = lax.broadcasted_iota(I32, (ln, ln), 1)

    def chunk_step(c, d):
        r0 = pl.multiple_of(c * ln, ln)
        rows = pl.ds(r0, ln)
        qb = qs_scr[rows, :]
        kf = ks_scr[rows, :]
        v = v_ref[0, rows, :]
        i_row = row_scr[2 * d, pl.ds(c, 1), :]
        b_row = row_scr[2 * d + 1, pl.ds(c, 1), :]
        i_col = _col_of(col_scr[2 * d], c)
        b_col = _col_of(col_scr[2 * d + 1], c)
        m_st = m_scr[d]
        mask = (ri >= ci) if d == 0 else (ri <= ci)
        logd = jnp.where(mask, b_col - b_row + i_row, NEG_INF)
        m_inter = b_col + m_st
        m_row = jnp.maximum(m_inter, jnp.max(logd, axis=1, keepdims=True))
        sc = _dot_nt(qb, kf.astype(BF16)) * jnp.exp(logd - m_row)
        inter = jnp.exp(m_inter - m_row)
        num = _dot(sc.astype(BF16), v) + inter * _dot(qb, c_scr[d].astype(BF16))
        den = jnp.sum(sc, axis=1, keepdims=True) + inter * jnp.sum(
            qb.astype(F32) * n_scr[d], axis=1, keepdims=True)
        hh = num / jnp.maximum(jnp.abs(den), jnp.exp(-m_row))
        b_end = b_row[:, ln - 1:ln] if d == 0 else b_row[:, 0:1]
        logw = b_end - b_col + i_col
        m_new = jnp.maximum(b_end + m_st, jnp.max(logw, axis=0, keepdims=True))
        kw = kf * jnp.exp(logw - m_new)
        dec = jnp.exp(b_end + m_st - m_new)
        c_scr[d] = dec * c_scr[d] + _dot_tn(kw.astype(BF16), v)
        n_scr[d] = dec * n_scr[d] + jnp.sum(kw, axis=0, keepdims=True)
        m_scr[d] = m_new
        acc_scr[d, rows, :] = hh

    c_scr[...] = jnp.zeros_like(c_scr)
    n_scr[...] = jnp.zeros_like(n_scr)
    m_scr[...] = jnp.zeros_like(m_scr)

    def step(i):
        chunk_step(i, 0)
        chunk_step(n_chunks - 1 - i, 1)

    def finish(c):
        rows = pl.ds(pl.multiple_of(c * ln, ln), ln)
        y = _head_norm(acc_scr[0, rows, :] + acc_scr[1, rows, :], gn_ref[0])
        o_ref[0, rows, :] = (y * _sigmoid(o_gate_ref[0, rows, :].astype(F32))).astype(o_ref.dtype)

    _two_ended_scan(n_chunks, prepare, step, finish)


def _mlstm(proj, gate_rows, gate_b, conv_w, conv_b, mlstm_gn, nb, seq):
    kq, kv = MLSTM_DK, MLSTM_DV
    q0 = 2 * RET_QK + 2 * RET_V
    k0 = q0 + MLSTM_QK
    v0 = k0 + MLSTM_QK
    o0 = v0 + MLSTM_V
    grid_spec = dict(
        grid=(nb, MLSTM_HEADS),
        in_specs=[
            pl.BlockSpec(memory_space=pltpu.SMEM),
            pl.BlockSpec((1, seq, kq), lambda b, h: (b, 0, q0 // kq + h)),
            pl.BlockSpec((1, seq, kq), lambda b, h: (b, 0, k0 // kq + h)),
            pl.BlockSpec((1, seq, kv), lambda b, h: (b, 0, v0 // kv + h)),
            pl.BlockSpec((1, seq, kv), lambda b, h: (b, 0, o0 // kv + h)),
            pl.BlockSpec((CONV_W, kq), lambda b, h: (0, h)),
            pl.BlockSpec((CONV_W, kq), lambda b, h: (0, MLSTM_QK // kq + h)),
            pl.BlockSpec((1, kq), lambda b, h: (0, h)),
            pl.BlockSpec((1, kq), lambda b, h: (0, MLSTM_QK // kq + h)),
            pl.BlockSpec((1, GATE_ROWS, seq), lambda b, h: (b, h, 0)),
            pl.BlockSpec((1, 1, kv), lambda b, h: (h, 0, 0)),
        ],
        out_specs=pl.BlockSpec((1, seq, kv), lambda b, h: (b, 0, h)),
        scratch_shapes=[
            pltpu.VMEM((seq + 2 * CONV_HALO, kq), F32),
            pltpu.VMEM((seq + 2 * CONV_HALO, kq), F32),
            pltpu.VMEM((seq, kq), BF16),
            pltpu.VMEM((seq, kq), F32),
            pltpu.VMEM((2, seq, kv), F32),
            pltpu.VMEM((2, kq, kv), F32),
            pltpu.VMEM((2, 1, kq), F32),
            pltpu.VMEM((2, 1, 1), F32),
            pltpu.VMEM((4, LANES, MLSTM_CHUNK), F32),
            pltpu.VMEM((4, MLSTM_CHUNK, LANES), F32),
        ],
    )
    return pl.pallas_call(
        functools.partial(_mlstm_body, seq=seq),
        out_shape=jax.ShapeDtypeStruct((nb, seq, MLSTM_V), BF16),
        **grid_spec,
        compiler_params=_cparams(("parallel", "parallel")),
        name="mlstm",
    )(gate_b, proj, proj, proj, proj, conv_w, conv_w, conv_b.reshape(1, -1), conv_b.reshape(1, -1),
      gate_rows, mlstm_gn.reshape(MLSTM_HEADS, 1, kv))


def _ssd_body(z_ref, x_ref, b_ref, c_ref, wx_ref, wb_ref, wc_ref, bx_ref, bb_ref, bc_ref,
              dtf_ref, dtb_ref, bias_ref, alog_ref, dskip_ref, o_ref,
              padx_scr, padb_scr, padc_scr, xs_scr, bs_scr, bst_scr, cs_scr, y_scr, st_scr,
              acr_scr, dtr_scr, er_scr, ur_scr, act_scr, *, seq):
    ln = SSD_CHUNK
    assert ln == CONV_ROWS
    n_chunks = seq // ln
    hp = SSD_HEADDIM
    n_pairs = SSD_HPG // 2

    _conv_fill(x_ref, padx_scr, seq)
    _conv_fill(b_ref, padb_scr, seq)
    _conv_fill(c_ref, padc_scr, seq)

    def prepare(c):
        r0 = pl.multiple_of(c * ln, ln)
        rows = pl.ds(r0, ln)

        def emit_x(lane0, y):
            xs_scr[rows, lane0:lane0 + LANES] = y

        def emit_b(lane0, y):
            bs_scr[rows, :] = y.astype(BF16)
            bst_scr[:, rows] = y.T.astype(BF16)

        def emit_c(lane0, y):
            cs_scr[rows, :] = y.astype(BF16)

        _conv_silu_rows(padx_scr, wx_ref, bx_ref, r0, emit_x)
        _conv_silu_rows(padb_scr, wb_ref, bb_ref, r0, emit_b)
        _conv_silu_rows(padc_scr, wc_ref, bc_ref, r0, emit_c)

    for d, dt_ref in enumerate((dtf_ref, dtb_ref)):
        acr_scr[d] = jnp.zeros((LANES, ln), F32)
        for c in range(n_chunks):
            acr_scr[d, c * SSD_HPG:(c + 1) * SSD_HPG, :] = dt_ref[0, :, c * ln:(c + 1) * ln]
        dt = _softplus(acr_scr[d] + bias_ref[0, d])
        adt = dt * (-jnp.exp(alog_ref[0, d]))
        acum = _dot01_right(adt, _tri(ln, "le" if d == 0 else "ge"))
        a_end = acum[:, ln - 1:ln] if d == 0 else acum[:, 0:1]
        acr_scr[d] = acum
        dtr_scr[d] = dt
        er_scr[d] = jnp.exp(acum)
        ur_scr[d] = dt * jnp.exp(a_end - acum)
        act_scr[d] = acum.T

    ri = lax.broadcasted_iota(I32, (ln, ln), 0)
    ci = lax.broadcasted_iota(I32, (ln, ln), 1)
    in_first = lax.broadcasted_iota(I32, (1, LANES), 1) < hp
    on_diag = ri == ci

    def chunk_rows(c):
        return pl.ds(pl.multiple_of(c * ln, ln), ln)

    def dir_step(d, c):
        rows = chunk_rows(c)
        xb = xs_scr[rows, :].astype(BF16)
        bcm = bs_scr[rows, :]
        bct = bst_scr[:, rows].astype(F32)
        ccm = cs_scr[rows, :]
        cb = _dot_nt(ccm, bcm)
        carried = _dot(ccm, st_scr[d].astype(BF16)).astype(BF16)
        head_rows = pl.ds(pl.multiple_of(c * SSD_HPG, SSD_HPG), SSD_HPG)
        arow = acr_scr[d, head_rows, :]
        dtrow = dtr_scr[d, head_rows, :]
        erow = er_scr[d, head_rows, :]
        urow = ur_scr[d, head_rows, :]
        acols = pltpu.roll(act_scr[d], (LANES - c * SSD_HPG) & (LANES - 1), axis=1)
        mask = (ri >= ci) if d == 0 else (ri <= ci)
        end = ln - 1 if d == 0 else 0
        pieces = []
        for pair in range(n_pairs):
            lanes = slice(pair * LANES, (pair + 1) * LANES)
            lhs, rhs, lhs_state, keep = [], [], [], []
            for sub in range(2):
                k = 2 * pair + sub
                sel = in_first if sub == 0 else ~in_first
                dec = jnp.exp(jnp.where(mask, acols[:, k:k + 1] - arow[k:k + 1, :], NEG_INF))
                lhs.append((cb * dec * dtrow[k:k + 1, :]).astype(BF16))
                rhs.append(jnp.where(sel, xb[:, lanes], jnp.zeros((ln, LANES), BF16)))
                lhs_state.append((bct * urow[k:k + 1, :]).astype(BF16))
                keep.append(erow[k:k + 1, end:end + 1])
            for sub in range(2):
                k = 2 * pair + sub
                sel = in_first if sub == 0 else ~in_first
                lhs.append(jnp.where(on_diag, erow[k:k + 1, :], 0.0).astype(BF16))
                rhs.append(jnp.where(sel, carried[:, lanes], jnp.zeros((ln, LANES), BF16)))
            pieces.append(_dot(jnp.concatenate(lhs, axis=1), jnp.concatenate(rhs, axis=0)))
            st_scr[d, :, lanes] = (jnp.where(in_first, keep[0], keep[1]) * st_scr[d, :, lanes]
                                   + _dot(jnp.concatenate(lhs_state, axis=1),
                                          jnp.concatenate(rhs[:2], axis=0)))
        y_scr[d, rows, :] = jnp.concatenate(pieces, axis=1)

    st_scr[...] = jnp.zeros_like(st_scr)

    def step(i):
        dir_step(0, i)
        dir_step(1, n_chunks - 1 - i)

    def finish(c):
        rows = chunk_rows(c)
        y = y_scr[0, rows, :] + y_scr[1, rows, :] + dskip_ref[0] * xs_scr[rows, :]
        o_ref[0, rows, :] = (y * _silu(z_ref[0, rows, :].astype(F32))).astype(o_ref.dtype)

    _two_ended_scan(n_chunks, prepare, step, finish)


def _ssd(proj, dt_t, bias_col, alog_col, dskip_x, conv_w, conv_b, nb, seq):
    width = SSD_HPG * SSD_HEADDIM
    ns = SSD_STATE
    x0 = SSD_INNER
    b0 = 2 * SSD_INNER
    c0 = b0 + SSD_BC
    cb = conv_b.reshape(1, -1)
    return pl.pallas_call(
        functools.partial(_ssd_body, seq=seq),
        out_shape=jax.ShapeDtypeStruct((nb, seq, SSD_INNER), BF16),
        grid=(nb, SSD_GROUPS),
        in_specs=[
            pl.BlockSpec((1, seq, width), lambda b, g: (b, 0, g)),
            pl.BlockSpec((1, seq, width), lambda b, g: (b, 0, x0 // width + g)),
            pl.BlockSpec((1, seq, ns), lambda b, g: (b, 0, b0 // ns + g)),
            pl.BlockSpec((1, seq, ns), lambda b, g: (b, 0, c0 // ns + g)),
            pl.BlockSpec((CONV_W, width), lambda b, g: (0, g)),
            pl.BlockSpec((CONV_W, ns), lambda b, g: (0, SSD_INNER // ns + g)),
            pl.BlockSpec((CONV_W, ns), lambda b, g: (0, (SSD_INNER + SSD_BC) // ns + g)),
            pl.BlockSpec((1, width), lambda b, g: (0, g)),
            pl.BlockSpec((1, ns), lambda b, g: (0, SSD_INNER // ns + g)),
            pl.BlockSpec((1, ns), lambda b, g: (0, (SSD_INNER + SSD_BC) // ns + g)),
            pl.BlockSpec((1, SSD_HPG, seq), lambda b, g: (b, g, 0)),
            pl.BlockSpec((1, SSD_HPG, seq), lambda b, g: (b, SSD_GROUPS + g, 0)),
            pl.BlockSpec((1, 2, LANES, 1), lambda b, g: (g, 0, 0, 0)),
            pl.BlockSpec((1, 2, LANES, 1), lambda b, g: (g, 0, 0, 0)),
            pl.BlockSpec((1, 1, width), lambda b, g: (g, 0, 0)),
        ],
        out_specs=pl.BlockSpec((1, seq, width), lambda b, g: (b, 0, g)),
        scratch_shapes=[
            pltpu.VMEM((seq + 2 * CONV_HALO, width), F32),
            pltpu.VMEM((seq + 2 * CONV_HALO, ns), F32),
            pltpu.VMEM((seq + 2 * CONV_HALO, ns), F32),
            pltpu.VMEM((seq, width), F32),
            pltpu.VMEM((seq, ns), BF16),
            pltpu.VMEM((ns, seq), BF16),
            pltpu.VMEM((seq, ns), BF16),
            pltpu.VMEM((2, seq, width), F32),
            pltpu.VMEM((2, ns, width), F32),
            pltpu.VMEM((2, LANES, SSD_CHUNK), F32),
            pltpu.VMEM((2, LANES, SSD_CHUNK), F32),
            pltpu.VMEM((2, LANES, SSD_CHUNK), F32),
            pltpu.VMEM((2, LANES, SSD_CHUNK), F32),
            pltpu.VMEM((2, SSD_CHUNK, LANES), F32),
        ],
        compiler_params=_cparams(("parallel", "parallel")),
        name="ssd",
    )(proj, proj, proj, proj, conv_w, conv_w, conv_w, cb, cb, cb,
      dt_t, dt_t, bias_col, alog_col, dskip_x)


META_E = 0
META_G = 2
META_R = 4
ROUTE_E0 = MOE_GROUPS


def _moe_input(x, g_ref, sc_ref, sh_ref):
    return _rms(x, g_ref[...]) * (1.0 + sc_ref[0]) + sh_ref[0]


def _router_body(x_ref, g_ref, sc_ref, sh_ref, w_ref, b_ref, meta_ref, cnt_ref,
                 carry_scr, whi_scr, wlo_scr):
    @pl.when(pl.program_id(0) == 0)
    def _():
        carry_scr[...] = jnp.zeros_like(carry_scr)
        w = w_ref[...]
        w_hi = w.astype(BF16)
        whi_scr[...] = w_hi
        wlo_scr[...] = (w - w_hi.astype(F32)).astype(BF16)

    meta_ref[...] = _route_rows(_moe_input(x_ref[...], g_ref, sc_ref, sh_ref),
                                whi_scr[...], wlo_scr[...], b_ref[...], carry_scr)
    cnt_ref[...] = jnp.broadcast_to(carry_scr[...], cnt_ref.shape)


def _route_rows(y, w_hi, w_lo, bias, carry_scr):
    tm = y.shape[0]
    h_hi = y.astype(BF16)
    h_lo = (y - h_hi.astype(F32)).astype(BF16)
    logits = _dot(h_hi, w_hi) + _dot(h_lo, w_hi) + _dot(h_hi, w_lo) + bias

    lane = lax.broadcasted_iota(I32, (tm, LANES), 1)
    lane_f = lane.astype(F32)
    big = float(LANES)
    is_grp = lane < MOE_GROUPS
    gl = jnp.where(is_grp, logits, NEG_INF)
    gmax = jnp.max(gl, axis=1, keepdims=True)
    gidx = jnp.min(jnp.where(gl == gmax, lane_f, big), axis=1, keepdims=True)
    gprob = 1.0 / jnp.sum(jnp.where(is_grp, jnp.exp(gl - gmax), 0.0), axis=1, keepdims=True)

    el = lane - ROUTE_E0
    el_f = el.astype(F32)
    valid = (el >= 0) & (el < MOE_EXPERTS)
    in_grp = valid & (_shr(el, MOE_EPG).astype(F32) == gidx)
    ev = jnp.where(in_grp, logits, NEG_INF)
    v1 = jnp.max(ev, axis=1, keepdims=True)
    i1 = jnp.min(jnp.where(ev == v1, el_f, big), axis=1, keepdims=True)
    ev2 = jnp.where(el_f == i1, NEG_INF, ev)
    v2 = jnp.max(ev2, axis=1, keepdims=True)
    i2 = jnp.min(jnp.where(ev2 == v2, el_f, big), axis=1, keepdims=True)
    p2 = jnp.exp(v2 - v1)
    s1 = 1.0 / (1.0 + p2)
    gate1 = s1 * gprob
    gate2 = p2 * s1 * gprob

    oh1 = jnp.where(el_f == i1, 1.0, 0.0)
    oh2 = jnp.where(el_f == i2, 1.0, 0.0)
    oh = oh1 + oh2
    before = _dot(_tri(tm, "gt"), oh.astype(BF16)) + carry_scr[...]
    rank1 = jnp.sum(oh1 * before, axis=1, keepdims=True)
    rank2 = jnp.sum(oh2 * before, axis=1, keepdims=True)
    carry_scr[...] = carry_scr[...] + jnp.sum(oh, axis=0, keepdims=True)

    meta = jnp.zeros((tm, LANES), F32)
    for col, val in ((META_E, i1), (META_E + 1, i2), (META_G, gate1), (META_G + 1, gate2),
                     (META_R, rank1), (META_R + 1, rank2)):
        meta = jnp.where(lane == col, val, meta)
    return meta


def _router(x, g, sc, sh, w_route, b_route, seq, tm=512):
    t, k = x.shape
    tm = min(tm, seq)
    tps = seq // tm
    return pl.pallas_call(
        _router_body,
        out_shape=[jax.ShapeDtypeStruct((t, LANES), F32),
                   jax.ShapeDtypeStruct((8, LANES), F32)],
        grid=(t // tm,),
        in_specs=[
            pl.BlockSpec((tm, k), lambda i: (i, 0)),
            pl.BlockSpec((1, k), lambda i: (0, 0)),
            pl.BlockSpec((1, 1, k), lambda i: (i // tps, 0, 0)),
            pl.BlockSpec((1, 1, k), lambda i: (i // tps, 0, 0)),
            pl.BlockSpec((k, LANES), lambda i: (0, 0)),
            pl.BlockSpec((1, LANES), lambda i: (0, 0)),
        ],
        out_specs=[pl.BlockSpec((tm, LANES), lambda i: (i, 0)),
                   pl.BlockSpec((8, LANES), lambda i: (0, 0))],
        scratch_shapes=[pltpu.VMEM((1, LANES), F32), pltpu.VMEM((k, LANES), BF16),
                        pltpu.VMEM((k, LANES), BF16)],
        compiler_params=_cparams(("arbitrary",)),
        name="router",
    )(x, g.reshape(1, k), sc, sh, w_route, b_route)


ZERO_BLOCKS = 2 * MOE_EXPERTS


def _dispatch_body(dest_ref, zero_ref, x_ref, g_ref, sc_ref, sh_ref, xs_hbm, h_scr, zero_scr, sem, zsem):
    i = pl.program_id(0)
    n_steps = pl.num_programs(0)
    tm = x_ref.shape[0]
    slot = i % 2

    def zero_copy(start):
        return pltpu.make_async_copy(zero_scr, xs_hbm.at[pl.ds(start, MOE_BLOCK)], zsem)

    @pl.when(i == 0)
    def _():
        zero_scr[...] = jnp.zeros_like(zero_scr)

        def issue(k, carry):
            @pl.when(zero_ref[k] >= 0)
            def _():
                zero_copy(pl.multiple_of(zero_ref[k], MOE_BLOCK)).start()
            return carry

        def drain(k, carry):
            @pl.when(zero_ref[k] >= 0)
            def _():
                zero_copy(0).wait()
            return carry

        lax.fori_loop(0, ZERO_BLOCKS, issue, 0)
        lax.fori_loop(0, ZERO_BLOCKS, drain, 0)

    def wait_rows(s):
        for _ in range(2):
            pltpu.make_async_copy(h_scr.at[s], xs_hbm.at[pl.ds(0, tm)], sem.at[s]).wait()

    @pl.when(i >= 2)
    def _():
        wait_rows(slot)

    h_scr[slot] = _moe_input(x_ref[...], g_ref, sc_ref, sh_ref)

    for r in range(tm):
        for j in range(2):
            pltpu.make_async_copy(h_scr.at[slot, pl.ds(r, 1)],
                                  xs_hbm.at[pl.ds(dest_ref[(i * tm + r) * 2 + j], 1)],
                                  sem.at[slot]).start()

    @pl.when(i == n_steps - 1)
    def _():
        @pl.when(n_steps >= 2)
        def _():
            wait_rows(1 - slot)
        wait_rows(slot)


def _dispatch(x, g, sc, sh, dest, zero_starts, n_rows, seq, tm=512):
    t, k = x.shape
    tm = min(tm, seq)
    tps = seq // tm
    grid_spec = pltpu.PrefetchScalarGridSpec(
        num_scalar_prefetch=2,
        grid=(t // tm,),
        in_specs=[
            pl.BlockSpec((tm, k), lambda i, ds, zs: (i, 0)),
            pl.BlockSpec((1, k), lambda i, ds, zs: (0, 0)),
            pl.BlockSpec((1, 1, k), lambda i, ds, zs: (i // tps, 0, 0)),
            pl.BlockSpec((1, 1, k), lambda i, ds, zs: (i // tps, 0, 0)),
        ],
        out_specs=pl.BlockSpec(memory_space=pl.ANY),
        scratch_shapes=[pltpu.VMEM((2, tm, k), F32), pltpu.VMEM((MOE_BLOCK, k), F32),
                        pltpu.SemaphoreType.DMA((2,)), pltpu.SemaphoreType.DMA],
    )
    return pl.pallas_call(
        _dispatch_body,
        out_shape=jax.ShapeDtypeStruct((n_rows, k), F32),
        grid_spec=grid_spec,
        compiler_params=_cparams(("arbitrary",)),
        name="moe_dispatch",
    )(dest, zero_starts, x, g.reshape(1, k), sc, sh)


def _row_copy(src_hbm, dst, sem, src_row, dst_row):
    return pltpu.make_async_copy(src_hbm.at[pl.ds(src_row, 1)], dst.at[pl.ds(dst_row, 1)], sem)


def _start_row_gather(idx_ref, base, n_rows, stride, src_hbm, dst, sem, straight_line=False):
    def body(r, carry):
        _row_copy(src_hbm, dst, sem, idx_ref[base + r * stride], r).start()
        return carry
    if straight_line:
        for r in range(n_rows):
            body(r, 0)
    else:
        lax.fori_loop(0, n_rows, body, 0, unroll=GATHER_UNROLL)


def _wait_row_gather(src_hbm, dst, sem, n_rows):
    pltpu.make_async_copy(src_hbm.at[pl.ds(0, n_rows)], dst, sem).wait()


def _expert_body(blk_exp_ref, n_used_ref, xs_ref, wg_ref, wu_ref, wd_ref, y_ref, wgu_scr, wdn_scr):
    i = pl.program_id(0)
    d = xs_ref.shape[1]
    used = i < n_used_ref[0]

    @pl.when(used & ((i == 0) | (blk_exp_ref[i] != blk_exp_ref[jnp.maximum(i - 1, 0)])))
    def _():
        def cast_up(c, carry):
            rows = pl.ds(pl.multiple_of(c * CAST_ROWS, CAST_ROWS), CAST_ROWS)
            wgu_scr[rows, :EXPERT_FF] = wg_ref[0, 0, rows, :].astype(BF16)
            wgu_scr[rows, EXPERT_FF:] = wu_ref[0, 0, rows, :].astype(BF16)
            return carry

        def cast_down(c, carry):
            rows = pl.ds(pl.multiple_of(c * CAST_ROWS, CAST_ROWS), CAST_ROWS)
            wdn_scr[rows, :] = wd_ref[0, 0, rows, :].astype(BF16)
            return carry

        lax.fori_loop(0, d // CAST_ROWS, cast_up, 0)
        lax.fori_loop(0, EXPERT_FF // CAST_ROWS, cast_down, 0)

    @pl.when(used)
    def _():
        a = _dot(xs_ref[...].astype(BF16), wgu_scr[...])
        hid = (_silu(a[:, :EXPERT_FF]) * a[:, EXPERT_FF:]).astype(BF16)
        y_ref[...] = _dot(hid, wdn_scr[...])

    @pl.when(jnp.logical_not(used))
    def _():
        y_ref[...] = jnp.zeros_like(y_ref)


def _experts(xs, w_gate, w_up, w_down, layer, blk_exp, n_used):
    n_rows, d = xs.shape
    n_blocks = n_rows // MOE_BLOCK
    grid_spec = pltpu.PrefetchScalarGridSpec(
        num_scalar_prefetch=2,
        grid=(n_blocks,),
        in_specs=[
            pl.BlockSpec((MOE_BLOCK, d), lambda i, be, nu: (i, 0)),
            pl.BlockSpec((1, 1, d, EXPERT_FF), lambda i, be, nu: (layer, be[i], 0, 0)),
            pl.BlockSpec((1, 1, d, EXPERT_FF), lambda i, be, nu: (layer, be[i], 0, 0)),
            pl.BlockSpec((1, 1, EXPERT_FF, d), lambda i, be, nu: (layer, be[i], 0, 0)),
        ],
        out_specs=pl.BlockSpec((MOE_BLOCK, d), lambda i, be, nu: (i, 0)),
        scratch_shapes=[pltpu.VMEM((d, 2 * EXPERT_FF), BF16),
                        pltpu.VMEM((EXPERT_FF, d), BF16)],
    )
    return pl.pallas_call(
        _expert_body,
        out_shape=jax.ShapeDtypeStruct((n_rows, d), F32),
        grid_spec=grid_spec,
        compiler_params=_cparams(("arbitrary",)),
        name="experts",
    )(blk_exp, n_used, xs, w_gate, w_up, w_down)


def _combine_body(dest_ref, y_hbm, x_ref, gate_ref, meta_ref, fn_ref, o_ref, ya_scr, sem,
                  *, final, tile0):
    i = pl.program_id(0)
    n_steps = pl.num_programs(0)
    tm = x_ref.shape[0]
    slot = i % 2

    def start(step, s, straight_line):
        for j in range(2):
            _start_row_gather(dest_ref, (tile0 + step) * tm * 2 + j, tm, 2, y_hbm,
                              ya_scr.at[s, j], sem.at[s, j], straight_line)

    @pl.when(i == 0)
    def _():
        start(0, 0, False)

    for j in range(2):
        _wait_row_gather(y_hbm, ya_scr.at[slot, j], sem.at[slot, j], tm)

    @pl.when(i + 1 < n_steps)
    def _():
        start(i + 1, 1 - slot, True)

    meta = meta_ref[...]
    moe = (ya_scr[slot, 0] * meta[:, META_G:META_G + 1]
           + ya_scr[slot, 1] * meta[:, META_G + 1:META_G + 2])
    out = x_ref[...] + gate_ref[0] * moe
    if final:
        out = _rms(out, fn_ref[...])
    o_ref[...] = out


def _combine(y, x, gate, meta, dest, final_norm, seq, final, tok0=0, n_tok=None, tm=512):
    t, d = x.shape
    n_tok = t if n_tok is None else n_tok
    tm = min(tm, seq)
    tps = seq // tm
    assert tok0 % seq == 0 and n_tok % seq == 0
    tile0 = tok0 // tm
    grid_spec = pltpu.PrefetchScalarGridSpec(
        num_scalar_prefetch=1,
        grid=(n_tok // tm,),
        in_specs=[
            pl.BlockSpec(memory_space=pl.ANY),
            pl.BlockSpec((tm, d), lambda i, ds: (tile0 + i, 0)),
            pl.BlockSpec((1, 1, d), lambda i, ds: ((tile0 + i) // tps, 0, 0)),
            pl.BlockSpec((tm, LANES), lambda i, ds: (tile0 + i, 0)),
            pl.BlockSpec((1, d), lambda i, ds: (0, 0)),
        ],
        out_specs=pl.BlockSpec((tm, d), lambda i, ds: (i, 0)),
        scratch_shapes=[pltpu.VMEM((2, 2, tm, d), F32), pltpu.SemaphoreType.DMA((2, 2))],
    )
    return pl.pallas_call(
        functools.partial(_combine_body, final=final, tile0=tile0),
        out_shape=jax.ShapeDtypeStruct((n_tok, d), F32),
        grid_spec=grid_spec,
        compiler_params=_cparams(("arbitrary",)),
        name="moe_combine",
    )(dest, y, x, gate, meta, final_norm.reshape(1, d))


def _moe_layer(x, g, sc, sh, gate, grp_w, grp_b, exp_w, exp_b, w_gate, w_up, w_down, layer,
               final_norm, seq, final, split=None):
    t, d = x.shape
    pad = LANES - MOE_GROUPS - MOE_EXPERTS
    w_route = jnp.concatenate([grp_w, exp_w, jnp.zeros((d, pad), F32)], axis=1)
    b_route = jnp.concatenate([grp_b, exp_b, jnp.zeros((pad,), F32)]).reshape(1, LANES)
    meta, cnt = _router(x, g, sc, sh, w_route, b_route, seq)

    expert = meta[:, META_E:META_E + 2].astype(I32)
    rank = meta[:, META_R:META_R + 2].astype(I32)
    counts = cnt[0, ROUTE_E0:ROUTE_E0 + MOE_EXPERTS].astype(I32)
    padded = (counts + MOE_BLOCK - 1) // MOE_BLOCK * MOE_BLOCK
    p_ends = jnp.cumsum(padded)
    p_starts = p_ends - padded
    dest = (p_starts[expert] + rank).reshape(-1)
    n_rows = t * 2 + MOE_EXPERTS * MOE_BLOCK
    n_blocks = n_rows // MOE_BLOCK
    blk_start = jnp.arange(n_blocks, dtype=I32) * MOE_BLOCK
    blk_exp = jnp.minimum(jnp.sum((p_ends[None, :] <= blk_start[:, None]).astype(I32), axis=1),
                          MOE_EXPERTS - 1)
    n_used = p_ends[-1:] // MOE_BLOCK
    tail = (n_used + jnp.arange(MOE_EXPERTS, dtype=I32)) * MOE_BLOCK
    zero_starts = jnp.concatenate([jnp.where(padded > counts, p_ends - MOE_BLOCK, -1),
                                   jnp.where(tail < n_rows, tail, -1)]).astype(I32)

    xs = _dispatch(x, g, sc, sh, dest, zero_starts, n_rows, seq)
    y = _experts(xs, w_gate, w_up, w_down, layer, blk_exp, n_used)
    if split is None:
        return _combine(y, x, gate, meta, dest, final_norm, seq, final)
    return tuple(_combine(y, x, gate, meta, dest, final_norm, seq, final, tok0=a, n_tok=b - a)
                 for a, b in ((0, split), (split, t)))


def _rope_tables(seq):
    half = RET_DK // 2
    inv = ROPE_BASE ** (-jnp.arange(half, dtype=F32) / half)
    ang = jnp.arange(seq, dtype=F32)[:, None] * inv[None, :]
    cos, sin = jnp.cos(ang), jnp.sin(ang)
    return jnp.concatenate([cos, cos], axis=1), jnp.concatenate([-sin, sin], axis=1)


def _pad_rows(a, axis, n):
    pad = [(0, 0)] * a.ndim
    pad[axis] = (0, n - a.shape[axis])
    return jnp.pad(a, pad)


def kernel(x_prompt, x_sample, c_prompt, c_sample, ada_w, ada_b, norm1, norm2, ev_w_in, ev_gate_b, ev_conv_w, ev_conv_b, ev_ret_gn, ev_mlstm_gn, ev_w_out, od_w_in, od_conv_w, od_conv_b, od_dt_bias, od_a_log, od_d_skip, od_norm, od_w_out, moe_grp_w, moe_grp_b, moe_exp_w, moe_exp_b, moe_w_gate, moe_w_up, moe_w_down, final_norm):
    n_prompt = x_prompt.shape[0]
    seq, d = x_prompt.shape[1], x_prompt.shape[2]
    assert x_sample.shape[1] == seq and d == D_MODEL
    assert seq % RET_CHUNK == 0 and seq // MLSTM_CHUNK <= LANES // SSD_HPG
    nb = n_prompt + x_sample.shape[0]
    t = nb * seq
    x = x_prompt.reshape(n_prompt * seq, d)
    x_tail = x_sample.reshape(t - n_prompt * seq, d)
    depth = ada_w.shape[0]

    c_all = jnp.concatenate([c_prompt, c_sample], axis=0)
    c_pad = _pad_rows(c_all, 0, -(-nb // 8) * 8)
    mod = _modulation(c_pad, ada_w, ada_b)[:, :nb].reshape(depth, nb, N_MOD, 1, d)

    heads = jnp.arange(RET_HEADS, dtype=F32)
    lg = jnp.stack([jnp.log1p(-jnp.exp2(-RET_DECAY_FWD - heads)),
                    jnp.log1p(-jnp.exp2(-RET_DECAY_BWD - heads))])
    cos_t, sin_t = _rope_tables(seq)

    for i in range(depth):
        sh1, sc1, g1, sh2, sc2, g2 = (mod[i, :, m] for m in range(N_MOD))
        j = i // 2
        if i % 2 == 0:
            w_in = ev_w_in[j]
            w_side = w_in[:, EVEN_MAIN:].reshape(d, 4, MLSTM_HEADS).transpose(0, 2, 1)
            w_side = _pad_rows(_pad_rows(w_side, 2, GATE_ROWS).reshape(d, -1), 1, LANES)
            proj, gates = _fused_matmul(x, w_in[:, :EVEN_MAIN].astype(BF16), seq=seq, x_tail=x_tail,
                                        prologue="normmod", g=norm1[i], sc=sc1, sh=sh1, w_side=w_side,
                                        name="even_in_proj")
            proj = proj.reshape(nb, seq, EVEN_MAIN)
            ret = _retention(proj, lg, cos_t, sin_t, ev_ret_gn[j], nb, seq)
            ml = _mlstm(proj, gates, ev_gate_b[j], ev_conv_w[j], ev_conv_b[j], ev_mlstm_gn[j], nb, seq)
            x = _fused_matmul(ret.reshape(t, RET_V), ev_w_out[j].astype(BF16), seq=seq,
                              x2=ml.reshape(t, MLSTM_V),
                              res=x, res_tail=x_tail, gate=g1, tn=512, name="even_out_proj")
        else:
            w_in = od_w_in[j]
            proj, dt_raw = _fused_matmul(x, w_in[:, :ODD_MAIN].astype(BF16), seq=seq, x_tail=x_tail,
                                         prologue="normmod", g=norm1[i], sc=sc1, sh=sh1,
                                         w_side=w_in[:, ODD_MAIN:], name="odd_in_proj")
            def per_row(p):
                p = p.reshape(2, SSD_GROUPS, SSD_HPG).transpose(1, 0, 2)
                return jnp.tile(p, (1, 1, LANES // SSD_HPG))[..., None]

            def per_lane(p):
                p = p.reshape(*p.shape[:-1], SSD_GROUPS, SSD_HPG)
                p = jnp.moveaxis(p, -2, 0)
                return jnp.repeat(p, SSD_HEADDIM, axis=-1)[..., None, :]

            y = _ssd(proj.reshape(nb, seq, ODD_MAIN), dt_raw, per_row(od_dt_bias[j]), per_row(od_a_log[j]),
                     per_lane(od_d_skip[j]), od_conv_w[j], od_conv_b[j], nb, seq)
            x = _fused_matmul(y.reshape(t, SSD_INNER), od_w_out[j].astype(BF16), seq=seq, prologue="norm",
                              g=od_norm[j], res=x, res_tail=x_tail, gate=g1, tn=512, name="odd_out_proj")
        x_tail = None
        last = i == depth - 1
        x = _moe_layer(x, norm2[i], sc2, sh2, g2, moe_grp_w[i], moe_grp_b[i], moe_exp_w[i], moe_exp_b[i],
                       moe_w_gate, moe_w_up, moe_w_down, i, final_norm, seq, final=last,
                       split=n_prompt * seq if last else None)
    y_prompt, y_sample = x
    return (y_prompt.reshape(n_prompt, seq, d), y_sample.reshape(nb - n_prompt, seq, d))
```
